```python
import math
import jax, jax.numpy as jnp
from jax import lax
import numpy as np

D_MODEL = 2048
BATCH = 4
SEQ = 2048
DEPTH = 2
DEC_BATCH = 16
DEC_SEQ = 32
PAST_LEN = 4096

CHUNK = 64
MIX_WIDTH = D_MODEL
HGRN_WIDTH = MIX_WIDTH // 2
HGRN_HEAD_DIM = 128
HGRN_HEADS = HGRN_WIDTH // HGRN_HEAD_DIM
S5_WIDTH = MIX_WIDTH - HGRN_WIDTH
S5_GROUP = 16
S5_GROUPS = S5_WIDTH // S5_GROUP
S5_STATE = 64
D_FF = 4 * D_MODEL
IN_COLS = 4 * HGRN_WIDTH + S5_WIDTH
EPS = 1e-6
DT_MIN = 1e-3
DT_MAX = 1e-1

kernel_name = "hymba_hgrn2_s5_streaming_step"


def rmsnorm(x, g):
    x32 = x.astype(jnp.float32)
    y = x32 * lax.rsqrt(jnp.mean(jnp.square(x32), axis=-1, keepdims=True) + EPS)
    return (y * g.astype(jnp.float32)).astype(x.dtype)


def _to_blocks(a, L):
    B, T, H, K = a.shape
    return a.reshape(B, T // L, L, H, K).transpose(1, 0, 3, 2, 4)


def hgrn_recurrence(q, k, v, logf, S0, L):
    B, T, H, K = q.shape
    V = v.shape[-1]
    causal = jnp.tril(jnp.ones((L, L), dtype=bool))

    def step(S, blk):
        qb, kb, vb, gb = blk
        cum = jnp.cumsum(gb, axis=2)
        diff = cum[:, :, :, None, :] - cum[:, :, None, :, :]
        decay = jnp.where(causal[None, None, :, :, None], jnp.exp(jnp.minimum(diff, 0.0)), 0.0)
        attn = jnp.einsum('bhtk,bhsk,bhtsk->bhts', qb, kb, decay)
        o = (jnp.einsum('bhts,bhsv->bhtv', attn, vb)
             + jnp.einsum('bhtk,bhkv->bhtv', qb * jnp.exp(cum), S))
        last = cum[:, :, -1:, :]
        S_new = (jnp.exp(last[:, :, 0, :])[..., None] * S
                 + jnp.einsum('bhsk,bhsv->bhkv', kb * jnp.exp(last - cum), vb))
        return S_new, o

    blocks = (_to_blocks(q, L), _to_blocks(k, L), _to_blocks(v, L), _to_blocks(logf, L))
    S_fin, o = lax.scan(step, S0, blocks)
    o = o.transpose(1, 0, 3, 2, 4).reshape(B, T, H, V)
    return o, S_fin


def hgrn_mixer(q_raw, f_raw, i_raw, g_raw, lb, onorm_g, S0):
    B, T, _ = q_raw.shape
    shp = (B, T, HGRN_HEADS, HGRN_HEAD_DIM)
    f32 = jnp.float32
    q = jax.nn.silu(q_raw.astype(f32)).reshape(shp)
    lbv = lb.astype(f32).reshape(HGRN_HEADS, HGRN_HEAD_DIM)
    logf = jnp.logaddexp(jnp.log(lbv),
                         jnp.log1p(-lbv) + jax.nn.log_sigmoid(f_raw.astype(f32).reshape(shp)))
    k = -jnp.expm1(logf)
    v = i_raw.astype(f32).reshape(shp)
    L = min(CHUNK, T)
    o, S = hgrn_recurrence(q, k, v, logf, S0.astype(f32), L)
    o = o * lax.rsqrt(jnp.mean(jnp.square(o), axis=-1, keepdims=True) + EPS)
    o = o * onorm_g.astype(f32).reshape(HGRN_HEADS, HGRN_HEAD_DIM)
    o = o.reshape(B, T, HGRN_WIDTH) * jax.nn.silu(g_raw.astype(f32))
    return o, S


def _ssm_combine(e1, e2):
    a1r, a1i, b1r, b1i = e1
    a2r, a2i, b2r, b2i = e2
    return (a2r * a1r - a2i * a1i,
            a2r * a1i + a2i * a1r,
            a2r * b1r - a2i * b1i + b2r,
            a2r * b1i + a2i * b1r + b2i)


def s5_mixer(u_raw, x0r, x0i, lam_re, lam_im, log_step, B_re, B_im, C_re, C_im, D, w_glu, b_glu):
    f32 = jnp.float32
    Bsz, T, _ = u_raw.shape
    u = u_raw.astype(f32)
    ug = u.reshape(Bsz, T, S5_GROUPS, S5_GROUP)
    lr, li = lam_re.astype(f32), lam_im.astype(f32)
    dt = jnp.exp(log_step.astype(f32))[:, None]
    mag = jnp.exp(dt * lr)
    lbr, lbi = mag * jnp.cos(dt * li), mag * jnp.sin(dt * li)
    nr, ni = lbr - 1.0, lbi
    den = lr * lr + li * li
    cr, ci = (nr * lr + ni * li) / den, (ni * lr - nr * li) / den
    Br, Bi = B_re.astype(f32), B_im.astype(f32)
    Bbr = cr[..., None] * Br - ci[..., None] * Bi
    Bbi = cr[..., None] * Bi + ci[..., None] * Br
    bu_r = jnp.einsum('btgc,gnc->btgn', ug, Bbr)
    bu_i = jnp.einsum('btgc,gnc->btgn', ug, Bbi)
    x0r, x0i = x0r.astype(f32), x0i.astype(f32)
    bu_r = bu_r.at[:, 0].add(lbr * x0r - lbi * x0i)
    bu_i = bu_i.at[:, 0].add(lbr * x0i + lbi * x0r)
    ar = jnp.broadcast_to(lbr, bu_r.shape)
    ai = jnp.broadcast_to(lbi, bu_i.shape)
    _, _, xr, xi = lax.associative_scan(_ssm_combine, (ar, ai, bu_r, bu_i), axis=1)
    y = (jnp.einsum('btgn,gcn->btgc', xr, C_re.astype(f32))
         - jnp.einsum('btgn,gcn->btgc', xi, C_im.astype(f32)))
    y = y + D.astype(f32).reshape(S5_GROUPS, S5_GROUP) * ug
    y = y.reshape(Bsz, T, S5_WIDTH)
    hh = jax.nn.gelu(y)
    out = hh * jax.nn.sigmoid(hh @ w_glu.astype(f32) + b_glu.astype(f32))
    return out, xr[:, -1], xi[:, -1]


def trunk(x, st_h, st_r, st_i, norm1_g, w_in, lb_all, hgrn_onorm_g, s5_lambda_re, s5_lambda_im,
          s5_log_step, s5_B_re, s5_B_im, s5_C_re, s5_C_im, s5_D, s5_w_glu, s5_b_glu, w_out,
          norm2_g, w_ff1, w_ff2, final_norm_g):
    new_h, new_r, new_i = [], [], []
    splits = [HGRN_WIDTH, 2 * HGRN_WIDTH, 3 * HGRN_WIDTH, 4 * HGRN_WIDTH]
    for l in range(DEPTH):
        h = rmsnorm(x, norm1_g[l])
        p = jnp.einsum('btd,dc->btc', h, w_in[l])
        q_raw, f_raw, i_raw, g_raw, u_raw = jnp.split(p, splits, axis=-1)
        o_h, S = hgrn_mixer(q_raw, f_raw, i_raw, g_raw, lb_all[l], hgrn_onorm_g[l], st_h[l])
        o_s, xr, xi = s5_mixer(u_raw, st_r[l], st_i[l], s5_lambda_re[l], s5_lambda_im[l],
                               s5_log_step[l], s5_B_re[l], s5_B_im[l], s5_C_re[l], s5_C_im[l],
                               s5_D[l], s5_w_glu[l], s5_b_glu[l])
        mix = jnp.concatenate([o_h, o_s], axis=-1).astype(x.dtype)
        x = x + jnp.einsum('btc,cd->btd', mix, w_out[l])
        h2 = rmsnorm(x, norm2_g[l])
        a = jax.nn.relu(jnp.einsum('btd,df->btf', h2, w_ff1[l]))
        x = x + jnp.einsum('btf,fd->btd', jnp.square(a), w_ff2[l])
        new_h.append(S)
        new_r.append(xr)
        new_i.append(xi)
    y = rmsnorm(x, final_norm_g)
    return y, jnp.stack(new_h), jnp.stack(new_r), jnp.stack(new_i)


def setup_inputs(seed: int = 0) -> dict:
    key = jax.random.key(seed)
    ks = jax.random.split(key, 32)
    f32 = jnp.float32
    nrm = lambda k, shp, s: jax.random.normal(k, shp, f32) * s
    n_idx = jnp.arange(S5_STATE, dtype=f32)
    inputs = {
        "x_prompt": nrm(ks[0], (BATCH, SEQ, D_MODEL), 1.0),
        "x_sample": nrm(ks[1], (DEC_BATCH, DEC_SEQ, D_MODEL), 1.0),
        "state_hgrn": nrm(ks[2], (DEPTH, DEC_BATCH, HGRN_HEADS, HGRN_HEAD_DIM, HGRN_HEAD_DIM), 0.5),
        "state_s5_re": nrm(ks[3], (DEPTH, DEC_BATCH, S5_GROUPS, S5_STATE), 0.1),
        "state_s5_im": nrm(ks[4], (DEPTH, DEC_BATCH, S5_GROUPS, S5_STATE), 0.1),
        "norm1_g": 1.0 + nrm(ks[5], (DEPTH, D_MODEL), 0.02),
        "w_in": nrm(ks[6], (DEPTH, D_MODEL, IN_COLS), D_MODEL ** -0.5),
        "hgrn_lb_logits": nrm(ks[7], (DEPTH, HGRN_WIDTH), 0.5),
        "hgrn_onorm_g": 1.0 + nrm(ks[8], (DEPTH, HGRN_WIDTH), 0.02),
        "s5_lambda_re": -0.5 + nrm(ks[9], (DEPTH, S5_GROUPS, S5_STATE), 0.01),
        "s5_lambda_im": math.pi * n_idx + nrm(ks[10], (DEPTH, S5_GROUPS, S5_STATE), 0.01),
        "s5_log_step": jax.random.uniform(ks[11], (DEPTH, S5_GROUPS), f32,
                                          math.log(DT_MIN), math.log(DT_MAX)),
        "s5_B_re": nrm(ks[12], (DEPTH, S5_GROUPS, S5_STATE, S5_GROUP), (2 * S5_GROUP) ** -0.5),
        "s5_B_im": nrm(ks[13], (DEPTH, S5_GROUPS, S5_STATE, S5_GROUP), (2 * S5_GROUP) ** -0.5),
        "s5_C_re": nrm(ks[14], (DEPTH, S5_GROUPS, S5_GROUP, S5_STATE), S5_STATE ** -0.5),
        "s5_C_im": nrm(ks[15], (DEPTH, S5_GROUPS, S5_GROUP, S5_STATE), S5_STATE ** -0.5),
        "s5_D": nrm(ks[16], (DEPTH, S5_WIDTH), 1.0),
        "s5_w_glu": nrm(ks[17], (DEPTH, S5_WIDTH, S5_WIDTH), S5_WIDTH ** -0.5),
        "s5_b_glu": nrm(ks[18], (DEPTH, S5_WIDTH), 0.01),
        "w_out": nrm(ks[19], (DEPTH, MIX_WIDTH, D_MODEL), MIX_WIDTH ** -0.5),
        "norm2_g": 1.0 + nrm(ks[20], (DEPTH, D_MODEL), 0.02),
        "w_ff1": nrm(ks[21], (DEPTH, D_MODEL, D_FF), D_MODEL ** -0.5),
        "w_ff2": nrm(ks[22], (DEPTH, D_FF, D_MODEL), 0.5 * D_FF ** -0.5),
        "final_norm_g": 1.0 + nrm(ks[23], (D_MODEL,), 0.02),
    }
    return inputs


def reference(x_prompt, x_sample, state_hgrn, state_s5_re, state_s5_im, norm1_g, w_in,
              hgrn_lb_logits, hgrn_onorm_g, s5_lambda_re, s5_lambda_im, s5_log_step, s5_B_re,
              s5_B_im, s5_C_re, s5_C_im, s5_D, s5_w_glu, s5_b_glu, w_out, norm2_g, w_ff1,
              w_ff2, final_norm_g):
    f32 = jnp.float32
    cum = jnp.cumsum(jax.nn.softmax(hgrn_lb_logits.astype(f32), axis=0), axis=0)
    lb_all = cum - cum[0:1]
    weights = (norm1_g, w_in, lb_all, hgrn_onorm_g, s5_lambda_re, s5_lambda_im, s5_log_step,
               s5_B_re, s5_B_im, s5_C_re, s5_C_im, s5_D, s5_w_glu, s5_b_glu, w_out, norm2_g,
               w_ff1, w_ff2, final_norm_g)
    Bp = x_prompt.shape[0]
    zh = jnp.zeros((DEPTH, Bp, HGRN_HEADS, HGRN_HEAD_DIM, HGRN_HEAD_DIM), f32)
    zs = jnp.zeros((DEPTH, Bp, S5_GROUPS, S5_STATE), f32)
    y_prompt, hp, rp, ip = trunk(x_prompt, zh, zs, zs, *weights)
    y_sample, hs, rs, is_ = trunk(x_sample, state_hgrn, state_s5_re, state_s5_im, *weights)
    return (y_prompt, y_sample, hp, rp, ip, hs, rs, is_)
```

```python
import functools

import jax
import jax.numpy as jnp
from jax import lax
from jax.experimental import pallas as pl
from jax.experimental.pallas import tpu as pltpu

F32 = jnp.float32
BF16 = jnp.bfloat16
EPS = 1e-6

HEAD_DIM = 128
HGRN_CHUNK = 64
HGRN_SUB = 16
S5_GROUP = 16
S5_STATE = 64
S5_BLOCK = 256
S5_SCAN_LANES = 512

V7X_VMEM_CAP = 56 * 1024 * 1024
NORM_ROWS = 64


def _vmem_limit(nbytes):
    return int(min(V7X_VMEM_CAP, nbytes * 5 // 4 + (4 << 20)))


def _rmsnorm(x, g):
    return x * lax.rsqrt(jnp.mean(jnp.square(x), axis=-1, keepdims=True) + EPS) * g


def _sigmoid(x):
    return 1.0 / (1.0 + jnp.exp(-x))


def _softplus(x):
    return jnp.maximum(x, 0.0) + jnp.log1p(jnp.exp(-jnp.abs(x)))


def _dot(a, b):
    return jnp.dot(a, b, preferred_element_type=F32)


def _dot_nt(a, b):
    return lax.dot_general(a, b, (((1,), (1,)), ((), ())), preferred_element_type=F32)


def _inproj_kernel(x_ref, g_ref, w_ref, o_ref, h_ref):
    tm = x_ref.shape[0]

    @pl.when(pl.program_id(1) == 0)
    def _():
        g = g_ref[...]

        def body(i, carry):
            r = pl.ds(pl.multiple_of(i * NORM_ROWS, NORM_ROWS), NORM_ROWS)
            h_ref[r, :] = _rmsnorm(x_ref[r, :], g).astype(BF16)
            return carry

        lax.fori_loop(0, tm // NORM_ROWS, body, 0)

    o_ref[...] = _dot(h_ref[...], w_ref[...])


def _inproj(x, g, w, *, tm, tn):
    n, d = x.shape
    cols = w.shape[1]
    need = 2 * tm * d * 4 + tm * d * 2 + 2 * d * tn * 2 + 3 * tm * tn * 4
    return pl.pallas_call(
        _inproj_kernel,
        out_shape=jax.ShapeDtypeStruct((n, cols), F32),
        grid=(n // tm, cols // tn),
        in_specs=[
            pl.BlockSpec((tm, d), lambda i, j: (i, 0)),
            pl.BlockSpec((1, d), lambda i, j: (0, 0)),
            pl.BlockSpec((d, tn), lambda i, j: (0, j)),
        ],
        out_specs=pl.BlockSpec((tm, tn), lambda i, j: (i, j)),
        scratch_shapes=[pltpu.VMEM((tm, d), BF16)],
        compiler_params=pltpu.CompilerParams(
            dimension_semantics=("parallel", "arbitrary"),
            vmem_limit_bytes=_vmem_limit(need)),
        name="inproj",
    )(x, g.reshape(1, d), w)


def _hgrn_kernel(q_ref, f_ref, i_ref, g_ref, lbl_ref, on_ref, s0_ref, o_ref, sout_ref,
                 st_ref, qs_ref, ks_ref, ls_ref, *, layer, chunk, sub):
    c_idx = pl.program_id(2)
    tc = q_ref.shape[0]
    n_sub = chunk // sub

    @pl.when(c_idx == 0)
    def _():
        st_ref[...] = s0_ref[0, 0].T

    logits = lbl_ref[...]
    e = jnp.exp(logits - jnp.max(logits, axis=0, keepdims=True))
    sm = e / jnp.sum(e, axis=0, keepdims=True)
    cum0 = sm[0:1]
    cuml = cum0
    for j in range(1, layer + 1):
        cuml = cuml + sm[j:j + 1]
    lb = cuml - cum0
    log_lb = jnp.log(lb)
    log_1m = jnp.log1p(-lb)

    z = f_ref[...]
    b = log_1m + (-_softplus(-z))
    delta = log_lb - b
    logf = jnp.where(jnp.isnan(delta), log_lb + b,
                     jnp.maximum(log_lb, b) + jnp.log1p(jnp.exp(-jnp.abs(delta))))
    ls_ref[...] = logf
    ks_ref[...] = jnp.exp(log_1m - _softplus(z))
    qr = q_ref[...]
    qs_ref[...] = qr * _sigmoid(qr)

    t_i = lax.broadcasted_iota(jnp.int32, (chunk, chunk), 0)
    s_i = lax.broadcasted_iota(jnp.int32, (chunk, chunk), 1)
    tri = jnp.where(t_i >= s_i, 1.0, 0.0).astype(BF16)
    row_sub = lax.broadcasted_iota(jnp.int32, (sub, HEAD_DIM), 0)
    onorm = on_ref[...]

    def chunk_body(c, carry):
        r = pl.ds(pl.multiple_of(c * chunk, chunk), chunk)
        lf = ls_ref[r, :]
        hi = lf.astype(BF16)
        r1 = lf - hi.astype(F32)
        mid = r1.astype(BF16)
        lo = (r1 - mid.astype(F32)).astype(BF16)
        cum = _dot(tri, hi) + _dot(tri, mid) + _dot(tri, lo)
        q = qs_ref[r, :]
        k = ks_ref[r, :]
        v = i_ref[r, :]
        cl = cum[chunk - 1:chunk, :]

        st = st_ref[...]
        o_inter = _dot_nt((q * jnp.exp(cum)).astype(BF16), st.astype(BF16))
        kdec = k * jnp.exp(cl - cum)
        st_ref[...] = st * jnp.exp(cl) + _dot(v.T.astype(BF16), kdec.astype(BF16))

        parts = []
        for i in range(n_sub):
            lo_r, hi_r = i * sub, (i + 1) * sub
            qi, ki, vi, ci = q[lo_r:hi_r], k[lo_r:hi_r], v[lo_r:hi_r], cum[lo_r:hi_r]
            oi = jnp.zeros((sub, HEAD_DIM), F32)
            if i > 0:
                ref = cum[lo_r - 1:lo_r, :]
                qt = qi * jnp.exp(ci - ref)
                kt = k[:lo_r] * jnp.exp(ref - cum[:lo_r])
                a = _dot_nt(qt.astype(BF16), kt.astype(BF16))
                oi = oi + _dot(a.astype(BF16), v[:lo_r].astype(BF16))
            for s in range(sub):
                d = jnp.where(row_sub >= s, ci - ci[s:s + 1], -1e30)
                p = qi * ki[s:s + 1] * jnp.exp(jnp.minimum(d, 0.0))
                oi = oi + jnp.sum(p, axis=1, keepdims=True) * vi[s:s + 1]
            parts.append(oi)
        o = o_inter + jnp.concatenate(parts, axis=0)

        o = o * lax.rsqrt(jnp.mean(jnp.square(o), axis=-1, keepdims=True) + EPS)
        o = o * onorm
        gr = g_ref[r, :]
        o_ref[r, :] = (o * (gr * _sigmoid(gr))).astype(BF16)
        return carry

    lax.fori_loop(0, tc // chunk, chunk_body, 0)

    @pl.when(c_idx == pl.num_programs(2) - 1)
    def _():
        sout_ref[0, 0] = st_ref[...].T


def _hgrn(p, lb_logits, onorm_g, s0, *, batch, seq, layer):
    n = p.shape[0]
    heads = s0.shape[1]
    width = heads * HEAD_DIM
    depth = lb_logits.shape[0]
    tc = min(512, seq)
    chunk = min(HGRN_CHUNK, seq)
    n_t = seq // tc
    row = lambda b, h, c: b * n_t + c
    need = 2 * 4 * tc * HEAD_DIM * 4 + 2 * tc * HEAD_DIM * 2 + 3 * tc * HEAD_DIM * 4 + 5 * HEAD_DIM * HEAD_DIM * 4
    kernel = functools.partial(_hgrn_kernel, layer=layer, chunk=chunk, sub=HGRN_SUB)
    return pl.pallas_call(
        kernel,
        out_shape=(jax.ShapeDtypeStruct((n, width), BF16),
                   jax.ShapeDtypeStruct(s0.shape, F32)),
        grid=(batch, heads, n_t),
        in_specs=[
            pl.BlockSpec((tc, HEAD_DIM), lambda b, h, c: (row(b, h, c), h)),
            pl.BlockSpec((tc, HEAD_DIM), lambda b, h, c: (row(b, h, c), heads + h)),
            pl.BlockSpec((tc, HEAD_DIM), lambda b, h, c: (row(b, h, c), 2 * heads + h)),
            pl.BlockSpec((tc, HEAD_DIM), lambda b, h, c: (row(b, h, c), 3 * heads + h)),
            pl.BlockSpec((depth, HEAD_DIM), lambda b, h, c: (0, h)),
            pl.BlockSpec((1, HEAD_DIM), lambda b, h, c: (0, h)),
            pl.BlockSpec((1, 1, HEAD_DIM, HEAD_DIM), lambda b, h, c: (b, h, 0, 0)),
        ],
        out_specs=(
            pl.BlockSpec((tc, HEAD_DIM), lambda b, h, c: (row(b, h, c), h)),
            pl.BlockSpec((1, 1, HEAD_DIM, HEAD_DIM), lambda b, h, c: (b, h, 0, 0)),
        ),
        scratch_shapes=[
            pltpu.VMEM((HEAD_DIM, HEAD_DIM), F32),
            pltpu.VMEM((tc, HEAD_DIM), F32),
            pltpu.VMEM((tc, HEAD_DIM), F32),
            pltpu.VMEM((tc, HEAD_DIM), F32),
        ],
        compiler_params=pltpu.CompilerParams(
            dimension_semantics=("parallel", "parallel", "arbitrary"),
            vmem_limit_bytes=_vmem_limit(need)),
        name="hgrn",
    )(p, p, p, p, lb_logits, onorm_g.reshape(1, width), s0)


def _s5_disc_lambda_kernel(lr_ref, li_ref, ls_ref, lbr_ref, lbi_ref, cr_ref, ci_ref):
    lr, li = lr_ref[...], li_ref[...]
    dt = jnp.exp(ls_ref[...])
    mag = jnp.exp(dt * lr)
    ang = dt * li
    lbr = mag * jnp.cos(ang)
    lbi = mag * jnp.sin(ang)
    nr, ni = lbr - 1.0, lbi
    den = lr * lr + li * li
    lbr_ref[...] = lbr
    lbi_ref[...] = lbi
    cr_ref[...] = (nr * lr + ni * li) / den
    ci_ref[...] = (ni * lr - nr * li) / den


def _s5_disc_b_kernel(cr_ref, ci_ref, br_ref, bi_ref, obr_ref, obi_ref):
    cr, ci, br, bi = cr_ref[...], ci_ref[...], br_ref[...], bi_ref[...]
    obr_ref[...] = cr * br - ci * bi
    obi_ref[...] = cr * bi + ci * br


def _s5_discretise(lam_re, lam_im, log_step, b_re, b_im):
    depth, groups, nst = lam_re.shape
    rows = depth * groups
    shp = jax.ShapeDtypeStruct((rows, nst), F32)
    lbr, lbi, cr, ci = pl.pallas_call(
        _s5_disc_lambda_kernel, out_shape=(shp, shp, shp, shp), name="s5_disc_lambda",
    )(lam_re.reshape(rows, nst), lam_im.reshape(rows, nst), log_step.reshape(rows, 1))
    flat = rows * nst
    cg = b_re.shape[-1]
    blk = 1024
    col = pl.BlockSpec((blk, 1), lambda i: (i, 0))
    mat = pl.BlockSpec((blk, cg), lambda i: (i, 0))
    oshp = jax.ShapeDtypeStruct((flat, cg), F32)
    bbr, bbi = pl.pallas_call(
        _s5_disc_b_kernel, out_shape=(oshp, oshp), grid=(flat // blk,),
        in_specs=[col, col, mat, mat], out_specs=(mat, mat), name="s5_disc_b",
    )(cr.reshape(flat, 1), ci.reshape(flat, 1), b_re.reshape(flat, cg), b_im.reshape(flat, cg))
    full = (depth, groups, nst, cg)
    return (lbr.reshape(depth, groups * nst), lbi.reshape(depth, groups * nst),
            bbr.reshape(full), bbi.reshape(full))


def _block_diag_in(bb):
    groups, nst, cg = bb.shape
    gpb = S5_BLOCK // cg
    nb = groups // gpb
    t = bb.reshape(nb, gpb, nst, cg).transpose(0, 1, 3, 2)
    eye = jnp.eye(gpb, dtype=bb.dtype)
    w = t[:, :, :, None, :] * eye[None, :, None, :, None]
    return w.reshape(nb, S5_BLOCK, gpb * nst).astype(BF16)


def _block_diag_out(cc):
    groups, cg, nst = cc.shape
    gpb = S5_BLOCK // cg
    nb = groups // gpb
    t = cc.reshape(nb, gpb, cg, nst).transpose(0, 1, 3, 2)
    eye = jnp.eye(gpb, dtype=cc.dtype)
    w = t[:, :, :, None, :] * eye[None, :, None, :, None]
    return w.reshape(nb, gpb * nst, S5_BLOCK).astype(BF16)


def _s5_kernel(u_ref, wbr_ref, wbi_ref, wcr_ref, wci_ref, lam_ref, d_ref, wg_ref, bg_ref,
               x0r_ref, x0i_ref, o_ref, xr_ref, xi_ref, xs_ref, y_ref, st_ref):
    c_idx = pl.program_id(1)
    tc = u_ref.shape[0]
    nb, _, bcols = wbr_ref.shape
    ns = nb * bcols

    @pl.when(c_idx == 0)
    def _():
        st_ref[0:1, :] = x0r_ref[0]
        st_ref[1:2, :] = x0i_ref[0]

    u = u_ref[...]
    ub = u.astype(BF16)
    for j in range(nb):
        uj = ub[:, j * S5_BLOCK:(j + 1) * S5_BLOCK]
        xs_ref[:, j * bcols:(j + 1) * bcols] = _dot(uj, wbr_ref[j])
        xs_ref[:, ns + j * bcols:ns + (j + 1) * bcols] = _dot(uj, wbi_ref[j])

    for cb in range(ns // S5_SCAN_LANES):
        lo = cb * S5_SCAN_LANES
        re_l = slice(lo, lo + S5_SCAN_LANES)
        im_l = slice(ns + lo, ns + lo + S5_SCAN_LANES)
        lr = lam_ref[0:1, re_l]
        li = lam_ref[1:2, re_l]

        def step(t, carry, lr=lr, li=li, re_l=re_l, im_l=im_l):
            xr, xi = carry
            row = pl.ds(t, 1)
            nr = lr * xr - li * xi + xs_ref[row, re_l]
            ni = lr * xi + li * xr + xs_ref[row, im_l]
            xs_ref[row, re_l] = nr
            xs_ref[row, im_l] = ni
            return nr, ni

        xr, xi = lax.fori_loop(0, tc, step, (st_ref[0:1, re_l], st_ref[1:2, re_l]), unroll=8)
        st_ref[0:1, re_l] = xr
        st_ref[1:2, re_l] = xi

    for j in range(nb):
        xr_j = xs_ref[:, j * bcols:(j + 1) * bcols].astype(BF16)
        xi_j = xs_ref[:, ns + j * bcols:ns + (j + 1) * bcols].astype(BF16)
        y_ref[:, j * S5_BLOCK:(j + 1) * S5_BLOCK] = _dot(xr_j, wcr_ref[j]) - _dot(xi_j, wci_ref[j])

    y = y_ref[...] + d_ref[...] * u
    hh = jax.nn.gelu(y)
    gate = _sigmoid(_dot(hh.astype(BF16), wg_ref[...]) + bg_ref[...])
    o_ref[...] = (hh * gate).astype(BF16)

    @pl.when(c_idx == pl.num_programs(1) - 1)
    def _():
        xr_ref[0] = st_ref[0:1, :]
        xi_ref[0] = st_ref[1:2, :]


def _s5(p, wbr, wbi, wcr, wci, lam, d_skip, w_glu, b_glu, x0r, x0i, *, batch, seq, col_block):
    n = p.shape[0]
    width = w_glu.shape[0]
    nb, _, bcols = wbr.shape
    ns = nb * bcols
    tc = min(256, seq)
    n_t = seq // tc
    const3 = lambda b, c: (0, 0, 0)
    const2 = lambda b, c: (0, 0)
    need = (2 * tc * width * 4 + 2 * 4 * nb * S5_BLOCK * bcols * 2 + 2 * width * width * 2
            + 2 * tc * width * 2 + tc * 2 * ns * 4 + tc * width * 4 + 4 * tc * width * 4 + 16 * ns * 4)
    st_shape = jax.ShapeDtypeStruct((batch, 1, ns), F32)
    return pl.pallas_call(
        _s5_kernel,
        out_shape=(jax.ShapeDtypeStruct((n, width), BF16), st_shape, st_shape),
        grid=(batch, n_t),
        in_specs=[
            pl.BlockSpec((tc, width), lambda b, c: (b * n_t + c, col_block)),
            pl.BlockSpec(wbr.shape, const3),
            pl.BlockSpec(wbi.shape, const3),
            pl.BlockSpec(wcr.shape, const3),
            pl.BlockSpec(wci.shape, const3),
            pl.BlockSpec((2, ns), const2),
            pl.BlockSpec((1, width), const2),
            pl.BlockSpec((width, width), const2),
            pl.BlockSpec((1, width), const2),
            pl.BlockSpec((1, 1, ns), lambda b, c: (b, 0, 0)),
            pl.BlockSpec((1, 1, ns), lambda b, c: (b, 0, 0)),
        ],
        out_specs=(
            pl.BlockSpec((tc, width), lambda b, c: (b * n_t + c, 0)),
            pl.BlockSpec((1, 1, ns), lambda b, c: (b, 0, 0)),
            pl.BlockSpec((1, 1, ns), lambda b, c: (b, 0, 0)),
        ),
        scratch_shapes=[
            pltpu.VMEM((tc, 2 * ns), F32),
            pltpu.VMEM((tc, width), F32),
            pltpu.VMEM((2, ns), F32),
        ],
        compiler_params=pltpu.CompilerParams(
            dimension_semantics=("parallel", "arbitrary"),
            vmem_limit_bytes=_vmem_limit(need)),
        name="s5",
    )(p, wbr, wbi, wcr, wci, lam, d_skip.reshape(1, width), w_glu, b_glu.reshape(1, width), x0r, x0i)


def _outproj_kernel(x_ref, oh_ref, os_ref, wh_ref, ws_ref, o_ref):
    o_ref[...] = x_ref[...] + _dot(oh_ref[...], wh_ref[...]) + _dot(os_ref[...], ws_ref[...])


def _outproj(x, o_h, o_s, w_h, w_s, *, tm, tn):
    n, d = x.shape
    kh, ks = o_h.shape[1], o_s.shape[1]
    need = 2 * (2 * tm * tn * 4 + tm * (kh + ks) * 2 + (kh + ks) * tn * 2) + 2 * tm * tn * 4
    return pl.pallas_call(
        _outproj_kernel,
        out_shape=jax.ShapeDtypeStruct((n, d), F32),
        grid=(n // tm, d // tn),
        in_specs=[
            pl.BlockSpec((tm, tn), lambda i, j: (i, j)),
            pl.BlockSpec((tm, kh), lambda i, j: (i, 0)),
            pl.BlockSpec((tm, ks), lambda i, j: (i, 0)),
            pl.BlockSpec((kh, tn), lambda i, j: (0, j)),
            pl.BlockSpec((ks, tn), lambda i, j: (0, j)),
        ],
        out_specs=pl.BlockSpec((tm, tn), lambda i, j: (i, j)),
        compiler_params=pltpu.CompilerParams(
            dimension_semantics=("parallel", "parallel"),
            vmem_limit_bytes=_vmem_limit(need)),
        name="outproj",
    )(x, o_h, o_s, w_h, w_s)


def _ffn_kernel(x_ref, g_ref, w1_ref, w2_ref, gf_ref, o_ref, h_ref, *, final_norm):
    f = pl.program_id(1)
    tm = x_ref.shape[0]

    @pl.when(f == 0)
    def _():
        g = g_ref[...]

        def body(i, carry):
            r = pl.ds(pl.multiple_of(i * NORM_ROWS, NORM_ROWS), NORM_ROWS)
            h_ref[r, :] = _rmsnorm(x_ref[r, :], g).astype(BF16)
            return carry

        lax.fori_loop(0, tm // NORM_ROWS, body, 0)

    a = _dot(h_ref[...], w1_ref[...])
    a = jnp.square(jnp.maximum(a, 0.0)).astype(BF16)
    contrib = _dot(a, w2_ref[...])

    @pl.when(f == 0)
    def _():
        o_ref[...] = contrib

    @pl.when(f > 0)
    def _():
        o_ref[...] += contrib

    @pl.when(f == pl.num_programs(1) - 1)
    def _():
        gf = gf_ref[...]

        def body(i, carry):
            r = pl.ds(pl.multiple_of(i * NORM_ROWS, NORM_ROWS), NORM_ROWS)
            y = x_ref[r, :] + o_ref[r, :]
            if final_norm:
                y = _rmsnorm(y, gf)
            o_ref[r, :] = y
            return carry

        lax.fori_loop(0, tm // NORM_ROWS, body, 0)


def _ffn(x, g, w1, w2, gf, *, tm, tf, final_norm):
    n, d = x.shape
    ff = w1.shape[1]
    need = tm * d * 4 + tm * d * 2 + 2 * tm * d * 4 + 4 * d * tf * 2 + tm * tf * 6 + tm * d * 4
    kernel = functools.partial(_ffn_kernel, final_norm=final_norm)
    return pl.pallas_call(
        kernel,
        out_shape=jax.ShapeDtypeStruct((n, d), F32),
        grid=(n // tm, ff // tf),
        in_specs=[
            pl.BlockSpec((tm, d), lambda i, f: (i, 0), pipeline_mode=pl.Buffered(1)),
            pl.BlockSpec((1, d), lambda i, f: (0, 0)),
            pl.BlockSpec((d, tf), lambda i, f: (0, f)),
            pl.BlockSpec((tf, d), lambda i, f: (f, 0)),
            pl.BlockSpec((1, d), lambda i, f: (0, 0)),
        ],
        out_specs=pl.BlockSpec((tm, d), lambda i, f: (i, 0)),
        scratch_shapes=[pltpu.VMEM((tm, d), BF16)],
        compiler_params=pltpu.CompilerParams(
            dimension_semantics=("parallel", "arbitrary"),
            vmem_limit_bytes=_vmem_limit(need)),
        name="ffn",
    )(x, g.reshape(1, d), w1, w2, gf.reshape(1, d))


def _trunk(x, st_h, st_r, st_i, wts):
    batch, seq, d = x.shape
    n = batch * seq
    depth = wts["w_in"].shape[0]
    heads = st_h.shape[2]
    hgrn_width = heads * HEAD_DIM
    groups, nst = st_r.shape[2], st_r.shape[3]
    tm = min(1024, n)
    xf = x.reshape(n, d)
    new_h, new_r, new_i = [], [], []
    for l in range(depth):
        p = _inproj(xf, wts["norm1_g"][l], wts["w_in"][l], tm=tm, tn=1024)
        o_h, s_new = _hgrn(p, wts["lb_logits"], wts["onorm_g"][l], st_h[l],
                           batch=batch, seq=seq, layer=l)
        o_s, xr, xi = _s5(p, wts["wbr"][l], wts["wbi"][l], wts["wcr"][l], wts["wci"][l], wts["lam"][l],
                          wts["s5_D"][l], wts["w_glu"][l], wts["b_glu"][l],
                          st_r[l].reshape(batch, 1, groups * nst), st_i[l].reshape(batch, 1, groups * nst),
                          batch=batch, seq=seq, col_block=4 * hgrn_width // wts["w_glu"].shape[1])
        x1 = _outproj(xf, o_h, o_s, wts["w_out"][l, :hgrn_width], wts["w_out"][l, hgrn_width:], tm=tm, tn=1024)
        xf = _ffn(x1, wts["norm2_g"][l], wts["w_ff1"][l], wts["w_ff2"][l], wts["final_g"],
                  tm=tm, tf=512, final_norm=(l == depth - 1))
        new_h.append(s_new)
        new_r.append(xr.reshape(batch, groups, nst))
        new_i.append(xi.reshape(batch, groups, nst))
    return xf.reshape(batch, seq, d), jnp.stack(new_h), jnp.stack(new_r), jnp.stack(new_i)


def kernel(x_prompt, x_sample, state_hgrn, state_s5_re, state_s5_im, norm1_g, w_in, hgrn_lb_logits,
           hgrn_onorm_g, s5_lambda_re, s5_lambda_im, s5_log_step, s5_B_re, s5_B_im, s5_C_re, s5_C_im,
           s5_D, s5_w_glu, s5_b_glu, w_out, norm2_g, w_ff1, w_ff2, final_norm_g):
    depth = w_in.shape[0]
    lbr, lbi, bbr, bbi = _s5_discretise(s5_lambda_re, s5_lambda_im, s5_log_step, s5_B_re, s5_B_im)
    wts = {
        "norm1_g": norm1_g, "norm2_g": norm2_g, "final_g": final_norm_g,
        "lb_logits": hgrn_lb_logits, "onorm_g": hgrn_onorm_g,
        "w_in": w_in.astype(BF16), "w_out": w_out.astype(BF16),
        "w_ff1": w_ff1.astype(BF16), "w_ff2": w_ff2.astype(BF16),
        "w_glu": s5_w_glu.astype(BF16), "b_glu": s5_b_glu, "s5_D": s5_D,
        "lam": jnp.stack([lbr, lbi], axis=1),
        "wbr": jnp.stack([_block_diag_in(bbr[l]) for l in range(depth)]),
        "wbi": jnp.stack([_block_diag_in(bbi[l]) for l in range(depth)]),
        "wcr": jnp.stack([_block_diag_out(s5_C_re[l]) for l in range(depth)]),
        "wci": jnp.stack([_block_diag_out(s5_C_im[l]) for l in range(depth)]),
    }
    bp = x_prompt.shape[0]
    zh = jnp.zeros((depth, bp) + state_hgrn.shape[2:], F32)
    zs = jnp.zeros((depth, bp) + state_s5_re.shape[2:], F32)
    y_p, hp, rp, ip = _trunk(x_prompt, zh, zs, zs, wts)
    y_s, hs, rs, is_ = _trunk(x_sample, state_hgrn, state_s5_re, state_s5_im, wts)
    return (y_p, y_s, hp, rp, ip, hs, rs, is_)
```

```python
import functools

import jax
import jax.numpy as jnp
from jax import lax
from jax.experimental import pallas as pl
from jax.experimental.pallas import tpu as pltpu

F32 = jnp.float32
BF16 = jnp.bfloat16
EPS = 1e-6

HEAD_DIM = 128
HGRN_CHUNK = 64
HGRN_SUB = 16
HGRN_HEADS_PER_STEP = 4
SUBLANES = 8
S5_GROUP = 16
S5_STATE = 64
S5_BLOCK = 256
S5_SCAN_LANES = 512

V7X_VMEM_CAP = 56 * 1024 * 1024
NORM_ROWS = 64


def _vmem_limit(nbytes):
    return int(min(V7X_VMEM_CAP, nbytes * 5 // 4 + (4 << 20)))


def _rmsnorm(x, g):
    return x * lax.rsqrt(jnp.mean(jnp.square(x), axis=-1, keepdims=True) + EPS) * g


def _sigmoid(x):
    return 1.0 / (1.0 + jnp.exp(-x))


def _softplus(x):
    return jnp.maximum(x, 0.0) + jnp.log1p(jnp.exp(-jnp.abs(x)))


def _dot(a, b):
    return jnp.dot(a, b, preferred_element_type=F32)


def _dot_nt(a, b):
    return lax.dot_general(a, b, (((1,), (1,)), ((), ())), preferred_element_type=F32)


def _inproj_kernel(x_ref, g_ref, w_ref, o_ref, h_ref):
    tm = x_ref.shape[0]

    @pl.when(pl.program_id(1) == 0)
    def _():
        g = g_ref[...]

        def body(i, carry):
            r = pl.ds(pl.multiple_of(i * NORM_ROWS, NORM_ROWS), NORM_ROWS)
            h_ref[r, :] = _rmsnorm(x_ref[r, :], g).astype(BF16)
            return carry

        lax.fori_loop(0, tm // NORM_ROWS, body, 0)

    o_ref[...] = _dot(h_ref[...], w_ref[...])


def _inproj(x, g, w, *, layer, tm, tn):
    n, d = x.shape
    cols = w.shape[2]
    need = 2 * tm * d * 4 + tm * d * 2 + 2 * d * tn * 2 + 3 * tm * tn * 4
    return pl.pallas_call(
        _inproj_kernel,
        out_shape=jax.ShapeDtypeStruct((n, cols), F32),
        grid=(n // tm, cols // tn),
        in_specs=[
            pl.BlockSpec((tm, d), lambda i, j: (i, 0)),
            pl.BlockSpec((None, 1, d), lambda i, j: (layer, 0, 0)),
            pl.BlockSpec((None, d, tn), lambda i, j: (layer, 0, j)),
        ],
        out_specs=pl.BlockSpec((tm, tn), lambda i, j: (i, j)),
        scratch_shapes=[pltpu.VMEM((tm, d), BF16)],
        compiler_params=pltpu.CompilerParams(
            dimension_semantics=("parallel", "arbitrary"),
            vmem_limit_bytes=_vmem_limit(need)),
        name="inproj",
    )(x, g, w)


def _hgrn_decayed_queries(q_lo, q_hi, f_row, row_is):
    slabs = []
    qd_lo = qd_hi = None
    for s in range(2 * SUBLANES - 1, -1, -1):
        if s == 2 * SUBLANES - 1:
            qd_hi = jnp.where(row_is[SUBLANES - 1], q_hi, 0.0)
        elif s >= SUBLANES:
            qd_hi = jnp.where(row_is[s - SUBLANES], q_hi, qd_hi * f_row(s + 1))
        else:
            fn = f_row(s + 1)
            qd_hi = qd_hi * fn
            qd_lo = jnp.where(row_is[s], q_lo, 0.0 if s == SUBLANES - 1 else qd_lo * fn)
        slabs.append((s, 1, qd_hi))
        if s < SUBLANES:
            slabs.append((s, 0, qd_lo))
    return slabs


def _hgrn_kernel(q_ref, f_ref, i_ref, g_ref, lbl_ref, on_ref, s0_ref, o_ref, sout_ref,
                 st_ref, qs_ref, ks_ref, cs_ref, fs_ref, fc_ref, op_ref, *, layer, chunk, heads_blk):
    c_idx = pl.program_id(2)
    tc = q_ref.shape[0]
    n_sub = chunk // HGRN_SUB

    @pl.when(c_idx == 0)
    def _():
        for h in range(heads_blk):
            st_ref[h] = s0_ref[0, h].T

    logits = lbl_ref[...]
    e = jnp.exp(logits - jnp.max(logits, axis=0, keepdims=True))
    sm = e / jnp.sum(e, axis=0, keepdims=True)
    cum0 = sm[0:1]
    cuml = cum0
    for j in range(1, layer + 1):
        cuml = cuml + sm[j:j + 1]
    lb = cuml - cum0
    log_lb = jnp.log(lb)
    log_1m = jnp.log1p(-lb)

    z = f_ref[...]
    sp_neg = _softplus(-z)
    b = log_1m - sp_neg
    delta = log_lb - b
    logf = jnp.where(jnp.isnan(delta), log_lb + b,
                     jnp.maximum(log_lb, b) + jnp.log1p(jnp.exp(-jnp.abs(delta))))
    fs_ref[...] = jnp.exp(logf)
    ks_ref[...] = jnp.exp(log_1m - (sp_neg + z))
    qr = q_ref[...]
    qs_ref[...] = qr * _sigmoid(qr)

    t_i = lax.broadcasted_iota(jnp.int32, (chunk, chunk), 0)
    s_i = lax.broadcasted_iota(jnp.int32, (chunk, chunk), 1)
    tri = jnp.where(t_i >= s_i, 1.0, 0.0).astype(BF16)
    for j in range(tc // chunk):
        lf = logf[j * chunk:(j + 1) * chunk]
        hi = lf.astype(BF16)
        r1 = lf - hi.astype(F32)
        mid = r1.astype(BF16)
        lo = (r1 - mid.astype(F32)).astype(BF16)
        cs_ref[j * chunk:(j + 1) * chunk, :] = _dot(tri, hi) + _dot(tri, mid) + _dot(tri, lo)

    row8 = lax.broadcasted_iota(jnp.int32, (SUBLANES, HEAD_DIM), 0)
    row_is = [row8 == j for j in range(SUBLANES)]
    lane8 = lax.broadcasted_iota(jnp.int32, (SUBLANES, chunk), 1)
    lane_is = [lane8 == j for j in range(chunk)]
    onorm = on_ref[...]
    blk = lambda a, j: a[j * HGRN_SUB:(j + 1) * HGRN_SUB]

    head_cols = [slice(h * HEAD_DIM, (h + 1) * HEAD_DIM) for h in range(heads_blk)]

    def finish(rows):
        for cols in head_cols:
            o = op_ref[:, cols]
            o = o * lax.rsqrt(jnp.mean(jnp.square(o), axis=-1, keepdims=True) + EPS)
            o = o * onorm[:, cols]
            gr = g_ref[rows, cols]
            o_ref[rows, cols] = (o * (gr * _sigmoid(gr))).astype(BF16)

    op_ref[...] = jnp.zeros(op_ref.shape, F32)

    def chunk_body(c, carry):
        base = pl.multiple_of(c * chunk, chunk)
        r = pl.ds(base, chunk)
        finish(pl.ds(pl.multiple_of(jnp.maximum(c - 1, 0) * chunk, chunk), chunk))
        fc_ref[...] = fs_ref[r, :]
        cum_all = cs_ref[r, :]

        zero_blk = jnp.zeros((HGRN_SUB, HEAD_DIM), F32)
        qs, vbs, kbs, o_inter, a_off = [], [], [], [], []
        for cols in head_cols:
            h = len(qs)
            cum = cum_all[:, cols]
            q = qs_ref[r, cols]
            k = ks_ref[r, cols]
            v = i_ref[r, cols]
            ends = [cum[(j + 1) * HGRN_SUB - 1:(j + 1) * HGRN_SUB, :] for j in range(n_sub)]
            cl = ends[-1]
            ke = [blk(k, j) * jnp.exp(ends[j] - blk(cum, j)) for j in range(n_sub)]
            qe = [blk(q, j) * jnp.exp(blk(cum, j) - ends[j - 1] if j else blk(cum, j)) for j in range(n_sub)]

            st = st_ref[h]
            q_in = jnp.concatenate([qe[j] * jnp.exp(ends[j - 1]) if j else qe[j] for j in range(n_sub)], axis=0)
            o_inter.append(_dot_nt(q_in.astype(BF16), st.astype(BF16)))
            kdec = jnp.concatenate(
                [ke[j] * jnp.exp(cl - ends[j]) if j < n_sub - 1 else ke[j] for j in range(n_sub)], axis=0)
            st_ref[h] = st * jnp.exp(cl) + _dot(v.T.astype(BF16), kdec.astype(BF16))

            offs = [None]
            for i in range(1, n_sub):
                kt = jnp.concatenate(
                    [ke[j] * jnp.exp(ends[i - 1] - ends[j]) if j < i - 1 else ke[j] for j in range(i)]
                    + [zero_blk] * (n_sub - i), axis=0)
                offs.append(_dot_nt(qe[i].astype(BF16), kt.astype(BF16)))
            a_off.append(offs)
            qs.append(q)
            vbs.append(v.astype(BF16))
            kbs.append(k.astype(BF16))

        res = []
        for h, cols in enumerate(head_cols):
            per_head = []
            for i in range(n_sub):
                lo_r = i * HGRN_SUB
                f_row = lambda s, lo_r=lo_r, cols=cols: fc_ref[lo_r + s:lo_r + s + 1, cols]
                q_blk = blk(qs[h], i)
                slabs = _hgrn_decayed_queries(q_blk[:SUBLANES], q_blk[SUBLANES:], f_row, row_is)
                stack = jnp.concatenate([slab for _, _, slab in slabs], axis=0)
                per_head.append(([(s, half) for s, half, _ in slabs],
                                 _dot_nt(stack.astype(BF16), kbs[h])))
            res.append(per_head)

        o_intra = []
        for h in range(heads_blk):
            rows = []
            for i in range(n_sub):
                order, prod = res[h][i]
                halves = [jnp.zeros((SUBLANES, chunk), F32), jnp.zeros((SUBLANES, chunk), F32)]
                for idx, (s, half) in enumerate(order):
                    piece = prod[idx * SUBLANES:(idx + 1) * SUBLANES]
                    halves[half] = jnp.where(lane_is[i * HGRN_SUB + s], piece, halves[half])
                a = jnp.concatenate(halves, axis=0)
                rows.append(a + a_off[h][i] if i else a)
            attn = jnp.concatenate(rows, axis=0)
            o_intra.append(_dot(attn.astype(BF16), vbs[h]))

        for h, cols in enumerate(head_cols):
            op_ref[:, cols] = o_inter[h] + o_intra[h]
        return carry

    lax.fori_loop(0, tc // chunk, chunk_body, 0)
    finish(pl.ds(tc - chunk, chunk))

    @pl.when(c_idx == pl.num_programs(2) - 1)
    def _():
        for h in range(heads_blk):
            sout_ref[0, h] = st_ref[h].T


def _hgrn(p, lb_logits, onorm_g, s0, *, batch, seq, layer):
    n = p.shape[0]
    heads = s0.shape[2]
    hb = HGRN_HEADS_PER_STEP
    wblk = hb * HEAD_DIM
    n_hb = heads // hb
    width = heads * HEAD_DIM
    depth = lb_logits.shape[0]
    tc = min(512, seq)
    chunk = min(HGRN_CHUNK, seq)
    n_t = seq // tc
    row = lambda b, h, c: b * n_t + c
    need = 2 * 4 * tc * wblk * 4 + 2 * tc * wblk * 2 + 4 * tc * wblk * 4 + 5 * hb * HEAD_DIM * HEAD_DIM * 4
    kernel = functools.partial(_hgrn_kernel, layer=layer, chunk=chunk, heads_blk=hb)
    st_spec = pl.BlockSpec((None, 1, hb, HEAD_DIM, HEAD_DIM), lambda b, h, c: (layer, b, h, 0, 0))
    return pl.pallas_call(
        kernel,
        out_shape=(jax.ShapeDtypeStruct((n, width), BF16),
                   jax.ShapeDtypeStruct((batch, heads, HEAD_DIM, HEAD_DIM), F32)),
        grid=(batch, n_hb, n_t),
        in_specs=[
            pl.BlockSpec((tc, wblk), lambda b, h, c: (row(b, h, c), h)),
            pl.BlockSpec((tc, wblk), lambda b, h, c: (row(b, h, c), n_hb + h)),
            pl.BlockSpec((tc, wblk), lambda b, h, c: (row(b, h, c), 2 * n_hb + h)),
            pl.BlockSpec((tc, wblk), lambda b, h, c: (row(b, h, c), 3 * n_hb + h)),
            pl.BlockSpec((depth, wblk), lambda b, h, c: (0, h)),
            pl.BlockSpec((None, 1, wblk), lambda b, h, c: (layer, 0, h)),
            st_spec,
        ],
        out_specs=(
            pl.BlockSpec((tc, wblk), lambda b, h, c: (row(b, h, c), h)),
            pl.BlockSpec((1, hb, HEAD_DIM, HEAD_DIM), lambda b, h, c: (b, h, 0, 0)),
        ),
        scratch_shapes=[
            pltpu.VMEM((hb, HEAD_DIM, HEAD_DIM), F32),
            pltpu.VMEM((tc, wblk), F32),
            pltpu.VMEM((tc, wblk), F32),
            pltpu.VMEM((tc, wblk), F32),
            pltpu.VMEM((tc, wblk), F32),
            pltpu.VMEM((chunk, wblk), F32),
            pltpu.VMEM((chunk, wblk), F32),
        ],
        compiler_params=pltpu.CompilerParams(
            dimension_semantics=("parallel", "parallel", "arbitrary"),
            vmem_limit_bytes=_vmem_limit(need)),
        name="hgrn",
    )(p, p, p, p, lb_logits, onorm_g, s0)


def _s5_disc_lambda_kernel(lr_ref, li_ref, ls_ref, lbr_ref, lbi_ref, cr_ref, ci_ref):
    lr, li = lr_ref[...], li_ref[...]
    dt = jnp.exp(ls_ref[...])
    mag = jnp.exp(dt * lr)
    ang = dt * li
    lbr = mag * jnp.cos(ang)
    lbi = mag * jnp.sin(ang)
    nr, ni = lbr - 1.0, lbi
    den = lr * lr + li * li
    lbr_ref[...] = lbr
    lbi_ref[...] = lbi
    cr_ref[...] = (nr * lr + ni * li) / den
    ci_ref[...] = (ni * lr - nr * li) / den


def _s5_disc_b_kernel(cr_ref, ci_ref, br_ref, bi_ref, obr_ref, obi_ref):
    cr, ci, br, bi = cr_ref[...], ci_ref[...], br_ref[...], bi_ref[...]
    obr_ref[...] = cr * br - ci * bi
    obi_ref[...] = cr * bi + ci * br


def _s5_discretise(lam_re, lam_im, log_step, b_re, b_im):
    depth, groups, nst = lam_re.shape
    rows = depth * groups
    shp = jax.ShapeDtypeStruct((rows, nst), F32)
    lbr, lbi, cr, ci = pl.pallas_call(
        _s5_disc_lambda_kernel, out_shape=(shp, shp, shp, shp), name="s5_disc_lambda",
    )(lam_re.reshape(rows, nst), lam_im.reshape(rows, nst), log_step.reshape(rows, 1))
    flat = rows * nst
    cg = b_re.shape[-1]
    blk = 1024
    col = pl.BlockSpec((blk, 1), lambda i: (i, 0))
    mat = pl.BlockSpec((blk, cg), lambda i: (i, 0))
    oshp = jax.ShapeDtypeStruct((flat, cg), F32)
    bbr, bbi = pl.pallas_call(
        _s5_disc_b_kernel, out_shape=(oshp, oshp), grid=(flat // blk,),
        in_specs=[col, col, mat, mat], out_specs=(mat, mat), name="s5_disc_b",
    )(cr.reshape(flat, 1), ci.reshape(flat, 1), b_re.reshape(flat, cg), b_im.reshape(flat, cg))
    full = (depth, groups, nst, cg)
    return (lbr.reshape(depth, groups * nst), lbi.reshape(depth, groups * nst),
            bbr.reshape(full), bbi.reshape(full))


def _block_diag_in(bb):
    depth, groups, nst, cg = bb.shape
    gpb = S5_BLOCK // cg
    nb = groups // gpb
    t = bb.reshape(depth, nb, gpb, nst, cg).transpose(0, 1, 2, 4, 3).astype(BF16)
    eye = jnp.eye(gpb, dtype=BF16)
    w = t[:, :, :, :, None, :] * eye[None, None, :, None, :, None]
    return w.reshape(depth, nb, S5_BLOCK, gpb * nst)


def _block_diag_out(cc):
    depth, groups, cg, nst = cc.shape
    gpb = S5_BLOCK // cg
    nb = groups // gpb
    t = cc.reshape(depth, nb, gpb, cg, nst).transpose(0, 1, 2, 4, 3).astype(BF16)
    eye = jnp.eye(gpb, dtype=BF16)
    w = t[:, :, :, :, None, :] * eye[None, None, :, None, :, None]
    return w.reshape(depth, nb, gpb * nst, S5_BLOCK)


def _s5_kernel(u_ref, wbr_ref, wbi_ref, wcr_ref, wci_ref, lam_ref, d_ref, wg_ref, bg_ref,
               x0r_ref, x0i_ref, o_ref, xr_ref, xi_ref, xs_ref, y_ref, st_ref):
    c_idx = pl.program_id(1)
    tc = u_ref.shape[0]
    nb, _, bcols = wbr_ref.shape
    ns = nb * bcols

    @pl.when(c_idx == 0)
    def _():
        st_ref[0:1, :] = x0r_ref[0]
        st_ref[1:2, :] = x0i_ref[0]

    u = u_ref[...]
    ub = u.astype(BF16)
    for j in range(nb):
        uj = ub[:, j * S5_BLOCK:(j + 1) * S5_BLOCK]
        xs_ref[:, j * bcols:(j + 1) * bcols] = _dot(uj, wbr_ref[j])
        xs_ref[:, ns + j * bcols:ns + (j + 1) * bcols] = _dot(uj, wbi_ref[j])

    for cb in range(ns // S5_SCAN_LANES):
        lo = cb * S5_SCAN_LANES
        re_l = slice(lo, lo + S5_SCAN_LANES)
        im_l = slice(ns + lo, ns + lo + S5_SCAN_LANES)
        lr = lam_ref[0:1, re_l]
        li = lam_ref[1:2, re_l]

        def step(t, carry, lr=lr, li=li, re_l=re_l, im_l=im_l):
            xr, xi = carry
            row = pl.ds(t, 1)
            nr = lr * xr - li * xi + xs_ref[row, re_l]
            ni = lr * xi + li * xr + xs_ref[row, im_l]
            xs_ref[row, re_l] = nr
            xs_ref[row, im_l] = ni
            return nr, ni

        xr, xi = lax.fori_loop(0, tc, step, (st_ref[0:1, re_l], st_ref[1:2, re_l]), unroll=8)
        st_ref[0:1, re_l] = xr
        st_ref[1:2, re_l] = xi

    for j in range(nb):
        xr_j = xs_ref[:, j * bcols:(j + 1) * bcols].astype(BF16)
        xi_j = xs_ref[:, ns + j * bcols:ns + (j + 1) * bcols].astype(BF16)
        y_ref[:, j * S5_BLOCK:(j + 1) * S5_BLOCK] = _dot(xr_j, wcr_ref[j]) - _dot(xi_j, wci_ref[j])

    y = y_ref[...] + d_ref[...] * u
    hh = jax.nn.gelu(y)
    gate = _sigmoid(_dot(hh.astype(BF16), wg_ref[...]) + bg_ref[...])
    o_ref[...] = (hh * gate).astype(BF16)

    @pl.when(c_idx == pl.num_programs(1) - 1)
    def _():
        xr_ref[0] = st_ref[0:1, :]
        xi_ref[0] = st_ref[1:2, :]


def _s5(p, wbr, wbi, wcr, wci, lam, d_skip, w_glu, b_glu, x0r, x0i, *, batch, seq, layer, col_block):
    n = p.shape[0]
    width = w_glu.shape[1]
    _, nb, _, bcols = wbr.shape
    ns = nb * bcols
    tc = min(256, seq)
    n_t = seq // tc
    lay4 = lambda b, c: (layer, 0, 0, 0)
    lay3 = lambda b, c: (layer, 0, 0)
    st_in = pl.BlockSpec((None, 1, 1, ns), lambda b, c: (layer, b, 0, 0))
    need = (2 * tc * width * 4 + 2 * 4 * nb * S5_BLOCK * bcols * 2 + 2 * width * width * 2
            + 2 * tc * width * 2 + tc * 2 * ns * 4 + tc * width * 4 + 4 * tc * width * 4 + 16 * ns * 4)
    st_shape = jax.ShapeDtypeStruct((batch, 1, ns), F32)
    return pl.pallas_call(
        _s5_kernel,
        out_shape=(jax.ShapeDtypeStruct((n, width), BF16), st_shape, st_shape),
        grid=(batch, n_t),
        in_specs=[
            pl.BlockSpec((tc, width), lambda b, c: (b * n_t + c, col_block)),
            pl.BlockSpec((None,) + wbr.shape[1:], lay4),
            pl.BlockSpec((None,) + wbi.shape[1:], lay4),
            pl.BlockSpec((None,) + wcr.shape[1:], lay4),
            pl.BlockSpec((None,) + wci.shape[1:], lay4),
            pl.BlockSpec((None, 2, ns), lay3),
            pl.BlockSpec((None, 1, width), lay3),
            pl.BlockSpec((None, width, width), lay3),
            pl.BlockSpec((None, 1, width), lay3),
            st_in,
            st_in,
        ],
        out_specs=(
            pl.BlockSpec((tc, width), lambda b, c: (b * n_t + c, 0)),
            pl.BlockSpec((1, 1, ns), lambda b, c: (b, 0, 0)),
            pl.BlockSpec((1, 1, ns), lambda b, c: (b, 0, 0)),
        ),
        scratch_shapes=[
            pltpu.VMEM((tc, 2 * ns), F32),
            pltpu.VMEM((tc, width), F32),
            pltpu.VMEM((2, ns), F32),
        ],
        compiler_params=pltpu.CompilerParams(
            dimension_semantics=("parallel", "arbitrary"),
            vmem_limit_bytes=_vmem_limit(need)),
        name="s5",
    )(p, wbr, wbi, wcr, wci, lam, d_skip, w_glu, b_glu, x0r, x0i)


def _outproj_kernel(x_ref, oh_ref, os_ref, wh_ref, ws_ref, o_ref):
    o_ref[...] = x_ref[...] + _dot(oh_ref[...], wh_ref[...]) + _dot(os_ref[...], ws_ref[...])


def _outproj(x, o_h, o_s, w_out, *, layer, tm, tn):
    n, d = x.shape
    kh, ks = o_h.shape[1], o_s.shape[1]
    assert kh == ks and w_out.shape[1] == kh + ks
    need = 2 * (2 * tm * tn * 4 + tm * (kh + ks) * 2 + (kh + ks) * tn * 2) + 2 * tm * tn * 4
    return pl.pallas_call(
        _outproj_kernel,
        out_shape=jax.ShapeDtypeStruct((n, d), F32),
        grid=(n // tm, d // tn),
        in_specs=[
            pl.BlockSpec((tm, tn), lambda i, j: (i, j)),
            pl.BlockSpec((tm, kh), lambda i, j: (i, 0)),
            pl.BlockSpec((tm, ks), lambda i, j: (i, 0)),
            pl.BlockSpec((None, kh, tn), lambda i, j: (layer, 0, j)),
            pl.BlockSpec((None, ks, tn), lambda i, j: (layer, 1, j)),
        ],
        out_specs=pl.BlockSpec((tm, tn), lambda i, j: (i, j)),
        compiler_params=pltpu.CompilerParams(
            dimension_semantics=("parallel", "parallel"),
            vmem_limit_bytes=_vmem_limit(need)),
        name="outproj",
    )(x, o_h, o_s, w_out, w_out)


def _ffn_kernel(x_ref, g_ref, w1_ref, w2_ref, gf_ref, o_ref, h_ref, *, final_norm):
    f = pl.program_id(1)
    tm = x_ref.shape[0]

    @pl.when(f == 0)
    def _():
        g = g_ref[...]

        def body(i, carry):
            r = pl.ds(pl.multiple_of(i * NORM_ROWS, NORM_ROWS), NORM_ROWS)
            h_ref[r, :] = _rmsnorm(x_ref[r, :], g).astype(BF16)
            return carry

        lax.fori_loop(0, tm // NORM_ROWS, body, 0)

    a = _dot(h_ref[...], w1_ref[...])
    a = jnp.square(jnp.maximum(a, 0.0)).astype(BF16)
    contrib = _dot(a, w2_ref[...])

    @pl.when(f == 0)
    def _():
        o_ref[...] = contrib

    @pl.when(f > 0)
    def _():
        o_ref[...] += contrib

    @pl.when(f == pl.num_programs(1) - 1)
    def _():
        gf = gf_ref[...]

        def body(i, carry):
            r = pl.ds(pl.multiple_of(i * NORM_ROWS, NORM_ROWS), NORM_ROWS)
            y = x_ref[r, :] + o_ref[r, :]
            if final_norm:
                y = _rmsnorm(y, gf)
            o_ref[r, :] = y
            return carry

        lax.fori_loop(0, tm // NORM_ROWS, body, 0)


def _ffn(x, g, w1, w2, gf, *, layer, tm, tf, final_norm):
    n, d = x.shape
    ff = w1.shape[2]
    need = tm * d * 4 + tm * d * 2 + 2 * tm * d * 4 + 4 * d * tf * 2 + tm * tf * 6 + tm * d * 4
    kernel = functools.partial(_ffn_kernel, final_norm=final_norm)
    return pl.pallas_call(
        kernel,
        out_shape=jax.ShapeDtypeStruct((n, d), F32),
        grid=(n // tm, ff // tf),
        in_specs=[
            pl.BlockSpec((tm, d), lambda i, f: (i, 0), pipeline_mode=pl.Buffered(1)),
            pl.BlockSpec((None, 1, d), lambda i, f: (layer, 0, 0)),
            pl.BlockSpec((None, d, tf), lambda i, f: (layer, 0, f)),
            pl.BlockSpec((None, tf, d), lambda i, f: (layer, f, 0)),
            pl.BlockSpec((1, d), lambda i, f: (0, 0)),
        ],
        out_specs=pl.BlockSpec((tm, d), lambda i, f: (i, 0)),
        scratch_shapes=[pltpu.VMEM((tm, d), BF16)],
        compiler_params=pltpu.CompilerParams(
            dimension_semantics=("parallel", "arbitrary"),
            vmem_limit_bytes=_vmem_limit(need)),
        name="ffn",
    )(x, g, w1, w2, gf)


def _trunk(x, st_h, st_r, st_i, wts):
    batch, seq, d = x.shape
    n = batch * seq
    depth = wts["w_in"].shape[0]
    heads = st_h.shape[2]
    hgrn_width = heads * HEAD_DIM
    groups, nst = st_r.shape[2], st_r.shape[3]
    s5_width = wts["w_glu"].shape[1]
    tm = min(1024, n)
    xf = x.reshape(n, d)
    x0r = st_r.reshape(depth, batch, 1, groups * nst)
    x0i = st_i.reshape(depth, batch, 1, groups * nst)
    new_h, new_r, new_i = [], [], []
    for l in range(depth):
        p = _inproj(xf, wts["norm1_g"], wts["w_in"], layer=l, tm=tm, tn=1024)
        o_h, s_new = _hgrn(p, wts["lb_logits"], wts["onorm_g"], st_h, batch=batch, seq=seq, layer=l)
        o_s, xr, xi = _s5(p, wts["wbr"], wts["wbi"], wts["wcr"], wts["wci"], wts["lam"],
                          wts["s5_D"], wts["w_glu"], wts["b_glu"], x0r, x0i,
                          batch=batch, seq=seq, layer=l, col_block=4 * hgrn_width // s5_width)
        x1 = _outproj(xf, o_h, o_s, wts["w_out"], layer=l, tm=tm, tn=1024)
        xf = _ffn(x1, wts["norm2_g"], wts["w_ff1"], wts["w_ff2"], wts["final_g"],
                  layer=l, tm=tm, tf=512, final_norm=(l == depth - 1))
        new_h.append(s_new)
        new_r.append(xr.reshape(batch, groups, nst))
        new_i.append(xi.reshape(batch, groups, nst))
    return xf.reshape(batch, seq, d), jnp.stack(new_h), jnp.stack(new_r), jnp.stack(new_i)


def kernel(x_prompt, x_sample, state_hgrn, state_s5_re, state_s5_im, norm1_g, w_in, hgrn_lb_logits,
           hgrn_onorm_g, s5_lambda_re, s5_lambda_im, s5_log_step, s5_B_re, s5_B_im, s5_C_re, s5_C_im,
           s5_D, s5_w_glu, s5_b_glu, w_out, norm2_g, w_ff1, w_ff2, final_norm_g):
    depth, d = norm1_g.shape
    row3 = lambda a: a.reshape(depth, 1, a.shape[-1])
    lbr, lbi, bbr, bbi = _s5_discretise(s5_lambda_re, s5_lambda_im, s5_log_step, s5_B_re, s5_B_im)
    wts = {
        "norm1_g": row3(norm1_g), "norm2_g": row3(norm2_g), "final_g": final_norm_g.reshape(1, d),
        "lb_logits": hgrn_lb_logits, "onorm_g": row3(hgrn_onorm_g),
        "w_in": w_in.astype(BF16), "w_out": w_out.astype(BF16),
        "w_ff1": w_ff1.astype(BF16), "w_ff2": w_ff2.astype(BF16),
        "w_glu": s5_w_glu.astype(BF16), "b_glu": row3(s5_b_glu), "s5_D": row3(s5_D),
        "lam": jnp.stack([lbr, lbi], axis=1),
        "wbr": _block_diag_in(bbr), "wbi": _block_diag_in(bbi),
        "wcr": _block_diag_out(s5_C_re), "wci": _block_diag_out(s5_C_im),
    }
    bp = x_prompt.shape[0]
    zh = jnp.zeros((depth, bp) + state_hgrn.shape[2:], F32)
    zs = jnp.zeros((depth, bp) + state_s5_re.shape[2:], F32)
    y_p, hp, rp, ip = _trunk(x_prompt, zh, zs, zs, wts)
    y_s, hs, rs, is_ = _trunk(x_sample, state_hgrn, state_s5_re, state_s5_im, wts)
    return (y_p, y_s, hp, rp, ip, hs, rs, is_)
```

```python
import functools

import jax
import jax.numpy as jnp
from jax import lax
from jax.experimental import pallas as pl
from jax.experimental.pallas import tpu as pltpu

F32 = jnp.float32
BF16 = jnp.bfloat16
EPS = 1e-6

HEAD_DIM = 128
HGRN_CHUNK = 64
HGRN_SUB = 16
HGRN_HEADS_PER_STEP = 4
SUBLANES = 8
S5_GROUP = 16
S5_STATE = 64
S5_BLOCK = 256
S5_SCAN_LANES = 256

V7X_VMEM_CAP = 56 * 1024 * 1024
NORM_ROWS = 64


def _vmem_limit(nbytes):
    return int(min(V7X_VMEM_CAP, nbytes * 5 // 4 + (4 << 20)))


def _rmsnorm(x, g):
    return x * lax.rsqrt(jnp.mean(jnp.square(x), axis=-1, keepdims=True) + EPS) * g


def _sigmoid(x):
    return 1.0 / (1.0 + jnp.exp(-x))


def _softplus(x):
    return jnp.maximum(x, 0.0) + jnp.log1p(jnp.exp(-jnp.abs(x)))


def _dot(a, b):
    return jnp.dot(a, b, preferred_element_type=F32)


def _dot_nt(a, b):
    return lax.dot_general(a, b, (((1,), (1,)), ((), ())), preferred_element_type=F32)


def _inproj_kernel(x_ref, g_ref, w_ref, p_ref, u_ref, h_ref):
    tm = x_ref.shape[0]
    j = pl.program_id(1)
    last = pl.num_programs(1) - 1

    @pl.when(j == 0)
    def _():
        g = g_ref[...]

        def body(i, carry):
            r = pl.ds(pl.multiple_of(i * NORM_ROWS, NORM_ROWS), NORM_ROWS)
            h_ref[r, :] = _rmsnorm(x_ref[r, :], g).astype(BF16)
            return carry

        lax.fori_loop(0, tm // NORM_ROWS, body, 0)

    @pl.when(j < last)
    def _():
        p_ref[...] = _dot(h_ref[...], w_ref[...])

    @pl.when(j == last)
    def _():
        u_ref[...] = _dot(h_ref[...], w_ref[...])


def _inproj(x, g, w, *, layer, tm, tn, batch, seq):
    n, d = x.shape
    cols = w.shape[2]
    n_j = cols // tn
    time_major = seq % tm == 0
    if time_major:
        n_t = seq // tm
        u_shape, u_map = (seq, batch * tn), (lambda i, j: (i % n_t, i // n_t))
    else:
        u_shape, u_map = (n, tn), (lambda i, j: (i, 0))
    need = 2 * tm * d * 4 + tm * d * 2 + 2 * d * tn * 2 + 5 * tm * tn * 4
    return pl.pallas_call(
        _inproj_kernel,
        out_shape=(jax.ShapeDtypeStruct((n, cols - tn), F32), jax.ShapeDtypeStruct(u_shape, F32)),
        grid=(n // tm, n_j),
        in_specs=[
            pl.BlockSpec((tm, d), lambda i, j: (i, 0)),
            pl.BlockSpec((None, 1, d), lambda i, j: (layer, 0, 0)),
            pl.BlockSpec((None, d, tn), lambda i, j: (layer, 0, j)),
        ],
        out_specs=(pl.BlockSpec((tm, tn), lambda i, j: (i, jnp.minimum(j, n_j - 2))),
                   pl.BlockSpec((tm, tn), u_map)),
        scratch_shapes=[pltpu.VMEM((tm, d), BF16)],
        compiler_params=pltpu.CompilerParams(
            dimension_semantics=("parallel", "arbitrary"),
            vmem_limit_bytes=_vmem_limit(need)),
        name="inproj",
    )(x, g, w)


def _hgrn_decayed_queries(q_lo, q_hi, f_row, row_is):
    slabs = []
    qd_lo = qd_hi = None
    for s in range(2 * SUBLANES - 1, -1, -1):
        if s == 2 * SUBLANES - 1:
            qd_hi = jnp.where(row_is[SUBLANES - 1], q_hi, 0.0)
        elif s >= SUBLANES:
            qd_hi = jnp.where(row_is[s - SUBLANES], q_hi, qd_hi * f_row(s + 1))
        else:
            fn = f_row(s + 1)
            qd_hi = qd_hi * fn
            qd_lo = jnp.where(row_is[s], q_lo, 0.0 if s == SUBLANES - 1 else qd_lo * fn)
        slabs.append((s, 1, qd_hi))
        if s < SUBLANES:
            slabs.append((s, 0, qd_lo))
    return slabs


def _hgrn_kernel(q_ref, f_ref, i_ref, g_ref, lbl_ref, on_ref, s0_ref, o_ref, sout_ref,
                 st_ref, qs_ref, ks_ref, cs_ref, fs_ref, fc_ref, op_ref, *, layer, chunk, heads_blk):
    c_idx = pl.program_id(2)
    tc = q_ref.shape[0]
    n_sub = chunk // HGRN_SUB

    @pl.when(c_idx == 0)
    def _():
        for h in range(heads_blk):
            st_ref[h] = s0_ref[0, h].T

    logits = lbl_ref[...]
    e = jnp.exp(logits - jnp.max(logits, axis=0, keepdims=True))
    sm = e / jnp.sum(e, axis=0, keepdims=True)
    cum0 = sm[0:1]
    cuml = cum0
    for j in range(1, layer + 1):
        cuml = cuml + sm[j:j + 1]
    lb = cuml - cum0
    log_lb = jnp.log(lb)
    log_1m = jnp.log1p(-lb)

    z = f_ref[...]
    sp_neg = _softplus(-z)
    b = log_1m - sp_neg
    delta = log_lb - b
    logf = jnp.where(jnp.isnan(delta), log_lb + b,
                     jnp.maximum(log_lb, b) + jnp.log1p(jnp.exp(-jnp.abs(delta))))
    fs_ref[...] = jnp.exp(logf)
    ks_ref[...] = jnp.exp(log_1m - (sp_neg + z))
    qr = q_ref[...]
    qs_ref[...] = qr * _sigmoid(qr)

    t_i = lax.broadcasted_iota(jnp.int32, (chunk, chunk), 0)
    s_i = lax.broadcasted_iota(jnp.int32, (chunk, chunk), 1)
    tri = jnp.where(t_i >= s_i, 1.0, 0.0).astype(BF16)
    for j in range(tc // chunk):
        lf = logf[j * chunk:(j + 1) * chunk]
        hi = lf.astype(BF16)
        r1 = lf - hi.astype(F32)
        mid = r1.astype(BF16)
        lo = (r1 - mid.astype(F32)).astype(BF16)
        cs_ref[j * chunk:(j + 1) * chunk, :] = _dot(tri, hi) + _dot(tri, mid) + _dot(tri, lo)

    row8 = lax.broadcasted_iota(jnp.int32, (SUBLANES, HEAD_DIM), 0)
    row_is = [row8 == j for j in range(SUBLANES)]
    lane8 = lax.broadcasted_iota(jnp.int32, (SUBLANES, chunk), 1)
    lane_is = [lane8 == j for j in range(chunk)]
    onorm = on_ref[...]
    blk = lambda a, j: a[j * HGRN_SUB:(j + 1) * HGRN_SUB]

    head_cols = [slice(h * HEAD_DIM, (h + 1) * HEAD_DIM) for h in range(heads_blk)]

    def finish(rows):
        for cols in head_cols:
            o = op_ref[:, cols]
            o = o * lax.rsqrt(jnp.mean(jnp.square(o), axis=-1, keepdims=True) + EPS)
            o = o * onorm[:, cols]
            gr = g_ref[rows, cols]
            o_ref[rows, cols] = (o * (gr * _sigmoid(gr))).astype(BF16)

    op_ref[...] = jnp.zeros(op_ref.shape, F32)

    def chunk_body(c, carry):
        base = pl.multiple_of(c * chunk, chunk)
        r = pl.ds(base, chunk)
        finish(pl.ds(pl.multiple_of(jnp.maximum(c - 1, 0) * chunk, chunk), chunk))
        fc_ref[...] = fs_ref[r, :]
        cum_all = cs_ref[r, :]

        zero_blk = jnp.zeros((HGRN_SUB, HEAD_DIM), F32)
        qs, vbs, kbs, o_inter, a_off = [], [], [], [], []
        for cols in head_cols:
            h = len(qs)
            cum = cum_all[:, cols]
            q = qs_ref[r, cols]
            k = ks_ref[r, cols]
            v = i_ref[r, cols]
            ends = [cum[(j + 1) * HGRN_SUB - 1:(j + 1) * HGRN_SUB, :] for j in range(n_sub)]
            cl = ends[-1]
            ke = [blk(k, j) * jnp.exp(ends[j] - blk(cum, j)) for j in range(n_sub)]
            qe = [blk(q, j) * jnp.exp(blk(cum, j) - ends[j - 1] if j else blk(cum, j)) for j in range(n_sub)]

            st = st_ref[h]
            q_in = jnp.concatenate([qe[j] * jnp.exp(ends[j - 1]) if j else qe[j] for j in range(n_sub)], axis=0)
            o_inter.append(_dot_nt(q_in.astype(BF16), st.astype(BF16)))
            kdec = jnp.concatenate(
                [ke[j] * jnp.exp(cl - ends[j]) if j < n_sub - 1 else ke[j] for j in range(n_sub)], axis=0)
            st_ref[h] = st * jnp.exp(cl) + _dot(v.T.astype(BF16), kdec.astype(BF16))

            offs = [None]
            for i in range(1, n_sub):
                kt = jnp.concatenate(
                    [ke[j] * jnp.exp(ends[i - 1] - ends[j]) if j < i - 1 else ke[j] for j in range(i)]
                    + [zero_blk] * (n_sub - i), axis=0)
                offs.append(_dot_nt(qe[i].astype(BF16), kt.astype(BF16)))
            a_off.append(offs)
            qs.append(q)
            vbs.append(v.astype(BF16))
            kbs.append(k.astype(BF16))

        res = []
        for h, cols in enumerate(head_cols):
            per_head = []
            for i in range(n_sub):
                lo_r = i * HGRN_SUB
                f_row = lambda s, lo_r=lo_r, cols=cols: fc_ref[lo_r + s:lo_r + s + 1, cols]
                q_blk = blk(qs[h], i)
                slabs = _hgrn_decayed_queries(q_blk[:SUBLANES], q_blk[SUBLANES:], f_row, row_is)
                stack = jnp.concatenate([slab for _, _, slab in slabs], axis=0)
                per_head.append(([(s, half) for s, half, _ in slabs],
                                 _dot_nt(stack.astype(BF16), kbs[h])))
            res.append(per_head)

        o_intra = []
        for h in range(heads_blk):
            rows = []
            for i in range(n_sub):
                order, prod = res[h][i]
                halves = [jnp.zeros((SUBLANES, chunk), F32), jnp.zeros((SUBLANES, chunk), F32)]
                for idx, (s, half) in enumerate(order):
                    piece = prod[idx * SUBLANES:(idx + 1) * SUBLANES]
                    halves[half] = jnp.where(lane_is[i * HGRN_SUB + s], piece, halves[half])
                a = jnp.concatenate(halves, axis=0)
                rows.append(a + a_off[h][i] if i else a)
            attn = jnp.concatenate(rows, axis=0)
            o_intra.append(_dot(attn.astype(BF16), vbs[h]))

        for h, cols in enumerate(head_cols):
            op_ref[:, cols] = o_inter[h] + o_intra[h]
        return carry

    lax.fori_loop(0, tc // chunk, chunk_body, 0)
    finish(pl.ds(tc - chunk, chunk))

    @pl.when(c_idx == pl.num_programs(2) - 1)
    def _():
        for h in range(heads_blk):
            sout_ref[0, h] = st_ref[h].T


def _hgrn(p, lb_logits, onorm_g, s0, *, batch, seq, layer):
    n = p.shape[0]
    heads = s0.shape[2]
    hb = HGRN_HEADS_PER_STEP
    wblk = hb * HEAD_DIM
    n_hb = heads // hb
    width = heads * HEAD_DIM
    depth = lb_logits.shape[0]
    tc = min(512, seq)
    chunk = min(HGRN_CHUNK, seq)
    n_t = seq // tc
    row = lambda b, h, c: b * n_t + c
    need = 2 * 4 * tc * wblk * 4 + 2 * tc * wblk * 2 + 4 * tc * wblk * 4 + 5 * hb * HEAD_DIM * HEAD_DIM * 4
    kernel = functools.partial(_hgrn_kernel, layer=layer, chunk=chunk, heads_blk=hb)
    st_spec = pl.BlockSpec((None, 1, hb, HEAD_DIM, HEAD_DIM), lambda b, h, c: (layer, b, h, 0, 0))
    return pl.pallas_call(
        kernel,
        out_shape=(jax.ShapeDtypeStruct((n, width), BF16),
                   jax.ShapeDtypeStruct((batch, heads, HEAD_DIM, HEAD_DIM), F32)),
        grid=(batch, n_hb, n_t),
        in_specs=[
            pl.BlockSpec((tc, wblk), lambda b, h, c: (row(b, h, c), h)),
            pl.BlockSpec((tc, wblk), lambda b, h, c: (row(b, h, c), n_hb + h)),
            pl.BlockSpec((tc, wblk), lambda b, h, c: (row(b, h, c), 2 * n_hb + h)),
            pl.BlockSpec((tc, wblk), lambda b, h, c: (row(b, h, c), 3 * n_hb + h)),
            pl.BlockSpec((depth, wblk), lambda b, h, c: (0, h)),
            pl.BlockSpec((None, 1, wblk), lambda b, h, c: (layer, 0, h)),
            st_spec,
        ],
        out_specs=(
            pl.BlockSpec((tc, wblk), lambda b, h, c: (row(b, h, c), h)),
            pl.BlockSpec((1, hb, HEAD_DIM, HEAD_DIM), lambda b, h, c: (b, h, 0, 0)),
        ),
        scratch_shapes=[
            pltpu.VMEM((hb, HEAD_DIM, HEAD_DIM), F32),
            pltpu.VMEM((tc, wblk), F32),
            pltpu.VMEM((tc, wblk), F32),
            pltpu.VMEM((tc, wblk), F32),
            pltpu.VMEM((tc, wblk), F32),
            pltpu.VMEM((chunk, wblk), F32),
            pltpu.VMEM((chunk, wblk), F32),
        ],
        compiler_params=pltpu.CompilerParams(
            dimension_semantics=("parallel", "parallel", "arbitrary"),
            vmem_limit_bytes=_vmem_limit(need)),
        name="hgrn",
    )(p, p, p, p, lb_logits, onorm_g, s0)


def _s5_disc_lambda_kernel(lr_ref, li_ref, ls_ref, lbr_ref, lbi_ref, cr_ref, ci_ref):
    lr, li = lr_ref[...], li_ref[...]
    dt = jnp.exp(ls_ref[...])
    mag = jnp.exp(dt * lr)
    ang = dt * li
    lbr = mag * jnp.cos(ang)
    lbi = mag * jnp.sin(ang)
    nr, ni = lbr - 1.0, lbi
    den = lr * lr + li * li
    lbr_ref[...] = lbr
    lbi_ref[...] = lbi
    cr_ref[...] = (nr * lr + ni * li) / den
    ci_ref[...] = (ni * lr - nr * li) / den


def _s5_disc_b_kernel(cr_ref, ci_ref, br_ref, bi_ref, obr_ref, obi_ref):
    cr, ci, br, bi = cr_ref[...], ci_ref[...], br_ref[...], bi_ref[...]
    obr_ref[...] = cr * br - ci * bi
    obi_ref[...] = cr * bi + ci * br


def _s5_discretise(lam_re, lam_im, log_step, b_re, b_im):
    depth, groups, nst = lam_re.shape
    rows = depth * groups
    shp = jax.ShapeDtypeStruct((rows, nst), F32)
    lbr, lbi, cr, ci = pl.pallas_call(
        _s5_disc_lambda_kernel, out_shape=(shp, shp, shp, shp), name="s5_disc_lambda",
    )(lam_re.reshape(rows, nst), lam_im.reshape(rows, nst), log_step.reshape(rows, 1))
    flat = rows * nst
    cg = b_re.shape[-1]
    blk = 1024
    col = pl.BlockSpec((blk, 1), lambda i: (i, 0))
    mat = pl.BlockSpec((blk, cg), lambda i: (i, 0))
    oshp = jax.ShapeDtypeStruct((flat, cg), F32)
    bbr, bbi = pl.pallas_call(
        _s5_disc_b_kernel, out_shape=(oshp, oshp), grid=(flat // blk,),
        in_specs=[col, col, mat, mat], out_specs=(mat, mat), name="s5_disc_b",
    )(cr.reshape(flat, 1), ci.reshape(flat, 1), b_re.reshape(flat, cg), b_im.reshape(flat, cg))
    full = (depth, groups, nst, cg)
    return (lbr.reshape(depth, groups * nst), lbi.reshape(depth, groups * nst),
            bbr.reshape(full), bbi.reshape(full))


def _block_diag_in(bb):
    depth, groups, nst, cg = bb.shape
    gpb = S5_BLOCK // cg
    nb = groups // gpb
    t = bb.reshape(depth, nb, gpb, nst, cg).transpose(0, 1, 2, 4, 3).astype(BF16)
    eye = jnp.eye(gpb, dtype=BF16)
    w = t[:, :, :, :, None, :] * eye[None, None, :, None, :, None]
    return w.reshape(depth, nb, S5_BLOCK, gpb * nst)


def _block_diag_out(cc):
    depth, groups, cg, nst = cc.shape
    gpb = S5_BLOCK // cg
    nb = groups // gpb
    t = cc.reshape(depth, nb, gpb, cg, nst).transpose(0, 1, 2, 4, 3).astype(BF16)
    eye = jnp.eye(gpb, dtype=BF16)
    w = t[:, :, :, :, None, :] * eye[None, None, :, None, :, None]
    return w.reshape(depth, nb, gpb * nst, S5_BLOCK)


def _s5_kernel(u_ref, wbr_ref, wbi_ref, wcr_ref, wci_ref, lam_ref, d_ref, wg_ref, bg_ref,
               x0r_ref, x0i_ref, o_ref, xr_ref, xi_ref, xs_ref, y_ref, st_ref, *, batch):
    c_idx = pl.program_id(0)
    rows = u_ref.shape[0]
    nb, _, bcols = wbr_ref.shape
    ns = nb * bcols
    grp = st_ref.shape[1]
    per = grp // batch

    @pl.when(c_idx == 0)
    def _():
        st_ref[0] = jnp.concatenate([x0r_ref[...]] * per, axis=0)
        st_ref[1] = jnp.concatenate([x0i_ref[...]] * per, axis=0)

    u = u_ref[...]
    ub = u.astype(BF16)
    for j in range(nb):
        uj = ub[:, j * S5_BLOCK:(j + 1) * S5_BLOCK]
        xs_ref[:, j * bcols:(j + 1) * bcols] = _dot(uj, wbr_ref[j])
        xs_ref[:, ns + j * bcols:ns + (j + 1) * bcols] = _dot(uj, wbi_ref[j])

    second = lax.broadcasted_iota(jnp.int32, (grp, S5_SCAN_LANES), 0) >= batch
    for cb in range(ns // S5_SCAN_LANES):
        lo = cb * S5_SCAN_LANES
        re_l = slice(lo, lo + S5_SCAN_LANES)
        im_l = slice(ns + lo, ns + lo + S5_SCAN_LANES)
        lr = lam_ref[0:1, re_l]
        li = lam_ref[1:2, re_l]
        if per == 2:
            pr = jnp.where(second, lr * lr - li * li, lr)
            pi = jnp.where(second, 2.0 * lr * li, li)
            qr = jnp.where(second, lr, 0.0)
            qi = jnp.where(second, li, 0.0)
        else:
            pr, pi, qr, qi = lr, li, None, None

        def step(g, carry, pr=pr, pi=pi, qr=qr, qi=qi, re_l=re_l, im_l=im_l):
            xr, xi = carry
            r = pl.ds(pl.multiple_of(g * grp, grp), grp)
            br = xs_ref[r, re_l]
            bi = xs_ref[r, im_l]
            if per == 2:
                sr = pltpu.roll(br, batch, axis=0)
                si = pltpu.roll(bi, batch, axis=0)
                br = br + (qr * sr - qi * si)
                bi = bi + (qr * si + qi * sr)
                xr = jnp.where(second, xr, pltpu.roll(xr, batch, axis=0))
                xi = jnp.where(second, xi, pltpu.roll(xi, batch, axis=0))
            nr = pr * xr - pi * xi + br
            ni = pr * xi + pi * xr + bi
            xs_ref[r, re_l] = nr
            xs_ref[r, im_l] = ni
            return nr, ni

        xr, xi = lax.fori_loop(0, rows // grp, step, (st_ref[0, :, re_l], st_ref[1, :, re_l]), unroll=2)
        st_ref[0, :, re_l] = xr
        st_ref[1, :, re_l] = xi

    for j in range(nb):
        xr_j = xs_ref[:, j * bcols:(j + 1) * bcols].astype(BF16)
        xi_j = xs_ref[:, ns + j * bcols:ns + (j + 1) * bcols].astype(BF16)
        y_ref[:, j * S5_BLOCK:(j + 1) * S5_BLOCK] = _dot(xr_j, wcr_ref[j]) - _dot(xi_j, wci_ref[j])

    y = y_ref[...] + d_ref[...] * u
    hh = jax.nn.gelu(y)
    gate = _sigmoid(_dot(hh.astype(BF16), wg_ref[...]) + bg_ref[...])
    o_ref[...] = (hh * gate).astype(BF16)

    @pl.when(c_idx == pl.num_programs(0) - 1)
    def _():
        xr_ref[...] = st_ref[0, grp - batch:grp, :]
        xi_ref[...] = st_ref[1, grp - batch:grp, :]


def _s5(u, wbr, wbi, wcr, wci, lam, d_skip, w_glu, b_glu, x0r, x0i, *, batch, layer):
    n, width = u.shape
    _, nb, _, bcols = wbr.shape
    ns = nb * bcols
    assert batch % SUBLANES == 0 or 2 * batch == SUBLANES
    grp = max(batch, SUBLANES)
    rows = min(256, n)
    lay4 = lambda c: (layer, 0, 0, 0)
    lay3 = lambda c: (layer, 0, 0)
    st_in = pl.BlockSpec((None, batch, ns), lay3)
    need = (2 * rows * width * 4 + 2 * 4 * nb * S5_BLOCK * bcols * 2 + 2 * width * width * 2
            + 2 * rows * width * 2 + rows * 2 * ns * 4 + rows * width * 4 + 4 * rows * width * 4
            + (8 * batch + 2 * grp) * ns * 4)
    st_shape = jax.ShapeDtypeStruct((batch, ns), F32)
    return pl.pallas_call(
        functools.partial(_s5_kernel, batch=batch),
        out_shape=(jax.ShapeDtypeStruct((n, width), BF16), st_shape, st_shape),
        grid=(n // rows,),
        in_specs=[
            pl.BlockSpec((rows, width), lambda c: (c, 0)),
            pl.BlockSpec((None,) + wbr.shape[1:], lay4),
            pl.BlockSpec((None,) + wbi.shape[1:], lay4),
            pl.BlockSpec((None,) + wcr.shape[1:], lay4),
            pl.BlockSpec((None,) + wci.shape[1:], lay4),
            pl.BlockSpec((None, 2, ns), lay3),
            pl.BlockSpec((None, 1, width), lay3),
            pl.BlockSpec((None, width, width), lay3),
            pl.BlockSpec((None, 1, width), lay3),
            st_in,
            st_in,
        ],
        out_specs=(
            pl.BlockSpec((rows, width), lambda c: (c, 0)),
            pl.BlockSpec((batch, ns), lambda c: (0, 0)),
            pl.BlockSpec((batch, ns), lambda c: (0, 0)),
        ),
        scratch_shapes=[
            pltpu.VMEM((rows, 2 * ns), F32),
            pltpu.VMEM((rows, width), F32),
            pltpu.VMEM((2, grp, ns), F32),
        ],
        compiler_params=pltpu.CompilerParams(
            dimension_semantics=("arbitrary",),
            vmem_limit_bytes=_vmem_limit(need)),
        name="s5",
    )(u, wbr, wbi, wcr, wci, lam, d_skip, w_glu, b_glu, x0r, x0i)


def _outproj_kernel(x_ref, oh_ref, os_ref, wh_ref, ws_ref, o_ref):
    o_ref[...] = x_ref[...] + _dot(oh_ref[...], wh_ref[...]) + _dot(os_ref[...], ws_ref[...])


def _outproj(x, o_h, o_s, w_out, *, layer, tm, tn, seq):
    n, d = x.shape
    kh = o_h.shape[1]
    ks = w_out.shape[1] - kh
    assert kh == ks
    if o_s.shape[0] == n:
        os_map = lambda i, j: (i, 0)
    else:
        n_t = seq // tm
        os_map = lambda i, j: (i % n_t, i // n_t)
    need = 2 * (2 * tm * tn * 4 + tm * (kh + ks) * 2 + (kh + ks) * tn * 2) + 2 * tm * tn * 4
    return pl.pallas_call(
        _outproj_kernel,
        out_shape=jax.ShapeDtypeStruct((n, d), F32),
        grid=(n // tm, d // tn),
        in_specs=[
            pl.BlockSpec((tm, tn), lambda i, j: (i, j)),
            pl.BlockSpec((tm, kh), lambda i, j: (i, 0)),
            pl.BlockSpec((tm, ks), os_map),
            pl.BlockSpec((None, kh, tn), lambda i, j: (layer, 0, j)),
            pl.BlockSpec((None, ks, tn), lambda i, j: (layer, 1, j)),
        ],
        out_specs=pl.BlockSpec((tm, tn), lambda i, j: (i, j)),
        compiler_params=pltpu.CompilerParams(
            dimension_semantics=("parallel", "parallel"),
            vmem_limit_bytes=_vmem_limit(need)),
        name="outproj",
    )(x, o_h, o_s, w_out, w_out)


def _ffn_kernel(x_ref, g_ref, w1_ref, w2_ref, gf_ref, o_ref, h_ref, *, final_norm):
    f = pl.program_id(1)
    tm = x_ref.shape[0]

    @pl.when(f == 0)
    def _():
        g = g_ref[...]

        def body(i, carry):
            r = pl.ds(pl.multiple_of(i * NORM_ROWS, NORM_ROWS), NORM_ROWS)
            h_ref[r, :] = _rmsnorm(x_ref[r, :], g).astype(BF16)
            return carry

        lax.fori_loop(0, tm // NORM_ROWS, body, 0)

    a = _dot(h_ref[...], w1_ref[...])
    a = jnp.square(jnp.maximum(a, 0.0)).astype(BF16)
    contrib = _dot(a, w2_ref[...])

    @pl.when(f == 0)
    def _():
        o_ref[...] = contrib

    @pl.when(f > 0)
    def _():
        o_ref[...] += contrib

    @pl.when(f == pl.num_programs(1) - 1)
    def _():
        gf = gf_ref[...]

        def body(i, carry):
            r = pl.ds(pl.multiple_of(i * NORM_ROWS, NORM_ROWS), NORM_ROWS)
            y = x_ref[r, :] + o_ref[r, :]
            if final_norm:
                y = _rmsnorm(y, gf)
            o_ref[r, :] = y
            return carry

        lax.fori_loop(0, tm // NORM_ROWS, body, 0)


def _ffn(x, g, w1, w2, gf, *, layer, tm, tf, final_norm):
    n, d = x.shape
    ff = w1.shape[2]
    need = tm * d * 4 + tm * d * 2 + 2 * tm * d * 4 + 4 * d * tf * 2 + tm * tf * 6 + tm * d * 4
    kernel = functools.partial(_ffn_kernel, final_norm=final_norm)
    return pl.pallas_call(
        kernel,
        out_shape=jax.ShapeDtypeStruct((n, d), F32),
        grid=(n // tm, ff // tf),
        in_specs=[
            pl.BlockSpec((tm, d), lambda i, f: (i, 0), pipeline_mode=pl.Buffered(1)),
            pl.BlockSpec((None, 1, d), lambda i, f: (layer, 0, 0)),
            pl.BlockSpec((None, d, tf), lambda i, f: (layer, 0, f)),
            pl.BlockSpec((None, tf, d), lambda i, f: (layer, f, 0)),
            pl.BlockSpec((1, d), lambda i, f: (0, 0)),
        ],
        out_specs=pl.BlockSpec((tm, d), lambda i, f: (i, 0)),
        scratch_shapes=[pltpu.VMEM((tm, d), BF16)],
        compiler_params=pltpu.CompilerParams(
            dimension_semantics=("parallel", "arbitrary"),
            vmem_limit_bytes=_vmem_limit(need)),
        name="ffn",
    )(x, g, w1, w2, gf)


def _trunk(x, st_h, st_r, st_i, wts):
    batch, seq, d = x.shape
    n = batch * seq
    depth = wts["w_in"].shape[0]
    heads = st_h.shape[2]
    hgrn_width = heads * HEAD_DIM
    groups, nst = st_r.shape[2], st_r.shape[3]
    s5_width = wts["w_glu"].shape[1]
    tm = min(1024, n)
    xf = x.reshape(n, d)
    x0r = st_r.reshape(depth, batch, groups * nst)
    x0i = st_i.reshape(depth, batch, groups * nst)
    time_major = seq % tm == 0
    new_h, new_r, new_i = [], [], []
    for l in range(depth):
        p, u = _inproj(xf, wts["norm1_g"], wts["w_in"], layer=l, tm=tm, tn=s5_width, batch=batch, seq=seq)
        o_h, s_new = _hgrn(p, wts["lb_logits"], wts["onorm_g"], st_h, batch=batch, seq=seq, layer=l)
        if not time_major:
            u = u.reshape(batch, seq, s5_width).transpose(1, 0, 2)
        o_s, xr, xi = _s5(u.reshape(n, s5_width), wts["wbr"], wts["wbi"], wts["wcr"], wts["wci"], wts["lam"],
                          wts["s5_D"], wts["w_glu"], wts["b_glu"], x0r, x0i, batch=batch, layer=l)
        if time_major:
            o_s = o_s.reshape(seq, batch * s5_width)
        else:
            o_s = o_s.reshape(seq, batch, s5_width).transpose(1, 0, 2).reshape(n, s5_width)
        x1 = _outproj(xf, o_h, o_s, wts["w_out"], layer=l, tm=tm, tn=1024, seq=seq)
        xf = _ffn(x1, wts["norm2_g"], wts["w_ff1"], wts["w_ff2"], wts["final_g"],
                  layer=l, tm=tm, tf=512, final_norm=(l == depth - 1))
        new_h.append(s_new)
        new_r.append(xr.reshape(batch, groups, nst))
        new_i.append(xi.reshape(batch, groups, nst))
    return xf.reshape(batch, seq, d), jnp.stack(new_h), jnp.stack(new_r), jnp.stack(new_i)


def kernel(x_prompt, x_sample, state_hgrn, state_s5_re, state_s5_im, norm1_g, w_in, hgrn_lb_logits,
           hgrn_onorm_g, s5_lambda_re, s5_lambda_im, s5_log_step, s5_B_re, s5_B_im, s5_C_re, s5_C_im,
           s5_D, s5_w_glu, s5_b_glu, w_out, norm2_g, w_ff1, w_ff2, final_norm_g):
    depth, d = norm1_g.shape
    row3 = lambda a: a.reshape(depth, 1, a.shape[-1])
    lbr, lbi, bbr, bbi = _s5_discretise(s5_lambda_re, s5_lambda_im, s5_log_step, s5_B_re, s5_B_im)
    wts = {
        "norm1_g": row3(norm1_g), "norm2_g": row3(norm2_g), "final_g": final_norm_g.reshape(1, d),
        "lb_logits": hgrn_lb_logits, "onorm_g": row3(hgrn_onorm_g),
        "w_in": w_in.astype(BF16), "w_out": w_out.astype(BF16),
        "w_ff1": w_ff1.astype(BF16), "w_ff2": w_ff2.astype(BF16),
        "w_glu": s5_w_glu.astype(BF16), "b_glu": row3(s5_b_glu), "s5_D": row3(s5_D),
        "lam": jnp.stack([lbr, lbi], axis=1),
        "wbr": _block_diag_in(bbr), "wbi": _block_diag_in(bbi),
        "wcr": _block_diag_out(s5_C_re), "wci": _block_diag_out(s5_C_im),
    }
    bp = x_prompt.shape[0]
    zh = jnp.zeros((depth, bp) + state_hgrn.shape[2:], F32)
    zs = jnp.zeros((depth, bp) + state_s5_re.shape[2:], F32)
    y_p, hp, rp, ip = _trunk(x_prompt, zh, zs, zs, wts)
    y_s, hs, rs, is_ = _trunk(x_sample, state_hgrn, state_s5_re, state_s5_im, wts)
    return (y_p, y_s, hp, rp, ip, hs, rs, is_)
```

```python
import functools

import jax
import jax.numpy as jnp
from jax import lax
from jax.experimental import pallas as pl
from jax.experimental.pallas import tpu as pltpu

F32 = jnp.float32
BF16 = jnp.bfloat16
EPS = 1e-6

HEAD_DIM = 128
HGRN_CHUNK = 64
HGRN_SUB = 16
HGRN_HEADS_PER_STEP = 4
SUBLANES = 8
LANES = 128
S5_GROUP = 16
S5_STATE = 64
S5_BLOCK = 256
S5_SCAN_LANES = 256

V7X_VMEM_CAP = 56 * 1024 * 1024
NORM_ROWS = 64
FFN_ACC_CHUNKS = 4


def _vmem_limit(nbytes):
    return int(min(V7X_VMEM_CAP, nbytes * 5 // 4 + (4 << 20)))


def _rmsnorm(x, g):
    return x * lax.rsqrt(jnp.mean(jnp.square(x), axis=-1, keepdims=True) + EPS) * g


def _sigmoid(x):
    return 1.0 / (1.0 + jnp.exp(-x))


def _log1p_exp_neg_abs(x):
    return jnp.log(1.0 + jnp.exp(-jnp.abs(x)))


def _softplus(x):
    return jnp.maximum(x, 0.0) + _log1p_exp_neg_abs(x)


def _dot(a, b):
    return jnp.dot(a, b, preferred_element_type=F32)


def _dot_nt(a, b):
    return lax.dot_general(a, b, (((1,), (1,)), ((), ())), preferred_element_type=F32)


def _inproj_kernel(x_ref, g_ref, w_ref, p_ref, u_ref, h_ref):
    tm = x_ref.shape[0]
    j = pl.program_id(1)
    last = pl.num_programs(1) - 1

    @pl.when(j == 0)
    def _():
        g = g_ref[...]

        def body(i, carry):
            r = pl.ds(pl.multiple_of(i * NORM_ROWS, NORM_ROWS), NORM_ROWS)
            h_ref[r, :] = _rmsnorm(x_ref[r, :], g).astype(BF16)
            return carry

        lax.fori_loop(0, tm // NORM_ROWS, body, 0)

    @pl.when(j < last)
    def _():
        p_ref[...] = _dot(h_ref[...], w_ref[...])

    @pl.when(j == last)
    def _():
        u_ref[...] = _dot(h_ref[...], w_ref[...])


def _inproj(x, g, w, *, layer, tm, tn, batch, seq):
    n, d = x.shape
    cols = w.shape[2]
    n_j = cols // tn
    time_major = seq % tm == 0
    if time_major:
        n_t = seq // tm
        u_shape, u_map = (seq, batch * tn), (lambda i, j: (i % n_t, i // n_t))
    else:
        u_shape, u_map = (n, tn), (lambda i, j: (i, 0))
    need = 2 * tm * d * 4 + tm * d * 2 + 2 * d * tn * 2 + 5 * tm * tn * 4
    return pl.pallas_call(
        _inproj_kernel,
        out_shape=(jax.ShapeDtypeStruct((n, cols - tn), F32), jax.ShapeDtypeStruct(u_shape, F32)),
        grid=(n // tm, n_j),
        in_specs=[
            pl.BlockSpec((tm, d), lambda i, j: (i, 0)),
            pl.BlockSpec((None, 1, d), lambda i, j: (layer, 0, 0)),
            pl.BlockSpec((None, d, tn), lambda i, j: (layer, 0, j)),
        ],
        out_specs=(pl.BlockSpec((tm, tn), lambda i, j: (i, jnp.minimum(j, n_j - 2))),
                   pl.BlockSpec((tm, tn), u_map)),
        scratch_shapes=[pltpu.VMEM((tm, d), BF16)],
        compiler_params=pltpu.CompilerParams(
            dimension_semantics=("parallel", "arbitrary"),
            vmem_limit_bytes=_vmem_limit(need)),
        name="inproj",
    )(x, g, w)


def _hgrn_decayed_queries(q_lo, q_hi, f_row, row_is):
    slabs = []
    qd_lo = qd_hi = None
    for s in range(2 * SUBLANES - 1, -1, -1):
        if s == 2 * SUBLANES - 1:
            qd_hi = jnp.where(row_is[SUBLANES - 1], q_hi, 0.0)
        elif s >= SUBLANES:
            qd_hi = jnp.where(row_is[s - SUBLANES], q_hi, qd_hi * f_row(s + 1))
        else:
            fn = f_row(s + 1)
            qd_hi = qd_hi * fn
            qd_lo = jnp.where(row_is[s], q_lo, 0.0 if s == SUBLANES - 1 else qd_lo * fn)
        slabs.append((s, 1, qd_hi))
        if s < SUBLANES:
            slabs.append((s, 0, qd_lo))
    return slabs


def _hgrn_kernel(q_ref, f_ref, i_ref, g_ref, lbl_ref, on_ref, s0_ref, o_ref, sout_ref,
                 st_ref, qs_ref, ks_ref, cs_ref, fs_ref, fc_ref, op_ref, *, layer, chunk, heads_blk):
    c_idx = pl.program_id(2)
    tc = q_ref.shape[0]
    n_sub = chunk // HGRN_SUB

    @pl.when(c_idx == 0)
    def _():
        for h in range(heads_blk):
            st_ref[h] = s0_ref[0, h].T

    logits = lbl_ref[...]
    e = jnp.exp(logits - jnp.max(logits, axis=0, keepdims=True))
    sm = e / jnp.sum(e, axis=0, keepdims=True)
    cum0 = sm[0:1]
    cuml = cum0
    for j in range(1, layer + 1):
        cuml = cuml + sm[j:j + 1]
    lb = cuml - cum0
    log_lb = jnp.log(lb)
    log_1m = jnp.log1p(-lb)

    z = f_ref[...]
    sp_neg = _softplus(-z)
    b = log_1m - sp_neg
    delta = log_lb - b
    logf = jnp.where(jnp.isnan(delta), log_lb + b,
                     jnp.maximum(log_lb, b) + _log1p_exp_neg_abs(delta))
    fs_ref[...] = jnp.exp(logf)
    ks_ref[...] = jnp.exp(log_1m - (sp_neg + z))
    qr = q_ref[...]
    qs_ref[...] = qr * _sigmoid(qr)

    t_i = lax.broadcasted_iota(jnp.int32, (chunk, chunk), 0)
    s_i = lax.broadcasted_iota(jnp.int32, (chunk, chunk), 1)
    tri = jnp.where(t_i >= s_i, 1.0, 0.0).astype(BF16)
    for j in range(tc // chunk):
        lf = logf[j * chunk:(j + 1) * chunk]
        hi = lf.astype(BF16)
        r1 = lf - hi.astype(F32)
        mid = r1.astype(BF16)
        lo = (r1 - mid.astype(F32)).astype(BF16)
        cs_ref[j * chunk:(j + 1) * chunk, :] = _dot(tri, hi) + _dot(tri, mid) + _dot(tri, lo)

    row8 = lax.broadcasted_iota(jnp.int32, (SUBLANES, HEAD_DIM), 0)
    row_is = [row8 == j for j in range(SUBLANES)]
    lane8 = lax.broadcasted_iota(jnp.int32, (SUBLANES, chunk), 1)
    lane_is = [lane8 == j for j in range(chunk)]
    onorm = on_ref[...]
    blk = lambda a, j: a[j * HGRN_SUB:(j + 1) * HGRN_SUB]

    head_cols = [slice(h * HEAD_DIM, (h + 1) * HEAD_DIM) for h in range(heads_blk)]

    def finish(rows):
        for cols in head_cols:
            o = op_ref[:, cols]
            o = o * lax.rsqrt(jnp.mean(jnp.square(o), axis=-1, keepdims=True) + EPS)
            o = o * onorm[:, cols]
            gr = g_ref[rows, cols]
            o_ref[rows, cols] = (o * (gr * _sigmoid(gr))).astype(BF16)

    op_ref[...] = jnp.zeros(op_ref.shape, F32)

    def chunk_body(c, carry):
        base = pl.multiple_of(c * chunk, chunk)
        r = pl.ds(base, chunk)
        finish(pl.ds(pl.multiple_of(jnp.maximum(c - 1, 0) * chunk, chunk), chunk))
        fc_ref[...] = fs_ref[r, :]
        cum_all = cs_ref[r, :]

        zero_blk = jnp.zeros((HGRN_SUB, HEAD_DIM), F32)
        qs, vbs, kbs, o_inter, a_off = [], [], [], [], []
        for cols in head_cols:
            h = len(qs)
            cum = cum_all[:, cols]
            q = qs_ref[r, cols]
            k = ks_ref[r, cols]
            v = i_ref[r, cols]
            ends = [cum[(j + 1) * HGRN_SUB - 1:(j + 1) * HGRN_SUB, :] for j in range(n_sub)]
            cl = ends[-1]
            ke = [blk(k, j) * jnp.exp(ends[j] - blk(cum, j)) for j in range(n_sub)]
            qe = [blk(q, j) * jnp.exp(blk(cum, j) - ends[j - 1] if j else blk(cum, j)) for j in range(n_sub)]

            st = st_ref[h]
            q_in = jnp.concatenate([qe[j] * jnp.exp(ends[j - 1]) if j else qe[j] for j in range(n_sub)], axis=0)
            o_inter.append(_dot_nt(q_in.astype(BF16), st.astype(BF16)))
            kdec = jnp.concatenate(
                [ke[j] * jnp.exp(cl - ends[j]) if j < n_sub - 1 else ke[j] for j in range(n_sub)], axis=0)
            st_ref[h] = st * jnp.exp(cl) + _dot(v.T.astype(BF16), kdec.astype(BF16))

            offs = [None]
            for i in range(1, n_sub):
                kt = jnp.concatenate(
                    [ke[j] * jnp.exp(ends[i - 1] - ends[j]) if j < i - 1 else ke[j] for j in range(i)]
                    + [zero_blk] * (n_sub - i), axis=0)
                offs.append(_dot_nt(qe[i].astype(BF16), kt.astype(BF16)))
            a_off.append(offs)
            qs.append(q)
            vbs.append(v.astype(BF16))
            kbs.append(k.astype(BF16))

        res = []
        for h, cols in enumerate(head_cols):
            per_head = []
            for i in range(n_sub):
                lo_r = i * HGRN_SUB
                f_row = lambda s, lo_r=lo_r, cols=cols: fc_ref[lo_r + s:lo_r + s + 1, cols]
                q_blk = blk(qs[h], i)
                slabs = _hgrn_decayed_queries(q_blk[:SUBLANES], q_blk[SUBLANES:], f_row, row_is)
                stack = jnp.concatenate([slab for _, _, slab in slabs], axis=0)
                per_head.append(([(s, half) for s, half, _ in slabs],
                                 _dot_nt(stack.astype(BF16), kbs[h])))
            res.append(per_head)

        o_intra = []
        for h in range(heads_blk):
            rows = []
            for i in range(n_sub):
                order, prod = res[h][i]
                halves = [jnp.zeros((SUBLANES, chunk), F32), jnp.zeros((SUBLANES, chunk), F32)]
                for idx, (s, half) in enumerate(order):
                    piece = prod[idx * SUBLANES:(idx + 1) * SUBLANES]
                    halves[half] = jnp.where(lane_is[i * HGRN_SUB + s], piece, halves[half])
                a = jnp.concatenate(halves, axis=0)
                rows.append(a + a_off[h][i] if i else a)
            attn = jnp.concatenate(rows, axis=0)
            o_intra.append(_dot(attn.astype(BF16), vbs[h]))

        for h, cols in enumerate(head_cols):
            op_ref[:, cols] = o_inter[h] + o_intra[h]
        return carry

    lax.fori_loop(0, tc // chunk, chunk_body, 0)
    finish(pl.ds(tc - chunk, chunk))

    @pl.when(c_idx == pl.num_programs(2) - 1)
    def _():
        for h in range(heads_blk):
            sout_ref[0, h] = st_ref[h].T


def _hgrn(p, lb_logits, onorm_g, s0, *, batch, seq, layer):
    n = p.shape[0]
    heads = s0.shape[2]
    hb = HGRN_HEADS_PER_STEP
    wblk = hb * HEAD_DIM
    n_hb = heads // hb
    width = heads * HEAD_DIM
    depth = lb_logits.shape[0]
    tc = min(512, seq)
    chunk = min(HGRN_CHUNK, seq)
    n_t = seq // tc
    row = lambda b, h, c: b * n_t + c
    need = 2 * 4 * tc * wblk * 4 + 2 * tc * wblk * 2 + 4 * tc * wblk * 4 + 5 * hb * HEAD_DIM * HEAD_DIM * 4
    kernel = functools.partial(_hgrn_kernel, layer=layer, chunk=chunk, heads_blk=hb)
    st_spec = pl.BlockSpec((None, 1, hb, HEAD_DIM, HEAD_DIM), lambda b, h, c: (layer, b, h, 0, 0))
    return pl.pallas_call(
        kernel,
        out_shape=(jax.ShapeDtypeStruct((n, width), BF16),
                   jax.ShapeDtypeStruct((batch, heads, HEAD_DIM, HEAD_DIM), F32)),
        grid=(batch, n_hb, n_t),
        in_specs=[
            pl.BlockSpec((tc, wblk), lambda b, h, c: (row(b, h, c), h)),
            pl.BlockSpec((tc, wblk), lambda b, h, c: (row(b, h, c), n_hb + h)),
            pl.BlockSpec((tc, wblk), lambda b, h, c: (row(b, h, c), 2 * n_hb + h)),
            pl.BlockSpec((tc, wblk), lambda b, h, c: (row(b, h, c), 3 * n_hb + h)),
            pl.BlockSpec((depth, wblk), lambda b, h, c: (0, h)),
            pl.BlockSpec((None, 1, wblk), lambda b, h, c: (layer, 0, h)),
            st_spec,
        ],
        out_specs=(
            pl.BlockSpec((tc, wblk), lambda b, h, c: (row(b, h, c), h)),
            pl.BlockSpec((1, hb, HEAD_DIM, HEAD_DIM), lambda b, h, c: (b, h, 0, 0)),
        ),
        scratch_shapes=[
            pltpu.VMEM((hb, HEAD_DIM, HEAD_DIM), F32),
            pltpu.VMEM((tc, wblk), F32),
            pltpu.VMEM((tc, wblk), F32),
            pltpu.VMEM((tc, wblk), F32),
            pltpu.VMEM((tc, wblk), F32),
            pltpu.VMEM((chunk, wblk), F32),
            pltpu.VMEM((chunk, wblk), F32),
        ],
        compiler_params=pltpu.CompilerParams(
            dimension_semantics=("parallel", "parallel", "arbitrary"),
            vmem_limit_bytes=_vmem_limit(need)),
        name="hgrn",
    )(p, p, p, p, lb_logits, onorm_g, s0)


def _s5_disc_lambda_kernel(lr_ref, li_ref, ls_ref, lbr_ref, lbi_ref, cr_ref, ci_ref):
    lr, li = lr_ref[...], li_ref[...]
    dt = jnp.exp(ls_ref[...])
    mag = jnp.exp(dt * lr)
    ang = dt * li
    lbr = mag * jnp.cos(ang)
    lbi = mag * jnp.sin(ang)
    nr, ni = lbr - 1.0, lbi
    den = lr * lr + li * li
    lbr_ref[...] = lbr
    lbi_ref[...] = lbi
    cr_ref[...] = (nr * lr + ni * li) / den
    ci_ref[...] = (ni * lr - nr * li) / den


def _s5_disc_b_kernel(cr_ref, ci_ref, br_ref, bi_ref, obr_ref, obi_ref):
    cr, ci, br, bi = cr_ref[...], ci_ref[...], br_ref[...], bi_ref[...]
    obr_ref[...] = cr * br - ci * bi
    obi_ref[...] = cr * bi + ci * br


def _s5_discretise(lam_re, lam_im, log_step, b_re, b_im):
    depth, groups, nst = lam_re.shape
    rows = depth * groups
    shp = jax.ShapeDtypeStruct((rows, nst), F32)
    lbr, lbi, cr, ci = pl.pallas_call(
        _s5_disc_lambda_kernel, out_shape=(shp, shp, shp, shp), name="s5_disc_lambda",
    )(lam_re.reshape(rows, nst), lam_im.reshape(rows, nst), log_step.reshape(rows, 1))
    flat = rows * nst
    cg = b_re.shape[-1]
    blk = 1024
    col = pl.BlockSpec((blk, 1), lambda i: (i, 0))
    mat = pl.BlockSpec((blk, cg), lambda i: (i, 0))
    oshp = jax.ShapeDtypeStruct((flat, cg), F32)
    bbr, bbi = pl.pallas_call(
        _s5_disc_b_kernel, out_shape=(oshp, oshp), grid=(flat // blk,),
        in_specs=[col, col, mat, mat], out_specs=(mat, mat), name="s5_disc_b",
    )(cr.reshape(flat, 1), ci.reshape(flat, 1), b_re.reshape(flat, cg), b_im.reshape(flat, cg))
    full = (depth, groups, nst, cg)
    return (lbr.reshape(depth, groups * nst), lbi.reshape(depth, groups * nst),
            bbr.reshape(full), bbi.reshape(full))


def _block_diag_in(bb):
    depth, groups, nst, cg = bb.shape
    gpb = S5_BLOCK // cg
    nb = groups // gpb
    t = bb.reshape(depth, nb, gpb, nst, cg).transpose(0, 1, 2, 4, 3).astype(BF16)
    eye = jnp.eye(gpb, dtype=BF16)
    w = t[:, :, :, :, None, :] * eye[None, None, :, None, :, None]
    return w.reshape(depth, nb, S5_BLOCK, gpb * nst)


def _block_diag_out(cc):
    depth, groups, cg, nst = cc.shape
    gpb = S5_BLOCK // cg
    nb = groups // gpb
    t = cc.reshape(depth, nb, gpb, cg, nst).transpose(0, 1, 2, 4, 3).astype(BF16)
    eye = jnp.eye(gpb, dtype=BF16)
    w = t[:, :, :, :, None, :] * eye[None, None, :, None, :, None]
    return w.reshape(depth, nb, gpb * nst, S5_BLOCK)


def _s5_kernel(u_ref, wbr_ref, wbi_ref, wcr_ref, wci_ref, lam_ref, d_ref, wg_ref, bg_ref,
               x0r_ref, x0i_ref, o_ref, xr_ref, xi_ref, xs_ref, y_ref, st_ref, ut_ref, *, batch):
    c_idx = pl.program_id(0)
    tt = u_ref.shape[0]
    rows = tt * batch
    width = u_ref.shape[1] // batch
    n_slab = width // LANES
    nb, _, bcols = wbr_ref.shape
    ns = nb * bcols
    grp = st_ref.shape[1]
    per = grp // batch

    @pl.when(c_idx == 0)
    def _():
        st_ref[0] = jnp.concatenate([x0r_ref[...]] * per, axis=0)
        st_ref[1] = jnp.concatenate([x0i_ref[...]] * per, axis=0)

    for b in range(batch):
        for m in range(n_slab):
            ut_ref[m, pl.ds(b, tt, stride=batch), :] = u_ref[:, b * width + m * LANES:b * width + (m + 1) * LANES]
    u = jnp.concatenate([ut_ref[m] for m in range(n_slab)], axis=1)
    ub = u.astype(BF16)
    for j in range(nb):
        uj = ub[:, j * S5_BLOCK:(j + 1) * S5_BLOCK]
        xs_ref[:, j * bcols:(j + 1) * bcols] = _dot(uj, wbr_ref[j])
        xs_ref[:, ns + j * bcols:ns + (j + 1) * bcols] = _dot(uj, wbi_ref[j])

    second = lax.broadcasted_iota(jnp.int32, (grp, S5_SCAN_LANES), 0) >= batch
    for cb in range(ns // S5_SCAN_LANES):
        lo = cb * S5_SCAN_LANES
        re_l = slice(lo, lo + S5_SCAN_LANES)
        im_l = slice(ns + lo, ns + lo + S5_SCAN_LANES)
        lr = lam_ref[0:1, re_l]
        li = lam_ref[1:2, re_l]
        if per == 2:
            pr = jnp.where(second, lr * lr - li * li, lr)
            pi = jnp.where(second, 2.0 * lr * li, li)
            qr = jnp.where(second, lr, 0.0)
            qi = jnp.where(second, li, 0.0)
        else:
            pr, pi, qr, qi = lr, li, None, None

        def step(g, carry, pr=pr, pi=pi, qr=qr, qi=qi, re_l=re_l, im_l=im_l):
            xr, xi = carry
            r = pl.ds(pl.multiple_of(g * grp, grp), grp)
            br = xs_ref[r, re_l]
            bi = xs_ref[r, im_l]
            if per == 2:
                sr = pltpu.roll(br, batch, axis=0)
                si = pltpu.roll(bi, batch, axis=0)
                br = br + (qr * sr - qi * si)
                bi = bi + (qr * si + qi * sr)
                xr = jnp.where(second, xr, pltpu.roll(xr, batch, axis=0))
                xi = jnp.where(second, xi, pltpu.roll(xi, batch, axis=0))
            nr = pr * xr - pi * xi + br
            ni = pr * xi + pi * xr + bi
            xs_ref[r, re_l] = nr
            xs_ref[r, im_l] = ni
            return nr, ni

        xr, xi = lax.fori_loop(0, rows // grp, step, (st_ref[0, :, re_l], st_ref[1, :, re_l]), unroll=2)
        st_ref[0, :, re_l] = xr
        st_ref[1, :, re_l] = xi

    for j in range(nb):
        xr_j = xs_ref[:, j * bcols:(j + 1) * bcols].astype(BF16)
        xi_j = xs_ref[:, ns + j * bcols:ns + (j + 1) * bcols].astype(BF16)
        y_ref[:, j * S5_BLOCK:(j + 1) * S5_BLOCK] = _dot(xr_j, wcr_ref[j]) - _dot(xi_j, wci_ref[j])

    y = y_ref[...] + d_ref[...] * u
    hh = jax.nn.gelu(y)
    gate = _sigmoid(_dot(hh.astype(BF16), wg_ref[...]) + bg_ref[...])
    out = hh * gate
    for m in range(n_slab):
        ut_ref[m] = out[:, m * LANES:(m + 1) * LANES]
    for b in range(batch):
        for m in range(n_slab):
            o_ref[:, b * width + m * LANES:b * width + (m + 1) * LANES] = (
                ut_ref[m, pl.ds(b, tt, stride=batch), :].astype(BF16))

    @pl.when(c_idx == pl.num_programs(0) - 1)
    def _():
        xr_ref[...] = st_ref[0, grp - batch:grp, :]
        xi_ref[...] = st_ref[1, grp - batch:grp, :]


def _s5(u, wbr, wbi, wcr, wci, lam, d_skip, w_glu, b_glu, x0r, x0i, *, batch, layer):
    seq = u.shape[0]
    width = u.shape[1] // batch
    n = seq * batch
    _, nb, _, bcols = wbr.shape
    ns = nb * bcols
    assert batch % SUBLANES == 0 or 2 * batch == SUBLANES
    grp = max(batch, SUBLANES)
    rows = min(256, n)
    tt = rows // batch
    lay4 = lambda c: (layer, 0, 0, 0)
    lay3 = lambda c: (layer, 0, 0)
    st_in = pl.BlockSpec((None, batch, ns), lay3)
    need = (2 * rows * width * 4 + 2 * 4 * nb * S5_BLOCK * bcols * 2 + 2 * width * width * 2
            + 2 * rows * width * 2 + rows * 2 * ns * 4 + rows * width * 4 + 4 * rows * width * 4
            + (8 * batch + 2 * grp) * ns * 4)
    st_shape = jax.ShapeDtypeStruct((batch, ns), F32)
    return pl.pallas_call(
        functools.partial(_s5_kernel, batch=batch),
        out_shape=(jax.ShapeDtypeStruct((seq, batch * width), BF16), st_shape, st_shape),
        grid=(n // rows,),
        in_specs=[
            pl.BlockSpec((tt, batch * width), lambda c: (c, 0)),
            pl.BlockSpec((None,) + wbr.shape[1:], lay4),
            pl.BlockSpec((None,) + wbi.shape[1:], lay4),
            pl.BlockSpec((None,) + wcr.shape[1:], lay4),
            pl.BlockSpec((None,) + wci.shape[1:], lay4),
            pl.BlockSpec((None, 2, ns), lay3),
            pl.BlockSpec((None, 1, width), lay3),
            pl.BlockSpec((None, width, width), lay3),
            pl.BlockSpec((None, 1, width), lay3),
            st_in,
            st_in,
        ],
        out_specs=(
            pl.BlockSpec((tt, batch * width), lambda c: (c, 0)),
            pl.BlockSpec((batch, ns), lambda c: (0, 0)),
            pl.BlockSpec((batch, ns), lambda c: (0, 0)),
        ),
        scratch_shapes=[
            pltpu.VMEM((rows, 2 * ns), F32),
            pltpu.VMEM((rows, width), F32),
            pltpu.VMEM((2, grp, ns), F32),
            pltpu.VMEM((width // LANES, rows, LANES), F32),
        ],
        compiler_params=pltpu.CompilerParams(
            dimension_semantics=("arbitrary",),
            vmem_limit_bytes=_vmem_limit(need)),
        name="s5",
    )(u, wbr, wbi, wcr, wci, lam, d_skip, w_glu, b_glu, x0r, x0i)


def _outproj_kernel(x_ref, oh_ref, os_ref, wh_ref, ws_ref, o_ref):
    o_ref[...] = x_ref[...] + _dot(oh_ref[...], wh_ref[...]) + _dot(os_ref[...], ws_ref[...])


def _outproj(x, o_h, o_s, w_out, *, layer, tm, tn, seq):
    n, d = x.shape
    kh = o_h.shape[1]
    ks = w_out.shape[1] - kh
    assert kh == ks
    if o_s.shape[0] == n:
        os_map = lambda i, j: (i, 0)
    else:
        n_t = seq // tm
        os_map = lambda i, j: (i % n_t, i // n_t)
    need = 2 * (2 * tm * tn * 4 + tm * (kh + ks) * 2 + (kh + ks) * tn * 2) + 2 * tm * tn * 4
    return pl.pallas_call(
        _outproj_kernel,
        out_shape=jax.ShapeDtypeStruct((n, d), F32),
        grid=(n // tm, d // tn),
        in_specs=[
            pl.BlockSpec((tm, tn), lambda i, j: (i, j)),
            pl.BlockSpec((tm, kh), lambda i, j: (i, 0)),
            pl.BlockSpec((tm, ks), os_map),
            pl.BlockSpec((None, kh, tn), lambda i, j: (layer, 0, j)),
            pl.BlockSpec((None, ks, tn), lambda i, j: (layer, 1, j)),
        ],
        out_specs=pl.BlockSpec((tm, tn), lambda i, j: (i, j)),
        compiler_params=pltpu.CompilerParams(
            dimension_semantics=("parallel", "parallel"),
            vmem_limit_bytes=_vmem_limit(need)),
        name="outproj",
    )(x, o_h, o_s, w_out, w_out)


def _ffn_kernel(x_ref, g_ref, w1_ref, w2_ref, gf_ref, o_ref, h_ref, *, final_norm):
    f = pl.program_id(1)
    tm = x_ref.shape[0]

    @pl.when(f == 0)
    def _():
        g = g_ref[...]

        def body(i, carry):
            r = pl.ds(pl.multiple_of(i * NORM_ROWS, NORM_ROWS), NORM_ROWS)
            h_ref[r, :] = _rmsnorm(x_ref[r, :], g).astype(BF16)
            o_ref[r, :] = jnp.zeros((NORM_ROWS, o_ref.shape[1]), F32)
            return carry

        lax.fori_loop(0, tm // NORM_ROWS, body, 0)

    a = _dot(h_ref[...], w1_ref[...])
    a = jnp.square(jnp.maximum(a, 0.0)).astype(BF16)
    cw = o_ref.shape[1] // FFN_ACC_CHUNKS
    for c in range(FFN_ACC_CHUNKS):
        o_ref[:, c * cw:(c + 1) * cw] += _dot(a, w2_ref[:, c * cw:(c + 1) * cw])

    @pl.when(f == pl.num_programs(1) - 1)
    def _():
        gf = gf_ref[...]

        def body(i, carry):
            r = pl.ds(pl.multiple_of(i * NORM_ROWS, NORM_ROWS), NORM_ROWS)
            y = x_ref[r, :] + o_ref[r, :]
            if final_norm:
                y = _rmsnorm(y, gf)
            o_ref[r, :] = y
            return carry

        lax.fori_loop(0, tm // NORM_ROWS, body, 0)


def _ffn(x, g, w1, w2, gf, *, layer, tm, tf, final_norm):
    n, d = x.shape
    ff = w1.shape[2]
    need = tm * d * 4 + tm * d * 2 + 2 * tm * d * 4 + 4 * d * tf * 2 + tm * tf * 6 + tm * d * 4
    kernel = functools.partial(_ffn_kernel, final_norm=final_norm)
    return pl.pallas_call(
        kernel,
        out_shape=jax.ShapeDtypeStruct((n, d), F32),
        grid=(n // tm, ff // tf),
        in_specs=[
            pl.BlockSpec((tm, d), lambda i, f: (i, 0), pipeline_mode=pl.Buffered(1)),
            pl.BlockSpec((None, 1, d), lambda i, f: (layer, 0, 0)),
            pl.BlockSpec((None, d, tf), lambda i, f: (layer, 0, f)),
            pl.BlockSpec((None, tf, d), lambda i, f: (layer, f, 0)),
            pl.BlockSpec((1, d), lambda i, f: (0, 0)),
        ],
        out_specs=pl.BlockSpec((tm, d), lambda i, f: (i, 0)),
        scratch_shapes=[pltpu.VMEM((tm, d), BF16)],
        compiler_params=pltpu.CompilerParams(
            dimension_semantics=("parallel", "arbitrary"),
            vmem_limit_bytes=_vmem_limit(need)),
        name="ffn",
    )(x, g, w1, w2, gf)


def _trunk(x, st_h, st_r, st_i, wts):
    batch, seq, d = x.shape
    n = batch * seq
    depth = wts["w_in"].shape[0]
    heads = st_h.shape[2]
    hgrn_width = heads * HEAD_DIM
    groups, nst = st_r.shape[2], st_r.shape[3]
    s5_width = wts["w_glu"].shape[1]
    tm = min(1024, n)
    xf = x.reshape(n, d)
    x0r = st_r.reshape(depth, batch, groups * nst)
    x0i = st_i.reshape(depth, batch, groups * nst)
    time_major = seq % tm == 0
    new_h, new_r, new_i = [], [], []
    for l in range(depth):
        p, u = _inproj(xf, wts["norm1_g"], wts["w_in"], layer=l, tm=tm, tn=s5_width, batch=batch, seq=seq)
        o_h, s_new = _hgrn(p, wts["lb_logits"], wts["onorm_g"], st_h, batch=batch, seq=seq, layer=l)
        if not time_major:
            u = u.reshape(batch, seq, s5_width).transpose(1, 0, 2).reshape(seq, batch * s5_width)
        o_s, xr, xi = _s5(u, wts["wbr"], wts["wbi"], wts["wcr"], wts["wci"], wts["lam"],
                          wts["s5_D"], wts["w_glu"], wts["b_glu"], x0r, x0i, batch=batch, layer=l)
        if not time_major:
            o_s = o_s.reshape(seq, batch, s5_width).transpose(1, 0, 2).reshape(n, s5_width)
        x1 = _outproj(xf, o_h, o_s, wts["w_out"], layer=l, tm=min(512, n), tn=d, seq=seq)
        xf = _ffn(x1, wts["norm2_g"], wts["w_ff1"], wts["w_ff2"], wts["final_g"],
                  layer=l, tm=tm, tf=512, final_norm=(l == depth - 1))
        new_h.append(s_new)
        new_r.append(xr.reshape(batch, groups, nst))
        new_i.append(xi.reshape(batch, groups, nst))
    return xf.reshape(batch, seq, d), jnp.stack(new_h), jnp.stack(new_r), jnp.stack(new_i)


def kernel(x_prompt, x_sample, state_hgrn, state_s5_re, state_s5_im, norm1_g, w_in, hgrn_lb_logits,
           hgrn_onorm_g, s5_lambda_re, s5_lambda_im, s5_log_step, s5_B_re, s5_B_im, s5_C_re, s5_C_im,
           s5_D, s5_w_glu, s5_b_glu, w_out, norm2_g, w_ff1, w_ff2, final_norm_g):
    depth, d = norm1_g.shape
    row3 = lambda a: a.reshape(depth, 1, a.shape[-1])
    lbr, lbi, bbr, bbi = _s5_discretise(s5_lambda_re, s5_lambda_im, s5_log_step, s5_B_re, s5_B_im)
    wts = {
        "norm1_g": row3(norm1_g), "norm2_g": row3(norm2_g), "final_g": final_norm_g.reshape(1, d),
        "lb_logits": hgrn_lb_logits, "onorm_g": row3(hgrn_onorm_g),
        "w_in": w_in.astype(BF16), "w_out": w_out.astype(BF16),
        "w_ff1": w_ff1.astype(BF16), "w_ff2": w_ff2.astype(BF16),
        "w_glu": s5_w_glu.astype(BF16), "b_glu": row3(s5_b_glu), "s5_D": row3(s5_D),
        "lam": jnp.stack([lbr, lbi], axis=1),
        "wbr": _block_diag_in(bbr), "wbi": _block_diag_in(bbi),
        "wcr": _block_diag_out(s5_C_re), "wci": _block_diag_out(s5_C_im),
    }
    bp = x_prompt.shape[0]
    zh = jnp.zeros((depth, bp) + state_hgrn.shape[2:], F32)
    zs = jnp.zeros((depth, bp) + state_s5_re.shape[2:], F32)
    y_p, hp, rp, ip = _trunk(x_prompt, zh, zs, zs, wts)
    y_s, hs, rs, is_ = _trunk(x_sample, state_hgrn, state_s5_re, state_s5_im, wts)
    return (y_p, y_s, hp, rp, ip, hs, rs, is_)
```

```python
import functools

import jax
import jax.numpy as jnp
from jax import lax
from jax.experimental import pallas as pl
from jax.experimental.pallas import tpu as pltpu

F32 = jnp.float32
BF16 = jnp.bfloat16
EPS = 1e-6

HEAD_DIM = 128
HGRN_CHUNK = 64
HGRN_SUB = 16
HGRN_HEADS_PER_STEP = 4
SUBLANES = 8
LANES = 128
S5_GROUP = 16
S5_STATE = 64
S5_BLOCK = 256
S5_SCAN_LANES = 256
S5_ROWS = 512

V7X_VMEM_CAP = 56 * 1024 * 1024
NORM_ROWS = 64
FFN_ACC_CHUNKS = 4


def _vmem_limit(nbytes):
    return int(min(V7X_VMEM_CAP, nbytes * 5 // 4 + (4 << 20)))


def _rmsnorm(x, g):
    return x * lax.rsqrt(jnp.mean(jnp.square(x), axis=-1, keepdims=True) + EPS) * g


def _sigmoid(x):
    return 1.0 / (1.0 + jnp.exp(-x))


def _log1p_exp_neg_abs(x):
    return jnp.log(1.0 + jnp.exp(-jnp.abs(x)))


def _softplus(x):
    return jnp.maximum(x, 0.0) + _log1p_exp_neg_abs(x)


def _dot(a, b):
    return jnp.dot(a, b, preferred_element_type=F32)


def _dot_nt(a, b):
    return lax.dot_general(a, b, (((1,), (1,)), ((), ())), preferred_element_type=F32)


def _inproj_kernel(x_ref, g_ref, w_ref, p_ref, u_ref, h_ref):
    tm = x_ref.shape[0]
    j = pl.program_id(1)
    last = pl.num_programs(1) - 1

    @pl.when(j == 0)
    def _():
        g = g_ref[...]

        def body(i, carry):
            r = pl.ds(pl.multiple_of(i * NORM_ROWS, NORM_ROWS), NORM_ROWS)
            h_ref[r, :] = _rmsnorm(x_ref[r, :], g).astype(BF16)
            return carry

        lax.fori_loop(0, tm // NORM_ROWS, body, 0)

    @pl.when(j < last)
    def _():
        p_ref[...] = _dot(h_ref[...], w_ref[...])

    @pl.when(j == last)
    def _():
        u_ref[...] = _dot(h_ref[...], w_ref[...])


def _inproj(x, g, w, *, layer, tm, tn, batch, seq):
    n, d = x.shape
    cols = w.shape[2]
    n_j = cols // tn
    time_major = seq % tm == 0
    if time_major:
        n_t = seq // tm
        u_shape, u_map = (seq, batch * tn), (lambda i, j: (i % n_t, i // n_t))
    else:
        u_shape, u_map = (n, tn), (lambda i, j: (i, 0))
    need = 2 * tm * d * 4 + tm * d * 2 + 2 * d * tn * 2 + 5 * tm * tn * 4
    return pl.pallas_call(
        _inproj_kernel,
        out_shape=(jax.ShapeDtypeStruct((n, cols - tn), F32), jax.ShapeDtypeStruct(u_shape, F32)),
        grid=(n // tm, n_j),
        in_specs=[
            pl.BlockSpec((tm, d), lambda i, j: (i, 0)),
            pl.BlockSpec((None, 1, d), lambda i, j: (layer, 0, 0)),
            pl.BlockSpec((None, d, tn), lambda i, j: (layer, 0, j)),
        ],
        out_specs=(pl.BlockSpec((tm, tn), lambda i, j: (i, jnp.minimum(j, n_j - 2))),
                   pl.BlockSpec((tm, tn), u_map)),
        scratch_shapes=[pltpu.VMEM((tm, d), BF16)],
        compiler_params=pltpu.CompilerParams(
            dimension_semantics=("parallel", "arbitrary"),
            vmem_limit_bytes=_vmem_limit(need)),
        name="inproj",
    )(x, g, w)


def _hgrn_decayed_queries(q_lo, q_hi, f_row, row_is):
    slabs = []
    qd_lo = qd_hi = None
    for s in range(2 * SUBLANES - 1, -1, -1):
        if s == 2 * SUBLANES - 1:
            qd_hi = jnp.where(row_is[SUBLANES - 1], q_hi, 0.0)
        elif s >= SUBLANES:
            qd_hi = jnp.where(row_is[s - SUBLANES], q_hi, qd_hi * f_row(s + 1))
        else:
            fn = f_row(s + 1)
            qd_hi = qd_hi * fn
            qd_lo = jnp.where(row_is[s], q_lo, 0.0 if s == SUBLANES - 1 else qd_lo * fn)
        slabs.append((s, 1, qd_hi))
        if s < SUBLANES:
            slabs.append((s, 0, qd_lo))
    return slabs


def _hgrn_kernel(q_ref, f_ref, i_ref, g_ref, lbl_ref, on_ref, s0_ref, o_ref, sout_ref,
                 st_ref, qs_ref, ks_ref, cs_ref, fs_ref, fc_ref, op_ref, *, layer, chunk, heads_blk):
    c_idx = pl.program_id(2)
    tc = q_ref.shape[0]
    n_sub = chunk // HGRN_SUB

    @pl.when(c_idx == 0)
    def _():
        for h in range(heads_blk):
            st_ref[h] = s0_ref[0, h].T

    logits = lbl_ref[...]
    e = jnp.exp(logits - jnp.max(logits, axis=0, keepdims=True))
    sm = e / jnp.sum(e, axis=0, keepdims=True)
    cum0 = sm[0:1]
    cuml = cum0
    for j in range(1, layer + 1):
        cuml = cuml + sm[j:j + 1]
    lb = cuml - cum0
    log_lb = jnp.log(lb)
    log_1m = jnp.log1p(-lb)

    z = f_ref[...]
    sp_neg = _softplus(-z)
    b = log_1m - sp_neg
    delta = log_lb - b
    logf = jnp.where(jnp.isnan(delta), log_lb + b,
                     jnp.maximum(log_lb, b) + _log1p_exp_neg_abs(delta))
    fs_ref[...] = jnp.exp(logf)
    ks_ref[...] = jnp.exp(log_1m - (sp_neg + z))
    qr = q_ref[...]
    qs_ref[...] = qr * _sigmoid(qr)

    t_i = lax.broadcasted_iota(jnp.int32, (chunk, chunk), 0)
    s_i = lax.broadcasted_iota(jnp.int32, (chunk, chunk), 1)
    tri = jnp.where(t_i >= s_i, 1.0, 0.0).astype(BF16)
    for j in range(tc // chunk):
        lf = logf[j * chunk:(j + 1) * chunk]
        hi = lf.astype(BF16)
        r1 = lf - hi.astype(F32)
        mid = r1.astype(BF16)
        lo = (r1 - mid.astype(F32)).astype(BF16)
        cs_ref[j * chunk:(j + 1) * chunk, :] = _dot(tri, hi) + _dot(tri, mid) + _dot(tri, lo)

    row8 = lax.broadcasted_iota(jnp.int32, (SUBLANES, HEAD_DIM), 0)
    row_is = [row8 == j for j in range(SUBLANES)]
    lane8 = lax.broadcasted_iota(jnp.int32, (SUBLANES, chunk), 1)
    lane_is = [lane8 == j for j in range(chunk)]
    onorm = on_ref[...]
    blk = lambda a, j: a[j * HGRN_SUB:(j + 1) * HGRN_SUB]

    head_cols = [slice(h * HEAD_DIM, (h + 1) * HEAD_DIM) for h in range(heads_blk)]

    def finish(rows):
        for cols in head_cols:
            o = op_ref[:, cols]
            o = o * lax.rsqrt(jnp.mean(jnp.square(o), axis=-1, keepdims=True) + EPS)
            o = o * onorm[:, cols]
            gr = g_ref[rows, cols]
            o_ref[rows, cols] = (o * (gr * _sigmoid(gr))).astype(BF16)

    op_ref[...] = jnp.zeros(op_ref.shape, F32)

    def chunk_body(c, carry):
        base = pl.multiple_of(c * chunk, chunk)
        r = pl.ds(base, chunk)
        finish(pl.ds(pl.multiple_of(jnp.maximum(c - 1, 0) * chunk, chunk), chunk))
        fc_ref[...] = fs_ref[r, :]
        cum_all = cs_ref[r, :]

        zero_blk = jnp.zeros((HGRN_SUB, HEAD_DIM), F32)
        qs, vbs, kbs, o_inter, a_off = [], [], [], [], []
        for cols in head_cols:
            h = len(qs)
            cum = cum_all[:, cols]
            q = qs_ref[r, cols]
            k = ks_ref[r, cols]
            v = i_ref[r, cols]
            ends = [cum[(j + 1) * HGRN_SUB - 1:(j + 1) * HGRN_SUB, :] for j in range(n_sub)]
            cl = ends[-1]
            ke = [blk(k, j) * jnp.exp(ends[j] - blk(cum, j)) for j in range(n_sub)]
            qe = [blk(q, j) * jnp.exp(blk(cum, j) - ends[j - 1] if j else blk(cum, j)) for j in range(n_sub)]

            st = st_ref[h]
            q_in = jnp.concatenate([qe[j] * jnp.exp(ends[j - 1]) if j else qe[j] for j in range(n_sub)], axis=0)
            o_inter.append(_dot_nt(q_in.astype(BF16), st.astype(BF16)))
            kdec = jnp.concatenate(
                [ke[j] * jnp.exp(cl - ends[j]) if j < n_sub - 1 else ke[j] for j in range(n_sub)], axis=0)
            st_ref[h] = st * jnp.exp(cl) + _dot(v.T.astype(BF16), kdec.astype(BF16))

            offs = [None]
            for i in range(1, n_sub):
                kt = jnp.concatenate(
                    [ke[j] * jnp.exp(ends[i - 1] - ends[j]) if j < i - 1 else ke[j] for j in range(i)]
                    + [zero_blk] * (n_sub - i), axis=0)
                offs.append(_dot_nt(qe[i].astype(BF16), kt.astype(BF16)))
            a_off.append(offs)
            qs.append(q)
            vbs.append(v.astype(BF16))
            kbs.append(k.astype(BF16))

        res = []
        for h, cols in enumerate(head_cols):
            per_head = []
            for i in range(n_sub):
                lo_r = i * HGRN_SUB
                f_row = lambda s, lo_r=lo_r, cols=cols: fc_ref[lo_r + s:lo_r + s + 1, cols]
                q_blk = blk(qs[h], i)
                slabs = _hgrn_decayed_queries(q_blk[:SUBLANES], q_blk[SUBLANES:], f_row, row_is)
                stack = jnp.concatenate([slab for _, _, slab in slabs], axis=0)
                per_head.append(([(s, half) for s, half, _ in slabs],
                                 _dot_nt(stack.astype(BF16), kbs[h])))
            res.append(per_head)

        o_intra = []
        for h in range(heads_blk):
            rows = []
            for i in range(n_sub):
                order, prod = res[h][i]
                halves = [jnp.zeros((SUBLANES, chunk), F32), jnp.zeros((SUBLANES, chunk), F32)]
                for idx, (s, half) in enumerate(order):
                    piece = prod[idx * SUBLANES:(idx + 1) * SUBLANES]
                    halves[half] = jnp.where(lane_is[i * HGRN_SUB + s], piece, halves[half])
                a = jnp.concatenate(halves, axis=0)
                rows.append(a + a_off[h][i] if i else a)
            attn = jnp.concatenate(rows, axis=0)
            o_intra.append(_dot(attn.astype(BF16), vbs[h]))

        for h, cols in enumerate(head_cols):
            op_ref[:, cols] = o_inter[h] + o_intra[h]
        return carry

    lax.fori_loop(0, tc // chunk, chunk_body, 0, unroll=min(2, tc // chunk))
    finish(pl.ds(tc - chunk, chunk))

    @pl.when(c_idx == pl.num_programs(2) - 1)
    def _():
        for h in range(heads_blk):
            sout_ref[0, h] = st_ref[h].T


def _hgrn(p, lb_logits, onorm_g, s0, *, batch, seq, layer):
    n = p.shape[0]
    heads = s0.shape[2]
    hb = HGRN_HEADS_PER_STEP
    wblk = hb * HEAD_DIM
    n_hb = heads // hb
    width = heads * HEAD_DIM
    depth = lb_logits.shape[0]
    tc = min(512, seq)
    chunk = min(HGRN_CHUNK, seq)
    n_t = seq // tc
    row = lambda b, h, c: b * n_t + c
    need = 2 * 4 * tc * wblk * 4 + 2 * tc * wblk * 2 + 4 * tc * wblk * 4 + 5 * hb * HEAD_DIM * HEAD_DIM * 4
    kernel = functools.partial(_hgrn_kernel, layer=layer, chunk=chunk, heads_blk=hb)
    st_spec = pl.BlockSpec((None, 1, hb, HEAD_DIM, HEAD_DIM), lambda b, h, c: (layer, b, h, 0, 0))
    return pl.pallas_call(
        kernel,
        out_shape=(jax.ShapeDtypeStruct((n, width), BF16),
                   jax.ShapeDtypeStruct((batch, heads, HEAD_DIM, HEAD_DIM), F32)),
        grid=(batch, n_hb, n_t),
        in_specs=[
            pl.BlockSpec((tc, wblk), lambda b, h, c: (row(b, h, c), h)),
            pl.BlockSpec((tc, wblk), lambda b, h, c: (row(b, h, c), n_hb + h)),
            pl.BlockSpec((tc, wblk), lambda b, h, c: (row(b, h, c), 2 * n_hb + h)),
            pl.BlockSpec((tc, wblk), lambda b, h, c: (row(b, h, c), 3 * n_hb + h)),
            pl.BlockSpec((depth, wblk), lambda b, h, c: (0, h)),
            pl.BlockSpec((None, 1, wblk), lambda b, h, c: (layer, 0, h)),
            st_spec,
        ],
        out_specs=(
            pl.BlockSpec((tc, wblk), lambda b, h, c: (row(b, h, c), h)),
            pl.BlockSpec((1, hb, HEAD_DIM, HEAD_DIM), lambda b, h, c: (b, h, 0, 0)),
        ),
        scratch_shapes=[
            pltpu.VMEM((hb, HEAD_DIM, HEAD_DIM), F32),
            pltpu.VMEM((tc, wblk), F32),
            pltpu.VMEM((tc, wblk), F32),
            pltpu.VMEM((tc, wblk), F32),
            pltpu.VMEM((tc, wblk), F32),
            pltpu.VMEM((chunk, wblk), F32),
            pltpu.VMEM((chunk, wblk), F32),
        ],
        compiler_params=pltpu.CompilerParams(
            dimension_semantics=("parallel", "parallel", "arbitrary"),
            vmem_limit_bytes=_vmem_limit(need)),
        name="hgrn",
    )(p, p, p, p, lb_logits, onorm_g, s0)


def _s5_disc_lambda_kernel(lr_ref, li_ref, ls_ref, lbr_ref, lbi_ref, cr_ref, ci_ref):
    lr, li = lr_ref[...], li_ref[...]
    dt = jnp.exp(ls_ref[...])
    mag = jnp.exp(dt * lr)
    ang = dt * li
    lbr = mag * jnp.cos(ang)
    lbi = mag * jnp.sin(ang)
    nr, ni = lbr - 1.0, lbi
    den = lr * lr + li * li
    lbr_ref[...] = lbr
    lbi_ref[...] = lbi
    cr_ref[...] = (nr * lr + ni * li) / den
    ci_ref[...] = (ni * lr - nr * li) / den


def _s5_disc_b_kernel(cr_ref, ci_ref, br_ref, bi_ref, obr_ref, obi_ref):
    cr, ci, br, bi = cr_ref[...], ci_ref[...], br_ref[...], bi_ref[...]
    obr_ref[...] = cr * br - ci * bi
    obi_ref[...] = cr * bi + ci * br


def _s5_discretise(lam_re, lam_im, log_step, b_re, b_im):
    depth, groups, nst = lam_re.shape
    rows = depth * groups
    shp = jax.ShapeDtypeStruct((rows, nst), F32)
    lbr, lbi, cr, ci = pl.pallas_call(
        _s5_disc_lambda_kernel, out_shape=(shp, shp, shp, shp), name="s5_disc_lambda",
    )(lam_re.reshape(rows, nst), lam_im.reshape(rows, nst), log_step.reshape(rows, 1))
    flat = rows * nst
    cg = b_re.shape[-1]
    blk = 1024
    col = pl.BlockSpec((blk, 1), lambda i: (i, 0))
    mat = pl.BlockSpec((blk, cg), lambda i: (i, 0))
    oshp = jax.ShapeDtypeStruct((flat, cg), F32)
    bbr, bbi = pl.pallas_call(
        _s5_disc_b_kernel, out_shape=(oshp, oshp), grid=(flat // blk,),
        in_specs=[col, col, mat, mat], out_specs=(mat, mat), name="s5_disc_b",
    )(cr.reshape(flat, 1), ci.reshape(flat, 1), b_re.reshape(flat, cg), b_im.reshape(flat, cg))
    full = (depth, groups, nst, cg)
    return (lbr.reshape(depth, groups * nst), lbi.reshape(depth, groups * nst),
            bbr.reshape(full), bbi.reshape(full))


def _block_diag_in(bb):
    depth, groups, nst, cg = bb.shape
    gpb = S5_BLOCK // cg
    nb = groups // gpb
    t = bb.reshape(depth, nb, gpb, nst, cg).transpose(0, 1, 2, 4, 3).astype(BF16)
    eye = jnp.eye(gpb, dtype=BF16)
    w = t[:, :, :, :, None, :] * eye[None, None, :, None, :, None]
    return w.reshape(depth, nb, S5_BLOCK, gpb * nst)


def _block_diag_out(cc):
    depth, groups, cg, nst = cc.shape
    gpb = S5_BLOCK // cg
    nb = groups // gpb
    t = cc.reshape(depth, nb, gpb, cg, nst).transpose(0, 1, 2, 4, 3).astype(BF16)
    eye = jnp.eye(gpb, dtype=BF16)
    w = t[:, :, :, :, None, :] * eye[None, None, :, None, :, None]
    return w.reshape(depth, nb, gpb * nst, S5_BLOCK)


def _s5_kernel(u_ref, wbr_ref, wbi_ref, wcr_ref, wci_ref, lam_ref, d_ref, wg_ref, bg_ref,
               x0r_ref, x0i_ref, o_ref, xr_ref, xi_ref, xs_ref, y_ref, st_ref, ut_ref, *, batch):
    c_idx = pl.program_id(0)
    tt = u_ref.shape[0]
    rows = tt * batch
    width = u_ref.shape[1] // batch
    n_slab = width // LANES
    nb, _, bcols = wbr_ref.shape
    ns = nb * bcols
    grp = st_ref.shape[1]
    per = grp // batch

    @pl.when(c_idx == 0)
    def _():
        st_ref[0] = jnp.concatenate([x0r_ref[...]] * per, axis=0)
        st_ref[1] = jnp.concatenate([x0i_ref[...]] * per, axis=0)

    for b in range(batch):
        for m in range(n_slab):
            ut_ref[m, pl.ds(b, tt, stride=batch), :] = u_ref[:, b * width + m * LANES:b * width + (m + 1) * LANES]
    u = jnp.concatenate([ut_ref[m] for m in range(n_slab)], axis=1)
    ub = u.astype(BF16)
    for j in range(nb):
        uj = ub[:, j * S5_BLOCK:(j + 1) * S5_BLOCK]
        xs_ref[:, j * bcols:(j + 1) * bcols] = _dot(uj, wbr_ref[j])
        xs_ref[:, ns + j * bcols:ns + (j + 1) * bcols] = _dot(uj, wbi_ref[j])

    second = lax.broadcasted_iota(jnp.int32, (grp, S5_SCAN_LANES), 0) >= batch
    for cb in range(ns // S5_SCAN_LANES):
        lo = cb * S5_SCAN_LANES
        re_l = slice(lo, lo + S5_SCAN_LANES)
        im_l = slice(ns + lo, ns + lo + S5_SCAN_LANES)
        lr = lam_ref[0:1, re_l]
        li = lam_ref[1:2, re_l]
        if per == 2:
            pr = jnp.where(second, lr * lr - li * li, lr)
            pi = jnp.where(second, 2.0 * lr * li, li)
            qr = jnp.where(second, lr, 0.0)
            qi = jnp.where(second, li, 0.0)
        else:
            pr, pi, qr, qi = lr, li, None, None

        def step(g, carry, pr=pr, pi=pi, qr=qr, qi=qi, re_l=re_l, im_l=im_l):
            xr, xi = carry
            r = pl.ds(pl.multiple_of(g * grp, grp), grp)
            br = xs_ref[r, re_l]
            bi = xs_ref[r, im_l]
            if per == 2:
                sr = pltpu.roll(br, batch, axis=0)
                si = pltpu.roll(bi, batch, axis=0)
                br = br + (qr * sr - qi * si)
                bi = bi + (qr * si + qi * sr)
                xr = jnp.where(second, xr, pltpu.roll(xr, batch, axis=0))
                xi = jnp.where(second, xi, pltpu.roll(xi, batch, axis=0))
            nr = pr * xr - pi * xi + br
            ni = pr * xi + pi * xr + bi
            xs_ref[r, re_l] = nr
            xs_ref[r, im_l] = ni
            return nr, ni

        xr, xi = lax.fori_loop(0, rows // grp, step, (st_ref[0, :, re_l], st_ref[1, :, re_l]), unroll=2)
        st_ref[0, :, re_l] = xr
        st_ref[1, :, re_l] = xi

    for j in range(nb):
        xr_j = xs_ref[:, j * bcols:(j + 1) * bcols].astype(BF16)
        xi_j = xs_ref[:, ns + j * bcols:ns + (j + 1) * bcols].astype(BF16)
        y_ref[:, j * S5_BLOCK:(j + 1) * S5_BLOCK] = _dot(xr_j, wcr_ref[j]) - _dot(xi_j, wci_ref[j])

    y = y_ref[...] + d_ref[...] * u
    hh = jax.nn.gelu(y)
    gate = _sigmoid(_dot(hh.astype(BF16), wg_ref[...]) + bg_ref[...])
    out = hh * gate
    for m in range(n_slab):
        ut_ref[m] = out[:, m * LANES:(m + 1) * LANES]
    for b in range(batch):
        for m in range(n_slab):
            o_ref[:, b * width + m * LANES:b * width + (m + 1) * LANES] = (
                ut_ref[m, pl.ds(b, tt, stride=batch), :].astype(BF16))

    @pl.when(c_idx == pl.num_programs(0) - 1)
    def _():
        xr_ref[...] = st_ref[0, grp - batch:grp, :]
        xi_ref[...] = st_ref[1, grp - batch:grp, :]


def _s5(u, wbr, wbi, wcr, wci, lam, d_skip, w_glu, b_glu, x0r, x0i, *, batch, layer):
    seq = u.shape[0]
    width = u.shape[1] // batch
    n = seq * batch
    _, nb, _, bcols = wbr.shape
    ns = nb * bcols
    assert batch % SUBLANES == 0 or 2 * batch == SUBLANES
    grp = max(batch, SUBLANES)
    rows = min(S5_ROWS, n)
    tt = rows // batch
    lay4 = lambda c: (layer, 0, 0, 0)
    lay3 = lambda c: (layer, 0, 0)
    st_in = pl.BlockSpec((None, batch, ns), lay3)
    resident = pl.Buffered(1)
    need = (2 * rows * width * 4 + 4 * nb * S5_BLOCK * bcols * 2 + width * width * 2
            + 2 * rows * width * 2 + rows * 2 * ns * 4 + rows * width * 4 + 4 * rows * width * 4
            + (8 * batch + 2 * grp) * ns * 4)
    st_shape = jax.ShapeDtypeStruct((batch, ns), F32)
    return pl.pallas_call(
        functools.partial(_s5_kernel, batch=batch),
        out_shape=(jax.ShapeDtypeStruct((seq, batch * width), BF16), st_shape, st_shape),
        grid=(n // rows,),
        in_specs=[
            pl.BlockSpec((tt, batch * width), lambda c: (c, 0)),
            pl.BlockSpec((None,) + wbr.shape[1:], lay4, pipeline_mode=resident),
            pl.BlockSpec((None,) + wbi.shape[1:], lay4, pipeline_mode=resident),
            pl.BlockSpec((None,) + wcr.shape[1:], lay4, pipeline_mode=resident),
            pl.BlockSpec((None,) + wci.shape[1:], lay4, pipeline_mode=resident),
            pl.BlockSpec((None, 2, ns), lay3),
            pl.BlockSpec((None, 1, width), lay3),
            pl.BlockSpec((None, width, width), lay3, pipeline_mode=resident),
            pl.BlockSpec((None, 1, width), lay3),
            st_in,
            st_in,
        ],
        out_specs=(
            pl.BlockSpec((tt, batch * width), lambda c: (c, 0)),
            pl.BlockSpec((batch, ns), lambda c: (0, 0)),
            pl.BlockSpec((batch, ns), lambda c: (0, 0)),
        ),
        scratch_shapes=[
            pltpu.VMEM((rows, 2 * ns), F32),
            pltpu.VMEM((rows, width), F32),
            pltpu.VMEM((2, grp, ns), F32),
            pltpu.VMEM((width // LANES, rows, LANES), F32),
        ],
        compiler_params=pltpu.CompilerParams(
            dimension_semantics=("arbitrary",),
            vmem_limit_bytes=_vmem_limit(need)),
        name="s5",
    )(u, wbr, wbi, wcr, wci, lam, d_skip, w_glu, b_glu, x0r, x0i)


def _outproj_kernel(x_ref, oh_ref, os_ref, wh_ref, ws_ref, o_ref):
    o_ref[...] = x_ref[...] + _dot(oh_ref[...], wh_ref[...]) + _dot(os_ref[...], ws_ref[...])


def _outproj(x, o_h, o_s, w_out, *, layer, tm, tn, seq):
    n, d = x.shape
    kh = o_h.shape[1]
    ks = w_out.shape[1] - kh
    assert kh == ks
    if o_s.shape[0] == n:
        os_map = lambda i, j: (i, 0)
    else:
        n_t = seq // tm
        os_map = lambda i, j: (i % n_t, i // n_t)
    need = 2 * (2 * tm * tn * 4 + tm * (kh + ks) * 2 + (kh + ks) * tn * 2) + 2 * tm * tn * 4
    return pl.pallas_call(
        _outproj_kernel,
        out_shape=jax.ShapeDtypeStruct((n, d), F32),
        grid=(n // tm, d // tn),
        in_specs=[
            pl.BlockSpec((tm, tn), lambda i, j: (i, j)),
            pl.BlockSpec((tm, kh), lambda i, j: (i, 0)),
            pl.BlockSpec((tm, ks), os_map),
            pl.BlockSpec((None, kh, tn), lambda i, j: (layer, 0, j)),
            pl.BlockSpec((None, ks, tn), lambda i, j: (layer, 1, j)),
        ],
        out_specs=pl.BlockSpec((tm, tn), lambda i, j: (i, j)),
        compiler_params=pltpu.CompilerParams(
            dimension_semantics=("parallel", "parallel"),
            vmem_limit_bytes=_vmem_limit(need)),
        name="outproj",
    )(x, o_h, o_s, w_out, w_out)


def _ffn_kernel(x_ref, g_ref, w1_ref, w2_ref, gf_ref, o_ref, h_ref, *, final_norm):
    f = pl.program_id(1)
    tm = x_ref.shape[0]

    @pl.when(f == 0)
    def _():
        g = g_ref[...]

        def body(i, carry):
            r = pl.ds(pl.multiple_of(i * NORM_ROWS, NORM_ROWS), NORM_ROWS)
            h_ref[r, :] = _rmsnorm(x_ref[r, :], g).astype(BF16)
            o_ref[r, :] = jnp.zeros((NORM_ROWS, o_ref.shape[1]), F32)
            return carry

        lax.fori_loop(0, tm // NORM_ROWS, body, 0)

    a = _dot(h_ref[...], w1_ref[...])
    a = jnp.square(jnp.maximum(a, 0.0)).astype(BF16)
    cw = o_ref.shape[1] // FFN_ACC_CHUNKS
    for c in range(FFN_ACC_CHUNKS):
        o_ref[:, c * cw:(c + 1) * cw] += _dot(a, w2_ref[:, c * cw:(c + 1) * cw])

    @pl.when(f == pl.num_programs(1) - 1)
    def _():
        gf = gf_ref[...]

        def body(i, carry):
            r = pl.ds(pl.multiple_of(i * NORM_ROWS, NORM_ROWS), NORM_ROWS)
            y = x_ref[r, :] + o_ref[r, :]
            if final_norm:
                y = _rmsnorm(y, gf)
            o_ref[r, :] = y
            return carry

        lax.fori_loop(0, tm // NORM_ROWS, body, 0)


def _ffn(x, g, w1, w2, gf, *, layer, tm, tf, final_norm):
    n, d = x.shape
    ff = w1.shape[2]
    need = tm * d * 4 + tm * d * 2 + 2 * tm * d * 4 + 4 * d * tf * 2 + tm * tf * 6 + tm * d * 4
    kernel = functools.partial(_ffn_kernel, final_norm=final_norm)
    return pl.pallas_call(
        kernel,
        out_shape=jax.ShapeDtypeStruct((n, d), F32),
        grid=(n // tm, ff // tf),
        in_specs=[
            pl.BlockSpec((tm, d), lambda i, f: (i, 0), pipeline_mode=pl.Buffered(1)),
            pl.BlockSpec((None, 1, d), lambda i, f: (layer, 0, 0)),
            pl.BlockSpec((None, d, tf), lambda i, f: (layer, 0, f)),
            pl.BlockSpec((None, tf, d), lambda i, f: (layer, f, 0)),
            pl.BlockSpec((1, d), lambda i, f: (0, 0)),
        ],
        out_specs=pl.BlockSpec((tm, d), lambda i, f: (i, 0)),
        scratch_shapes=[pltpu.VMEM((tm, d), BF16)],
        compiler_params=pltpu.CompilerParams(
            dimension_semantics=("parallel", "arbitrary"),
            vmem_limit_bytes=_vmem_limit(need)),
        name="ffn",
    )(x, g, w1, w2, gf)


def _trunk(x, st_h, st_r, st_i, wts):
    batch, seq, d = x.shape
    n = batch * seq
    depth = wts["w_in"].shape[0]
    heads = st_h.shape[2]
    hgrn_width = heads * HEAD_DIM
    groups, nst = st_r.shape[2], st_r.shape[3]
    s5_width = wts["w_glu"].shape[1]
    tm = min(1024, n)
    xf = x.reshape(n, d)
    x0r = st_r.reshape(depth, batch, groups * nst)
    x0i = st_i.reshape(depth, batch, groups * nst)
    time_major = seq % tm == 0
    new_h, new_r, new_i = [], [], []
    for l in range(depth):
        p, u = _inproj(xf, wts["norm1_g"], wts["w_in"], layer=l, tm=tm, tn=s5_width, batch=batch, seq=seq)
        o_h, s_new = _hgrn(p, wts["lb_logits"], wts["onorm_g"], st_h, batch=batch, seq=seq, layer=l)
        if not time_major:
            u = u.reshape(batch, seq, s5_width).transpose(1, 0, 2).reshape(seq, batch * s5_width)
        o_s, xr, xi = _s5(u, wts["wbr"], wts["wbi"], wts["wcr"], wts["wci"], wts["lam"],
                          wts["s5_D"], wts["w_glu"], wts["b_glu"], x0r, x0i, batch=batch, layer=l)
        if not time_major:
            o_s = o_s.reshape(seq, batch, s5_width).transpose(1, 0, 2).reshape(n, s5_width)
        x1 = _outproj(xf, o_h, o_s, wts["w_out"], layer=l, tm=min(512, n), tn=d, seq=seq)
        xf = _ffn(x1, wts["norm2_g"], wts["w_ff1"], wts["w_ff2"], wts["final_g"],
                  layer=l, tm=tm, tf=1024, final_norm=(l == depth - 1))
        new_h.append(s_new)
        new_r.append(xr.reshape(batch, groups, nst))
        new_i.append(xi.reshape(batch, groups, nst))
    return xf.reshape(batch, seq, d), jnp.stack(new_h), jnp.stack(new_r), jnp.stack(new_i)


def kernel(x_prompt, x_sample, state_hgrn, state_s5_re, state_s5_im, norm1_g, w_in, hgrn_lb_logits,
           hgrn_onorm_g, s5_lambda_re, s5_lambda_im, s5_log_step, s5_B_re, s5_B_im, s5_C_re, s5_C_im,
           s5_D, s5_w_glu, s5_b_glu, w_out, norm2_g, w_ff1, w_ff2, final_norm_g):
    depth, d = norm1_g.shape
    row3 = lambda a: a.reshape(depth, 1, a.shape[-1])
    lbr, lbi, bbr, bbi = _s5_discretise(s5_lambda_re, s5_lambda_im, s5_log_step, s5_B_re, s5_B_im)
    wts = {
        "norm1_g": row3(norm1_g), "norm2_g": row3(norm2_g), "final_g": final_norm_g.reshape(1, d),
        "lb_logits": hgrn_lb_logits, "onorm_g": row3(hgrn_onorm_g),
        "w_in": w_in.astype(BF16), "w_out": w_out.astype(BF16),
        "w_ff1": w_ff1.astype(BF16), "w_ff2": w_ff2.astype(BF16),
        "w_glu": s5_w_glu.astype(BF16), "b_glu": row3(s5_b_glu), "s5_D": row3(s5_D),
        "lam": jnp.stack([lbr, lbi], axis=1),
        "wbr": _block_diag_in(bbr), "wbi": _block_diag_in(bbi),
        "wcr": _block_diag_out(s5_C_re), "wci": _block_diag_out(s5_C_im),
    }
    bp = x_prompt.shape[0]
    zh = jnp.zeros((depth, bp) + state_hgrn.shape[2:], F32)
    zs = jnp.zeros((depth, bp) + state_s5_re.shape[2:], F32)
    y_p, hp, rp, ip = _trunk(x_prompt, zh, zs, zs, wts)
    y_s, hs, rs, is_ = _trunk(x_sample, state_hgrn, state_s5_re, state_s5_im, wts)
    return (y_p, y_s, hp, rp, ip, hs, rs, is_)
```

```python
import functools

import jax
import jax.numpy as jnp
from jax import lax
from jax.experimental import pallas as pl
from jax.experimental.pallas import tpu as pltpu

F32 = jnp.float32
BF16 = jnp.bfloat16
EPS = 1e-6

HEAD_DIM = 128
HGRN_CHUNK = 64
HGRN_SUB = 16
HGRN_HEADS_PER_STEP = 4
SUBLANES = 8
LANES = 128
S5_SCAN_LANES = 256
S5_ROWS = 512

V7X_VMEM_CAP = 56 * 1024 * 1024
NORM_ROWS = 64
FFN_ACC_CHUNKS = 4


def _vmem_limit(nbytes):
    return int(min(V7X_VMEM_CAP, nbytes * 5 // 4 + (4 << 20)))


def _rmsnorm(x, g):
    return x * lax.rsqrt(jnp.mean(jnp.square(x), axis=-1, keepdims=True) + EPS) * g


def _sigmoid(x):
    return 1.0 / (1.0 + jnp.exp(-x))


def _log1p_exp_neg_abs(x):
    return jnp.log(1.0 + jnp.exp(-jnp.abs(x)))


def _softplus(x):
    return jnp.maximum(x, 0.0) + _log1p_exp_neg_abs(x)


def _dot(a, b):
    return jnp.dot(a, b, preferred_element_type=F32)


def _dot_nt(a, b):
    return lax.dot_general(a, b, (((1,), (1,)), ((), ())), preferred_element_type=F32)


def _inproj_kernel(x_ref, g_ref, w_ref, p_ref, u_ref, h_ref):
    tm = x_ref.shape[0]
    j = pl.program_id(1)
    last = pl.num_programs(1) - 1

    @pl.when(j == 0)
    def _():
        g = g_ref[...]

        def body(i, carry):
            r = pl.ds(pl.multiple_of(i * NORM_ROWS, NORM_ROWS), NORM_ROWS)
            h_ref[r, :] = _rmsnorm(x_ref[r, :], g).astype(BF16)
            return carry

        lax.fori_loop(0, tm // NORM_ROWS, body, 0)

    @pl.when(j < last)
    def _():
        p_ref[...] = _dot(h_ref[...], w_ref[...])

    @pl.when(j == last)
    def _():
        u_ref[...] = _dot(h_ref[...], w_ref[...])


def _inproj(x, g, w, *, layer, tm, tn, batch, seq):
    n, d = x.shape
    cols = w.shape[2]
    n_j = cols // tn
    time_major = seq % tm == 0
    if time_major:
        n_t = seq // tm
        u_shape, u_map = (seq, batch * tn), (lambda i, j: (i % n_t, i // n_t))
    else:
        u_shape, u_map = (n, tn), (lambda i, j: (i, 0))
    need = 2 * tm * d * 4 + tm * d * 2 + 2 * d * tn * 2 + 5 * tm * tn * 4
    return pl.pallas_call(
        _inproj_kernel,
        out_shape=(jax.ShapeDtypeStruct((n, cols - tn), F32), jax.ShapeDtypeStruct(u_shape, F32)),
        grid=(n // tm, n_j),
        in_specs=[
            pl.BlockSpec((tm, d), lambda i, j: (i, 0)),
            pl.BlockSpec((None, 1, d), lambda i, j: (layer, 0, 0)),
            pl.BlockSpec((None, d, tn), lambda i, j: (layer, 0, j)),
        ],
        out_specs=(pl.BlockSpec((tm, tn), lambda i, j: (i, jnp.minimum(j, n_j - 2))),
                   pl.BlockSpec((tm, tn), u_map)),
        scratch_shapes=[pltpu.VMEM((tm, d), BF16)],
        compiler_params=pltpu.CompilerParams(
            dimension_semantics=("parallel", "arbitrary"),
            vmem_limit_bytes=_vmem_limit(need)),
        name="inproj",
    )(x, g, w)


def _hgrn_decayed_queries(q_lo, q_hi, f_row, row_is):
    slabs = []
    qd_lo = qd_hi = None
    for s in range(2 * SUBLANES - 1, -1, -1):
        if s == 2 * SUBLANES - 1:
            qd_hi = jnp.where(row_is[SUBLANES - 1], q_hi, 0.0)
        elif s >= SUBLANES:
            qd_hi = jnp.where(row_is[s - SUBLANES], q_hi, qd_hi * f_row(s + 1))
        else:
            fn = f_row(s + 1)
            qd_hi = qd_hi * fn
            qd_lo = jnp.where(row_is[s], q_lo, 0.0 if s == SUBLANES - 1 else qd_lo * fn)
        slabs.append((s, 1, qd_hi))
        if s < SUBLANES:
            slabs.append((s, 0, qd_lo))
    return slabs


def _hgrn_kernel(q_ref, f_ref, i_ref, g_ref, lbl_ref, on_ref, s0_ref, o_ref, sout_ref,
                 st_ref, qs_ref, ks_ref, cs_ref, fs_ref, fc_ref, op_ref, *, layer, chunk, heads_blk):
    c_idx = pl.program_id(2)
    tc = q_ref.shape[0]
    n_sub = chunk // HGRN_SUB

    @pl.when(c_idx == 0)
    def _():
        for h in range(heads_blk):
            st_ref[h] = s0_ref[0, h].T

    logits = lbl_ref[...]
    e = jnp.exp(logits - jnp.max(logits, axis=0, keepdims=True))
    sm = e / jnp.sum(e, axis=0, keepdims=True)
    cum0 = sm[0:1]
    cuml = cum0
    for j in range(1, layer + 1):
        cuml = cuml + sm[j:j + 1]
    lb = cuml - cum0
    log_lb = jnp.log(lb)
    log_1m = jnp.log1p(-lb)

    z = f_ref[...]
    sp_neg = _softplus(-z)
    b = log_1m - sp_neg
    delta = log_lb - b
    logf = jnp.where(jnp.isnan(delta), log_lb + b,
                     jnp.maximum(log_lb, b) + _log1p_exp_neg_abs(delta))
    fs_ref[...] = jnp.exp(logf)
    ks_ref[...] = jnp.exp(log_1m - (sp_neg + z))
    qr = q_ref[...]
    qs_ref[...] = qr * _sigmoid(qr)

    t_i = lax.broadcasted_iota(jnp.int32, (chunk, chunk), 0)
    s_i = lax.broadcasted_iota(jnp.int32, (chunk, chunk), 1)
    tri = jnp.where(t_i >= s_i, 1.0, 0.0).astype(BF16)
    for j in range(tc // chunk):
        lf = logf[j * chunk:(j + 1) * chunk]
        hi = lf.astype(BF16)
        r1 = lf - hi.astype(F32)
        mid = r1.astype(BF16)
        lo = (r1 - mid.astype(F32)).astype(BF16)
        cs_ref[j * chunk:(j + 1) * chunk, :] = _dot(tri, hi) + _dot(tri, mid) + _dot(tri, lo)

    row8 = lax.broadcasted_iota(jnp.int32, (SUBLANES, HEAD_DIM), 0)
    row_is = [row8 == j for j in range(SUBLANES)]
    lane8 = lax.broadcasted_iota(jnp.int32, (SUBLANES, chunk), 1)
    lane_is = [lane8 == j for j in range(chunk)]
    onorm = on_ref[...]
    blk = lambda a, j: a[j * HGRN_SUB:(j + 1) * HGRN_SUB]

    head_cols = [slice(h * HEAD_DIM, (h + 1) * HEAD_DIM) for h in range(heads_blk)]

    def finish(rows):
        for cols in head_cols:
            o = op_ref[:, cols]
            o = o * lax.rsqrt(jnp.mean(jnp.square(o), axis=-1, keepdims=True) + EPS)
            o = o * onorm[:, cols]
            gr = g_ref[rows, cols]
            o_ref[rows, cols] = (o * (gr * _sigmoid(gr))).astype(BF16)

    op_ref[...] = jnp.zeros(op_ref.shape, F32)

    def chunk_body(c, carry):
        base = pl.multiple_of(c * chunk, chunk)
        r = pl.ds(base, chunk)
        finish(pl.ds(pl.multiple_of(jnp.maximum(c - 1, 0) * chunk, chunk), chunk))
        fc_ref[...] = fs_ref[r, :]
        cum_all = cs_ref[r, :]

        zero_blk = jnp.zeros((HGRN_SUB, HEAD_DIM), F32)
        qs, vbs, kbs, o_inter, a_off = [], [], [], [], []
        for cols in head_cols:
            h = len(qs)
            cum = cum_all[:, cols]
            q = qs_ref[r, cols]
            k = ks_ref[r, cols]
            v = i_ref[r, cols]
            ends = [cum[(j + 1) * HGRN_SUB - 1:(j + 1) * HGRN_SUB, :] for j in range(n_sub)]
            cl = ends[-1]
            ke = [blk(k, j) * jnp.exp(ends[j] - blk(cum, j)) for j in range(n_sub)]
            qe = [blk(q, j) * jnp.exp(blk(cum, j) - ends[j - 1] if j else blk(cum, j)) for j in range(n_sub)]

            st = st_ref[h]
            q_in = jnp.concatenate([qe[j] * jnp.exp(ends[j - 1]) if j else qe[j] for j in range(n_sub)], axis=0)
            o_inter.append(_dot_nt(q_in.astype(BF16), st.astype(BF16)))
            kdec = jnp.concatenate(
                [ke[j] * jnp.exp(cl - ends[j]) if j < n_sub - 1 else ke[j] for j in range(n_sub)], axis=0)
            st_ref[h] = st * jnp.exp(cl) + _dot(v.T.astype(BF16), kdec.astype(BF16))

            offs = [None]
            for i in range(1, n_sub):
                kt = jnp.concatenate(
                    [ke[j] * jnp.exp(ends[i - 1] - ends[j]) if j < i - 1 else ke[j] for j in range(i)]
                    + [zero_blk] * (n_sub - i), axis=0)
                offs.append(_dot_nt(qe[i].astype(BF16), kt.astype(BF16)))
            a_off.append(offs)
            qs.append(q)
            vbs.append(v.astype(BF16))
            kbs.append(k.astype(BF16))

        res = []
        for h, cols in enumerate(head_cols):
            per_head = []
            for i in range(n_sub):
                lo_r = i * HGRN_SUB
                f_row = lambda s, lo_r=lo_r, cols=cols: fc_ref[lo_r + s:lo_r + s + 1, cols]
                q_blk = blk(qs[h], i)
                slabs = _hgrn_decayed_queries(q_blk[:SUBLANES], q_blk[SUBLANES:], f_row, row_is)
                stack = jnp.concatenate([slab for _, _, slab in slabs], axis=0)
                per_head.append(([(s, half) for s, half, _ in slabs],
                                 _dot_nt(stack.astype(BF16), kbs[h])))
            res.append(per_head)

        o_intra = []
        for h in range(heads_blk):
            rows = []
            for i in range(n_sub):
                order, prod = res[h][i]
                halves = [jnp.zeros((SUBLANES, chunk), F32), jnp.zeros((SUBLANES, chunk), F32)]
                for idx, (s, half) in enumerate(order):
                    piece = prod[idx * SUBLANES:(idx + 1) * SUBLANES]
                    halves[half] = jnp.where(lane_is[i * HGRN_SUB + s], piece, halves[half])
                a = jnp.concatenate(halves, axis=0)
                rows.append(a + a_off[h][i] if i else a)
            attn = jnp.concatenate(rows, axis=0)
            o_intra.append(_dot(attn.astype(BF16), vbs[h]))

        for h, cols in enumerate(head_cols):
            op_ref[:, cols] = o_inter[h] + o_intra[h]
        return carry

    lax.fori_loop(0, tc // chunk, chunk_body, 0, unroll=min(2, tc // chunk))
    finish(pl.ds(tc - chunk, chunk))

    @pl.when(c_idx == pl.num_programs(2) - 1)
    def _():
        for h in range(heads_blk):
            sout_ref[0, h] = st_ref[h].T


def _hgrn(p, lb_logits, onorm_g, s0, *, batch, seq, layer):
    n = p.shape[0]
    heads = s0.shape[2]
    hb = HGRN_HEADS_PER_STEP
    wblk = hb * HEAD_DIM
    n_hb = heads // hb
    width = heads * HEAD_DIM
    depth = lb_logits.shape[0]
    tc = min(512, seq)
    chunk = min(HGRN_CHUNK, seq)
    n_t = seq // tc
    row = lambda b, h, c: b * n_t + c
    need = 2 * 4 * tc * wblk * 4 + 2 * tc * wblk * 2 + 4 * tc * wblk * 4 + 5 * hb * HEAD_DIM * HEAD_DIM * 4
    kernel = functools.partial(_hgrn_kernel, layer=layer, chunk=chunk, heads_blk=hb)
    st_spec = pl.BlockSpec((None, 1, hb, HEAD_DIM, HEAD_DIM), lambda b, h, c: (layer, b, h, 0, 0))
    return pl.pallas_call(
        kernel,
        out_shape=(jax.ShapeDtypeStruct((n, width), BF16),
                   jax.ShapeDtypeStruct((batch, heads, HEAD_DIM, HEAD_DIM), F32)),
        grid=(batch, n_hb, n_t),
        in_specs=[
            pl.BlockSpec((tc, wblk), lambda b, h, c: (row(b, h, c), h)),
            pl.BlockSpec((tc, wblk), lambda b, h, c: (row(b, h, c), n_hb + h)),
            pl.BlockSpec((tc, wblk), lambda b, h, c: (row(b, h, c), 2 * n_hb + h)),
            pl.BlockSpec((tc, wblk), lambda b, h, c: (row(b, h, c), 3 * n_hb + h)),
            pl.BlockSpec((depth, wblk), lambda b, h, c: (0, h)),
            pl.BlockSpec((None, 1, wblk), lambda b, h, c: (layer, 0, h)),
            st_spec,
        ],
        out_specs=(
            pl.BlockSpec((tc, wblk), lambda b, h, c: (row(b, h, c), h)),
            pl.BlockSpec((1, hb, HEAD_DIM, HEAD_DIM), lambda b, h, c: (b, h, 0, 0)),
        ),
        scratch_shapes=[
            pltpu.VMEM((hb, HEAD_DIM, HEAD_DIM), F32),
            pltpu.VMEM((tc, wblk), F32),
            pltpu.VMEM((tc, wblk), F32),
            pltpu.VMEM((tc, wblk), F32),
            pltpu.VMEM((tc, wblk), F32),
            pltpu.VMEM((chunk, wblk), F32),
            pltpu.VMEM((chunk, wblk), F32),
        ],
        compiler_params=pltpu.CompilerParams(
            dimension_semantics=("parallel", "parallel", "arbitrary"),
            vmem_limit_bytes=_vmem_limit(need)),
        name="hgrn",
    )(p, p, p, p, lb_logits, onorm_g, s0)


def _s5_disc_lambda_kernel(lr_ref, li_ref, ls_ref, lbr_ref, lbi_ref, cr_ref, ci_ref):
    lr, li = lr_ref[...], li_ref[...]
    dt = jnp.exp(ls_ref[...])
    mag = jnp.exp(dt * lr)
    ang = dt * li
    lbr = mag * jnp.cos(ang)
    lbi = mag * jnp.sin(ang)
    nr, ni = lbr - 1.0, lbi
    den = lr * lr + li * li
    lbr_ref[...] = lbr
    lbi_ref[...] = lbi
    cr_ref[...] = (nr * lr + ni * li) / den
    ci_ref[...] = (ni * lr - nr * li) / den


def _s5_disc_b_kernel(cr_ref, ci_ref, br_ref, bi_ref, obr_ref, obi_ref):
    cr, ci, br, bi = cr_ref[...], ci_ref[...], br_ref[...], bi_ref[...]
    obr_ref[...] = cr * br - ci * bi
    obi_ref[...] = cr * bi + ci * br


def _s5_discretise(lam_re, lam_im, log_step, b_re, b_im):
    depth, groups, nst = lam_re.shape
    rows = depth * groups
    shp = jax.ShapeDtypeStruct((rows, nst), F32)
    lbr, lbi, cr, ci = pl.pallas_call(
        _s5_disc_lambda_kernel, out_shape=(shp, shp, shp, shp), name="s5_disc_lambda",
    )(lam_re.reshape(rows, nst), lam_im.reshape(rows, nst), log_step.reshape(rows, 1))
    flat = rows * nst
    cg = b_re.shape[-1]
    blk = 1024
    col = pl.BlockSpec((blk, 1), lambda i: (i, 0))
    mat = pl.BlockSpec((blk, cg), lambda i: (i, 0))
    oshp = jax.ShapeDtypeStruct((flat, cg), F32)
    bbr, bbi = pl.pallas_call(
        _s5_disc_b_kernel, out_shape=(oshp, oshp), grid=(flat // blk,),
        in_specs=[col, col, mat, mat], out_specs=(mat, mat), name="s5_disc_b",
    )(cr.reshape(flat, 1), ci.reshape(flat, 1), b_re.reshape(flat, cg), b_im.reshape(flat, cg))
    full = (depth, groups, nst, cg)
    return (lbr.reshape(depth, groups * nst), lbi.reshape(depth, groups * nst),
            bbr.reshape(full), bbi.reshape(full))


def _s5_pair_weights(lbr, lbi, bbr, bbi, c_re, c_im, d_skip):
    depth, groups, nst, cg = bbr.shape
    gpb = LANES // cg
    nbk = groups // gpb
    eye = jnp.eye(gpb, dtype=F32)
    hi = lax.Precision.HIGHEST

    def embed(t, rows, cols):
        x = t.reshape(depth, nbk, gpb, rows, cols)
        w = x[:, :, :, :, None, :] * eye[None, None, :, None, :, None]
        return w.reshape(depth, nbk, gpb * rows, gpb * cols)

    lr, li = lbr[..., None], lbi[..., None]
    swap = lambda t: jnp.swapaxes(t, 2, 3)
    blr, bli = bbr * lr - bbi * li, bbr * li + bbi * lr
    wb_re = jnp.concatenate([embed(swap(blr), cg, nst), embed(swap(bbr), cg, nst)], axis=2)
    wb_im = jnp.concatenate([embed(swap(bli), cg, nst), embed(swap(bbi), cg, nst)], axis=2)
    wb = jnp.concatenate([wb_re, wb_im], axis=3).astype(BF16)

    lrc, lic = lbr[:, :, None, :], lbi[:, :, None, :]
    pr, pi = c_re * lrc - c_im * lic, c_re * lic + c_im * lrc
    qr, qi = pr * lrc - pi * lic, pr * lic + pi * lrc
    mm = lambda a, b: jnp.einsum("lgcn,lgnd->lgcd", a, b, precision=hi)
    g0 = mm(c_re, bbr) - mm(c_im, bbi)
    g1 = mm(pr, bbr) - mm(pi, bbi)
    g0d = g0 + d_skip.reshape(depth, groups, cg)[..., None] * jnp.eye(cg, dtype=F32)
    zero = jnp.zeros((depth, nbk, LANES, LANES), F32)
    even = jnp.concatenate([embed(swap(pr), nst, cg), -embed(swap(pi), nst, cg),
                            embed(swap(g0d), cg, cg), zero], axis=2)
    odd = jnp.concatenate([embed(swap(qr), nst, cg), -embed(swap(qi), nst, cg),
                           embed(swap(g1), cg, cg), embed(swap(g0d), cg, cg)], axis=2)
    wc = jnp.concatenate([even, odd], axis=3).astype(BF16)
    lam2 = jnp.stack([lbr * lbr - lbi * lbi, 2.0 * lbr * lbi], axis=1).reshape(depth, 2, groups * nst)
    return wb, wc, lam2


def _s5_kernel(u_ref, wb_ref, wc_ref, lam2_ref, wg_ref, bg_ref, x0r_ref, x0i_ref,
               o_ref, xr_ref, xi_ref, xs_ref, y_ref, st_ref, ue_ref, uo_ref, tmp_ref, *, batch):
    c_idx = pl.program_id(0)
    tt = u_ref.shape[0]
    tp = tt // 2
    prows = tp * batch
    width = u_ref.shape[1] // batch
    n_slab = width // LANES
    nbk = wb_ref.shape[0]
    bl = wb_ref.shape[2] // 2
    ns = nbk * bl
    grp = st_ref.shape[1]
    per = grp // batch

    @pl.when(c_idx == 0)
    def _():
        st_ref[0] = jnp.concatenate([x0r_ref[...]] * per, axis=0)
        st_ref[1] = jnp.concatenate([x0i_ref[...]] * per, axis=0)

    for b in range(batch):
        for m in range(n_slab):
            tmp_ref[...] = u_ref[:, b * width + m * LANES:b * width + (m + 1) * LANES]
            ue_ref[m, pl.ds(b, tp, stride=batch), :] = tmp_ref[pl.ds(0, tp, stride=2), :]
            uo_ref[m, pl.ds(b, tp, stride=batch), :] = tmp_ref[pl.ds(1, tp, stride=2), :]

    for j in range(nbk):
        lhs = jnp.concatenate([ue_ref[j], uo_ref[j]], axis=1).astype(BF16)
        w = _dot(lhs, wb_ref[j])
        xs_ref[:, j * bl:(j + 1) * bl] = w[:, :bl]
        xs_ref[:, ns + j * bl:ns + (j + 1) * bl] = w[:, bl:]

    second = lax.broadcasted_iota(jnp.int32, (grp, S5_SCAN_LANES), 0) >= batch
    for cb in range(ns // S5_SCAN_LANES):
        lo = cb * S5_SCAN_LANES
        re_l = slice(lo, lo + S5_SCAN_LANES)
        im_l = slice(ns + lo, ns + lo + S5_SCAN_LANES)
        ar = lam2_ref[0:1, re_l]
        ai = lam2_ref[1:2, re_l]

        def step(g, carry, ar=ar, ai=ai, re_l=re_l, im_l=im_l):
            cr, ci = carry
            r = pl.ds(pl.multiple_of(g * grp, grp), grp)
            wr = xs_ref[r, re_l]
            wi = xs_ref[r, im_l]
            if per == 2:
                tr = ar * cr - ai * ci + pltpu.roll(wr, batch, axis=0)
                ti = ar * ci + ai * cr + pltpu.roll(wi, batch, axis=0)
                xs_ref[r, re_l] = jnp.where(second, tr, cr)
                xs_ref[r, im_l] = jnp.where(second, ti, ci)
                vr = ar * tr - ai * ti + wr
                vi = ar * ti + ai * tr + wi
                nr = jnp.where(second, vr, pltpu.roll(vr, batch, axis=0))
                ni = jnp.where(second, vi, pltpu.roll(vi, batch, axis=0))
            else:
                xs_ref[r, re_l] = cr
                xs_ref[r, im_l] = ci
                nr = ar * cr - ai * ci + wr
                ni = ar * ci + ai * cr + wi
            return nr, ni

        cr, ci = lax.fori_loop(0, prows // grp, step, (st_ref[0, :, re_l], st_ref[1, :, re_l]), unroll=2)
        st_ref[0, :, re_l] = cr
        st_ref[1, :, re_l] = ci

    for j in range(nbk):
        lhs = jnp.concatenate([xs_ref[:, j * bl:(j + 1) * bl], xs_ref[:, ns + j * bl:ns + (j + 1) * bl],
                               ue_ref[j], uo_ref[j]], axis=1).astype(BF16)
        yj = _dot(lhs, wc_ref[j])
        y_ref[0:prows, j * LANES:(j + 1) * LANES] = yj[:, :LANES]
        y_ref[prows:2 * prows, j * LANES:(j + 1) * LANES] = yj[:, LANES:]

    hh = jax.nn.gelu(y_ref[...])
    gate = _sigmoid(_dot(hh.astype(BF16), wg_ref[...]) + bg_ref[...])
    out = hh * gate
    for m in range(n_slab):
        ue_ref[m] = out[0:prows, m * LANES:(m + 1) * LANES]
        uo_ref[m] = out[prows:2 * prows, m * LANES:(m + 1) * LANES]
    for b in range(batch):
        for m in range(n_slab):
            tmp_ref[pl.ds(0, tp, stride=2), :] = ue_ref[m, pl.ds(b, tp, stride=batch), :]
            tmp_ref[pl.ds(1, tp, stride=2), :] = uo_ref[m, pl.ds(b, tp, stride=batch), :]
            o_ref[:, b * width + m * LANES:b * width + (m + 1) * LANES] = tmp_ref[...].astype(BF16)

    @pl.when(c_idx == pl.num_programs(0) - 1)
    def _():
        xr_ref[...] = st_ref[0, grp - batch:grp, :]
        xi_ref[...] = st_ref[1, grp - batch:grp, :]


def _s5(u, wb, wc, lam2, w_glu, b_glu, x0r, x0i, *, batch, layer):
    seq = u.shape[0]
    width = u.shape[1] // batch
    n = seq * batch
    _, nbk, _, bl2 = wb.shape
    ns = nbk * bl2 // 2
    assert batch % SUBLANES == 0 or 2 * batch == SUBLANES
    assert seq % 2 == 0 and width == nbk * LANES
    grp = max(batch, SUBLANES)
    rows = min(S5_ROWS, n)
    tt = rows // batch
    prows = rows // 2
    lay4 = lambda c: (layer, 0, 0, 0)
    lay3 = lambda c: (layer, 0, 0)
    st_in = pl.BlockSpec((None, batch, ns), lay3)
    resident = pl.Buffered(1)
    need = (2 * rows * width * 4 + (wb[0].size + wc[0].size + width * width) * 2
            + 2 * rows * width * 2 + prows * 2 * ns * 4 + rows * width * 4 + 4 * rows * width * 4
            + 4 * prows * (2 * bl2 + 2 * LANES) + (8 * batch + 2 * grp) * ns * 4)
    st_shape = jax.ShapeDtypeStruct((batch, ns), F32)
    return pl.pallas_call(
        functools.partial(_s5_kernel, batch=batch),
        out_shape=(jax.ShapeDtypeStruct((seq, batch * width), BF16), st_shape, st_shape),
        grid=(n // rows,),
        in_specs=[
            pl.BlockSpec((tt, batch * width), lambda c: (c, 0)),
            pl.BlockSpec((None,) + wb.shape[1:], lay4, pipeline_mode=resident),
            pl.BlockSpec((None,) + wc.shape[1:], lay4, pipeline_mode=resident),
            pl.BlockSpec((None, 2, ns), lay3),
            pl.BlockSpec((None, width, width), lay3, pipeline_mode=resident),
            pl.BlockSpec((None, 1, width), lay3),
            st_in,
            st_in,
        ],
        out_specs=(
            pl.BlockSpec((tt, batch * width), lambda c: (c, 0)),
            pl.BlockSpec((batch, ns), lambda c: (0, 0)),
            pl.BlockSpec((batch, ns), lambda c: (0, 0)),
        ),
        scratch_shapes=[
            pltpu.VMEM((prows, 2 * ns), F32),
            pltpu.VMEM((rows, width), F32),
            pltpu.VMEM((2, grp, ns), F32),
            pltpu.VMEM((width // LANES, prows, LANES), F32),
            pltpu.VMEM((width // LANES, prows, LANES), F32),
            pltpu.VMEM((tt, LANES), F32),
        ],
        compiler_params=pltpu.CompilerParams(
            dimension_semantics=("arbitrary",),
            vmem_limit_bytes=_vmem_limit(need)),
        name="s5",
    )(u, wb, wc, lam2, w_glu, b_glu, x0r, x0i)


def _outproj_kernel(x_ref, oh_ref, os_ref, wh_ref, ws_ref, o_ref):
    o_ref[...] = x_ref[...] + _dot(oh_ref[...], wh_ref[...]) + _dot(os_ref[...], ws_ref[...])


def _outproj(x, o_h, o_s, w_out, *, layer, tm, tn, seq):
    n, d = x.shape
    kh = o_h.shape[1]
    ks = w_out.shape[1] - kh
    assert kh == ks
    if o_s.shape[0] == n:
        os_map = lambda i, j: (i, 0)
    else:
        n_t = seq // tm
        os_map = lambda i, j: (i % n_t, i // n_t)
    need = 2 * (2 * tm * tn * 4 + tm * (kh + ks) * 2 + (kh + ks) * tn * 2) + 2 * tm * tn * 4
    return pl.pallas_call(
        _outproj_kernel,
        out_shape=jax.ShapeDtypeStruct((n, d), F32),
        grid=(n // tm, d // tn),
        in_specs=[
            pl.BlockSpec((tm, tn), lambda i, j: (i, j)),
            pl.BlockSpec((tm, kh), lambda i, j: (i, 0)),
            pl.BlockSpec((tm, ks), os_map),
            pl.BlockSpec((None, kh, tn), lambda i, j: (layer, 0, j)),
            pl.BlockSpec((None, ks, tn), lambda i, j: (layer, 1, j)),
        ],
        out_specs=pl.BlockSpec((tm, tn), lambda i, j: (i, j)),
        compiler_params=pltpu.CompilerParams(
            dimension_semantics=("parallel", "parallel"),
            vmem_limit_bytes=_vmem_limit(need)),
        name="outproj",
    )(x, o_h, o_s, w_out, w_out)


def _ffn_kernel(x_ref, g_ref, w1_ref, w2_ref, gf_ref, o_ref, h_ref, *, final_norm):
    f = pl.program_id(1)
    tm = x_ref.shape[0]

    @pl.when(f == 0)
    def _():
        g = g_ref[...]

        def body(i, carry):
            r = pl.ds(pl.multiple_of(i * NORM_ROWS, NORM_ROWS), NORM_ROWS)
            h_ref[r, :] = _rmsnorm(x_ref[r, :], g).astype(BF16)
            o_ref[r, :] = jnp.zeros((NORM_ROWS, o_ref.shape[1]), F32)
            return carry

        lax.fori_loop(0, tm // NORM_ROWS, body, 0)

    a = _dot(h_ref[...], w1_ref[...])
    a = jnp.square(jnp.maximum(a, 0.0)).astype(BF16)
    cw = o_ref.shape[1] // FFN_ACC_CHUNKS
    for c in range(FFN_ACC_CHUNKS):
        o_ref[:, c * cw:(c + 1) * cw] += _dot(a, w2_ref[:, c * cw:(c + 1) * cw])

    @pl.when(f == pl.num_programs(1) - 1)
    def _():
        gf = gf_ref[...]

        def body(i, carry):
            r = pl.ds(pl.multiple_of(i * NORM_ROWS, NORM_ROWS), NORM_ROWS)
            y = x_ref[r, :] + o_ref[r, :]
            if final_norm:
                y = _rmsnorm(y, gf)
            o_ref[r, :] = y
            return carry

        lax.fori_loop(0, tm // NORM_ROWS, body, 0)


def _ffn(x, g, w1, w2, gf, *, layer, tm, tf, final_norm):
    n, d = x.shape
    ff = w1.shape[2]
    need = tm * d * 4 + tm * d * 2 + 2 * tm * d * 4 + 4 * d * tf * 2 + tm * tf * 6 + tm * d * 4
    kernel = functools.partial(_ffn_kernel, final_norm=final_norm)
    return pl.pallas_call(
        kernel,
        out_shape=jax.ShapeDtypeStruct((n, d), F32),
        grid=(n // tm, ff // tf),
        in_specs=[
            pl.BlockSpec((tm, d), lambda i, f: (i, 0), pipeline_mode=pl.Buffered(1)),
            pl.BlockSpec((None, 1, d), lambda i, f: (layer, 0, 0)),
            pl.BlockSpec((None, d, tf), lambda i, f: (layer, 0, f)),
            pl.BlockSpec((None, tf, d), lambda i, f: (layer, f, 0)),
            pl.BlockSpec((1, d), lambda i, f: (0, 0)),
        ],
        out_specs=pl.BlockSpec((tm, d), lambda i, f: (i, 0)),
        scratch_shapes=[pltpu.VMEM((tm, d), BF16)],
        compiler_params=pltpu.CompilerParams(
            dimension_semantics=("parallel", "arbitrary"),
            vmem_limit_bytes=_vmem_limit(need)),
        name="ffn",
    )(x, g, w1, w2, gf)


def _trunk(x, st_h, st_r, st_i, wts):
    batch, seq, d = x.shape
    n = batch * seq
    depth = wts["w_in"].shape[0]
    heads = st_h.shape[2]
    hgrn_width = heads * HEAD_DIM
    groups, nst = st_r.shape[2], st_r.shape[3]
    s5_width = wts["w_glu"].shape[1]
    tm = min(1024, n)
    xf = x.reshape(n, d)
    x0r = st_r.reshape(depth, batch, groups * nst)
    x0i = st_i.reshape(depth, batch, groups * nst)
    time_major = seq % tm == 0
    new_h, new_r, new_i = [], [], []
    for l in range(depth):
        p, u = _inproj(xf, wts["norm1_g"], wts["w_in"], layer=l, tm=tm, tn=s5_width, batch=batch, seq=seq)
        o_h, s_new = _hgrn(p, wts["lb_logits"], wts["onorm_g"], st_h, batch=batch, seq=seq, layer=l)
        if not time_major:
            u = u.reshape(batch, seq, s5_width).transpose(1, 0, 2).reshape(seq, batch * s5_width)
        o_s, xr, xi = _s5(u, wts["s5_wb"], wts["s5_wc"], wts["s5_lam2"], wts["w_glu"], wts["b_glu"],
                          x0r, x0i, batch=batch, layer=l)
        if not time_major:
            o_s = o_s.reshape(seq, batch, s5_width).transpose(1, 0, 2).reshape(n, s5_width)
        x1 = _outproj(xf, o_h, o_s, wts["w_out"], layer=l, tm=min(512, n), tn=d, seq=seq)
        xf = _ffn(x1, wts["norm2_g"], wts["w_ff1"], wts["w_ff2"], wts["final_g"],
                  layer=l, tm=tm, tf=1024, final_norm=(l == depth - 1))
        new_h.append(s_new)
        new_r.append(xr.reshape(batch, groups, nst))
        new_i.append(xi.reshape(batch, groups, nst))
    return xf.reshape(batch, seq, d), jnp.stack(new_h), jnp.stack(new_r), jnp.stack(new_i)


def kernel(x_prompt, x_sample, state_hgrn, state_s5_re, state_s5_im, norm1_g, w_in, hgrn_lb_logits,
           hgrn_onorm_g, s5_lambda_re, s5_lambda_im, s5_log_step, s5_B_re, s5_B_im, s5_C_re, s5_C_im,
           s5_D, s5_w_glu, s5_b_glu, w_out, norm2_g, w_ff1, w_ff2, final_norm_g):
    depth, d = norm1_g.shape
    row3 = lambda a: a.reshape(depth, 1, a.shape[-1])
    lbr, lbi, bbr, bbi = _s5_discretise(s5_lambda_re, s5_lambda_im, s5_log_step, s5_B_re, s5_B_im)
    s5_wb, s5_wc, s5_lam2 = _s5_pair_weights(lbr.reshape(s5_lambda_re.shape), lbi.reshape(s5_lambda_re.shape),
                                             bbr, bbi, s5_C_re, s5_C_im, s5_D)
    wts = {
        "norm1_g": row3(norm1_g), "norm2_g": row3(norm2_g), "final_g": final_norm_g.reshape(1, d),
        "lb_logits": hgrn_lb_logits, "onorm_g": row3(hgrn_onorm_g),
        "w_in": w_in.astype(BF16), "w_out": w_out.astype(BF16),
        "w_ff1": w_ff1.astype(BF16), "w_ff2": w_ff2.astype(BF16),
        "w_glu": s5_w_glu.astype(BF16), "b_glu": row3(s5_b_glu),
        "s5_wb": s5_wb, "s5_wc": s5_wc, "s5_lam2": s5_lam2,
    }
    bp = x_prompt.shape[0]
    zh = jnp.zeros((depth, bp) + state_hgrn.shape[2:], F32)
    zs = jnp.zeros((depth, bp) + state_s5_re.shape[2:], F32)
    y_p, hp, rp, ip = _trunk(x_prompt, zh, zs, zs, wts)
    y_s, hs, rs, is_ = _trunk(x_sample, state_hgrn, state_s5_re, state_s5_im, wts)
    return (y_p, y_s, hp, rp, ip, hs, rs, is_)
```

```python
import functools

import jax
import jax.numpy as jnp
from jax import lax
from jax.experimental import pallas as pl
from jax.experimental.pallas import tpu as pltpu

F32 = jnp.float32
BF16 = jnp.bfloat16
EPS = 1e-6

HEAD_DIM = 128
HGRN_CHUNK = 64
HGRN_SUB = 16
HGRN_HEADS_PER_STEP = 4
SUBLANES = 8
LANES = 128
S5_SCAN_LANES = 256
S5_ROWS = 512

V7X_VMEM_CAP = 56 * 1024 * 1024
NORM_ROWS = 64
FFN_ACC_CHUNKS = 4


def _vmem_limit(nbytes):
    return int(min(V7X_VMEM_CAP, nbytes * 5 // 4 + (4 << 20)))


def _rmsnorm(x, g):
    return x * lax.rsqrt(jnp.mean(jnp.square(x), axis=-1, keepdims=True) + EPS) * g


def _sigmoid(x):
    return 1.0 / (1.0 + jnp.exp(-x))


def _log1p_exp_neg_abs(x):
    return jnp.log(1.0 + jnp.exp(-jnp.abs(x)))


def _softplus(x):
    return jnp.maximum(x, 0.0) + _log1p_exp_neg_abs(x)


def _dot(a, b):
    return jnp.dot(a, b, preferred_element_type=F32)


def _dot_nt(a, b):
    return lax.dot_general(a, b, (((1,), (1,)), ((), ())), preferred_element_type=F32)


def _inproj_kernel(x_ref, g_ref, w_ref, p_ref, u_ref, h_ref):
    tm = x_ref.shape[0]
    j = pl.program_id(1)
    last = pl.num_programs(1) - 1

    @pl.when(j == 0)
    def _():
        g = g_ref[...]

        def body(i, carry):
            r = pl.ds(pl.multiple_of(i * NORM_ROWS, NORM_ROWS), NORM_ROWS)
            h_ref[r, :] = _rmsnorm(x_ref[r, :], g).astype(BF16)
            return carry

        lax.fori_loop(0, tm // NORM_ROWS, body, 0)

    @pl.when(j < last)
    def _():
        p_ref[...] = _dot(h_ref[...], w_ref[...])

    @pl.when(j == last)
    def _():
        u_ref[...] = _dot(h_ref[...], w_ref[...])


def _inproj(x, g, w, *, layer, tm, tn, batch, seq):
    n, d = x.shape
    cols = w.shape[2]
    n_j = cols // tn
    time_major = seq % tm == 0
    if time_major:
        n_t = seq // tm
        u_shape, u_map = (seq, batch * tn), (lambda i, j: (i % n_t, i // n_t))
    else:
        u_shape, u_map = (n, tn), (lambda i, j: (i, 0))
    need = 2 * tm * d * 4 + tm * d * 2 + 2 * d * tn * 2 + 5 * tm * tn * 4
    return pl.pallas_call(
        _inproj_kernel,
        out_shape=(jax.ShapeDtypeStruct((n, cols - tn), F32), jax.ShapeDtypeStruct(u_shape, F32)),
        grid=(n // tm, n_j),
        in_specs=[
            pl.BlockSpec((tm, d), lambda i, j: (i, 0)),
            pl.BlockSpec((None, 1, d), lambda i, j: (layer, 0, 0)),
            pl.BlockSpec((None, d, tn), lambda i, j: (layer, 0, j)),
        ],
        out_specs=(pl.BlockSpec((tm, tn), lambda i, j: (i, jnp.minimum(j, n_j - 2))),
                   pl.BlockSpec((tm, tn), u_map)),
        scratch_shapes=[pltpu.VMEM((tm, d), BF16)],
        compiler_params=pltpu.CompilerParams(
            dimension_semantics=("parallel", "arbitrary"),
            vmem_limit_bytes=_vmem_limit(need)),
        name="inproj",
    )(x, g, w)


def _hgrn_decayed_queries(q_lo, q_hi, f_row, row_is):
    slabs = []
    qd_lo = qd_hi = None
    for s in range(2 * SUBLANES - 1, -1, -1):
        if s == 2 * SUBLANES - 1:
            qd_hi = jnp.where(row_is[SUBLANES - 1], q_hi, 0.0)
        elif s >= SUBLANES:
            qd_hi = jnp.where(row_is[s - SUBLANES], q_hi, qd_hi * f_row(s + 1))
        else:
            fn = f_row(s + 1)
            qd_hi = qd_hi * fn
            qd_lo = jnp.where(row_is[s], q_lo, 0.0 if s == SUBLANES - 1 else qd_lo * fn)
        slabs.append((s, 1, qd_hi))
        if s < SUBLANES:
            slabs.append((s, 0, qd_lo))
    return slabs


def _hgrn_kernel(q_ref, f_ref, i_ref, g_ref, lbl_ref, on_ref, s0_ref, o_ref, sout_ref,
                 st_ref, qs_ref, ks_ref, cs_ref, fs_ref, fc_ref, op_ref, *, layer, chunk, heads_blk):
    c_idx = pl.program_id(2)
    tc = q_ref.shape[0]
    n_sub = chunk // HGRN_SUB

    @pl.when(c_idx == 0)
    def _():
        for h in range(heads_blk):
            st_ref[h] = s0_ref[0, h].T

    logits = lbl_ref[...]
    e = jnp.exp(logits - jnp.max(logits, axis=0, keepdims=True))
    sm = e / jnp.sum(e, axis=0, keepdims=True)
    cum0 = sm[0:1]
    cuml = cum0
    for j in range(1, layer + 1):
        cuml = cuml + sm[j:j + 1]
    lb = cuml - cum0
    log_lb = jnp.log(lb)
    log_1m = jnp.log1p(-lb)

    z = f_ref[...]
    sp_neg = _softplus(-z)
    b = log_1m - sp_neg
    delta = log_lb - b
    logf = jnp.where(jnp.isnan(delta), log_lb + b,
                     jnp.maximum(log_lb, b) + _log1p_exp_neg_abs(delta))
    fs_ref[...] = jnp.exp(logf)
    ks_ref[...] = jnp.exp(log_1m - (sp_neg + z))
    qr = q_ref[...]
    qs_ref[...] = qr * _sigmoid(qr)

    t_i = lax.broadcasted_iota(jnp.int32, (chunk, chunk), 0)
    s_i = lax.broadcasted_iota(jnp.int32, (chunk, chunk), 1)
    tri = jnp.where(t_i >= s_i, 1.0, 0.0).astype(BF16)
    for j in range(tc // chunk):
        lf = logf[j * chunk:(j + 1) * chunk]
        hi = lf.astype(BF16)
        r1 = lf - hi.astype(F32)
        mid = r1.astype(BF16)
        lo = (r1 - mid.astype(F32)).astype(BF16)
        cs_ref[j * chunk:(j + 1) * chunk, :] = _dot(tri, hi) + _dot(tri, mid) + _dot(tri, lo)

    row8 = lax.broadcasted_iota(jnp.int32, (SUBLANES, HEAD_DIM), 0)
    row_is = [row8 == j for j in range(SUBLANES)]
    lane8 = lax.broadcasted_iota(jnp.int32, (SUBLANES, chunk), 1)
    lane_is = [lane8 == j for j in range(chunk)]
    onorm = on_ref[...]
    blk = lambda a, j: a[j * HGRN_SUB:(j + 1) * HGRN_SUB]

    head_cols = [slice(h * HEAD_DIM, (h + 1) * HEAD_DIM) for h in range(heads_blk)]

    def finish(rows):
        for cols in head_cols:
            o = op_ref[:, cols]
            o = o * lax.rsqrt(jnp.mean(jnp.square(o), axis=-1, keepdims=True) + EPS)
            o = o * onorm[:, cols]
            gr = g_ref[rows, cols]
            o_ref[rows, cols] = (o * (gr * _sigmoid(gr))).astype(BF16)

    op_ref[...] = jnp.zeros(op_ref.shape, F32)

    def chunk_body(c, carry):
        base = pl.multiple_of(c * chunk, chunk)
        r = pl.ds(base, chunk)
        finish(pl.ds(pl.multiple_of(jnp.maximum(c - 1, 0) * chunk, chunk), chunk))
        fc_ref[...] = fs_ref[r, :]
        cum_all = cs_ref[r, :]

        zero_blk = jnp.zeros((HGRN_SUB, HEAD_DIM), F32)
        qs, vbs, kbs, o_inter, a_off = [], [], [], [], []
        for cols in head_cols:
            h = len(qs)
            cum = cum_all[:, cols]
            q = qs_ref[r, cols]
            k = ks_ref[r, cols]
            v = i_ref[r, cols]
            ends = [cum[(j + 1) * HGRN_SUB - 1:(j + 1) * HGRN_SUB, :] for j in range(n_sub)]
            cl = ends[-1]
            ke = [blk(k, j) * jnp.exp(ends[j] - blk(cum, j)) for j in range(n_sub)]
            qe = [blk(q, j) * jnp.exp(blk(cum, j) - ends[j - 1] if j else blk(cum, j)) for j in range(n_sub)]

            st = st_ref[h]
            q_in = jnp.concatenate([qe[j] * jnp.exp(ends[j - 1]) if j else qe[j] for j in range(n_sub)], axis=0)
            o_inter.append(_dot_nt(q_in.astype(BF16), st.astype(BF16)))
            kdec = jnp.concatenate(
                [ke[j] * jnp.exp(cl - ends[j]) if j < n_sub - 1 else ke[j] for j in range(n_sub)], axis=0)
            st_ref[h] = st * jnp.exp(cl) + _dot(v.T.astype(BF16), kdec.astype(BF16))

            offs = [None]
            for i in range(1, n_sub):
                kt = jnp.concatenate(
                    [ke[j] * jnp.exp(ends[i - 1] - ends[j]) if j < i - 1 else ke[j] for j in range(i)]
                    + [zero_blk] * (n_sub - i), axis=0)
                offs.append(_dot_nt(qe[i].astype(BF16), kt.astype(BF16)))
            a_off.append(offs)
            qs.append(q)
            vbs.append(v.astype(BF16))
            kbs.append(k.astype(BF16))

        res = []
        for h, cols in enumerate(head_cols):
            per_head = []
            for i in range(n_sub):
                lo_r = i * HGRN_SUB
                f_row = lambda s, lo_r=lo_r, cols=cols: fc_ref[lo_r + s:lo_r + s + 1, cols]
                q_blk = blk(qs[h], i)
                slabs = _hgrn_decayed_queries(q_blk[:SUBLANES], q_blk[SUBLANES:], f_row, row_is)
                stack = jnp.concatenate([slab for _, _, slab in slabs], axis=0)
                per_head.append(([(s, half) for s, half, _ in slabs],
                                 _dot_nt(stack.astype(BF16), kbs[h])))
            res.append(per_head)

        o_intra = []
        for h in range(heads_blk):
            rows = []
            for i in range(n_sub):
                order, prod = res[h][i]
                halves = [jnp.zeros((SUBLANES, chunk), F32), jnp.zeros((SUBLANES, chunk), F32)]
                for idx, (s, half) in enumerate(order):
                    piece = prod[idx * SUBLANES:(idx + 1) * SUBLANES]
                    halves[half] = jnp.where(lane_is[i * HGRN_SUB + s], piece, halves[half])
                a = jnp.concatenate(halves, axis=0)
                rows.append(a + a_off[h][i] if i else a)
            attn = jnp.concatenate(rows, axis=0)
            o_intra.append(_dot(attn.astype(BF16), vbs[h]))

        for h, cols in enumerate(head_cols):
            op_ref[:, cols] = o_inter[h] + o_intra[h]
        return carry

    lax.fori_loop(0, tc // chunk, chunk_body, 0, unroll=min(2, tc // chunk))
    finish(pl.ds(tc - chunk, chunk))

    @pl.when(c_idx == pl.num_programs(2) - 1)
    def _():
        for h in range(heads_blk):
            sout_ref[0, h] = st_ref[h].T


def _hgrn(p, lb_logits, onorm_g, s0, *, batch, seq, layer):
    n = p.shape[0]
    heads = s0.shape[2]
    hb = HGRN_HEADS_PER_STEP
    wblk = hb * HEAD_DIM
    n_hb = heads // hb
    width = heads * HEAD_DIM
    depth = lb_logits.shape[0]
    tc = min(512, seq)
    chunk = min(HGRN_CHUNK, seq)
    n_t = seq // tc
    row = lambda b, h, c: b * n_t + c
    need = 2 * 4 * tc * wblk * 4 + 2 * tc * wblk * 2 + 4 * tc * wblk * 4 + 5 * hb * HEAD_DIM * HEAD_DIM * 4
    kernel = functools.partial(_hgrn_kernel, layer=layer, chunk=chunk, heads_blk=hb)
    st_spec = pl.BlockSpec((None, 1, hb, HEAD_DIM, HEAD_DIM), lambda b, h, c: (layer, b, h, 0, 0))
    return pl.pallas_call(
        kernel,
        out_shape=(jax.ShapeDtypeStruct((n, width), BF16),
                   jax.ShapeDtypeStruct((batch, heads, HEAD_DIM, HEAD_DIM), F32)),
        grid=(batch, n_hb, n_t),
        in_specs=[
            pl.BlockSpec((tc, wblk), lambda b, h, c: (row(b, h, c), h)),
            pl.BlockSpec((tc, wblk), lambda b, h, c: (row(b, h, c), n_hb + h)),
            pl.BlockSpec((tc, wblk), lambda b, h, c: (row(b, h, c), 2 * n_hb + h)),
            pl.BlockSpec((tc, wblk), lambda b, h, c: (row(b, h, c), 3 * n_hb + h)),
            pl.BlockSpec((depth, wblk), lambda b, h, c: (0, h)),
            pl.BlockSpec((None, 1, wblk), lambda b, h, c: (layer, 0, h)),
            st_spec,
        ],
        out_specs=(
            pl.BlockSpec((tc, wblk), lambda b, h, c: (row(b, h, c), h)),
            pl.BlockSpec((1, hb, HEAD_DIM, HEAD_DIM), lambda b, h, c: (b, h, 0, 0)),
        ),
        scratch_shapes=[
            pltpu.VMEM((hb, HEAD_DIM, HEAD_DIM), F32),
            pltpu.VMEM((tc, wblk), F32),
            pltpu.VMEM((tc, wblk), F32),
            pltpu.VMEM((tc, wblk), F32),
            pltpu.VMEM((tc, wblk), F32),
            pltpu.VMEM((chunk, wblk), F32),
            pltpu.VMEM((chunk, wblk), F32),
        ],
        compiler_params=pltpu.CompilerParams(
            dimension_semantics=("parallel", "parallel", "arbitrary"),
            vmem_limit_bytes=_vmem_limit(need)),
        name="hgrn",
    )(p, p, p, p, lb_logits, onorm_g, s0)


def _s5_disc_lambda_kernel(lr_ref, li_ref, ls_ref, lbr_ref, lbi_ref, cr_ref, ci_ref):
    lr, li = lr_ref[...], li_ref[...]
    dt = jnp.exp(ls_ref[...])
    mag = jnp.exp(dt * lr)
    ang = dt * li
    lbr = mag * jnp.cos(ang)
    lbi = mag * jnp.sin(ang)
    nr, ni = lbr - 1.0, lbi
    den = lr * lr + li * li
    lbr_ref[...] = lbr
    lbi_ref[...] = lbi
    cr_ref[...] = (nr * lr + ni * li) / den
    ci_ref[...] = (ni * lr - nr * li) / den


def _s5_discretise(lam_re, lam_im, log_step):
    depth, groups, nst = lam_re.shape
    rows = depth * groups
    shp = jax.ShapeDtypeStruct((rows, nst), F32)
    outs = pl.pallas_call(
        _s5_disc_lambda_kernel, out_shape=(shp, shp, shp, shp), name="s5_disc_lambda",
    )(lam_re.reshape(rows, nst), lam_im.reshape(rows, nst), log_step.reshape(rows, 1))
    return tuple(o.reshape(depth, groups, nst) for o in outs)


def _s5_pair_weights(lbr, lbi, zoh_r, zoh_i, b_re, b_im, c_re, c_im, d_skip):
    depth, groups, nst, cg = b_re.shape
    hi = lax.Precision.HIGHEST
    zr, zi = zoh_r[..., None], zoh_i[..., None]
    bbr, bbi = zr * b_re - zi * b_im, zr * b_im + zi * b_re
    lr, li = lbr[..., None], lbi[..., None]
    blr, bli = bbr * lr - bbi * li, bbr * li + bbi * lr
    tb = jnp.swapaxes(jnp.stack([blr, bli, bbr, bbi], axis=1), 3, 4).reshape(depth, 4, groups * cg, nst)

    lrc, lic = lbr[:, :, None, :], lbi[:, :, None, :]
    pr, pi = c_re * lrc - c_im * lic, c_re * lic + c_im * lrc
    qr, qi = pr * lrc - pi * lic, pr * lic + pi * lrc
    tcc = jnp.stack([pr, -pi, qr, -qi], axis=1).transpose(0, 1, 3, 2, 4).reshape(depth, 4, cg, groups * nst)

    mm = lambda a, b: jnp.einsum("lgcn,lgnd->lgcd", a, b, precision=hi)
    g0 = mm(c_re, bbr) - mm(c_im, bbi)
    g1 = mm(pr, bbr) - mm(pi, bbi)
    g0d = g0 + d_skip.reshape(depth, groups, cg)[..., None] * jnp.eye(cg, dtype=F32)
    tg = jnp.stack([g0d, g1], axis=1).transpose(0, 1, 3, 2, 4).reshape(depth, 2, cg, groups * cg)
    lam2 = jnp.stack([lbr * lbr - lbi * lbi, 2.0 * lbr * lbi], axis=1).reshape(depth, 2, groups * nst)
    return tb, tcc, tg, lam2


def _s5_embed_weights(tb_ref, tcc_ref, tg_ref, wb_ref, wct_ref):
    nbk = wb_ref.shape[0]
    cg, nst = tcc_ref.shape[1], tb_ref.shape[2]
    bl = wb_ref.shape[2] // 2
    assert cg & (cg - 1) == 0 and nst & (nst - 1) == 0
    c_shift, n_shift = cg.bit_length() - 1, nst.bit_length() - 1
    iota = lambda shape, dim: lax.broadcasted_iota(jnp.int32, shape, dim)
    e_in = jnp.where((iota((nst, bl), 1) & (nst - 1)) == iota((nst, bl), 0), 1.0, 0.0).astype(BF16)
    e_out = jnp.where((iota((LANES, cg), 0) & (cg - 1)) == iota((LANES, cg), 1), 1.0, 0.0).astype(BF16)
    m_state = (iota((LANES, bl), 0) >> c_shift) == (iota((LANES, bl), 1) >> n_shift)
    m_direct = (iota((LANES, LANES), 0) >> c_shift) == (iota((LANES, LANES), 1) >> c_shift)
    zeros = jnp.zeros((LANES, LANES), BF16)
    for j in range(nbk):
        for k in range(4):
            half, part = divmod(k, 2)
            a = tb_ref[k, j * LANES:(j + 1) * LANES, :].astype(BF16)
            wb_ref[j, half * LANES:(half + 1) * LANES, part * bl:(part + 1) * bl] = (
                jnp.where(m_state, _dot(a, e_in), 0.0).astype(BF16))
        for k in range(4):
            eo, part = divmod(k, 2)
            x = tcc_ref[k, :, j * bl:(j + 1) * bl].astype(BF16)
            wct_ref[j, eo * LANES:(eo + 1) * LANES, part * bl:(part + 1) * bl] = (
                jnp.where(m_state, _dot(e_out, x), 0.0).astype(BF16))
        direct = [jnp.where(m_direct, _dot(e_out, tg_ref[k, :, j * LANES:(j + 1) * LANES].astype(BF16)), 0.0
                            ).astype(BF16) for k in range(2)]
        base = 2 * bl
        wct_ref[j, 0:LANES, base:base + LANES] = direct[0]
        wct_ref[j, 0:LANES, base + LANES:base + 2 * LANES] = zeros
        wct_ref[j, LANES:2 * LANES, base:base + LANES] = direct[1]
        wct_ref[j, LANES:2 * LANES, base + LANES:base + 2 * LANES] = direct[0]


def _s5_kernel(u_ref, tb_ref, tcc_ref, tg_ref, lam2_ref, wg_ref, bg_ref, x0r_ref, x0i_ref,
               o_ref, xr_ref, xi_ref, xs_ref, y_ref, st_ref, ue_ref, uo_ref, tmp_ref, wb_ref, wct_ref,
               *, batch):
    c_idx = pl.program_id(0)
    tt = u_ref.shape[0]
    tp = tt // 2
    prows = tp * batch
    width = u_ref.shape[1] // batch
    n_slab = width // LANES
    nbk = wb_ref.shape[0]
    bl = wb_ref.shape[2] // 2
    ns = nbk * bl
    grp = st_ref.shape[1]
    per = grp // batch

    @pl.when(c_idx == 0)
    def _():
        st_ref[0] = jnp.concatenate([x0r_ref[...]] * per, axis=0)
        st_ref[1] = jnp.concatenate([x0i_ref[...]] * per, axis=0)
        _s5_embed_weights(tb_ref, tcc_ref, tg_ref, wb_ref, wct_ref)

    for b in range(batch):
        for m in range(n_slab):
            tmp_ref[...] = u_ref[:, b * width + m * LANES:b * width + (m + 1) * LANES]
            ue_ref[m, pl.ds(b, tp, stride=batch), :] = tmp_ref[pl.ds(0, tp, stride=2), :]
            uo_ref[m, pl.ds(b, tp, stride=batch), :] = tmp_ref[pl.ds(1, tp, stride=2), :]

    for j in range(nbk):
        lhs = jnp.concatenate([ue_ref[j], uo_ref[j]], axis=1).astype(BF16)
        w = _dot(lhs, wb_ref[j])
        xs_ref[:, j * bl:(j + 1) * bl] = w[:, :bl]
        xs_ref[:, ns + j * bl:ns + (j + 1) * bl] = w[:, bl:]

    second = lax.broadcasted_iota(jnp.int32, (grp, S5_SCAN_LANES), 0) >= batch
    for cb in range(ns // S5_SCAN_LANES):
        lo = cb * S5_SCAN_LANES
        re_l = slice(lo, lo + S5_SCAN_LANES)
        im_l = slice(ns + lo, ns + lo + S5_SCAN_LANES)
        ar = lam2_ref[0:1, re_l]
        ai = lam2_ref[1:2, re_l]

        def step(g, carry, ar=ar, ai=ai, re_l=re_l, im_l=im_l):
            cr, ci = carry
            r = pl.ds(pl.multiple_of(g * grp, grp), grp)
            wr = xs_ref[r, re_l]
            wi = xs_ref[r, im_l]
            if per == 2:
                tr = ar * cr - ai * ci + pltpu.roll(wr, batch, axis=0)
                ti = ar * ci + ai * cr + pltpu.roll(wi, batch, axis=0)
                xs_ref[r, re_l] = jnp.where(second, tr, cr)
                xs_ref[r, im_l] = jnp.where(second, ti, ci)
                vr = ar * tr - ai * ti + wr
                vi = ar * ti + ai * tr + wi
                nr = jnp.where(second, vr, pltpu.roll(vr, batch, axis=0))
                ni = jnp.where(second, vi, pltpu.roll(vi, batch, axis=0))
            else:
                xs_ref[r, re_l] = cr
                xs_ref[r, im_l] = ci
                nr = ar * cr - ai * ci + wr
                ni = ar * ci + ai * cr + wi
            return nr, ni

        cr, ci = lax.fori_loop(0, prows // grp, step, (st_ref[0, :, re_l], st_ref[1, :, re_l]), unroll=2)
        st_ref[0, :, re_l] = cr
        st_ref[1, :, re_l] = ci

    for j in range(nbk):
        lhs = jnp.concatenate([xs_ref[:, j * bl:(j + 1) * bl], xs_ref[:, ns + j * bl:ns + (j + 1) * bl],
                               ue_ref[j], uo_ref[j]], axis=1).astype(BF16)
        yj = _dot_nt(lhs, wct_ref[j])
        y_ref[0:prows, j * LANES:(j + 1) * LANES] = yj[:, :LANES]
        y_ref[prows:2 * prows, j * LANES:(j + 1) * LANES] = yj[:, LANES:]

    hh = jax.nn.gelu(y_ref[...])
    gate = _sigmoid(_dot(hh.astype(BF16), wg_ref[...]) + bg_ref[...])
    out = hh * gate
    for m in range(n_slab):
        ue_ref[m] = out[0:prows, m * LANES:(m + 1) * LANES]
        uo_ref[m] = out[prows:2 * prows, m * LANES:(m + 1) * LANES]
    for b in range(batch):
        for m in range(n_slab):
            tmp_ref[pl.ds(0, tp, stride=2), :] = ue_ref[m, pl.ds(b, tp, stride=batch), :]
            tmp_ref[pl.ds(1, tp, stride=2), :] = uo_ref[m, pl.ds(b, tp, stride=batch), :]
            o_ref[:, b * width + m * LANES:b * width + (m + 1) * LANES] = tmp_ref[...].astype(BF16)

    @pl.when(c_idx == pl.num_programs(0) - 1)
    def _():
        xr_ref[...] = st_ref[0, grp - batch:grp, :]
        xi_ref[...] = st_ref[1, grp - batch:grp, :]


def _s5(u, tb, tcc, tg, lam2, w_glu, b_glu, x0r, x0i, *, batch, layer):
    seq = u.shape[0]
    width = u.shape[1] // batch
    n = seq * batch
    ns = lam2.shape[2]
    nbk = width // LANES
    bl = ns // nbk
    assert batch % SUBLANES == 0 or 2 * batch == SUBLANES
    assert seq % 2 == 0 and tb.shape[2] == width
    grp = max(batch, SUBLANES)
    rows = min(S5_ROWS, n)
    tt = rows // batch
    prows = rows // 2
    lay4 = lambda c: (layer, 0, 0, 0)
    lay3 = lambda c: (layer, 0, 0)
    st_in = pl.BlockSpec((None, batch, ns), lay3)
    resident = pl.Buffered(1)
    wb_shape = (nbk, 2 * LANES, 2 * bl)
    wct_shape = (nbk, 2 * LANES, 2 * bl + 2 * LANES)
    need = (2 * rows * width * 4 + (nbk * 2 * LANES * (4 * bl + 2 * LANES) + width * width) * 2
            + 4 * width * LANES * 4 + 6 * 16 * ns * 4
            + 2 * rows * width * 2 + prows * 2 * ns * 4 + rows * width * 4 + 4 * rows * width * 4
            + 4 * prows * (4 * bl + 2 * LANES) + (8 * batch + 2 * grp) * ns * 4)
    st_shape = jax.ShapeDtypeStruct((batch, ns), F32)
    return pl.pallas_call(
        functools.partial(_s5_kernel, batch=batch),
        out_shape=(jax.ShapeDtypeStruct((seq, batch * width), BF16), st_shape, st_shape),
        grid=(n // rows,),
        in_specs=[
            pl.BlockSpec((tt, batch * width), lambda c: (c, 0)),
            pl.BlockSpec((None,) + tb.shape[1:], lay4, pipeline_mode=resident),
            pl.BlockSpec((None,) + tcc.shape[1:], lay4, pipeline_mode=resident),
            pl.BlockSpec((None,) + tg.shape[1:], lay4, pipeline_mode=resident),
            pl.BlockSpec((None, 2, ns), lay3),
            pl.BlockSpec((None, width, width), lay3, pipeline_mode=resident),
            pl.BlockSpec((None, 1, width), lay3),
            st_in,
            st_in,
        ],
        out_specs=(
            pl.BlockSpec((tt, batch * width), lambda c: (c, 0)),
            pl.BlockSpec((batch, ns), lambda c: (0, 0)),
            pl.BlockSpec((batch, ns), lambda c: (0, 0)),
        ),
        scratch_shapes=[
            pltpu.VMEM((prows, 2 * ns), F32),
            pltpu.VMEM((rows, width), F32),
            pltpu.VMEM((2, grp, ns), F32),
            pltpu.VMEM((width // LANES, prows, LANES), F32),
            pltpu.VMEM((width // LANES, prows, LANES), F32),
            pltpu.VMEM((tt, LANES), F32),
            pltpu.VMEM(wb_shape, BF16),
            pltpu.VMEM(wct_shape, BF16),
        ],
        compiler_params=pltpu.CompilerParams(
            dimension_semantics=("arbitrary",),
            vmem_limit_bytes=_vmem_limit(need)),
        name="s5",
    )(u, tb, tcc, tg, lam2, w_glu, b_glu, x0r, x0i)


def _outproj_kernel(x_ref, oh_ref, os_ref, wh_ref, ws_ref, o_ref):
    o_ref[...] = x_ref[...] + _dot(oh_ref[...], wh_ref[...]) + _dot(os_ref[...], ws_ref[...])


def _outproj(x, o_h, o_s, w_out, *, layer, tm, tn, seq):
    n, d = x.shape
    kh = o_h.shape[1]
    ks = w_out.shape[1] - kh
    assert kh == ks
    if o_s.shape[0] == n:
        os_map = lambda i, j: (i, 0)
    else:
        n_t = seq // tm
        os_map = lambda i, j: (i % n_t, i // n_t)
    need = 2 * (2 * tm * tn * 4 + tm * (kh + ks) * 2 + (kh + ks) * tn * 2) + 2 * tm * tn * 4
    return pl.pallas_call(
        _outproj_kernel,
        out_shape=jax.ShapeDtypeStruct((n, d), F32),
        grid=(n // tm, d // tn),
        in_specs=[
            pl.BlockSpec((tm, tn), lambda i, j: (i, j)),
            pl.BlockSpec((tm, kh), lambda i, j: (i, 0)),
            pl.BlockSpec((tm, ks), os_map),
            pl.BlockSpec((None, kh, tn), lambda i, j: (layer, 0, j)),
            pl.BlockSpec((None, ks, tn), lambda i, j: (layer, 1, j)),
        ],
        out_specs=pl.BlockSpec((tm, tn), lambda i, j: (i, j)),
        compiler_params=pltpu.CompilerParams(
            dimension_semantics=("parallel", "parallel"),
            vmem_limit_bytes=_vmem_limit(need)),
        name="outproj",
    )(x, o_h, o_s, w_out, w_out)


def _ffn_kernel(x_ref, g_ref, w1_ref, w2_ref, gf_ref, o_ref, h_ref, *, final_norm):
    f = pl.program_id(1)
    tm = x_ref.shape[0]

    @pl.when(f == 0)
    def _():
        g = g_ref[...]

        def body(i, carry):
            r = pl.ds(pl.multiple_of(i * NORM_ROWS, NORM_ROWS), NORM_ROWS)
            h_ref[r, :] = _rmsnorm(x_ref[r, :], g).astype(BF16)
            o_ref[r, :] = jnp.zeros((NORM_ROWS, o_ref.shape[1]), F32)
            return carry

        lax.fori_loop(0, tm // NORM_ROWS, body, 0)

    a = _dot(h_ref[...], w1_ref[...])
    a = jnp.square(jnp.maximum(a, 0.0)).astype(BF16)
    cw = o_ref.shape[1] // FFN_ACC_CHUNKS
    for c in range(FFN_ACC_CHUNKS):
        o_ref[:, c * cw:(c + 1) * cw] += _dot(a, w2_ref[:, c * cw:(c + 1) * cw])

    @pl.when(f == pl.num_programs(1) - 1)
    def _():
        gf = gf_ref[...]

        def body(i, carry):
            r = pl.ds(pl.multiple_of(i * NORM_ROWS, NORM_ROWS), NORM_ROWS)
            y = x_ref[r, :] + o_ref[r, :]
            if final_norm:
                y = _rmsnorm(y, gf)
            o_ref[r, :] = y
            return carry

        lax.fori_loop(0, tm // NORM_ROWS, body, 0)


def _ffn(x, g, w1, w2, gf, *, layer, tm, tf, final_norm):
    n, d = x.shape
    ff = w1.shape[2]
    need = tm * d * 4 + tm * d * 2 + 2 * tm * d * 4 + 4 * d * tf * 2 + tm * tf * 6 + tm * d * 4
    kernel = functools.partial(_ffn_kernel, final_norm=final_norm)
    return pl.pallas_call(
        kernel,
        out_shape=jax.ShapeDtypeStruct((n, d), F32),
        grid=(n // tm, ff // tf),
        in_specs=[
            pl.BlockSpec((tm, d), lambda i, f: (i, 0), pipeline_mode=pl.Buffered(1)),
            pl.BlockSpec((None, 1, d), lambda i, f: (layer, 0, 0)),
            pl.BlockSpec((None, d, tf), lambda i, f: (layer, 0, f)),
            pl.BlockSpec((None, tf, d), lambda i, f: (layer, f, 0)),
            pl.BlockSpec((1, d), lambda i, f: (0, 0)),
        ],
        out_specs=pl.BlockSpec((tm, d), lambda i, f: (i, 0)),
        scratch_shapes=[pltpu.VMEM((tm, d), BF16)],
        compiler_params=pltpu.CompilerParams(
            dimension_semantics=("parallel", "arbitrary"),
            vmem_limit_bytes=_vmem_limit(need)),
        name="ffn",
    )(x, g, w1, w2, gf)


def _trunk(x, st_h, st_r, st_i, wts):
    batch, seq, d = x.shape
    n = batch * seq
    depth = wts["w_in"].shape[0]
    heads = st_h.shape[2]
    hgrn_width = heads * HEAD_DIM
    groups, nst = st_r.shape[2], st_r.shape[3]
    s5_width = wts["w_glu"].shape[1]
    tm = min(1024, n)
    xf = x.reshape(n, d)
    x0r = st_r.reshape(depth, batch, groups * nst)
    x0i = st_i.reshape(depth, batch, groups * nst)
    time_major = seq % tm == 0
    new_h, new_r, new_i = [], [], []
    for l in range(depth):
        p, u = _inproj(xf, wts["norm1_g"], wts["w_in"], layer=l, tm=tm, tn=s5_width, batch=batch, seq=seq)
        o_h, s_new = _hgrn(p, wts["lb_logits"], wts["onorm_g"], st_h, batch=batch, seq=seq, layer=l)
        if not time_major:
            u = u.reshape(batch, seq, s5_width).transpose(1, 0, 2).reshape(seq, batch * s5_width)
        o_s, xr, xi = _s5(u, wts["s5_tb"], wts["s5_tcc"], wts["s5_tg"], wts["s5_lam2"], wts["w_glu"],
                          wts["b_glu"], x0r, x0i, batch=batch, layer=l)
        if not time_major:
            o_s = o_s.reshape(seq, batch, s5_width).transpose(1, 0, 2).reshape(n, s5_width)
        x1 = _outproj(xf, o_h, o_s, wts["w_out"], layer=l, tm=min(512, n), tn=d, seq=seq)
        xf = _ffn(x1, wts["norm2_g"], wts["w_ff1"], wts["w_ff2"], wts["final_g"],
                  layer=l, tm=tm, tf=1024, final_norm=(l == depth - 1))
        new_h.append(s_new)
        new_r.append(xr.reshape(batch, groups, nst))
        new_i.append(xi.reshape(batch, groups, nst))
    return xf.reshape(batch, seq, d), jnp.stack(new_h), jnp.stack(new_r), jnp.stack(new_i)


def kernel(x_prompt, x_sample, state_hgrn, state_s5_re, state_s5_im, norm1_g, w_in, hgrn_lb_logits,
           hgrn_onorm_g, s5_lambda_re, s5_lambda_im, s5_log_step, s5_B_re, s5_B_im, s5_C_re, s5_C_im,
           s5_D, s5_w_glu, s5_b_glu, w_out, norm2_g, w_ff1, w_ff2, final_norm_g):
    depth, d = norm1_g.shape
    row3 = lambda a: a.reshape(depth, 1, a.shape[-1])
    lbr, lbi, zoh_r, zoh_i = _s5_discretise(s5_lambda_re, s5_lambda_im, s5_log_step)
    s5_tb, s5_tcc, s5_tg, s5_lam2 = _s5_pair_weights(lbr, lbi, zoh_r, zoh_i, s5_B_re, s5_B_im,
                                                     s5_C_re, s5_C_im, s5_D)
    wts = {
        "norm1_g": row3(norm1_g), "norm2_g": row3(norm2_g), "final_g": final_norm_g.reshape(1, d),
        "lb_logits": hgrn_lb_logits, "onorm_g": row3(hgrn_onorm_g),
        "w_in": w_in.astype(BF16), "w_out": w_out.astype(BF16),
        "w_ff1": w_ff1.astype(BF16), "w_ff2": w_ff2.astype(BF16),
        "w_glu": s5_w_glu.astype(BF16), "b_glu": row3(s5_b_glu),
        "s5_tb": s5_tb, "s5_tcc": s5_tcc, "s5_tg": s5_tg, "s5_lam2": s5_lam2,
    }
    bp = x_prompt.shape[0]
    zh = jnp.zeros((depth, bp) + state_hgrn.shape[2:], F32)
    zs = jnp.zeros((depth, bp) + state_s5_re.shape[2:], F32)
    y_p, hp, rp, ip = _trunk(x_prompt, zh, zs, zs, wts)
    y_s, hs, rs, is_ = _trunk(x_sample, state_hgrn, state_s5_re, state_s5_im, wts)
    return (y_p, y_s, hp, rp, ip, hs, rs, is_)
```

```python
import functools

import jax
import jax.numpy as jnp
from jax import lax
from jax.experimental import pallas as pl
from jax.experimental.pallas import tpu as pltpu

F32 = jnp.float32
BF16 = jnp.bfloat16
EPS = 1e-6

HEAD_DIM = 128
HGRN_CHUNK = 64
HGRN_SUB = 16
HGRN_HEADS_PER_STEP = 4
SUBLANES = 8
LANES = 128
S5_SCAN_LANES = 256
S5_ROWS = 512

V7X_VMEM_CAP = 56 * 1024 * 1024
NORM_ROWS = 64
FFN_ACC_CHUNKS = 4
INPROJ_CHUNKS = 4


def _vmem_limit(nbytes):
    return int(min(V7X_VMEM_CAP, nbytes * 5 // 4 + (4 << 20)))


def _rmsnorm(x, g):
    return x * lax.rsqrt(jnp.mean(jnp.square(x), axis=-1, keepdims=True) + EPS) * g


def _sigmoid(x):
    return 1.0 / (1.0 + jnp.exp(-x))


def _log1p_exp_neg_abs(x):
    return jnp.log(1.0 + jnp.exp(-jnp.abs(x)))


def _softplus(x):
    return jnp.maximum(x, 0.0) + _log1p_exp_neg_abs(x)


def _dot(a, b):
    return jnp.dot(a, b, preferred_element_type=F32)


def _dot_nt(a, b):
    return lax.dot_general(a, b, (((1,), (1,)), ((), ())), preferred_element_type=F32)


def _inproj_kernel(x_ref, g_ref, w_ref, lbl_ref, p_ref, u_ref, h_ref, *, layer):
    tm = x_ref.shape[0]
    cw = w_ref.shape[1] // INPROJ_CHUNKS
    j = pl.program_id(1)

    @pl.when(j == 0)
    def _():
        g = g_ref[...]

        def body(i, carry):
            r = pl.ds(pl.multiple_of(i * NORM_ROWS, NORM_ROWS), NORM_ROWS)
            h_ref[r, :] = _rmsnorm(x_ref[r, :], g).astype(BF16)
            return carry

        lax.fori_loop(0, tm // NORM_ROWS, body, 0)

    def emit(out_ref, fn):
        for c in range(INPROJ_CHUNKS):
            sl = slice(c * cw, (c + 1) * cw)
            out_ref[:, sl] = fn(_dot(h_ref[...], w_ref[:, sl]), sl)

    def silu(a, sl):
        return a * _sigmoid(a)

    def log_forget(z, sl):
        logits = lbl_ref[:, sl]
        e = jnp.exp(logits - jnp.max(logits, axis=0, keepdims=True))
        sm = e / jnp.sum(e, axis=0, keepdims=True)
        cum0 = sm[0:1]
        cuml = cum0
        for k in range(1, layer + 1):
            cuml = cuml + sm[k:k + 1]
        lb = cuml - cum0
        log_lb = jnp.log(lb)
        b = jnp.log1p(-lb) - _softplus(-z)
        delta = log_lb - b
        return jnp.where(jnp.isnan(delta), log_lb + b,
                         jnp.maximum(log_lb, b) + _log1p_exp_neg_abs(delta))

    for tile, (out_ref, fn) in enumerate([(p_ref, silu), (p_ref, log_forget), (p_ref, lambda a, sl: a),
                                          (p_ref, silu), (u_ref, lambda a, sl: a)]):
        pl.when(j == tile)(functools.partial(emit, out_ref, fn))


def _inproj(x, g, w, lb_logits, *, layer, tm, tn, batch, seq):
    n, d = x.shape
    cols = w.shape[2]
    n_j = cols // tn
    depth = lb_logits.shape[0]
    assert n_j == 5 and lb_logits.shape[1] == tn
    time_major = seq % tm == 0
    if time_major:
        n_t = seq // tm
        u_shape, u_map = (seq, batch * tn), (lambda i, j: (i % n_t, i // n_t))
    else:
        u_shape, u_map = (n, tn), (lambda i, j: (i, 0))
    need = 2 * tm * d * 4 + tm * d * 2 + 2 * d * tn * 2 + 5 * tm * tn * 4
    return pl.pallas_call(
        functools.partial(_inproj_kernel, layer=layer),
        out_shape=(jax.ShapeDtypeStruct((n, cols - tn), F32), jax.ShapeDtypeStruct(u_shape, F32)),
        grid=(n // tm, n_j),
        in_specs=[
            pl.BlockSpec((tm, d), lambda i, j: (i, 0)),
            pl.BlockSpec((None, 1, d), lambda i, j: (layer, 0, 0)),
            pl.BlockSpec((None, d, tn), lambda i, j: (layer, 0, j)),
            pl.BlockSpec((depth, tn), lambda i, j: (0, 0)),
        ],
        out_specs=(pl.BlockSpec((tm, tn), lambda i, j: (i, jnp.minimum(j, n_j - 2))),
                   pl.BlockSpec((tm, tn), u_map)),
        scratch_shapes=[pltpu.VMEM((tm, d), BF16)],
        compiler_params=pltpu.CompilerParams(
            dimension_semantics=("parallel", "arbitrary"),
            vmem_limit_bytes=_vmem_limit(need)),
        name="inproj",
    )(x, g, w, lb_logits)


def _hgrn_decayed_queries(q_lo, q_hi, f_row, row_is):
    slabs = []
    qd_lo = qd_hi = None
    for s in range(2 * SUBLANES - 1, -1, -1):
        if s == 2 * SUBLANES - 1:
            qd_hi = jnp.where(row_is[SUBLANES - 1], q_hi, 0.0)
        elif s >= SUBLANES:
            qd_hi = jnp.where(row_is[s - SUBLANES], q_hi, qd_hi * f_row(s + 1))
        else:
            fn = f_row(s + 1)
            qd_hi = qd_hi * fn
            qd_lo = jnp.where(row_is[s], q_lo, 0.0 if s == SUBLANES - 1 else qd_lo * fn)
        slabs.append((s, 1, qd_hi))
        if s < SUBLANES:
            slabs.append((s, 0, qd_lo))
    return slabs


def _hgrn_kernel(q_ref, f_ref, i_ref, g_ref, on_ref, s0_ref, o_ref, sout_ref,
                 st_ref, ks_ref, cs_ref, fs_ref, fc_ref, op_ref, *, chunk, heads_blk):
    c_idx = pl.program_id(2)
    tc = q_ref.shape[0]
    n_sub = chunk // HGRN_SUB

    @pl.when(c_idx == 0)
    def _():
        for h in range(heads_blk):
            st_ref[h] = s0_ref[0, h].T

    logf = f_ref[...]
    fgate = jnp.exp(logf)
    fs_ref[...] = fgate
    ks_ref[...] = 1.0 - fgate

    t_i = lax.broadcasted_iota(jnp.int32, (chunk, chunk), 0)
    s_i = lax.broadcasted_iota(jnp.int32, (chunk, chunk), 1)
    tri = jnp.where(t_i >= s_i, 1.0, 0.0).astype(BF16)
    for j in range(tc // chunk):
        lf = logf[j * chunk:(j + 1) * chunk]
        hi = lf.astype(BF16)
        r1 = lf - hi.astype(F32)
        mid = r1.astype(BF16)
        lo = (r1 - mid.astype(F32)).astype(BF16)
        cs_ref[j * chunk:(j + 1) * chunk, :] = _dot(tri, hi) + _dot(tri, mid) + _dot(tri, lo)

    row8 = lax.broadcasted_iota(jnp.int32, (SUBLANES, HEAD_DIM), 0)
    row_is = [row8 == j for j in range(SUBLANES)]
    lane8 = lax.broadcasted_iota(jnp.int32, (SUBLANES, chunk), 1)
    lane_is = [lane8 == j for j in range(chunk)]
    onorm = on_ref[...]
    blk = lambda a, j: a[j * HGRN_SUB:(j + 1) * HGRN_SUB]

    head_cols = [slice(h * HEAD_DIM, (h + 1) * HEAD_DIM) for h in range(heads_blk)]

    def finish(rows):
        for cols in head_cols:
            o = op_ref[:, cols]
            o = o * lax.rsqrt(jnp.mean(jnp.square(o), axis=-1, keepdims=True) + EPS)
            o = o * onorm[:, cols]
            o_ref[rows, cols] = (o * g_ref[rows, cols]).astype(BF16)

    op_ref[...] = jnp.zeros(op_ref.shape, F32)

    def chunk_body(c, carry):
        base = pl.multiple_of(c * chunk, chunk)
        r = pl.ds(base, chunk)
        finish(pl.ds(pl.multiple_of(jnp.maximum(c - 1, 0) * chunk, chunk), chunk))
        fc_ref[...] = fs_ref[r, :]
        cum_all = cs_ref[r, :]

        zero_blk = jnp.zeros((HGRN_SUB, HEAD_DIM), F32)
        qs, vbs, kbs, o_inter, a_off = [], [], [], [], []
        for cols in head_cols:
            h = len(qs)
            cum = cum_all[:, cols]
            q = q_ref[r, cols]
            k = ks_ref[r, cols]
            v = i_ref[r, cols]
            ends = [cum[(j + 1) * HGRN_SUB - 1:(j + 1) * HGRN_SUB, :] for j in range(n_sub)]
            cl = ends[-1]
            ke = [blk(k, j) * jnp.exp(ends[j] - blk(cum, j)) for j in range(n_sub)]
            qe = [blk(q, j) * jnp.exp(blk(cum, j) - ends[j - 1] if j else blk(cum, j)) for j in range(n_sub)]

            st = st_ref[h]
            q_in = jnp.concatenate([qe[j] * jnp.exp(ends[j - 1]) if j else qe[j] for j in range(n_sub)], axis=0)
            o_inter.append(_dot_nt(q_in.astype(BF16), st.astype(BF16)))
            kdec = jnp.concatenate(
                [ke[j] * jnp.exp(cl - ends[j]) if j < n_sub - 1 else ke[j] for j in range(n_sub)], axis=0)
            st_ref[h] = st * jnp.exp(cl) + _dot(v.T.astype(BF16), kdec.astype(BF16))

            offs = [None]
            for i in range(1, n_sub):
                kt = jnp.concatenate(
                    [ke[j] * jnp.exp(ends[i - 1] - ends[j]) if j < i - 1 else ke[j] for j in range(i)]
                    + [zero_blk] * (n_sub - i), axis=0)
                offs.append(_dot_nt(qe[i].astype(BF16), kt.astype(BF16)))
            a_off.append(offs)
            qs.append(q)
            vbs.append(v.astype(BF16))
            kbs.append(k.astype(BF16))

        res = []
        for h, cols in enumerate(head_cols):
            per_head = []
            for i in range(n_sub):
                lo_r = i * HGRN_SUB
                f_row = lambda s, lo_r=lo_r, cols=cols: fc_ref[lo_r + s:lo_r + s + 1, cols]
                q_blk = blk(qs[h], i)
                slabs = _hgrn_decayed_queries(q_blk[:SUBLANES], q_blk[SUBLANES:], f_row, row_is)
                stack = jnp.concatenate([slab for _, _, slab in slabs], axis=0)
                per_head.append(([(s, half) for s, half, _ in slabs],
                                 _dot_nt(stack.astype(BF16), kbs[h])))
            res.append(per_head)

        o_intra = []
        for h in range(heads_blk):
            rows = []
            for i in range(n_sub):
                order, prod = res[h][i]
                halves = [jnp.zeros((SUBLANES, chunk), F32), jnp.zeros((SUBLANES, chunk), F32)]
                for idx, (s, half) in enumerate(order):
                    piece = prod[idx * SUBLANES:(idx + 1) * SUBLANES]
                    halves[half] = jnp.where(lane_is[i * HGRN_SUB + s], piece, halves[half])
                a = jnp.concatenate(halves, axis=0)
                rows.append(a + a_off[h][i] if i else a)
            attn = jnp.concatenate(rows, axis=0)
            o_intra.append(_dot(attn.astype(BF16), vbs[h]))

        for h, cols in enumerate(head_cols):
            op_ref[:, cols] = o_inter[h] + o_intra[h]
        return carry

    lax.fori_loop(0, tc // chunk, chunk_body, 0, unroll=min(2, tc // chunk))
    finish(pl.ds(tc - chunk, chunk))

    @pl.when(c_idx == pl.num_programs(2) - 1)
    def _():
        for h in range(heads_blk):
            sout_ref[0, h] = st_ref[h].T


def _hgrn(p, onorm_g, s0, *, batch, seq, layer):
    n = p.shape[0]
    heads = s0.shape[2]
    hb = HGRN_HEADS_PER_STEP
    wblk = hb * HEAD_DIM
    n_hb = heads // hb
    width = heads * HEAD_DIM
    tc = min(512, seq)
    chunk = min(HGRN_CHUNK, seq)
    n_t = seq // tc
    row = lambda b, h, c: b * n_t + c
    need = 2 * 4 * tc * wblk * 4 + 2 * tc * wblk * 2 + 3 * tc * wblk * 4 + 5 * hb * HEAD_DIM * HEAD_DIM * 4
    kernel = functools.partial(_hgrn_kernel, chunk=chunk, heads_blk=hb)
    st_spec = pl.BlockSpec((None, 1, hb, HEAD_DIM, HEAD_DIM), lambda b, h, c: (layer, b, h, 0, 0))
    return pl.pallas_call(
        kernel,
        out_shape=(jax.ShapeDtypeStruct((n, width), BF16),
                   jax.ShapeDtypeStruct((batch, heads, HEAD_DIM, HEAD_DIM), F32)),
        grid=(batch, n_hb, n_t),
        in_specs=[
            pl.BlockSpec((tc, wblk), lambda b, h, c: (row(b, h, c), h)),
            pl.BlockSpec((tc, wblk), lambda b, h, c: (row(b, h, c), n_hb + h)),
            pl.BlockSpec((tc, wblk), lambda b, h, c: (row(b, h, c), 2 * n_hb + h)),
            pl.BlockSpec((tc, wblk), lambda b, h, c: (row(b, h, c), 3 * n_hb + h)),
            pl.BlockSpec((None, 1, wblk), lambda b, h, c: (layer, 0, h)),
            st_spec,
        ],
        out_specs=(
            pl.BlockSpec((tc, wblk), lambda b, h, c: (row(b, h, c), h)),
            pl.BlockSpec((1, hb, HEAD_DIM, HEAD_DIM), lambda b, h, c: (b, h, 0, 0)),
        ),
        scratch_shapes=[
            pltpu.VMEM((hb, HEAD_DIM, HEAD_DIM), F32),
            pltpu.VMEM((tc, wblk), F32),
            pltpu.VMEM((tc, wblk), F32),
            pltpu.VMEM((tc, wblk), F32),
            pltpu.VMEM((chunk, wblk), F32),
            pltpu.VMEM((chunk, wblk), F32),
        ],
        compiler_params=pltpu.CompilerParams(
            dimension_semantics=("parallel", "parallel", "arbitrary"),
            vmem_limit_bytes=_vmem_limit(need)),
        name="hgrn",
    )(p, p, p, p, onorm_g, s0)


def _s5_disc_lambda_kernel(lr_ref, li_ref, ls_ref, lbr_ref, lbi_ref, cr_ref, ci_ref):
    lr, li = lr_ref[...], li_ref[...]
    dt = jnp.exp(ls_ref[...])
    mag = jnp.exp(dt * lr)
    ang = dt * li
    lbr = mag * jnp.cos(ang)
    lbi = mag * jnp.sin(ang)
    nr, ni = lbr - 1.0, lbi
    den = lr * lr + li * li
    lbr_ref[...] = lbr
    lbi_ref[...] = lbi
    cr_ref[...] = (nr * lr + ni * li) / den
    ci_ref[...] = (ni * lr - nr * li) / den


def _s5_discretise(lam_re, lam_im, log_step):
    depth, groups, nst = lam_re.shape
    rows = depth * groups
    shp = jax.ShapeDtypeStruct((rows, nst), F32)
    outs = pl.pallas_call(
        _s5_disc_lambda_kernel, out_shape=(shp, shp, shp, shp), name="s5_disc_lambda",
    )(lam_re.reshape(rows, nst), lam_im.reshape(rows, nst), log_step.reshape(rows, 1))
    return tuple(o.reshape(depth, groups, nst) for o in outs)


def _s5_pair_weights(lbr, lbi, zoh_r, zoh_i, b_re, b_im, c_re, c_im, d_skip):
    depth, groups, nst, cg = b_re.shape
    hi = lax.Precision.HIGHEST
    zr, zi = zoh_r[..., None], zoh_i[..., None]
    bbr, bbi = zr * b_re - zi * b_im, zr * b_im + zi * b_re
    lr, li = lbr[..., None], lbi[..., None]
    blr, bli = bbr * lr - bbi * li, bbr * li + bbi * lr
    tb = jnp.swapaxes(jnp.stack([blr, bli, bbr, bbi], axis=1), 3, 4).reshape(depth, 4, groups * cg, nst)

    lrc, lic = lbr[:, :, None, :], lbi[:, :, None, :]
    pr, pi = c_re * lrc - c_im * lic, c_re * lic + c_im * lrc
    qr, qi = pr * lrc - pi * lic, pr * lic + pi * lrc
    tcc = jnp.stack([pr, -pi, qr, -qi], axis=1).transpose(0, 1, 3, 2, 4).reshape(depth, 4, cg, groups * nst)

    mm = lambda a, b: jnp.einsum("lgcn,lgnd->lgcd", a, b, precision=hi)
    g0 = mm(c_re, bbr) - mm(c_im, bbi)
    g1 = mm(pr, bbr) - mm(pi, bbi)
    g0d = g0 + d_skip.reshape(depth, groups, cg)[..., None] * jnp.eye(cg, dtype=F32)
    tg = jnp.stack([g0d, g1], axis=1).transpose(0, 1, 3, 2, 4).reshape(depth, 2, cg, groups * cg)
    lam2 = jnp.stack([lbr * lbr - lbi * lbi, 2.0 * lbr * lbi], axis=1).reshape(depth, 2, groups * nst)
    return tb, tcc, tg, lam2


def _s5_embed_weights(tb_ref, tcc_ref, tg_ref, wb_ref, wct_ref):
    nbk = wb_ref.shape[0]
    cg, nst = tcc_ref.shape[1], tb_ref.shape[2]
    bl = wb_ref.shape[2] // 2
    assert cg & (cg - 1) == 0 and nst & (nst - 1) == 0
    c_shift, n_shift = cg.bit_length() - 1, nst.bit_length() - 1
    iota = lambda shape, dim: lax.broadcasted_iota(jnp.int32, shape, dim)
    e_in = jnp.where((iota((nst, bl), 1) & (nst - 1)) == iota((nst, bl), 0), 1.0, 0.0).astype(BF16)
    e_out = jnp.where((iota((LANES, cg), 0) & (cg - 1)) == iota((LANES, cg), 1), 1.0, 0.0).astype(BF16)
    m_state = (iota((LANES, bl), 0) >> c_shift) == (iota((LANES, bl), 1) >> n_shift)
    m_direct = (iota((LANES, LANES), 0) >> c_shift) == (iota((LANES, LANES), 1) >> c_shift)
    zeros = jnp.zeros((LANES, LANES), BF16)
    for j in range(nbk):
        for k in range(4):
            half, part = divmod(k, 2)
            a = tb_ref[k, j * LANES:(j + 1) * LANES, :].astype(BF16)
            wb_ref[j, half * LANES:(half + 1) * LANES, part * bl:(part + 1) * bl] = (
                jnp.where(m_state, _dot(a, e_in), 0.0).astype(BF16))
        for k in range(4):
            eo, part = divmod(k, 2)
            x = tcc_ref[k, :, j * bl:(j + 1) * bl].astype(BF16)
            wct_ref[j, eo * LANES:(eo + 1) * LANES, part * bl:(part + 1) * bl] = (
                jnp.where(m_state, _dot(e_out, x), 0.0).astype(BF16))
        direct = [jnp.where(m_direct, _dot(e_out, tg_ref[k, :, j * LANES:(j + 1) * LANES].astype(BF16)), 0.0
                            ).astype(BF16) for k in range(2)]
        base = 2 * bl
        wct_ref[j, 0:LANES, base:base + LANES] = direct[0]
        wct_ref[j, 0:LANES, base + LANES:base + 2 * LANES] = zeros
        wct_ref[j, LANES:2 * LANES, base:base + LANES] = direct[1]
        wct_ref[j, LANES:2 * LANES, base + LANES:base + 2 * LANES] = direct[0]


def _s5_kernel(u_ref, tb_ref, tcc_ref, tg_ref, lam2_ref, wg_ref, bg_ref, x0r_ref, x0i_ref,
               o_ref, xr_ref, xi_ref, xs_ref, y_ref, st_ref, ue_ref, uo_ref, tmp_ref, wb_ref, wct_ref,
               *, batch):
    c_idx = pl.program_id(0)
    tt = u_ref.shape[0]
    tp = tt // 2
    prows = tp * batch
    width = u_ref.shape[1] // batch
    n_slab = width // LANES
    nbk = wb_ref.shape[0]
    bl = wb_ref.shape[2] // 2
    ns = nbk * bl
    grp = st_ref.shape[1]
    per = grp // batch

    @pl.when(c_idx == 0)
    def _():
        st_ref[0] = jnp.concatenate([x0r_ref[...]] * per, axis=0)
        st_ref[1] = jnp.concatenate([x0i_ref[...]] * per, axis=0)
        _s5_embed_weights(tb_ref, tcc_ref, tg_ref, wb_ref, wct_ref)

    for b in range(batch):
        for m in range(n_slab):
            tmp_ref[...] = u_ref[:, b * width + m * LANES:b * width + (m + 1) * LANES]
            ue_ref[m, pl.ds(b, tp, stride=batch), :] = tmp_ref[pl.ds(0, tp, stride=2), :]
            uo_ref[m, pl.ds(b, tp, stride=batch), :] = tmp_ref[pl.ds(1, tp, stride=2), :]

    for j in range(nbk):
        lhs = jnp.concatenate([ue_ref[j], uo_ref[j]], axis=1).astype(BF16)
        w = _dot(lhs, wb_ref[j])
        xs_ref[:, j * bl:(j + 1) * bl] = w[:, :bl]
        xs_ref[:, ns + j * bl:ns + (j + 1) * bl] = w[:, bl:]

    second = lax.broadcasted_iota(jnp.int32, (grp, S5_SCAN_LANES), 0) >= batch
    for cb in range(ns // S5_SCAN_LANES):
        lo = cb * S5_SCAN_LANES
        re_l = slice(lo, lo + S5_SCAN_LANES)
        im_l = slice(ns + lo, ns + lo + S5_SCAN_LANES)
        ar = lam2_ref[0:1, re_l]
        ai = lam2_ref[1:2, re_l]

        def step(g, carry, ar=ar, ai=ai, re_l=re_l, im_l=im_l):
            cr, ci = carry
            r = pl.ds(pl.multiple_of(g * grp, grp), grp)
            wr = xs_ref[r, re_l]
            wi = xs_ref[r, im_l]
            if per == 2:
                tr = ar * cr - ai * ci + pltpu.roll(wr, batch, axis=0)
                ti = ar * ci + ai * cr + pltpu.roll(wi, batch, axis=0)
                xs_ref[r, re_l] = jnp.where(second, tr, cr)
                xs_ref[r, im_l] = jnp.where(second, ti, ci)
                vr = ar * tr - ai * ti + wr
                vi = ar * ti + ai * tr + wi
                nr = jnp.where(second, vr, pltpu.roll(vr, batch, axis=0))
                ni = jnp.where(second, vi, pltpu.roll(vi, batch, axis=0))
            else:
                xs_ref[r, re_l] = cr
                xs_ref[r, im_l] = ci
                nr = ar * cr - ai * ci + wr
                ni = ar * ci + ai * cr + wi
            return nr, ni

        cr, ci = lax.fori_loop(0, prows // grp, step, (st_ref[0, :, re_l], st_ref[1, :, re_l]), unroll=2)
        st_ref[0, :, re_l] = cr
        st_ref[1, :, re_l] = ci

    for j in range(nbk):
        lhs = jnp.concatenate([xs_ref[:, j * bl:(j + 1) * bl], xs_ref[:, ns + j * bl:ns + (j + 1) * bl],
                               ue_ref[j], uo_ref[j]], axis=1).astype(BF16)
        yj = _dot_nt(lhs, wct_ref[j])
        y_ref[0:prows, j * LANES:(j + 1) * LANES] = yj[:, :LANES]
        y_ref[prows:2 * prows, j * LANES:(j + 1) * LANES] = yj[:, LANES:]

    hh = jax.nn.gelu(y_ref[...])
    gate = _sigmoid(_dot(hh.astype(BF16), wg_ref[...]) + bg_ref[...])
    out = hh * gate
    for m in range(n_slab):
        ue_ref[m] = out[0:prows, m * LANES:(m + 1) * LANES]
        uo_ref[m] = out[prows:2 * prows, m * LANES:(m + 1) * LANES]
    for b in range(batch):
        for m in range(n_slab):
            tmp_ref[pl.ds(0, tp, stride=2), :] = ue_ref[m, pl.ds(b, tp, stride=batch), :]
            tmp_ref[pl.ds(1, tp, stride=2), :] = uo_ref[m, pl.ds(b, tp, stride=batch), :]
            o_ref[:, b * width + m * LANES:b * width + (m + 1) * LANES] = tmp_ref[...].astype(BF16)

    @pl.when(c_idx == pl.num_programs(0) - 1)
    def _():
        xr_ref[...] = st_ref[0, grp - batch:grp, :]
        xi_ref[...] = st_ref[1, grp - batch:grp, :]


def _s5(u, tb, tcc, tg, lam2, w_glu, b_glu, x0r, x0i, *, batch, layer):
    seq = u.shape[0]
    width = u.shape[1] // batch
    n = seq * batch
    ns = lam2.shape[2]
    nbk = width // LANES
    bl = ns // nbk
    assert batch % SUBLANES == 0 or 2 * batch == SUBLANES
    assert seq % 2 == 0 and tb.shape[2] == width
    grp = max(batch, SUBLANES)
    rows = min(S5_ROWS, n)
    tt = rows // batch
    prows = rows // 2
    lay4 = lambda c: (layer, 0, 0, 0)
    lay3 = lambda c: (layer, 0, 0)
    st_in = pl.BlockSpec((None, batch, ns), lay3)
    resident = pl.Buffered(1)
    wb_shape = (nbk, 2 * LANES, 2 * bl)
    wct_shape = (nbk, 2 * LANES, 2 * bl + 2 * LANES)
    need = (2 * rows * width * 4 + (nbk * 2 * LANES * (4 * bl + 2 * LANES) + width * width) * 2
            + 4 * width * LANES * 4 + 6 * 16 * ns * 4
            + 2 * rows * width * 2 + prows * 2 * ns * 4 + rows * width * 4 + 4 * rows * width * 4
            + 4 * prows * (4 * bl + 2 * LANES) + (8 * batch + 2 * grp) * ns * 4)
    st_shape = jax.ShapeDtypeStruct((batch, ns), F32)
    return pl.pallas_call(
        functools.partial(_s5_kernel, batch=batch),
        out_shape=(jax.ShapeDtypeStruct((seq, batch * width), BF16), st_shape, st_shape),
        grid=(n // rows,),
        in_specs=[
            pl.BlockSpec((tt, batch * width), lambda c: (c, 0)),
            pl.BlockSpec((None,) + tb.shape[1:], lay4, pipeline_mode=resident),
            pl.BlockSpec((None,) + tcc.shape[1:], lay4, pipeline_mode=resident),
            pl.BlockSpec((None,) + tg.shape[1:], lay4, pipeline_mode=resident),
            pl.BlockSpec((None, 2, ns), lay3),
            pl.BlockSpec((None, width, width), lay3, pipeline_mode=resident),
            pl.BlockSpec((None, 1, width), lay3),
            st_in,
            st_in,
        ],
        out_specs=(
            pl.BlockSpec((tt, batch * width), lambda c: (c, 0)),
            pl.BlockSpec((batch, ns), lambda c: (0, 0)),
            pl.BlockSpec((batch, ns), lambda c: (0, 0)),
        ),
        scratch_shapes=[
            pltpu.VMEM((prows, 2 * ns), F32),
            pltpu.VMEM((rows, width), F32),
            pltpu.VMEM((2, grp, ns), F32),
            pltpu.VMEM((width // LANES, prows, LANES), F32),
            pltpu.VMEM((width // LANES, prows, LANES), F32),
            pltpu.VMEM((tt, LANES), F32),
            pltpu.VMEM(wb_shape, BF16),
            pltpu.VMEM(wct_shape, BF16),
        ],
        compiler_params=pltpu.CompilerParams(
            dimension_semantics=("arbitrary",),
            vmem_limit_bytes=_vmem_limit(need)),
        name="s5",
    )(u, tb, tcc, tg, lam2, w_glu, b_glu, x0r, x0i)


def _outproj_kernel(x_ref, oh_ref, os_ref, wh_ref, ws_ref, o_ref):
    o_ref[...] = x_ref[...] + _dot(oh_ref[...], wh_ref[...]) + _dot(os_ref[...], ws_ref[...])


def _outproj(x, o_h, o_s, w_out, *, layer, tm, tn, seq):
    n, d = x.shape
    kh = o_h.shape[1]
    ks = w_out.shape[1] - kh
    assert kh == ks
    if o_s.shape[0] == n:
        os_map = lambda i, j: (i, 0)
    else:
        n_t = seq // tm
        os_map = lambda i, j: (i % n_t, i // n_t)
    need = 2 * (2 * tm * tn * 4 + tm * (kh + ks) * 2 + (kh + ks) * tn * 2) + 2 * tm * tn * 4
    return pl.pallas_call(
        _outproj_kernel,
        out_shape=jax.ShapeDtypeStruct((n, d), F32),
        grid=(n // tm, d // tn),
        in_specs=[
            pl.BlockSpec((tm, tn), lambda i, j: (i, j)),
            pl.BlockSpec((tm, kh), lambda i, j: (i, 0)),
            pl.BlockSpec((tm, ks), os_map),
            pl.BlockSpec((None, kh, tn), lambda i, j: (layer, 0, j)),
            pl.BlockSpec((None, ks, tn), lambda i, j: (layer, 1, j)),
        ],
        out_specs=pl.BlockSpec((tm, tn), lambda i, j: (i, j)),
        compiler_params=pltpu.CompilerParams(
            dimension_semantics=("parallel", "parallel"),
            vmem_limit_bytes=_vmem_limit(need)),
        name="outproj",
    )(x, o_h, o_s, w_out, w_out)


def _ffn_kernel(x_ref, g_ref, w1_ref, w2_ref, gf_ref, o_ref, h_ref, *, final_norm):
    f = pl.program_id(1)
    tm = x_ref.shape[0]

    @pl.when(f == 0)
    def _():
        g = g_ref[...]

        def body(i, carry):
            r = pl.ds(pl.multiple_of(i * NORM_ROWS, NORM_ROWS), NORM_ROWS)
            h_ref[r, :] = _rmsnorm(x_ref[r, :], g).astype(BF16)
            o_ref[r, :] = jnp.zeros((NORM_ROWS, o_ref.shape[1]), F32)
            return carry

        lax.fori_loop(0, tm // NORM_ROWS, body, 0)

    a = _dot(h_ref[...], w1_ref[...])
    a = jnp.square(jnp.maximum(a, 0.0)).astype(BF16)
    cw = o_ref.shape[1] // FFN_ACC_CHUNKS
    for c in range(FFN_ACC_CHUNKS):
        o_ref[:, c * cw:(c + 1) * cw] += _dot(a, w2_ref[:, c * cw:(c + 1) * cw])

    @pl.when(f == pl.num_programs(1) - 1)
    def _():
        gf = gf_ref[...]

        def body(i, carry):
            r = pl.ds(pl.multiple_of(i * NORM_ROWS, NORM_ROWS), NORM_ROWS)
            y = x_ref[r, :] + o_ref[r, :]
            if final_norm:
                y = _rmsnorm(y, gf)
            o_ref[r, :] = y
            return carry

        lax.fori_loop(0, tm // NORM_ROWS, body, 0)


def _ffn(x, g, w1, w2, gf, *, layer, tm, tf, final_norm):
    n, d = x.shape
    ff = w1.shape[2]
    need = tm * d * 4 + tm * d * 2 + 2 * tm * d * 4 + 4 * d * tf * 2 + tm * tf * 6 + tm * d * 4
    kernel = functools.partial(_ffn_kernel, final_norm=final_norm)
    return pl.pallas_call(
        kernel,
        out_shape=jax.ShapeDtypeStruct((n, d), F32),
        grid=(n // tm, ff // tf),
        in_specs=[
            pl.BlockSpec((tm, d), lambda i, f: (i, 0), pipeline_mode=pl.Buffered(1)),
            pl.BlockSpec((None, 1, d), lambda i, f: (layer, 0, 0)),
            pl.BlockSpec((None, d, tf), lambda i, f: (layer, 0, f)),
            pl.BlockSpec((None, tf, d), lambda i, f: (layer, f, 0)),
            pl.BlockSpec((1, d), lambda i, f: (0, 0)),
        ],
        out_specs=pl.BlockSpec((tm, d), lambda i, f: (i, 0)),
        scratch_shapes=[pltpu.VMEM((tm, d), BF16)],
        compiler_params=pltpu.CompilerParams(
            dimension_semantics=("parallel", "arbitrary"),
            vmem_limit_bytes=_vmem_limit(need)),
        name="ffn",
    )(x, g, w1, w2, gf)


def _trunk(x, st_h, st_r, st_i, wts):
    batch, seq, d = x.shape
    n = batch * seq
    depth = wts["w_in"].shape[0]
    heads = st_h.shape[2]
    hgrn_width = heads * HEAD_DIM
    groups, nst = st_r.shape[2], st_r.shape[3]
    s5_width = wts["w_glu"].shape[1]
    tm = min(1024, n)
    xf = x.reshape(n, d)
    x0r = st_r.reshape(depth, batch, groups * nst)
    x0i = st_i.reshape(depth, batch, groups * nst)
    time_major = seq % tm == 0
    new_h, new_r, new_i = [], [], []
    for l in range(depth):
        p, u = _inproj(xf, wts["norm1_g"], wts["w_in"], wts["lb_logits"],
                       layer=l, tm=tm, tn=s5_width, batch=batch, seq=seq)
        o_h, s_new = _hgrn(p, wts["onorm_g"], st_h, batch=batch, seq=seq, layer=l)
        if not time_major:
            u = u.reshape(batch, seq, s5_width).transpose(1, 0, 2).reshape(seq, batch * s5_width)
        o_s, xr, xi = _s5(u, wts["s5_tb"], wts["s5_tcc"], wts["s5_tg"], wts["s5_lam2"], wts["w_glu"],
                          wts["b_glu"], x0r, x0i, batch=batch, layer=l)
        if not time_major:
            o_s = o_s.reshape(seq, batch, s5_width).transpose(1, 0, 2).reshape(n, s5_width)
        x1 = _outproj(xf, o_h, o_s, wts["w_out"], layer=l, tm=min(512, n), tn=d, seq=seq)
        xf = _ffn(x1, wts["norm2_g"], wts["w_ff1"], wts["w_ff2"], wts["final_g"],
                  layer=l, tm=tm, tf=1024, final_norm=(l == depth - 1))
        new_h.append(s_new)
        new_r.append(xr.reshape(batch, groups, nst))
        new_i.append(xi.reshape(batch, groups, nst))
    return xf.reshape(batch, seq, d), jnp.stack(new_h), jnp.stack(new_r), jnp.stack(new_i)


def kernel(x_prompt, x_sample, state_hgrn, state_s5_re, state_s5_im, norm1_g, w_in, hgrn_lb_logits,
           hgrn_onorm_g, s5_lambda_re, s5_lambda_im, s5_log_step, s5_B_re, s5_B_im, s5_C_re, s5_C_im,
           s5_D, s5_w_glu, s5_b_glu, w_out, norm2_g, w_ff1, w_ff2, final_norm_g):
    depth, d = norm1_g.shape
    row3 = lambda a: a.reshape(depth, 1, a.shape[-1])
    lbr, lbi, zoh_r, zoh_i = _s5_discretise(s5_lambda_re, s5_lambda_im, s5_log_step)
    s5_tb, s5_tcc, s5_tg, s5_lam2 = _s5_pair_weights(lbr, lbi, zoh_r, zoh_i, s5_B_re, s5_B_im,
                                                     s5_C_re, s5_C_im, s5_D)
    wts = {
        "norm1_g": row3(norm1_g), "norm2_g": row3(norm2_g), "final_g": final_norm_g.reshape(1, d),
        "lb_logits": hgrn_lb_logits, "onorm_g": row3(hgrn_onorm_g),
        "w_in": w_in.astype(BF16), "w_out": w_out.astype(BF16),
        "w_ff1": w_ff1.astype(BF16), "w_ff2": w_ff2.astype(BF16),
        "w_glu": s5_w_glu.astype(BF16), "b_glu": row3(s5_b_glu),
        "s5_tb": s5_tb, "s5_tcc": s5_tcc, "s5_tg": s5_tg, "s5_lam2": s5_lam2,
    }
    bp = x_prompt.shape[0]
    zh = jnp.zeros((depth, bp) + state_hgrn.shape[2:], F32)
    zs = jnp.zeros((depth, bp) + state_s5_re.shape[2:], F32)
    y_p, hp, rp, ip = _trunk(x_prompt, zh, zs, zs, wts)
    y_s, hs, rs, is_ = _trunk(x_sample, state_hgrn, state_s5_re, state_s5_im, wts)
    return (y_p, y_s, hp, rp, ip, hs, rs, is_)
```

```python
import functools

import jax
import jax.numpy as jnp
from jax import lax
from jax.experimental import pallas as pl
from jax.experimental.pallas import tpu as pltpu

F32 = jnp.float32
BF16 = jnp.bfloat16
EPS = 1e-6

HEAD_DIM = 128
HGRN_CHUNK = 64
HGRN_SUB = 16
HGRN_HEADS_PER_STEP = 4
SUBLANES = 8
LANES = 128
S5_SCAN_LANES = 256
S5_ROWS = 512

V7X_VMEM_CAP = 56 * 1024 * 1024
NORM_ROWS = 64
FFN_ACC_CHUNKS = 4
INPROJ_CHUNKS = 4


def _vmem_limit(nbytes):
    return int(min(V7X_VMEM_CAP, nbytes * 5 // 4 + (4 << 20)))


def _rmsnorm(x, g):
    return x * lax.rsqrt(jnp.mean(jnp.square(x), axis=-1, keepdims=True) + EPS) * g


def _sigmoid(x):
    return 1.0 / (1.0 + jnp.exp(-x))


def _log1p_exp_neg_abs(x):
    return jnp.log(1.0 + jnp.exp(-jnp.abs(x)))


def _softplus(x):
    return jnp.maximum(x, 0.0) + _log1p_exp_neg_abs(x)


def _dot(a, b):
    return jnp.dot(a, b, preferred_element_type=F32)


def _dot_nt(a, b):
    return lax.dot_general(a, b, (((1,), (1,)), ((), ())), preferred_element_type=F32)


def _inproj_kernel(x_ref, g_ref, w_ref, lbl_ref, p_ref, u_ref, h_ref, *, layer):
    tm = x_ref.shape[0]
    cw = w_ref.shape[1] // INPROJ_CHUNKS
    j = pl.program_id(1)

    @pl.when(j == 0)
    def _():
        g = g_ref[...]

        def body(i, carry):
            r = pl.ds(pl.multiple_of(i * NORM_ROWS, NORM_ROWS), NORM_ROWS)
            h_ref[r, :] = _rmsnorm(x_ref[r, :], g).astype(BF16)
            return carry

        lax.fori_loop(0, tm // NORM_ROWS, body, 0)

    def emit(out_ref, fn):
        for c in range(INPROJ_CHUNKS):
            sl = slice(c * cw, (c + 1) * cw)
            out_ref[:, sl] = fn(_dot(h_ref[...], w_ref[:, sl]), sl)

    def silu(a, sl):
        return a * _sigmoid(a)

    def log_forget(z, sl):
        logits = lbl_ref[:, sl]
        e = jnp.exp(logits - jnp.max(logits, axis=0, keepdims=True))
        sm = e / jnp.sum(e, axis=0, keepdims=True)
        cum0 = sm[0:1]
        cuml = cum0
        for k in range(1, layer + 1):
            cuml = cuml + sm[k:k + 1]
        lb = cuml - cum0
        log_lb = jnp.log(lb)
        b = jnp.log1p(-lb) - _softplus(-z)
        delta = log_lb - b
        return jnp.where(jnp.isnan(delta), log_lb + b,
                         jnp.maximum(log_lb, b) + _log1p_exp_neg_abs(delta))

    for tile, (out_ref, fn) in enumerate([(p_ref, silu), (p_ref, log_forget), (p_ref, lambda a, sl: a),
                                          (p_ref, silu), (u_ref, lambda a, sl: a)]):
        pl.when(j == tile)(functools.partial(emit, out_ref, fn))


def _inproj(x, g, w, lb_logits, *, layer, tm, tn, batch, seq):
    n, d = x.shape
    cols = w.shape[2]
    n_j = cols // tn
    depth = lb_logits.shape[0]
    assert n_j == 5 and lb_logits.shape[1] == tn
    time_major = seq % tm == 0
    if time_major:
        n_t = seq // tm
        u_shape, u_map = (seq, batch * tn), (lambda i, j: (i % n_t, i // n_t))
    else:
        u_shape, u_map = (n, tn), (lambda i, j: (i, 0))
    need = 2 * tm * d * 4 + tm * d * 2 + 2 * d * tn * 2 + 5 * tm * tn * 4
    return pl.pallas_call(
        functools.partial(_inproj_kernel, layer=layer),
        out_shape=(jax.ShapeDtypeStruct((n, cols - tn), F32), jax.ShapeDtypeStruct(u_shape, F32)),
        grid=(n // tm, n_j),
        in_specs=[
            pl.BlockSpec((tm, d), lambda i, j: (i, 0)),
            pl.BlockSpec((None, 1, d), lambda i, j: (layer, 0, 0)),
            pl.BlockSpec((None, d, tn), lambda i, j: (layer, 0, j)),
            pl.BlockSpec((depth, tn), lambda i, j: (0, 0)),
        ],
        out_specs=(pl.BlockSpec((tm, tn), lambda i, j: (i, jnp.minimum(j, n_j - 2))),
                   pl.BlockSpec((tm, tn), u_map)),
        scratch_shapes=[pltpu.VMEM((tm, d), BF16)],
        compiler_params=pltpu.CompilerParams(
            dimension_semantics=("parallel", "arbitrary"),
            vmem_limit_bytes=_vmem_limit(need)),
        name="inproj",
    )(x, g, w, lb_logits)


def _hgrn_decayed_queries(q_lo, q_hi, f_row, row_is):
    slabs = []
    qd_lo = qd_hi = None
    for s in range(2 * SUBLANES - 1, -1, -1):
        if s == 2 * SUBLANES - 1:
            qd_hi = jnp.where(row_is[SUBLANES - 1], q_hi, 0.0)
        elif s >= SUBLANES:
            qd_hi = jnp.where(row_is[s - SUBLANES], q_hi, qd_hi * f_row(s + 1))
        else:
            fn = f_row(s + 1)
            qd_hi = qd_hi * fn
            qd_lo = jnp.where(row_is[s], q_lo, 0.0 if s == SUBLANES - 1 else qd_lo * fn)
        slabs.append((s, 1, qd_hi))
        if s < SUBLANES:
            slabs.append((s, 0, qd_lo))
    return slabs


def _split_cast_refs(refs, n_out, with_cast):
    if not with_cast:
        return refs, None, None
    return refs[1:1 + n_out] + refs[2 + n_out:], refs[0], refs[1 + n_out]


def _hgrn_kernel(q_ref, f_ref, i_ref, g_ref, on_ref, s0_ref, *refs, chunk, heads_blk, with_cast):
    refs, cast_src, cast_dst = _split_cast_refs(refs, 2, with_cast)
    o_ref, sout_ref, st_ref, ks_ref, cs_ref, fs_ref, fc_ref, op_ref = refs
    if with_cast:
        cast_dst[...] = cast_src[...].astype(BF16)
    c_idx = pl.program_id(2)
    tc = q_ref.shape[0]
    n_sub = chunk // HGRN_SUB

    @pl.when(c_idx == 0)
    def _():
        for h in range(heads_blk):
            st_ref[h] = s0_ref[0, h].T

    logf = f_ref[...]
    fgate = jnp.exp(logf)
    fs_ref[...] = fgate
    ks_ref[...] = 1.0 - fgate

    t_i = lax.broadcasted_iota(jnp.int32, (chunk, chunk), 0)
    s_i = lax.broadcasted_iota(jnp.int32, (chunk, chunk), 1)
    tri = jnp.where(t_i >= s_i, 1.0, 0.0).astype(BF16)
    for j in range(tc // chunk):
        lf = logf[j * chunk:(j + 1) * chunk]
        hi = lf.astype(BF16)
        r1 = lf - hi.astype(F32)
        mid = r1.astype(BF16)
        lo = (r1 - mid.astype(F32)).astype(BF16)
        cs_ref[j * chunk:(j + 1) * chunk, :] = _dot(tri, hi) + _dot(tri, mid) + _dot(tri, lo)

    row8 = lax.broadcasted_iota(jnp.int32, (SUBLANES, HEAD_DIM), 0)
    row_is = [row8 == j for j in range(SUBLANES)]
    lane8 = lax.broadcasted_iota(jnp.int32, (SUBLANES, chunk), 1)
    lane_is = [lane8 == j for j in range(chunk)]
    onorm = on_ref[...]
    blk = lambda a, j: a[j * HGRN_SUB:(j + 1) * HGRN_SUB]

    head_cols = [slice(h * HEAD_DIM, (h + 1) * HEAD_DIM) for h in range(heads_blk)]

    def finish(rows):
        for cols in head_cols:
            o = op_ref[:, cols]
            o = o * lax.rsqrt(jnp.mean(jnp.square(o), axis=-1, keepdims=True) + EPS)
            o = o * onorm[:, cols]
            o_ref[rows, cols] = (o * g_ref[rows, cols]).astype(BF16)

    op_ref[...] = jnp.zeros(op_ref.shape, F32)

    def chunk_body(c, carry):
        base = pl.multiple_of(c * chunk, chunk)
        r = pl.ds(base, chunk)
        finish(pl.ds(pl.multiple_of(jnp.maximum(c - 1, 0) * chunk, chunk), chunk))
        fc_ref[...] = fs_ref[r, :]
        cum_all = cs_ref[r, :]

        zero_blk = jnp.zeros((HGRN_SUB, HEAD_DIM), F32)
        qs, vbs, kbs, o_inter, a_off = [], [], [], [], []
        for cols in head_cols:
            h = len(qs)
            cum = cum_all[:, cols]
            q = q_ref[r, cols]
            k = ks_ref[r, cols]
            v = i_ref[r, cols]
            ends = [cum[(j + 1) * HGRN_SUB - 1:(j + 1) * HGRN_SUB, :] for j in range(n_sub)]
            cl = ends[-1]
            ke = [blk(k, j) * jnp.exp(ends[j] - blk(cum, j)) for j in range(n_sub)]
            qe = [blk(q, j) * jnp.exp(blk(cum, j) - ends[j - 1] if j else blk(cum, j)) for j in range(n_sub)]

            st = st_ref[h]
            q_in = jnp.concatenate([qe[j] * jnp.exp(ends[j - 1]) if j else qe[j] for j in range(n_sub)], axis=0)
            o_inter.append(_dot_nt(q_in.astype(BF16), st.astype(BF16)))
            kdec = jnp.concatenate(
                [ke[j] * jnp.exp(cl - ends[j]) if j < n_sub - 1 else ke[j] for j in range(n_sub)], axis=0)
            st_ref[h] = st * jnp.exp(cl) + _dot(v.T.astype(BF16), kdec.astype(BF16))

            offs = [None]
            for i in range(1, n_sub):
                kt = jnp.concatenate(
                    [ke[j] * jnp.exp(ends[i - 1] - ends[j]) if j < i - 1 else ke[j] for j in range(i)]
                    + [zero_blk] * (n_sub - i), axis=0)
                offs.append(_dot_nt(qe[i].astype(BF16), kt.astype(BF16)))
            a_off.append(offs)
            qs.append(q)
            vbs.append(v.astype(BF16))
            kbs.append(k.astype(BF16))

        res = []
        for h, cols in enumerate(head_cols):
            per_head = []
            for i in range(n_sub):
                lo_r = i * HGRN_SUB
                f_row = lambda s, lo_r=lo_r, cols=cols: fc_ref[lo_r + s:lo_r + s + 1, cols]
                q_blk = blk(qs[h], i)
                slabs = _hgrn_decayed_queries(q_blk[:SUBLANES], q_blk[SUBLANES:], f_row, row_is)
                stack = jnp.concatenate([slab for _, _, slab in slabs], axis=0)
                per_head.append(([(s, half) for s, half, _ in slabs],
                                 _dot_nt(stack.astype(BF16), kbs[h])))
            res.append(per_head)

        o_intra = []
        for h in range(heads_blk):
            rows = []
            for i in range(n_sub):
                order, prod = res[h][i]
                halves = [jnp.zeros((SUBLANES, chunk), F32), jnp.zeros((SUBLANES, chunk), F32)]
                for idx, (s, half) in enumerate(order):
                    piece = prod[idx * SUBLANES:(idx + 1) * SUBLANES]
                    halves[half] = jnp.where(lane_is[i * HGRN_SUB + s], piece, halves[half])
                a = jnp.concatenate(halves, axis=0)
                rows.append(a + a_off[h][i] if i else a)
            attn = jnp.concatenate(rows, axis=0)
            o_intra.append(_dot(attn.astype(BF16), vbs[h]))

        for h, cols in enumerate(head_cols):
            op_ref[:, cols] = o_inter[h] + o_intra[h]
        return carry

    lax.fori_loop(0, tc // chunk, chunk_body, 0, unroll=min(2, tc // chunk))
    finish(pl.ds(tc - chunk, chunk))

    @pl.when(c_idx == pl.num_programs(2) - 1)
    def _():
        for h in range(heads_blk):
            sout_ref[0, h] = st_ref[h].T


def _cast_job(cast_src, layer, steps, step_of, axis):
    _, r, c = cast_src.shape
    if axis == 2:
        assert c % (steps * LANES) == 0
        blk = (r, c // steps)
        at = lambda *g: (0, step_of(*g))
    else:
        assert r % (steps * 2 * SUBLANES) == 0
        blk = (r // steps, c)
        at = lambda *g: (step_of(*g), 0)
    in_spec = pl.BlockSpec((None,) + blk, lambda *g: (layer,) + at(*g))
    return in_spec, pl.BlockSpec(blk, at), jax.ShapeDtypeStruct((r, c), BF16), 2 * blk[0] * blk[1] * 6


def _hgrn(p, onorm_g, s0, *, batch, seq, layer, cast_src=None):
    n = p.shape[0]
    heads = s0.shape[2]
    hb = HGRN_HEADS_PER_STEP
    wblk = hb * HEAD_DIM
    n_hb = heads // hb
    width = heads * HEAD_DIM
    tc = min(512, seq)
    chunk = min(HGRN_CHUNK, seq)
    n_t = seq // tc
    row = lambda b, h, c: b * n_t + c
    need = 2 * 4 * tc * wblk * 4 + 2 * tc * wblk * 2 + 3 * tc * wblk * 4 + 5 * hb * HEAD_DIM * HEAD_DIM * 4
    kernel = functools.partial(_hgrn_kernel, chunk=chunk, heads_blk=hb, with_cast=cast_src is not None)
    st_spec = pl.BlockSpec((None, 1, hb, HEAD_DIM, HEAD_DIM), lambda b, h, c: (layer, b, h, 0, 0))
    in_specs = [
        pl.BlockSpec((tc, wblk), lambda b, h, c: (row(b, h, c), h)),
        pl.BlockSpec((tc, wblk), lambda b, h, c: (row(b, h, c), n_hb + h)),
        pl.BlockSpec((tc, wblk), lambda b, h, c: (row(b, h, c), 2 * n_hb + h)),
        pl.BlockSpec((tc, wblk), lambda b, h, c: (row(b, h, c), 3 * n_hb + h)),
        pl.BlockSpec((None, 1, wblk), lambda b, h, c: (layer, 0, h)),
        st_spec,
    ]
    out_specs = [
        pl.BlockSpec((tc, wblk), lambda b, h, c: (row(b, h, c), h)),
        pl.BlockSpec((1, hb, HEAD_DIM, HEAD_DIM), lambda b, h, c: (b, h, 0, 0)),
    ]
    out_shape = [jax.ShapeDtypeStruct((n, width), BF16),
                 jax.ShapeDtypeStruct((batch, heads, HEAD_DIM, HEAD_DIM), F32)]
    args = [p, p, p, p, onorm_g, s0]
    if cast_src is not None:
        c_in, c_out, c_shape, c_bytes = _cast_job(cast_src, layer, batch * n_hb * n_t,
                                                  lambda b, h, c: (b * n_hb + h) * n_t + c, axis=2)
        in_specs.append(c_in)
        out_specs.append(c_out)
        out_shape.append(c_shape)
        args.append(cast_src)
        need += c_bytes
    return pl.pallas_call(
        kernel,
        out_shape=tuple(out_shape),
        grid=(batch, n_hb, n_t),
        in_specs=in_specs,
        out_specs=tuple(out_specs),
        scratch_shapes=[
            pltpu.VMEM((hb, HEAD_DIM, HEAD_DIM), F32),
            pltpu.VMEM((tc, wblk), F32),
            pltpu.VMEM((tc, wblk), F32),
            pltpu.VMEM((tc, wblk), F32),
            pltpu.VMEM((chunk, wblk), F32),
            pltpu.VMEM((chunk, wblk), F32),
        ],
        compiler_params=pltpu.CompilerParams(
            dimension_semantics=("parallel", "parallel", "arbitrary"),
            vmem_limit_bytes=_vmem_limit(need)),
        name="hgrn",
    )(*args)


def _s5_disc_lambda_kernel(lr_ref, li_ref, ls_ref, lbr_ref, lbi_ref, cr_ref, ci_ref):
    lr, li = lr_ref[...], li_ref[...]
    dt = jnp.exp(ls_ref[...])
    mag = jnp.exp(dt * lr)
    ang = dt * li
    lbr = mag * jnp.cos(ang)
    lbi = mag * jnp.sin(ang)
    nr, ni = lbr - 1.0, lbi
    den = lr * lr + li * li
    lbr_ref[...] = lbr
    lbi_ref[...] = lbi
    cr_ref[...] = (nr * lr + ni * li) / den
    ci_ref[...] = (ni * lr - nr * li) / den


def _s5_discretise(lam_re, lam_im, log_step):
    depth, groups, nst = lam_re.shape
    rows = depth * groups
    shp = jax.ShapeDtypeStruct((rows, nst), F32)
    outs = pl.pallas_call(
        _s5_disc_lambda_kernel, out_shape=(shp, shp, shp, shp), name="s5_disc_lambda",
    )(lam_re.reshape(rows, nst), lam_im.reshape(rows, nst), log_step.reshape(rows, 1))
    return tuple(o.reshape(depth, groups, nst) for o in outs)


def _s5_pair_weights(lbr, lbi, zoh_r, zoh_i, b_re, b_im, c_re, c_im, d_skip):
    depth, groups, nst, cg = b_re.shape
    hi = lax.Precision.HIGHEST
    zr, zi = zoh_r[..., None], zoh_i[..., None]
    bbr, bbi = zr * b_re - zi * b_im, zr * b_im + zi * b_re
    lr, li = lbr[..., None], lbi[..., None]
    blr, bli = bbr * lr - bbi * li, bbr * li + bbi * lr
    tb = jnp.swapaxes(jnp.stack([blr, bli, bbr, bbi], axis=1), 3, 4).reshape(depth, 4, groups * cg, nst)

    lrc, lic = lbr[:, :, None, :], lbi[:, :, None, :]
    pr, pi = c_re * lrc - c_im * lic, c_re * lic + c_im * lrc
    qr, qi = pr * lrc - pi * lic, pr * lic + pi * lrc
    tcc = jnp.stack([pr, -pi, qr, -qi], axis=1).transpose(0, 1, 3, 2, 4).reshape(depth, 4, cg, groups * nst)

    mm = lambda a, b: jnp.einsum("lgcn,lgnd->lgcd", a, b, precision=hi)
    g0 = mm(c_re, bbr) - mm(c_im, bbi)
    g1 = mm(pr, bbr) - mm(pi, bbi)
    g0d = g0 + d_skip.reshape(depth, groups, cg)[..., None] * jnp.eye(cg, dtype=F32)
    tg = jnp.stack([g0d, g1], axis=1).transpose(0, 1, 3, 2, 4).reshape(depth, 2, cg, groups * cg)
    lam2 = jnp.stack([lbr * lbr - lbi * lbi, 2.0 * lbr * lbi], axis=1).reshape(depth, 2, groups * nst)
    return tb, tcc, tg, lam2


def _s5_embed_weights(tb_ref, tcc_ref, tg_ref, wb_ref, wct_ref):
    nbk = wb_ref.shape[0]
    cg, nst = tcc_ref.shape[1], tb_ref.shape[2]
    bl = wb_ref.shape[2] // 2
    assert cg & (cg - 1) == 0 and nst & (nst - 1) == 0
    c_shift, n_shift = cg.bit_length() - 1, nst.bit_length() - 1
    iota = lambda shape, dim: lax.broadcasted_iota(jnp.int32, shape, dim)
    e_in = jnp.where((iota((nst, bl), 1) & (nst - 1)) == iota((nst, bl), 0), 1.0, 0.0).astype(BF16)
    e_out = jnp.where((iota((LANES, cg), 0) & (cg - 1)) == iota((LANES, cg), 1), 1.0, 0.0).astype(BF16)
    m_state = (iota((LANES, bl), 0) >> c_shift) == (iota((LANES, bl), 1) >> n_shift)
    m_direct = (iota((LANES, LANES), 0) >> c_shift) == (iota((LANES, LANES), 1) >> c_shift)
    zeros = jnp.zeros((LANES, LANES), BF16)
    for j in range(nbk):
        for k in range(4):
            half, part = divmod(k, 2)
            a = tb_ref[k, j * LANES:(j + 1) * LANES, :].astype(BF16)
            wb_ref[j, half * LANES:(half + 1) * LANES, part * bl:(part + 1) * bl] = (
                jnp.where(m_state, _dot(a, e_in), 0.0).astype(BF16))
        for k in range(4):
            eo, part = divmod(k, 2)
            x = tcc_ref[k, :, j * bl:(j + 1) * bl].astype(BF16)
            wct_ref[j, eo * LANES:(eo + 1) * LANES, part * bl:(part + 1) * bl] = (
                jnp.where(m_state, _dot(e_out, x), 0.0).astype(BF16))
        direct = [jnp.where(m_direct, _dot(e_out, tg_ref[k, :, j * LANES:(j + 1) * LANES].astype(BF16)), 0.0
                            ).astype(BF16) for k in range(2)]
        base = 2 * bl
        wct_ref[j, 0:LANES, base:base + LANES] = direct[0]
        wct_ref[j, 0:LANES, base + LANES:base + 2 * LANES] = zeros
        wct_ref[j, LANES:2 * LANES, base:base + LANES] = direct[1]
        wct_ref[j, LANES:2 * LANES, base + LANES:base + 2 * LANES] = direct[0]


def _s5_kernel(u_ref, tb_ref, tcc_ref, tg_ref, lam2_ref, wg_ref, bg_ref, x0r_ref, x0i_ref, *refs,
               batch, with_cast):
    refs, cast_src, cast_dst = _split_cast_refs(refs, 3, with_cast)
    o_ref, xr_ref, xi_ref, xs_ref, y_ref, st_ref, ue_ref, uo_ref, tmp_ref, wb_ref, wct_ref = refs
    if with_cast:
        cast_dst[...] = cast_src[...].astype(BF16)
    c_idx = pl.program_id(0)
    tt = u_ref.shape[0]
    tp = tt // 2
    prows = tp * batch
    width = u_ref.shape[1] // batch
    n_slab = width // LANES
    nbk = wb_ref.shape[0]
    bl = wb_ref.shape[2] // 2
    ns = nbk * bl
    grp = st_ref.shape[1]
    per = grp // batch

    @pl.when(c_idx == 0)
    def _():
        st_ref[0] = jnp.concatenate([x0r_ref[...]] * per, axis=0)
        st_ref[1] = jnp.concatenate([x0i_ref[...]] * per, axis=0)
        _s5_embed_weights(tb_ref, tcc_ref, tg_ref, wb_ref, wct_ref)

    for b in range(batch):
        for m in range(n_slab):
            tmp_ref[...] = u_ref[:, b * width + m * LANES:b * width + (m + 1) * LANES]
            ue_ref[m, pl.ds(b, tp, stride=batch), :] = tmp_ref[pl.ds(0, tp, stride=2), :]
            uo_ref[m, pl.ds(b, tp, stride=batch), :] = tmp_ref[pl.ds(1, tp, stride=2), :]

    for j in range(nbk):
        lhs = jnp.concatenate([ue_ref[j], uo_ref[j]], axis=1).astype(BF16)
        w = _dot(lhs, wb_ref[j])
        xs_ref[:, j * bl:(j + 1) * bl] = w[:, :bl]
        xs_ref[:, ns + j * bl:ns + (j + 1) * bl] = w[:, bl:]

    second = lax.broadcasted_iota(jnp.int32, (grp, S5_SCAN_LANES), 0) >= batch
    for cb in range(ns // S5_SCAN_LANES):
        lo = cb * S5_SCAN_LANES
        re_l = slice(lo, lo + S5_SCAN_LANES)
        im_l = slice(ns + lo, ns + lo + S5_SCAN_LANES)
        ar = lam2_ref[0:1, re_l]
        ai = lam2_ref[1:2, re_l]

        def step(g, carry, ar=ar, ai=ai, re_l=re_l, im_l=im_l):
            cr, ci = carry
            r = pl.ds(pl.multiple_of(g * grp, grp), grp)
            wr = xs_ref[r, re_l]
            wi = xs_ref[r, im_l]
            if per == 2:
                tr = ar * cr - ai * ci + pltpu.roll(wr, batch, axis=0)
                ti = ar * ci + ai * cr + pltpu.roll(wi, batch, axis=0)
                xs_ref[r, re_l] = jnp.where(second, tr, cr)
                xs_ref[r, im_l] = jnp.where(second, ti, ci)
                vr = ar * tr - ai * ti + wr
                vi = ar * ti + ai * tr + wi
                nr = jnp.where(second, vr, pltpu.roll(vr, batch, axis=0))
                ni = jnp.where(second, vi, pltpu.roll(vi, batch, axis=0))
            else:
                xs_ref[r, re_l] = cr
                xs_ref[r, im_l] = ci
                nr = ar * cr - ai * ci + wr
                ni = ar * ci + ai * cr + wi
            return nr, ni

        cr, ci = lax.fori_loop(0, prows // grp, step, (st_ref[0, :, re_l], st_ref[1, :, re_l]), unroll=2)
        st_ref[0, :, re_l] = cr
        st_ref[1, :, re_l] = ci

    for j in range(nbk):
        lhs = jnp.concatenate([xs_ref[:, j * bl:(j + 1) * bl], xs_ref[:, ns + j * bl:ns + (j + 1) * bl],
                               ue_ref[j], uo_ref[j]], axis=1).astype(BF16)
        yj = _dot_nt(lhs, wct_ref[j])
        y_ref[0:prows, j * LANES:(j + 1) * LANES] = yj[:, :LANES]
        y_ref[prows:2 * prows, j * LANES:(j + 1) * LANES] = yj[:, LANES:]

    hh = jax.nn.gelu(y_ref[...])
    gate = _sigmoid(_dot(hh.astype(BF16), wg_ref[...]) + bg_ref[...])
    out = hh * gate
    for m in range(n_slab):
        ue_ref[m] = out[0:prows, m * LANES:(m + 1) * LANES]
        uo_ref[m] = out[prows:2 * prows, m * LANES:(m + 1) * LANES]
    for b in range(batch):
        for m in range(n_slab):
            tmp_ref[pl.ds(0, tp, stride=2), :] = ue_ref[m, pl.ds(b, tp, stride=batch), :]
            tmp_ref[pl.ds(1, tp, stride=2), :] = uo_ref[m, pl.ds(b, tp, stride=batch), :]
            o_ref[:, b * width + m * LANES:b * width + (m + 1) * LANES] = tmp_ref[...].astype(BF16)

    @pl.when(c_idx == pl.num_programs(0) - 1)
    def _():
        xr_ref[...] = st_ref[0, grp - batch:grp, :]
        xi_ref[...] = st_ref[1, grp - batch:grp, :]


def _s5(u, tb, tcc, tg, lam2, w_glu, b_glu, x0r, x0i, *, batch, layer, cast_src=None):
    seq = u.shape[0]
    width = u.shape[1] // batch
    n = seq * batch
    ns = lam2.shape[2]
    nbk = width // LANES
    bl = ns // nbk
    assert batch % SUBLANES == 0 or 2 * batch == SUBLANES
    assert seq % 2 == 0 and tb.shape[2] == width
    grp = max(batch, SUBLANES)
    rows = min(S5_ROWS, n)
    tt = rows // batch
    prows = rows // 2
    lay4 = lambda c: (layer, 0, 0, 0)
    lay3 = lambda c: (layer, 0, 0)
    st_in = pl.BlockSpec((None, batch, ns), lay3)
    resident = pl.Buffered(1)
    wb_shape = (nbk, 2 * LANES, 2 * bl)
    wct_shape = (nbk, 2 * LANES, 2 * bl + 2 * LANES)
    need = (2 * rows * width * 4 + (nbk * 2 * LANES * (4 * bl + 2 * LANES) + width * width) * 2
            + 4 * width * LANES * 4 + 6 * 16 * ns * 4
            + 2 * rows * width * 2 + prows * 2 * ns * 4 + rows * width * 4 + 4 * rows * width * 4
            + 4 * prows * (4 * bl + 2 * LANES) + (8 * batch + 2 * grp) * ns * 4)
    st_shape = jax.ShapeDtypeStruct((batch, ns), F32)
    in_specs = [
        pl.BlockSpec((tt, batch * width), lambda c: (c, 0)),
        pl.BlockSpec((None,) + tb.shape[1:], lay4, pipeline_mode=resident),
        pl.BlockSpec((None,) + tcc.shape[1:], lay4, pipeline_mode=resident),
        pl.BlockSpec((None,) + tg.shape[1:], lay4, pipeline_mode=resident),
        pl.BlockSpec((None, 2, ns), lay3),
        pl.BlockSpec((None, width, width), lay3, pipeline_mode=resident),
        pl.BlockSpec((None, 1, width), lay3),
        st_in,
        st_in,
    ]
    out_specs = [
        pl.BlockSpec((tt, batch * width), lambda c: (c, 0)),
        pl.BlockSpec((batch, ns), lambda c: (0, 0)),
        pl.BlockSpec((batch, ns), lambda c: (0, 0)),
    ]
    out_shape = [jax.ShapeDtypeStruct((seq, batch * width), BF16), st_shape, st_shape]
    args = [u, tb, tcc, tg, lam2, w_glu, b_glu, x0r, x0i]
    if cast_src is not None:
        c_in, c_out, c_shape, c_bytes = _cast_job(cast_src, layer, n // rows, lambda c: c, axis=1)
        in_specs.append(c_in)
        out_specs.append(c_out)
        out_shape.append(c_shape)
        args.append(cast_src)
        need += c_bytes
    return pl.pallas_call(
        functools.partial(_s5_kernel, batch=batch, with_cast=cast_src is not None),
        out_shape=tuple(out_shape),
        grid=(n // rows,),
        in_specs=in_specs,
        out_specs=tuple(out_specs),
        scratch_shapes=[
            pltpu.VMEM((prows, 2 * ns), F32),
            pltpu.VMEM((rows, width), F32),
            pltpu.VMEM((2, grp, ns), F32),
            pltpu.VMEM((width // LANES, prows, LANES), F32),
            pltpu.VMEM((width // LANES, prows, LANES), F32),
            pltpu.VMEM((tt, LANES), F32),
            pltpu.VMEM(wb_shape, BF16),
            pltpu.VMEM(wct_shape, BF16),
        ],
        compiler_params=pltpu.CompilerParams(
            dimension_semantics=("arbitrary",),
            vmem_limit_bytes=_vmem_limit(need)),
        name="s5",
    )(*args)


def _outproj_kernel(x_ref, oh_ref, os_ref, wh_ref, ws_ref, o_ref):
    o_ref[...] = x_ref[...] + _dot(oh_ref[...], wh_ref[...]) + _dot(os_ref[...], ws_ref[...])


def _outproj(x, o_h, o_s, w_out, *, layer, tm, tn, seq):
    n, d = x.shape
    kh = o_h.shape[1]
    ks = w_out.shape[1] - kh
    assert kh == ks
    if o_s.shape[0] == n:
        os_map = lambda i, j: (i, 0)
    else:
        n_t = seq // tm
        os_map = lambda i, j: (i % n_t, i // n_t)
    need = 2 * (2 * tm * tn * 4 + tm * (kh + ks) * 2 + (kh + ks) * tn * 2) + 2 * tm * tn * 4
    return pl.pallas_call(
        _outproj_kernel,
        out_shape=jax.ShapeDtypeStruct((n, d), F32),
        grid=(n // tm, d // tn),
        in_specs=[
            pl.BlockSpec((tm, tn), lambda i, j: (i, j)),
            pl.BlockSpec((tm, kh), lambda i, j: (i, 0)),
            pl.BlockSpec((tm, ks), os_map),
            pl.BlockSpec((None, kh, tn), lambda i, j: (layer, 0, j)),
            pl.BlockSpec((None, ks, tn), lambda i, j: (layer, 1, j)),
        ],
        out_specs=pl.BlockSpec((tm, tn), lambda i, j: (i, j)),
        compiler_params=pltpu.CompilerParams(
            dimension_semantics=("parallel", "parallel"),
            vmem_limit_bytes=_vmem_limit(need)),
        name="outproj",
    )(x, o_h, o_s, w_out, w_out)


def _ffn_kernel(x_ref, g_ref, w1_ref, w2_ref, gf_ref, o_ref, h_ref, *, final_norm):
    f = pl.program_id(1)
    tm = x_ref.shape[0]

    @pl.when(f == 0)
    def _():
        g = g_ref[...]

        def body(i, carry):
            r = pl.ds(pl.multiple_of(i * NORM_ROWS, NORM_ROWS), NORM_ROWS)
            h_ref[r, :] = _rmsnorm(x_ref[r, :], g).astype(BF16)
            o_ref[r, :] = jnp.zeros((NORM_ROWS, o_ref.shape[1]), F32)
            return carry

        lax.fori_loop(0, tm // NORM_ROWS, body, 0)

    a = _dot(h_ref[...], w1_ref[...])
    a = jnp.square(jnp.maximum(a, 0.0)).astype(BF16)
    cw = o_ref.shape[1] // FFN_ACC_CHUNKS
    for c in range(FFN_ACC_CHUNKS):
        o_ref[:, c * cw:(c + 1) * cw] += _dot(a, w2_ref[:, c * cw:(c + 1) * cw])

    @pl.when(f == pl.num_programs(1) - 1)
    def _():
        gf = gf_ref[...]

        def body(i, carry):
            r = pl.ds(pl.multiple_of(i * NORM_ROWS, NORM_ROWS), NORM_ROWS)
            y = x_ref[r, :] + o_ref[r, :]
            if final_norm:
                y = _rmsnorm(y, gf)
            o_ref[r, :] = y
            return carry

        lax.fori_loop(0, tm // NORM_ROWS, body, 0)


def _ffn(x, g, w1, w2, gf, *, layer, tm, tf, final_norm):
    n, d = x.shape
    ff = w1.shape[1]
    need = tm * d * 4 + tm * d * 2 + 2 * tm * d * 4 + 4 * d * tf * 2 + tm * tf * 6 + tm * d * 4
    kernel = functools.partial(_ffn_kernel, final_norm=final_norm)
    return pl.pallas_call(
        kernel,
        out_shape=jax.ShapeDtypeStruct((n, d), F32),
        grid=(n // tm, ff // tf),
        in_specs=[
            pl.BlockSpec((tm, d), lambda i, f: (i, 0), pipeline_mode=pl.Buffered(1)),
            pl.BlockSpec((None, 1, d), lambda i, f: (layer, 0, 0)),
            pl.BlockSpec((d, tf), lambda i, f: (0, f)),
            pl.BlockSpec((tf, d), lambda i, f: (f, 0)),
            pl.BlockSpec((1, d), lambda i, f: (0, 0)),
        ],
        out_specs=pl.BlockSpec((tm, d), lambda i, f: (i, 0)),
        scratch_shapes=[pltpu.VMEM((tm, d), BF16)],
        compiler_params=pltpu.CompilerParams(
            dimension_semantics=("parallel", "arbitrary"),
            vmem_limit_bytes=_vmem_limit(need)),
        name="ffn",
    )(x, g, w1, w2, gf)


def _trunk(x, st_h, st_r, st_i, wts, ffn_w=None):
    batch, seq, d = x.shape
    n = batch * seq
    depth = wts["w_in"].shape[0]
    heads = st_h.shape[2]
    hgrn_width = heads * HEAD_DIM
    groups, nst = st_r.shape[2], st_r.shape[3]
    s5_width = wts["w_glu"].shape[1]
    tm = min(1024, n)
    xf = x.reshape(n, d)
    x0r = st_r.reshape(depth, batch, groups * nst)
    x0i = st_i.reshape(depth, batch, groups * nst)
    time_major = seq % tm == 0
    new_h, new_r, new_i, used_w = [], [], [], []
    for l in range(depth):
        cast = ffn_w is None
        p, u = _inproj(xf, wts["norm1_g"], wts["w_in"], wts["lb_logits"],
                       layer=l, tm=tm, tn=s5_width, batch=batch, seq=seq)
        o_h, s_new, *w1 = _hgrn(p, wts["onorm_g"], st_h, batch=batch, seq=seq, layer=l,
                                cast_src=wts["w_ff1"] if cast else None)
        if not time_major:
            u = u.reshape(batch, seq, s5_width).transpose(1, 0, 2).reshape(seq, batch * s5_width)
        o_s, xr, xi, *w2 = _s5(u, wts["s5_tb"], wts["s5_tcc"], wts["s5_tg"], wts["s5_lam2"], wts["w_glu"],
                               wts["b_glu"], x0r, x0i, batch=batch, layer=l,
                               cast_src=wts["w_ff2"] if cast else None)
        if not time_major:
            o_s = o_s.reshape(seq, batch, s5_width).transpose(1, 0, 2).reshape(n, s5_width)
        used_w.append((w1[0], w2[0]) if cast else ffn_w[l])
        x1 = _outproj(xf, o_h, o_s, wts["w_out"], layer=l, tm=min(512, n), tn=d, seq=seq)
        xf = _ffn(x1, wts["norm2_g"], *used_w[l], wts["final_g"],
                  layer=l, tm=tm, tf=1024, final_norm=(l == depth - 1))
        new_h.append(s_new)
        new_r.append(xr.reshape(batch, groups, nst))
        new_i.append(xi.reshape(batch, groups, nst))
    return (xf.reshape(batch, seq, d), jnp.stack(new_h), jnp.stack(new_r), jnp.stack(new_i)), used_w


def kernel(x_prompt, x_sample, state_hgrn, state_s5_re, state_s5_im, norm1_g, w_in, hgrn_lb_logits,
           hgrn_onorm_g, s5_lambda_re, s5_lambda_im, s5_log_step, s5_B_re, s5_B_im, s5_C_re, s5_C_im,
           s5_D, s5_w_glu, s5_b_glu, w_out, norm2_g, w_ff1, w_ff2, final_norm_g):
    depth, d = norm1_g.shape
    row3 = lambda a: a.reshape(depth, 1, a.shape[-1])
    lbr, lbi, zoh_r, zoh_i = _s5_discretise(s5_lambda_re, s5_lambda_im, s5_log_step)
    s5_tb, s5_tcc, s5_tg, s5_lam2 = _s5_pair_weights(lbr, lbi, zoh_r, zoh_i, s5_B_re, s5_B_im,
                                                     s5_C_re, s5_C_im, s5_D)
    wts = {
        "norm1_g": row3(norm1_g), "norm2_g": row3(norm2_g), "final_g": final_norm_g.reshape(1, d),
        "lb_logits": hgrn_lb_logits, "onorm_g": row3(hgrn_onorm_g),
        "w_in": w_in.astype(BF16), "w_out": w_out.astype(BF16),
        "w_ff1": w_ff1, "w_ff2": w_ff2,
        "w_glu": s5_w_glu.astype(BF16), "b_glu": row3(s5_b_glu),
        "s5_tb": s5_tb, "s5_tcc": s5_tcc, "s5_tg": s5_tg, "s5_lam2": s5_lam2,
    }
    bp = x_prompt.shape[0]
    zh = jnp.zeros((depth, bp) + state_hgrn.shape[2:], F32)
    zs = jnp.zeros((depth, bp) + state_s5_re.shape[2:], F32)
    (y_p, hp, rp, ip), ffn_w = _trunk(x_prompt, zh, zs, zs, wts)
    (y_s, hs, rs, is_), _ = _trunk(x_sample, state_hgrn, state_s5_re, state_s5_im, wts, ffn_w)
    return (y_p, y_s, hp, rp, ip, hs, rs, is_)
```

```python
import functools

import jax
import jax.numpy as jnp
from jax import lax
from jax.experimental import pallas as pl
from jax.experimental.pallas import tpu as pltpu

F32 = jnp.float32
BF16 = jnp.bfloat16
EPS = 1e-6

HEAD_DIM = 128
HGRN_CHUNK = 64
HGRN_SUB = 16
HGRN_HEADS_PER_STEP = 4
SUBLANES = 8
LANES = 128
S5_SCAN_LANES = 256
S5_ROWS = 512

V7X_VMEM_CAP = 56 * 1024 * 1024
NORM_ROWS = 64
NORM_UNROLL = 4
FFN_ACC_CHUNKS = 4
INPROJ_CHUNKS = 4


def _vmem_limit(nbytes):
    return int(min(V7X_VMEM_CAP, nbytes * 5 // 4 + (4 << 20)))


def _rmsnorm(x, g):
    return x * lax.rsqrt(jnp.mean(jnp.square(x), axis=-1, keepdims=True) + EPS) * g


def _sigmoid(x):
    return 1.0 / (1.0 + jnp.exp(-x))


def _log1p_exp_neg_abs(x):
    return jnp.log(1.0 + jnp.exp(-jnp.abs(x)))


def _softplus(x):
    return jnp.maximum(x, 0.0) + _log1p_exp_neg_abs(x)


def _dot(a, b):
    return jnp.dot(a, b, preferred_element_type=F32)


def _dot_nt(a, b):
    return lax.dot_general(a, b, (((1,), (1,)), ((), ())), preferred_element_type=F32)


def _inproj_kernel(x_ref, g_ref, w_ref, lbl_ref, p_ref, u_ref, h_ref, *, layer):
    tm = x_ref.shape[0]
    cw = w_ref.shape[1] // INPROJ_CHUNKS
    j = pl.program_id(1)

    @pl.when(j == 0)
    def _():
        g = g_ref[...]

        def body(i, carry):
            r = pl.ds(pl.multiple_of(i * NORM_ROWS, NORM_ROWS), NORM_ROWS)
            h_ref[r, :] = _rmsnorm(x_ref[r, :], g).astype(BF16)
            return carry

        lax.fori_loop(0, tm // NORM_ROWS, body, 0, unroll=NORM_UNROLL)

    def emit(out_ref, fn):
        for c in range(INPROJ_CHUNKS):
            sl = slice(c * cw, (c + 1) * cw)
            out_ref[:, sl] = fn(_dot(h_ref[...], w_ref[:, sl]), sl)

    def silu(a, sl):
        return a * _sigmoid(a)

    def log_forget(z, sl):
        logits = lbl_ref[:, sl]
        e = jnp.exp(logits - jnp.max(logits, axis=0, keepdims=True))
        sm = e / jnp.sum(e, axis=0, keepdims=True)
        cum0 = sm[0:1]
        cuml = cum0
        for k in range(1, layer + 1):
            cuml = cuml + sm[k:k + 1]
        lb = cuml - cum0
        log_lb = jnp.log(lb)
        b = jnp.log1p(-lb) - _softplus(-z)
        delta = log_lb - b
        return jnp.where(jnp.isnan(delta), log_lb + b,
                         jnp.maximum(log_lb, b) + _log1p_exp_neg_abs(delta))

    for tile, (out_ref, fn) in enumerate([(p_ref, silu), (p_ref, log_forget), (p_ref, lambda a, sl: a),
                                          (p_ref, silu), (u_ref, lambda a, sl: a)]):
        pl.when(j == tile)(functools.partial(emit, out_ref, fn))


def _inproj(x, g, w, lb_logits, *, layer, tm, tn, batch, seq):
    n, d = x.shape
    cols = w.shape[2]
    n_j = cols // tn
    depth = lb_logits.shape[0]
    assert n_j == 5 and lb_logits.shape[1] == tn
    time_major = seq % tm == 0
    if time_major:
        n_t = seq // tm
        u_shape, u_map = (seq, batch * tn), (lambda i, j: (i % n_t, i // n_t))
    else:
        u_shape, u_map = (n, tn), (lambda i, j: (i, 0))
    need = 2 * tm * d * 4 + tm * d * 2 + 2 * d * tn * 2 + 5 * tm * tn * 4
    return pl.pallas_call(
        functools.partial(_inproj_kernel, layer=layer),
        out_shape=(jax.ShapeDtypeStruct((n, cols - tn), F32), jax.ShapeDtypeStruct(u_shape, F32)),
        grid=(n // tm, n_j),
        in_specs=[
            pl.BlockSpec((tm, d), lambda i, j: (i, 0)),
            pl.BlockSpec((None, 1, d), lambda i, j: (layer, 0, 0)),
            pl.BlockSpec((None, d, tn), lambda i, j: (layer, 0, j)),
            pl.BlockSpec((depth, tn), lambda i, j: (0, 0)),
        ],
        out_specs=(pl.BlockSpec((tm, tn), lambda i, j: (i, jnp.minimum(j, n_j - 2))),
                   pl.BlockSpec((tm, tn), u_map)),
        scratch_shapes=[pltpu.VMEM((tm, d), BF16)],
        compiler_params=pltpu.CompilerParams(
            dimension_semantics=("parallel", "arbitrary"),
            vmem_limit_bytes=_vmem_limit(need)),
        name="inproj",
    )(x, g, w, lb_logits)


def _hgrn_decayed_queries(q_lo, q_hi, f_row, row_is):
    slabs = []
    qd_lo = qd_hi = None
    for s in range(2 * SUBLANES - 1, -1, -1):
        if s == 2 * SUBLANES - 1:
            qd_hi = jnp.where(row_is[SUBLANES - 1], q_hi, 0.0)
        elif s >= SUBLANES:
            qd_hi = jnp.where(row_is[s - SUBLANES], q_hi, qd_hi * f_row(s + 1))
        else:
            fn = f_row(s + 1)
            qd_hi = qd_hi * fn
            qd_lo = jnp.where(row_is[s], q_lo, 0.0 if s == SUBLANES - 1 else qd_lo * fn)
        slabs.append((s, 1, qd_hi))
        if s < SUBLANES:
            slabs.append((s, 0, qd_lo))
    return slabs


def _split_cast_refs(refs, n_out, with_cast):
    if not with_cast:
        return refs, None, None
    return refs[1:1 + n_out] + refs[2 + n_out:], refs[0], refs[1 + n_out]


def _hgrn_kernel(q_ref, f_ref, i_ref, g_ref, on_ref, s0_ref, *refs, chunk, heads_blk, with_cast):
    refs, cast_src, cast_dst = _split_cast_refs(refs, 2, with_cast)
    o_ref, sout_ref, st_ref, ks_ref, cs_ref, fs_ref, fc_ref, op_ref = refs
    if with_cast:
        cast_dst[...] = cast_src[...].astype(BF16)
    c_idx = pl.program_id(2)
    tc = q_ref.shape[0]
    n_sub = chunk // HGRN_SUB

    @pl.when(c_idx == 0)
    def _():
        for h in range(heads_blk):
            st_ref[h] = s0_ref[0, h].T

    logf = f_ref[...]
    fgate = jnp.exp(logf)
    fs_ref[...] = fgate
    ks_ref[...] = 1.0 - fgate

    t_i = lax.broadcasted_iota(jnp.int32, (chunk, chunk), 0)
    s_i = lax.broadcasted_iota(jnp.int32, (chunk, chunk), 1)
    tri = jnp.where(t_i >= s_i, 1.0, 0.0).astype(BF16)
    for j in range(tc // chunk):
        lf = logf[j * chunk:(j + 1) * chunk]
        hi = lf.astype(BF16)
        r1 = lf - hi.astype(F32)
        mid = r1.astype(BF16)
        lo = (r1 - mid.astype(F32)).astype(BF16)
        cs_ref[j * chunk:(j + 1) * chunk, :] = _dot(tri, hi) + _dot(tri, mid) + _dot(tri, lo)

    row8 = lax.broadcasted_iota(jnp.int32, (SUBLANES, HEAD_DIM), 0)
    row_is = [row8 == j for j in range(SUBLANES)]
    lane8 = lax.broadcasted_iota(jnp.int32, (SUBLANES, chunk), 1)
    lane_is = [lane8 == j for j in range(chunk)]
    onorm = on_ref[...]
    blk = lambda a, j: a[j * HGRN_SUB:(j + 1) * HGRN_SUB]

    head_cols = [slice(h * HEAD_DIM, (h + 1) * HEAD_DIM) for h in range(heads_blk)]

    def finish(rows):
        for cols in head_cols:
            o = op_ref[:, cols]
            o = o * lax.rsqrt(jnp.mean(jnp.square(o), axis=-1, keepdims=True) + EPS)
            o = o * onorm[:, cols]
            o_ref[rows, cols] = (o * g_ref[rows, cols]).astype(BF16)

    op_ref[...] = jnp.zeros(op_ref.shape, F32)

    def chunk_body(c, carry):
        base = pl.multiple_of(c * chunk, chunk)
        r = pl.ds(base, chunk)
        finish(pl.ds(pl.multiple_of(jnp.maximum(c - 1, 0) * chunk, chunk), chunk))
        fc_ref[...] = fs_ref[r, :]
        cum_all = cs_ref[r, :]

        zero_blk = jnp.zeros((HGRN_SUB, HEAD_DIM), F32)
        qs, vbs, kbs, o_inter, a_off = [], [], [], [], []
        for cols in head_cols:
            h = len(qs)
            cum = cum_all[:, cols]
            q = q_ref[r, cols]
            k = ks_ref[r, cols]
            v = i_ref[r, cols]
            ends = [cum[(j + 1) * HGRN_SUB - 1:(j + 1) * HGRN_SUB, :] for j in range(n_sub)]
            cl = ends[-1]
            ke = [blk(k, j) * jnp.exp(ends[j] - blk(cum, j)) for j in range(n_sub)]
            qe = [blk(q, j) * jnp.exp(blk(cum, j) - ends[j - 1] if j else blk(cum, j)) for j in range(n_sub)]

            st = st_ref[h]
            q_in = jnp.concatenate([qe[j] * jnp.exp(ends[j - 1]) if j else qe[j] for j in range(n_sub)], axis=0)
            o_inter.append(_dot_nt(q_in.astype(BF16), st.astype(BF16)))
            kdec = jnp.concatenate(
                [ke[j] * jnp.exp(cl - ends[j]) if j < n_sub - 1 else ke[j] for j in range(n_sub)], axis=0)
            st_ref[h] = st * jnp.exp(cl) + _dot(v.T.astype(BF16), kdec.astype(BF16))

            offs = [None]
            for i in range(1, n_sub):
                kt = jnp.concatenate(
                    [ke[j] * jnp.exp(ends[i - 1] - ends[j]) if j < i - 1 else ke[j] for j in range(i)]
                    + [zero_blk] * (n_sub - i), axis=0)
                offs.append(_dot_nt(qe[i].astype(BF16), kt.astype(BF16)))
            a_off.append(offs)
            qs.append(q)
            vbs.append(v.astype(BF16))
            kbs.append(k.astype(BF16))

        res = []
        for h, cols in enumerate(head_cols):
            per_head = []
            for i in range(n_sub):
                lo_r = i * HGRN_SUB
                f_row = lambda s, lo_r=lo_r, cols=cols: fc_ref[lo_r + s:lo_r + s + 1, cols]
                q_blk = blk(qs[h], i)
                slabs = _hgrn_decayed_queries(q_blk[:SUBLANES], q_blk[SUBLANES:], f_row, row_is)
                stack = jnp.concatenate([slab for _, _, slab in slabs], axis=0)
                per_head.append(([(s, half) for s, half, _ in slabs],
                                 _dot_nt(stack.astype(BF16), kbs[h])))
            res.append(per_head)

        o_intra = []
        for h in range(heads_blk):
            rows = []
            for i in range(n_sub):
                order, prod = res[h][i]
                halves = [jnp.zeros((SUBLANES, chunk), F32), jnp.zeros((SUBLANES, chunk), F32)]
                for idx, (s, half) in enumerate(order):
                    piece = prod[idx * SUBLANES:(idx + 1) * SUBLANES]
                    halves[half] = jnp.where(lane_is[i * HGRN_SUB + s], piece, halves[half])
                a = jnp.concatenate(halves, axis=0)
                rows.append(a + a_off[h][i] if i else a)
            attn = jnp.concatenate(rows, axis=0)
            o_intra.append(_dot(attn.astype(BF16), vbs[h]))

        for h, cols in enumerate(head_cols):
            op_ref[:, cols] = o_inter[h] + o_intra[h]
        return carry

    lax.fori_loop(0, tc // chunk, chunk_body, 0, unroll=min(2, tc // chunk))
    finish(pl.ds(tc - chunk, chunk))

    @pl.when(c_idx == pl.num_programs(2) - 1)
    def _():
        for h in range(heads_blk):
            sout_ref[0, h] = st_ref[h].T


def _cast_job(cast_src, layer, steps, step_of, axis):
    _, r, c = cast_src.shape
    if axis == 2:
        assert c % (steps * LANES) == 0
        blk = (r, c // steps)
        at = lambda *g: (0, step_of(*g))
    else:
        assert r % (steps * 2 * SUBLANES) == 0
        blk = (r // steps, c)
        at = lambda *g: (step_of(*g), 0)
    in_spec = pl.BlockSpec((None,) + blk, lambda *g: (layer,) + at(*g))
    return in_spec, pl.BlockSpec(blk, at), jax.ShapeDtypeStruct((r, c), BF16), 2 * blk[0] * blk[1] * 6


def _hgrn(p, onorm_g, s0, *, batch, seq, layer, cast_src=None):
    n = p.shape[0]
    heads = s0.shape[2]
    hb = HGRN_HEADS_PER_STEP
    wblk = hb * HEAD_DIM
    n_hb = heads // hb
    width = heads * HEAD_DIM
    tc = min(512, seq)
    chunk = min(HGRN_CHUNK, seq)
    n_t = seq // tc
    row = lambda b, h, c: b * n_t + c
    need = 2 * 4 * tc * wblk * 4 + 2 * tc * wblk * 2 + 3 * tc * wblk * 4 + 5 * hb * HEAD_DIM * HEAD_DIM * 4
    kernel = functools.partial(_hgrn_kernel, chunk=chunk, heads_blk=hb, with_cast=cast_src is not None)
    st_spec = pl.BlockSpec((None, 1, hb, HEAD_DIM, HEAD_DIM), lambda b, h, c: (layer, b, h, 0, 0))
    in_specs = [
        pl.BlockSpec((tc, wblk), lambda b, h, c: (row(b, h, c), h)),
        pl.BlockSpec((tc, wblk), lambda b, h, c: (row(b, h, c), n_hb + h)),
        pl.BlockSpec((tc, wblk), lambda b, h, c: (row(b, h, c), 2 * n_hb + h)),
        pl.BlockSpec((tc, wblk), lambda b, h, c: (row(b, h, c), 3 * n_hb + h)),
        pl.BlockSpec((None, 1, wblk), lambda b, h, c: (layer, 0, h)),
        st_spec,
    ]
    out_specs = [
        pl.BlockSpec((tc, wblk), lambda b, h, c: (row(b, h, c), h)),
        pl.BlockSpec((1, hb, HEAD_DIM, HEAD_DIM), lambda b, h, c: (b, h, 0, 0)),
    ]
    out_shape = [jax.ShapeDtypeStruct((n, width), BF16),
                 jax.ShapeDtypeStruct((batch, heads, HEAD_DIM, HEAD_DIM), F32)]
    args = [p, p, p, p, onorm_g, s0]
    if cast_src is not None:
        c_in, c_out, c_shape, c_bytes = _cast_job(cast_src, layer, batch * n_hb * n_t,
                                                  lambda b, h, c: (b * n_hb + h) * n_t + c, axis=2)
        in_specs.append(c_in)
        out_specs.append(c_out)
        out_shape.append(c_shape)
        args.append(cast_src)
        need += c_bytes
    return pl.pallas_call(
        kernel,
        out_shape=tuple(out_shape),
        grid=(batch, n_hb, n_t),
        in_specs=in_specs,
        out_specs=tuple(out_specs),
        scratch_shapes=[
            pltpu.VMEM((hb, HEAD_DIM, HEAD_DIM), F32),
            pltpu.VMEM((tc, wblk), F32),
            pltpu.VMEM((tc, wblk), F32),
            pltpu.VMEM((tc, wblk), F32),
            pltpu.VMEM((chunk, wblk), F32),
            pltpu.VMEM((chunk, wblk), F32),
        ],
        compiler_params=pltpu.CompilerParams(
            dimension_semantics=("parallel", "parallel", "arbitrary"),
            vmem_limit_bytes=_vmem_limit(need)),
        name="hgrn",
    )(*args)


def _s5_disc_lambda_kernel(lr_ref, li_ref, ls_ref, lbr_ref, lbi_ref, cr_ref, ci_ref):
    lr, li = lr_ref[...], li_ref[...]
    dt = jnp.exp(ls_ref[...])
    mag = jnp.exp(dt * lr)
    ang = dt * li
    lbr = mag * jnp.cos(ang)
    lbi = mag * jnp.sin(ang)
    nr, ni = lbr - 1.0, lbi
    den = lr * lr + li * li
    lbr_ref[...] = lbr
    lbi_ref[...] = lbi
    cr_ref[...] = (nr * lr + ni * li) / den
    ci_ref[...] = (ni * lr - nr * li) / den


def _s5_discretise(lam_re, lam_im, log_step):
    depth, groups, nst = lam_re.shape
    rows = depth * groups
    shp = jax.ShapeDtypeStruct((rows, nst), F32)
    outs = pl.pallas_call(
        _s5_disc_lambda_kernel, out_shape=(shp, shp, shp, shp), name="s5_disc_lambda",
    )(lam_re.reshape(rows, nst), lam_im.reshape(rows, nst), log_step.reshape(rows, 1))
    return tuple(o.reshape(depth, groups, nst) for o in outs)


def _s5_pair_weights(lbr, lbi, zoh_r, zoh_i, b_re, b_im, c_re, c_im, d_skip):
    depth, groups, nst, cg = b_re.shape
    hi = lax.Precision.HIGHEST
    zr, zi = zoh_r[..., None], zoh_i[..., None]
    bbr, bbi = zr * b_re - zi * b_im, zr * b_im + zi * b_re
    lr, li = lbr[..., None], lbi[..., None]
    blr, bli = bbr * lr - bbi * li, bbr * li + bbi * lr
    tb = jnp.swapaxes(jnp.stack([blr, bli, bbr, bbi], axis=1), 3, 4).reshape(depth, 4, groups * cg, nst)

    lrc, lic = lbr[:, :, None, :], lbi[:, :, None, :]
    pr, pi = c_re * lrc - c_im * lic, c_re * lic + c_im * lrc
    qr, qi = pr * lrc - pi * lic, pr * lic + pi * lrc
    tcc = jnp.stack([pr, -pi, qr, -qi], axis=1).transpose(0, 1, 3, 2, 4).reshape(depth, 4, cg, groups * nst)

    mm = lambda a, b: jnp.einsum("lgcn,lgnd->lgcd", a, b, precision=hi)
    g0 = mm(c_re, bbr) - mm(c_im, bbi)
    g1 = mm(pr, bbr) - mm(pi, bbi)
    g0d = g0 + d_skip.reshape(depth, groups, cg)[..., None] * jnp.eye(cg, dtype=F32)
    tg = jnp.stack([g0d, g1], axis=1).transpose(0, 1, 3, 2, 4).reshape(depth, 2, cg, groups * cg)
    lam2 = jnp.stack([lbr * lbr - lbi * lbi, 2.0 * lbr * lbi], axis=1).reshape(depth, 2, groups * nst)
    return tb, tcc, tg, lam2


def _s5_embed_weights(tb_ref, tcc_ref, tg_ref, wb_ref, wct_ref):
    nbk = wb_ref.shape[0]
    cg, nst = tcc_ref.shape[1], tb_ref.shape[2]
    bl = wb_ref.shape[2] // 2
    assert cg & (cg - 1) == 0 and nst & (nst - 1) == 0
    c_shift, n_shift = cg.bit_length() - 1, nst.bit_length() - 1
    iota = lambda shape, dim: lax.broadcasted_iota(jnp.int32, shape, dim)
    e_in = jnp.where((iota((nst, bl), 1) & (nst - 1)) == iota((nst, bl), 0), 1.0, 0.0).astype(BF16)
    e_out = jnp.where((iota((LANES, cg), 0) & (cg - 1)) == iota((LANES, cg), 1), 1.0, 0.0).astype(BF16)
    m_state = (iota((LANES, bl), 0) >> c_shift) == (iota((LANES, bl), 1) >> n_shift)
    m_direct = (iota((LANES, LANES), 0) >> c_shift) == (iota((LANES, LANES), 1) >> c_shift)
    zeros = jnp.zeros((LANES, LANES), BF16)
    for j in range(nbk):
        for k in range(4):
            half, part = divmod(k, 2)
            a = tb_ref[k, j * LANES:(j + 1) * LANES, :].astype(BF16)
            wb_ref[j, half * LANES:(half + 1) * LANES, part * bl:(part + 1) * bl] = (
                jnp.where(m_state, _dot(a, e_in), 0.0).astype(BF16))
        for k in range(4):
            eo, part = divmod(k, 2)
            x = tcc_ref[k, :, j * bl:(j + 1) * bl].astype(BF16)
            wct_ref[j, eo * LANES:(eo + 1) * LANES, part * bl:(part + 1) * bl] = (
                jnp.where(m_state, _dot(e_out, x), 0.0).astype(BF16))
        direct = [jnp.where(m_direct, _dot(e_out, tg_ref[k, :, j * LANES:(j + 1) * LANES].astype(BF16)), 0.0
                            ).astype(BF16) for k in range(2)]
        base = 2 * bl
        wct_ref[j, 0:LANES, base:base + LANES] = direct[0]
        wct_ref[j, 0:LANES, base + LANES:base + 2 * LANES] = zeros
        wct_ref[j, LANES:2 * LANES, base:base + LANES] = direct[1]
        wct_ref[j, LANES:2 * LANES, base + LANES:base + 2 * LANES] = direct[0]


def _s5_kernel(u_ref, tb_ref, tcc_ref, tg_ref, lam2_ref, wg_ref, bg_ref, x0r_ref, x0i_ref, *refs,
               batch, with_cast):
    refs, cast_src, cast_dst = _split_cast_refs(refs, 3, with_cast)
    o_ref, xr_ref, xi_ref, xs_ref, y_ref, st_ref, ue_ref, uo_ref, tmp_ref, wb_ref, wct_ref = refs
    if with_cast:
        cast_dst[...] = cast_src[...].astype(BF16)
    c_idx = pl.program_id(0)
    tt = u_ref.shape[0]
    tp = tt // 2
    prows = tp * batch
    width = u_ref.shape[1] // batch
    n_slab = width // LANES
    nbk = wb_ref.shape[0]
    bl = wb_ref.shape[2] // 2
    ns = nbk * bl
    grp = st_ref.shape[1]
    per = grp // batch

    @pl.when(c_idx == 0)
    def _():
        st_ref[0] = jnp.concatenate([x0r_ref[...]] * per, axis=0)
        st_ref[1] = jnp.concatenate([x0i_ref[...]] * per, axis=0)
        _s5_embed_weights(tb_ref, tcc_ref, tg_ref, wb_ref, wct_ref)

    for b in range(batch):
        for m in range(n_slab):
            tmp_ref[...] = u_ref[:, b * width + m * LANES:b * width + (m + 1) * LANES]
            ue_ref[m, pl.ds(b, tp, stride=batch), :] = tmp_ref[pl.ds(0, tp, stride=2), :]
            uo_ref[m, pl.ds(b, tp, stride=batch), :] = tmp_ref[pl.ds(1, tp, stride=2), :]

    for j in range(nbk):
        lhs = jnp.concatenate([ue_ref[j], uo_ref[j]], axis=1).astype(BF16)
        w = _dot(lhs, wb_ref[j])
        xs_ref[:, j * bl:(j + 1) * bl] = w[:, :bl]
        xs_ref[:, ns + j * bl:ns + (j + 1) * bl] = w[:, bl:]

    second = lax.broadcasted_iota(jnp.int32, (grp, S5_SCAN_LANES), 0) >= batch
    for cb in range(ns // S5_SCAN_LANES):
        lo = cb * S5_SCAN_LANES
        re_l = slice(lo, lo + S5_SCAN_LANES)
        im_l = slice(ns + lo, ns + lo + S5_SCAN_LANES)
        ar = lam2_ref[0:1, re_l]
        ai = lam2_ref[1:2, re_l]

        def step(g, carry, ar=ar, ai=ai, re_l=re_l, im_l=im_l):
            cr, ci = carry
            r = pl.ds(pl.multiple_of(g * grp, grp), grp)
            wr = xs_ref[r, re_l]
            wi = xs_ref[r, im_l]
            if per == 2:
                tr = ar * cr - ai * ci + pltpu.roll(wr, batch, axis=0)
                ti = ar * ci + ai * cr + pltpu.roll(wi, batch, axis=0)
                xs_ref[r, re_l] = jnp.where(second, tr, cr)
                xs_ref[r, im_l] = jnp.where(second, ti, ci)
                vr = ar * tr - ai * ti + wr
                vi = ar * ti + ai * tr + wi
                nr = jnp.where(second, vr, pltpu.roll(vr, batch, axis=0))
                ni = jnp.where(second, vi, pltpu.roll(vi, batch, axis=0))
            else:
                xs_ref[r, re_l] = cr
                xs_ref[r, im_l] = ci
                nr = ar * cr - ai * ci + wr
                ni = ar * ci + ai * cr + wi
            return nr, ni

        cr, ci = lax.fori_loop(0, prows // grp, step, (st_ref[0, :, re_l], st_ref[1, :, re_l]), unroll=2)
        st_ref[0, :, re_l] = cr
        st_ref[1, :, re_l] = ci

    for j in range(nbk):
        lhs = jnp.concatenate([xs_ref[:, j * bl:(j + 1) * bl], xs_ref[:, ns + j * bl:ns + (j + 1) * bl],
                               ue_ref[j], uo_ref[j]], axis=1).astype(BF16)
        yj = _dot_nt(lhs, wct_ref[j])
        y_ref[0:prows, j * LANES:(j + 1) * LANES] = yj[:, :LANES]
        y_ref[prows:2 * prows, j * LANES:(j + 1) * LANES] = yj[:, LANES:]

    hh = jax.nn.gelu(y_ref[...])
    gate = _sigmoid(_dot(hh.astype(BF16), wg_ref[...]) + bg_ref[...])
    out = hh * gate
    for m in range(n_slab):
        ue_ref[m] = out[0:prows, m * LANES:(m + 1) * LANES]
        uo_ref[m] = out[prows:2 * prows, m * LANES:(m + 1) * LANES]
    for b in range(batch):
        for m in range(n_slab):
            tmp_ref[pl.ds(0, tp, stride=2), :] = ue_ref[m, pl.ds(b, tp, stride=batch), :]
            tmp_ref[pl.ds(1, tp, stride=2), :] = uo_ref[m, pl.ds(b, tp, stride=batch), :]
            o_ref[:, b * width + m * LANES:b * width + (m + 1) * LANES] = tmp_ref[...].astype(BF16)

    @pl.when(c_idx == pl.num_programs(0) - 1)
    def _():
        xr_ref[...] = st_ref[0, grp - batch:grp, :]
        xi_ref[...] = st_ref[1, grp - batch:grp, :]


def _s5(u, tb, tcc, tg, lam2, w_glu, b_glu, x0r, x0i, *, batch, layer, cast_src=None):
    seq = u.shape[0]
    width = u.shape[1] // batch
    n = seq * batch
    ns = lam2.shape[2]
    nbk = width // LANES
    bl = ns // nbk
    assert batch % SUBLANES == 0 or 2 * batch == SUBLANES
    assert seq % 2 == 0 and tb.shape[2] == width
    grp = max(batch, SUBLANES)
    rows = min(S5_ROWS, n)
    tt = rows // batch
    prows = rows // 2
    lay4 = lambda c: (layer, 0, 0, 0)
    lay3 = lambda c: (layer, 0, 0)
    st_in = pl.BlockSpec((None, batch, ns), lay3)
    resident = pl.Buffered(1)
    wb_shape = (nbk, 2 * LANES, 2 * bl)
    wct_shape = (nbk, 2 * LANES, 2 * bl + 2 * LANES)
    need = (2 * rows * width * 4 + (nbk * 2 * LANES * (4 * bl + 2 * LANES) + width * width) * 2
            + 4 * width * LANES * 4 + 6 * 16 * ns * 4
            + 2 * rows * width * 2 + prows * 2 * ns * 4 + rows * width * 4 + 4 * rows * width * 4
            + 4 * prows * (4 * bl + 2 * LANES) + (8 * batch + 2 * grp) * ns * 4)
    st_shape = jax.ShapeDtypeStruct((batch, ns), F32)
    in_specs = [
        pl.BlockSpec((tt, batch * width), lambda c: (c, 0)),
        pl.BlockSpec((None,) + tb.shape[1:], lay4, pipeline_mode=resident),
        pl.BlockSpec((None,) + tcc.shape[1:], lay4, pipeline_mode=resident),
        pl.BlockSpec((None,) + tg.shape[1:], lay4, pipeline_mode=resident),
        pl.BlockSpec((None, 2, ns), lay3),
        pl.BlockSpec((None, width, width), lay3, pipeline_mode=resident),
        pl.BlockSpec((None, 1, width), lay3),
        st_in,
        st_in,
    ]
    out_specs = [
        pl.BlockSpec((tt, batch * width), lambda c: (c, 0)),
        pl.BlockSpec((batch, ns), lambda c: (0, 0)),
        pl.BlockSpec((batch, ns), lambda c: (0, 0)),
    ]
    out_shape = [jax.ShapeDtypeStruct((seq, batch * width), BF16), st_shape, st_shape]
    args = [u, tb, tcc, tg, lam2, w_glu, b_glu, x0r, x0i]
    if cast_src is not None:
        c_in, c_out, c_shape, c_bytes = _cast_job(cast_src, layer, n // rows, lambda c: c, axis=1)
        in_specs.append(c_in)
        out_specs.append(c_out)
        out_shape.append(c_shape)
        args.append(cast_src)
        need += c_bytes
    return pl.pallas_call(
        functools.partial(_s5_kernel, batch=batch, with_cast=cast_src is not None),
        out_shape=tuple(out_shape),
        grid=(n // rows,),
        in_specs=in_specs,
        out_specs=tuple(out_specs),
        scratch_shapes=[
            pltpu.VMEM((prows, 2 * ns), F32),
            pltpu.VMEM((rows, width), F32),
            pltpu.VMEM((2, grp, ns), F32),
            pltpu.VMEM((width // LANES, prows, LANES), F32),
            pltpu.VMEM((width // LANES, prows, LANES), F32),
            pltpu.VMEM((tt, LANES), F32),
            pltpu.VMEM(wb_shape, BF16),
            pltpu.VMEM(wct_shape, BF16),
        ],
        compiler_params=pltpu.CompilerParams(
            dimension_semantics=("arbitrary",),
            vmem_limit_bytes=_vmem_limit(need)),
        name="s5",
    )(*args)


def _outproj_kernel(x_ref, oh_ref, os_ref, wh_ref, ws_ref, o_ref):
    o_ref[...] = x_ref[...] + _dot(oh_ref[...], wh_ref[...]) + _dot(os_ref[...], ws_ref[...])


def _outproj(x, o_h, o_s, w_out, *, layer, tm, tn, seq):
    n, d = x.shape
    kh = o_h.shape[1]
    ks = w_out.shape[1] - kh
    assert kh == ks
    if o_s.shape[0] == n:
        os_map = lambda i, j: (i, 0)
    else:
        n_t = seq // tm
        os_map = lambda i, j: (i % n_t, i // n_t)
    need = 2 * (2 * tm * tn * 4 + tm * (kh + ks) * 2 + (kh + ks) * tn * 2) + 2 * tm * tn * 4
    return pl.pallas_call(
        _outproj_kernel,
        out_shape=jax.ShapeDtypeStruct((n, d), F32),
        grid=(n // tm, d // tn),
        in_specs=[
            pl.BlockSpec((tm, tn), lambda i, j: (i, j)),
            pl.BlockSpec((tm, kh), lambda i, j: (i, 0)),
            pl.BlockSpec((tm, ks), os_map),
            pl.BlockSpec((None, kh, tn), lambda i, j: (layer, 0, j)),
            pl.BlockSpec((None, ks, tn), lambda i, j: (layer, 1, j)),
        ],
        out_specs=pl.BlockSpec((tm, tn), lambda i, j: (i, j)),
        compiler_params=pltpu.CompilerParams(
            dimension_semantics=("parallel", "parallel"),
            vmem_limit_bytes=_vmem_limit(need)),
        name="outproj",
    )(x, o_h, o_s, w_out, w_out)


def _ffn_kernel(x_ref, g_ref, w1_ref, w2_ref, gf_ref, o_ref, h_ref, *, final_norm):
    f = pl.program_id(1)
    tm = x_ref.shape[0]

    @pl.when(f == 0)
    def _():
        g = g_ref[...]

        def body(i, carry):
            r = pl.ds(pl.multiple_of(i * NORM_ROWS, NORM_ROWS), NORM_ROWS)
            h_ref[r, :] = _rmsnorm(x_ref[r, :], g).astype(BF16)
            o_ref[r, :] = jnp.zeros((NORM_ROWS, o_ref.shape[1]), F32)
            return carry

        lax.fori_loop(0, tm // NORM_ROWS, body, 0, unroll=NORM_UNROLL)

    a = _dot(h_ref[...], w1_ref[...])
    a = jnp.square(jnp.maximum(a, 0.0)).astype(BF16)
    cw = o_ref.shape[1] // FFN_ACC_CHUNKS
    for c in range(FFN_ACC_CHUNKS):
        o_ref[:, c * cw:(c + 1) * cw] += _dot(a, w2_ref[:, c * cw:(c + 1) * cw])

    @pl.when(f == pl.num_programs(1) - 1)
    def _():
        gf = gf_ref[...]

        def body(i, carry):
            r = pl.ds(pl.multiple_of(i * NORM_ROWS, NORM_ROWS), NORM_ROWS)
            y = x_ref[r, :] + o_ref[r, :]
            if final_norm:
                y = _rmsnorm(y, gf)
            o_ref[r, :] = y
            return carry

        lax.fori_loop(0, tm // NORM_ROWS, body, 0, unroll=NORM_UNROLL)


def _ffn(x, g, w1, w2, gf, *, layer, tm, tf, final_norm):
    n, d = x.shape
    ff = w1.shape[1]
    need = tm * d * 4 + tm * d * 2 + 2 * tm * d * 4 + 4 * d * tf * 2 + tm * tf * 6 + tm * d * 4
    kernel = functools.partial(_ffn_kernel, final_norm=final_norm)
    return pl.pallas_call(
        kernel,
        out_shape=jax.ShapeDtypeStruct((n, d), F32),
        grid=(n // tm, ff // tf),
        in_specs=[
            pl.BlockSpec((tm, d), lambda i, f: (i, 0), pipeline_mode=pl.Buffered(1)),
            pl.BlockSpec((None, 1, d), lambda i, f: (layer, 0, 0)),
            pl.BlockSpec((d, tf), lambda i, f: (0, f)),
            pl.BlockSpec((tf, d), lambda i, f: (f, 0)),
            pl.BlockSpec((1, d), lambda i, f: (0, 0)),
        ],
        out_specs=pl.BlockSpec((tm, d), lambda i, f: (i, 0)),
        scratch_shapes=[pltpu.VMEM((tm, d), BF16)],
        compiler_params=pltpu.CompilerParams(
            dimension_semantics=("parallel", "arbitrary"),
            vmem_limit_bytes=_vmem_limit(need)),
        name="ffn",
    )(x, g, w1, w2, gf)


def _trunk(x, st_h, st_r, st_i, wts, ffn_w=None):
    batch, seq, d = x.shape
    n = batch * seq
    depth = wts["w_in"].shape[0]
    heads = st_h.shape[2]
    hgrn_width = heads * HEAD_DIM
    groups, nst = st_r.shape[2], st_r.shape[3]
    s5_width = wts["w_glu"].shape[1]
    tm = min(1024, n)
    xf = x.reshape(n, d)
    x0r = st_r.reshape(depth, batch, groups * nst)
    x0i = st_i.reshape(depth, batch, groups * nst)
    time_major = seq % tm == 0
    new_h, new_r, new_i, used_w = [], [], [], []
    for l in range(depth):
        cast = ffn_w is None
        p, u = _inproj(xf, wts["norm1_g"], wts["w_in"], wts["lb_logits"],
                       layer=l, tm=tm, tn=s5_width, batch=batch, seq=seq)
        o_h, s_new, *w1 = _hgrn(p, wts["onorm_g"], st_h, batch=batch, seq=seq, layer=l,
                                cast_src=wts["w_ff1"] if cast else None)
        if not time_major:
            u = u.reshape(batch, seq, s5_width).transpose(1, 0, 2).reshape(seq, batch * s5_width)
        o_s, xr, xi, *w2 = _s5(u, wts["s5_tb"], wts["s5_tcc"], wts["s5_tg"], wts["s5_lam2"], wts["w_glu"],
                               wts["b_glu"], x0r, x0i, batch=batch, layer=l,
                               cast_src=wts["w_ff2"] if cast else None)
        if not time_major:
            o_s = o_s.reshape(seq, batch, s5_width).transpose(1, 0, 2).reshape(n, s5_width)
        used_w.append((w1[0], w2[0]) if cast else ffn_w[l])
        x1 = _outproj(xf, o_h, o_s, wts["w_out"], layer=l, tm=min(512, n), tn=d, seq=seq)
        xf = _ffn(x1, wts["norm2_g"], *used_w[l], wts["final_g"],
                  layer=l, tm=tm, tf=1024, final_norm=(l == depth - 1))
        new_h.append(s_new)
        new_r.append(xr.reshape(batch, groups, nst))
        new_i.append(xi.reshape(batch, groups, nst))
    return (xf.reshape(batch, seq, d), jnp.stack(new_h), jnp.stack(new_r), jnp.stack(new_i)), used_w


def kernel(x_prompt, x_sample, state_hgrn, state_s5_re, state_s5_im, norm1_g, w_in, hgrn_lb_logits,
           hgrn_onorm_g, s5_lambda_re, s5_lambda_im, s5_log_step, s5_B_re, s5_B_im, s5_C_re, s5_C_im,
           s5_D, s5_w_glu, s5_b_glu, w_out, norm2_g, w_ff1, w_ff2, final_norm_g):
    depth, d = norm1_g.shape
    row3 = lambda a: a.reshape(depth, 1, a.shape[-1])
    lbr, lbi, zoh_r, zoh_i = _s5_discretise(s5_lambda_re, s5_lambda_im, s5_log_step)
    s5_tb, s5_tcc, s5_tg, s5_lam2 = _s5_pair_weights(lbr, lbi, zoh_r, zoh_i, s5_B_re, s5_B_im,
                                                     s5_C_re, s5_C_im, s5_D)
    wts = {
        "norm1_g": row3(norm1_g), "norm2_g": row3(norm2_g), "final_g": final_norm_g.reshape(1, d),
        "lb_logits": hgrn_lb_logits, "onorm_g": row3(hgrn_onorm_g),
        "w_in": w_in.astype(BF16), "w_out": w_out.astype(BF16),
        "w_ff1": w_ff1, "w_ff2": w_ff2,
        "w_glu": s5_w_glu.astype(BF16), "b_glu": row3(s5_b_glu),
        "s5_tb": s5_tb, "s5_tcc": s5_tcc, "s5_tg": s5_tg, "s5_lam2": s5_lam2,
    }
    bp = x_prompt.shape[0]
    zh = jnp.zeros((depth, bp) + state_hgrn.shape[2:], F32)
    zs = jnp.zeros((depth, bp) + state_s5_re.shape[2:], F32)
    (y_p, hp, rp, ip), ffn_w = _trunk(x_prompt, zh, zs, zs, wts)
    (y_s, hs, rs, is_), _ = _trunk(x_sample, state_hgrn, state_s5_re, state_s5_im, wts, ffn_w)
    return (y_p, y_s, hp, rp, ip, hs, rs, is_)
```

```python
import functools

import jax
import jax.numpy as jnp
from jax import lax
from jax.experimental import pallas as pl
from jax.experimental.pallas import tpu as pltpu

F32 = jnp.float32
BF16 = jnp.bfloat16
EPS = 1e-6

HEAD_DIM = 128
HGRN_CHUNK = 64
HGRN_SUB = 16
HGRN_HEADS_PER_STEP = 4
SUBLANES = 8
LANES = 128
S5_SCAN_LANES = 512
S5_ROWS = 512

V7X_VMEM_CAP = 56 * 1024 * 1024
NORM_ROWS = 64
NORM_UNROLL = 4
FFN_ACC_CHUNKS = 4
INPROJ_CHUNKS = 4


def _vmem_limit(nbytes):
    return int(min(V7X_VMEM_CAP, nbytes * 5 // 4 + (4 << 20)))


def _rmsnorm(x, g):
    return x * lax.rsqrt(jnp.mean(jnp.square(x), axis=-1, keepdims=True) + EPS) * g


def _sigmoid(x):
    return 1.0 / (1.0 + jnp.exp(-x))


def _log1p_exp_neg_abs(x):
    return jnp.log(1.0 + jnp.exp(-jnp.abs(x)))


def _softplus(x):
    return jnp.maximum(x, 0.0) + _log1p_exp_neg_abs(x)


def _dot(a, b):
    return jnp.dot(a, b, preferred_element_type=F32)


def _dot_nt(a, b):
    return lax.dot_general(a, b, (((1,), (1,)), ((), ())), preferred_element_type=F32)


def _inproj_kernel(x_ref, g_ref, w_ref, lbl_ref, p_ref, u_ref, h_ref, *, layer):
    tm = x_ref.shape[0]
    cw = w_ref.shape[1] // INPROJ_CHUNKS
    j = pl.program_id(1)

    @pl.when(j == 0)
    def _():
        g = g_ref[...]

        def body(i, carry):
            r = pl.ds(pl.multiple_of(i * NORM_ROWS, NORM_ROWS), NORM_ROWS)
            h_ref[r, :] = _rmsnorm(x_ref[r, :], g).astype(BF16)
            return carry

        lax.fori_loop(0, tm // NORM_ROWS, body, 0, unroll=NORM_UNROLL)

    def emit(out_ref, fn):
        for c in range(INPROJ_CHUNKS):
            sl = slice(c * cw, (c + 1) * cw)
            out_ref[:, sl] = fn(_dot(h_ref[...], w_ref[:, sl]), sl)

    def silu(a, sl):
        return a * _sigmoid(a)

    def log_forget(z, sl):
        logits = lbl_ref[:, sl]
        e = jnp.exp(logits - jnp.max(logits, axis=0, keepdims=True))
        sm = e / jnp.sum(e, axis=0, keepdims=True)
        cum0 = sm[0:1]
        cuml = cum0
        for k in range(1, layer + 1):
            cuml = cuml + sm[k:k + 1]
        lb = cuml - cum0
        log_lb = jnp.log(lb)
        b = jnp.log1p(-lb) - _softplus(-z)
        delta = log_lb - b
        return jnp.where(jnp.isnan(delta), log_lb + b,
                         jnp.maximum(log_lb, b) + _log1p_exp_neg_abs(delta))

    for tile, (out_ref, fn) in enumerate([(p_ref, silu), (p_ref, log_forget), (p_ref, lambda a, sl: a),
                                          (p_ref, silu), (u_ref, lambda a, sl: a)]):
        pl.when(j == tile)(functools.partial(emit, out_ref, fn))


def _inproj(x, g, w, lb_logits, *, layer, tm, tn, batch, seq):
    n, d = x.shape
    cols = w.shape[2]
    n_j = cols // tn
    depth = lb_logits.shape[0]
    assert n_j == 5 and lb_logits.shape[1] == tn
    time_major = seq % tm == 0
    if time_major:
        n_t = seq // tm
        u_shape, u_map = (seq, batch * tn), (lambda i, j: (i % n_t, i // n_t))
    else:
        u_shape, u_map = (n, tn), (lambda i, j: (i, 0))
    need = 2 * tm * d * 4 + tm * d * 2 + 2 * d * tn * 2 + 5 * tm * tn * 4
    return pl.pallas_call(
        functools.partial(_inproj_kernel, layer=layer),
        out_shape=(jax.ShapeDtypeStruct((n, cols - tn), F32), jax.ShapeDtypeStruct(u_shape, F32)),
        grid=(n // tm, n_j),
        in_specs=[
            pl.BlockSpec((tm, d), lambda i, j: (i, 0)),
            pl.BlockSpec((None, 1, d), lambda i, j: (layer, 0, 0)),
            pl.BlockSpec((None, d, tn), lambda i, j: (layer, 0, j)),
            pl.BlockSpec((depth, tn), lambda i, j: (0, 0)),
        ],
        out_specs=(pl.BlockSpec((tm, tn), lambda i, j: (i, jnp.minimum(j, n_j - 2))),
                   pl.BlockSpec((tm, tn), u_map)),
        scratch_shapes=[pltpu.VMEM((tm, d), BF16)],
        compiler_params=pltpu.CompilerParams(
            dimension_semantics=("parallel", "arbitrary"),
            vmem_limit_bytes=_vmem_limit(need)),
        name="inproj",
    )(x, g, w, lb_logits)


def _hgrn_decayed_queries(q_lo, q_hi, f_row, row_is):
    slabs = []
    qd_lo = qd_hi = None
    for s in range(2 * SUBLANES - 1, -1, -1):
        if s == 2 * SUBLANES - 1:
            qd_hi = jnp.where(row_is[SUBLANES - 1], q_hi, 0.0)
        elif s >= SUBLANES:
            qd_hi = jnp.where(row_is[s - SUBLANES], q_hi, qd_hi * f_row(s + 1))
        else:
            fn = f_row(s + 1)
            qd_hi = qd_hi * fn
            qd_lo = jnp.where(row_is[s], q_lo, 0.0 if s == SUBLANES - 1 else qd_lo * fn)
        slabs.append((s, 1, qd_hi))
        if s < SUBLANES:
            slabs.append((s, 0, qd_lo))
    return slabs


def _split_cast_refs(refs, n_out, with_cast):
    if not with_cast:
        return refs, None, None
    return refs[1:1 + n_out] + refs[2 + n_out:], refs[0], refs[1 + n_out]


def _hgrn_kernel(q_ref, f_ref, i_ref, g_ref, on_ref, s0_ref, *refs, chunk, heads_blk, with_cast):
    refs, cast_src, cast_dst = _split_cast_refs(refs, 2, with_cast)
    o_ref, sout_ref, st_ref, ks_ref, cs_ref, fs_ref, fc_ref, op_ref = refs
    if with_cast:
        cast_dst[...] = cast_src[...].astype(BF16)
    c_idx = pl.program_id(2)
    tc = q_ref.shape[0]
    n_sub = chunk // HGRN_SUB

    @pl.when(c_idx == 0)
    def _():
        for h in range(heads_blk):
            st_ref[h] = s0_ref[0, h].T

    logf = f_ref[...]
    fgate = jnp.exp(logf)
    fs_ref[...] = fgate
    ks_ref[...] = 1.0 - fgate

    t_i = lax.broadcasted_iota(jnp.int32, (chunk, chunk), 0)
    s_i = lax.broadcasted_iota(jnp.int32, (chunk, chunk), 1)
    tri = jnp.where(t_i >= s_i, 1.0, 0.0).astype(BF16)
    for j in range(tc // chunk):
        lf = logf[j * chunk:(j + 1) * chunk]
        hi = lf.astype(BF16)
        r1 = lf - hi.astype(F32)
        mid = r1.astype(BF16)
        lo = (r1 - mid.astype(F32)).astype(BF16)
        cs_ref[j * chunk:(j + 1) * chunk, :] = _dot(tri, hi) + _dot(tri, mid) + _dot(tri, lo)

    row8 = lax.broadcasted_iota(jnp.int32, (SUBLANES, HEAD_DIM), 0)
    row_is = [row8 == j for j in range(SUBLANES)]
    lane8 = lax.broadcasted_iota(jnp.int32, (SUBLANES, chunk), 1)
    lane_is = [lane8 == j for j in range(chunk)]
    onorm = on_ref[...]
    blk = lambda a, j: a[j * HGRN_SUB:(j + 1) * HGRN_SUB]

    head_cols = [slice(h * HEAD_DIM, (h + 1) * HEAD_DIM) for h in range(heads_blk)]

    def finish(rows):
        for cols in head_cols:
            o = op_ref[:, cols]
            o = o * lax.rsqrt(jnp.mean(jnp.square(o), axis=-1, keepdims=True) + EPS)
            o = o * onorm[:, cols]
            o_ref[rows, cols] = (o * g_ref[rows, cols]).astype(BF16)

    op_ref[...] = jnp.zeros(op_ref.shape, F32)

    def chunk_body(c, carry):
        base = pl.multiple_of(c * chunk, chunk)
        r = pl.ds(base, chunk)
        finish(pl.ds(pl.multiple_of(jnp.maximum(c - 1, 0) * chunk, chunk), chunk))
        fc_ref[...] = fs_ref[r, :]
        cum_all = cs_ref[r, :]

        zero_blk = jnp.zeros((HGRN_SUB, HEAD_DIM), F32)
        qs, vbs, kbs, o_inter, a_off = [], [], [], [], []
        for cols in head_cols:
            h = len(qs)
            cum = cum_all[:, cols]
            q = q_ref[r, cols]
            k = ks_ref[r, cols]
            v = i_ref[r, cols]
            ends = [cum[(j + 1) * HGRN_SUB - 1:(j + 1) * HGRN_SUB, :] for j in range(n_sub)]
            cl = ends[-1]
            ke = [blk(k, j) * jnp.exp(ends[j] - blk(cum, j)) for j in range(n_sub)]
            qe = [blk(q, j) * jnp.exp(blk(cum, j) - ends[j - 1] if j else blk(cum, j)) for j in range(n_sub)]

            st = st_ref[h]
            q_in = jnp.concatenate([qe[j] * jnp.exp(ends[j - 1]) if j else qe[j] for j in range(n_sub)], axis=0)
            o_inter.append(_dot_nt(q_in.astype(BF16), st.astype(BF16)))
            kdec = jnp.concatenate(
                [ke[j] * jnp.exp(cl - ends[j]) if j < n_sub - 1 else ke[j] for j in range(n_sub)], axis=0)
            st_ref[h] = st * jnp.exp(cl) + _dot(v.T.astype(BF16), kdec.astype(BF16))

            offs = [None]
            for i in range(1, n_sub):
                kt = jnp.concatenate(
                    [ke[j] * jnp.exp(ends[i - 1] - ends[j]) if j < i - 1 else ke[j] for j in range(i)]
                    + [zero_blk] * (n_sub - i), axis=0)
                offs.append(_dot_nt(qe[i].astype(BF16), kt.astype(BF16)))
            a_off.append(offs)
            qs.append(q)
            vbs.append(v.astype(BF16))
            kbs.append(k.astype(BF16))

        res = []
        for h, cols in enumerate(head_cols):
            per_head = []
            for i in range(n_sub):
                lo_r = i * HGRN_SUB
                f_row = lambda s, lo_r=lo_r, cols=cols: fc_ref[lo_r + s:lo_r + s + 1, cols]
                q_blk = blk(qs[h], i)
                slabs = _hgrn_decayed_queries(q_blk[:SUBLANES], q_blk[SUBLANES:], f_row, row_is)
                stack = jnp.concatenate([slab for _, _, slab in slabs], axis=0)
                per_head.append(([(s, half) for s, half, _ in slabs],
                                 _dot_nt(stack.astype(BF16), kbs[h])))
            res.append(per_head)

        o_intra = []
        for h in range(heads_blk):
            rows = []
            for i in range(n_sub):
                order, prod = res[h][i]
                halves = [jnp.zeros((SUBLANES, chunk), F32), jnp.zeros((SUBLANES, chunk), F32)]
                for idx, (s, half) in enumerate(order):
                    piece = prod[idx * SUBLANES:(idx + 1) * SUBLANES]
                    halves[half] = jnp.where(lane_is[i * HGRN_SUB + s], piece, halves[half])
                a = jnp.concatenate(halves, axis=0)
                rows.append(a + a_off[h][i] if i else a)
            attn = jnp.concatenate(rows, axis=0)
            o_intra.append(_dot(attn.astype(BF16), vbs[h]))

        for h, cols in enumerate(head_cols):
            op_ref[:, cols] = o_inter[h] + o_intra[h]
        return carry

    lax.fori_loop(0, tc // chunk, chunk_body, 0, unroll=min(2, tc // chunk))
    finish(pl.ds(tc - chunk, chunk))

    @pl.when(c_idx == pl.num_programs(2) - 1)
    def _():
        for h in range(heads_blk):
            sout_ref[0, h] = st_ref[h].T


def _cast_job(cast_src, layer, steps, step_of, axis):
    _, r, c = cast_src.shape
    if axis == 2:
        assert c % (steps * LANES) == 0
        blk = (r, c // steps)
        at = lambda *g: (0, step_of(*g))
    else:
        assert r % (steps * 2 * SUBLANES) == 0
        blk = (r // steps, c)
        at = lambda *g: (step_of(*g), 0)
    in_spec = pl.BlockSpec((None,) + blk, lambda *g: (layer,) + at(*g))
    return in_spec, pl.BlockSpec(blk, at), jax.ShapeDtypeStruct((r, c), BF16), 2 * blk[0] * blk[1] * 6


def _hgrn(p, onorm_g, s0, *, batch, seq, layer, cast_src=None):
    n = p.shape[0]
    heads = s0.shape[2]
    hb = HGRN_HEADS_PER_STEP
    wblk = hb * HEAD_DIM
    n_hb = heads // hb
    width = heads * HEAD_DIM
    tc = min(512, seq)
    chunk = min(HGRN_CHUNK, seq)
    n_t = seq // tc
    row = lambda b, h, c: b * n_t + c
    need = 2 * 4 * tc * wblk * 4 + 2 * tc * wblk * 2 + 3 * tc * wblk * 4 + 5 * hb * HEAD_DIM * HEAD_DIM * 4
    kernel = functools.partial(_hgrn_kernel, chunk=chunk, heads_blk=hb, with_cast=cast_src is not None)
    st_spec = pl.BlockSpec((None, 1, hb, HEAD_DIM, HEAD_DIM), lambda b, h, c: (layer, b, h, 0, 0))
    in_specs = [
        pl.BlockSpec((tc, wblk), lambda b, h, c: (row(b, h, c), h)),
        pl.BlockSpec((tc, wblk), lambda b, h, c: (row(b, h, c), n_hb + h)),
        pl.BlockSpec((tc, wblk), lambda b, h, c: (row(b, h, c), 2 * n_hb + h)),
        pl.BlockSpec((tc, wblk), lambda b, h, c: (row(b, h, c), 3 * n_hb + h)),
        pl.BlockSpec((None, 1, wblk), lambda b, h, c: (layer, 0, h)),
        st_spec,
    ]
    out_specs = [
        pl.BlockSpec((tc, wblk), lambda b, h, c: (row(b, h, c), h)),
        pl.BlockSpec((1, hb, HEAD_DIM, HEAD_DIM), lambda b, h, c: (b, h, 0, 0)),
    ]
    out_shape = [jax.ShapeDtypeStruct((n, width), BF16),
                 jax.ShapeDtypeStruct((batch, heads, HEAD_DIM, HEAD_DIM), F32)]
    args = [p, p, p, p, onorm_g, s0]
    if cast_src is not None:
        c_in, c_out, c_shape, c_bytes = _cast_job(cast_src, layer, batch * n_hb * n_t,
                                                  lambda b, h, c: (b * n_hb + h) * n_t + c, axis=2)
        in_specs.append(c_in)
        out_specs.append(c_out)
        out_shape.append(c_shape)
        args.append(cast_src)
        need += c_bytes
    return pl.pallas_call(
        kernel,
        out_shape=tuple(out_shape),
        grid=(batch, n_hb, n_t),
        in_specs=in_specs,
        out_specs=tuple(out_specs),
        scratch_shapes=[
            pltpu.VMEM((hb, HEAD_DIM, HEAD_DIM), F32),
            pltpu.VMEM((tc, wblk), F32),
            pltpu.VMEM((tc, wblk), F32),
            pltpu.VMEM((tc, wblk), F32),
            pltpu.VMEM((chunk, wblk), F32),
            pltpu.VMEM((chunk, wblk), F32),
        ],
        compiler_params=pltpu.CompilerParams(
            dimension_semantics=("parallel", "parallel", "arbitrary"),
            vmem_limit_bytes=_vmem_limit(need)),
        name="hgrn",
    )(*args)


def _s5_disc_lambda_kernel(lr_ref, li_ref, ls_ref, lbr_ref, lbi_ref, cr_ref, ci_ref):
    lr, li = lr_ref[...], li_ref[...]
    dt = jnp.exp(ls_ref[...])
    mag = jnp.exp(dt * lr)
    ang = dt * li
    lbr = mag * jnp.cos(ang)
    lbi = mag * jnp.sin(ang)
    nr, ni = lbr - 1.0, lbi
    den = lr * lr + li * li
    lbr_ref[...] = lbr
    lbi_ref[...] = lbi
    cr_ref[...] = (nr * lr + ni * li) / den
    ci_ref[...] = (ni * lr - nr * li) / den


def _s5_discretise(lam_re, lam_im, log_step):
    depth, groups, nst = lam_re.shape
    rows = depth * groups
    shp = jax.ShapeDtypeStruct((rows, nst), F32)
    outs = pl.pallas_call(
        _s5_disc_lambda_kernel, out_shape=(shp, shp, shp, shp), name="s5_disc_lambda",
    )(lam_re.reshape(rows, nst), lam_im.reshape(rows, nst), log_step.reshape(rows, 1))
    return tuple(o.reshape(depth, groups, nst) for o in outs)


def _s5_pair_weights(lbr, lbi, zoh_r, zoh_i, b_re, b_im, c_re, c_im, d_skip):
    depth, groups, nst, cg = b_re.shape
    hi = lax.Precision.HIGHEST
    zr, zi = zoh_r[..., None], zoh_i[..., None]
    bbr, bbi = zr * b_re - zi * b_im, zr * b_im + zi * b_re
    lr, li = lbr[..., None], lbi[..., None]
    blr, bli = bbr * lr - bbi * li, bbr * li + bbi * lr
    tb = jnp.swapaxes(jnp.stack([blr, bli, bbr, bbi], axis=1), 3, 4).reshape(depth, 4, groups * cg, nst)

    lrc, lic = lbr[:, :, None, :], lbi[:, :, None, :]
    pr, pi = c_re * lrc - c_im * lic, c_re * lic + c_im * lrc
    qr, qi = pr * lrc - pi * lic, pr * lic + pi * lrc
    tcc = jnp.stack([pr, -pi, qr, -qi], axis=1).transpose(0, 1, 3, 2, 4).reshape(depth, 4, cg, groups * nst)

    mm = lambda a, b: jnp.einsum("lgcn,lgnd->lgcd", a, b, precision=hi)
    g0 = mm(c_re, bbr) - mm(c_im, bbi)
    g1 = mm(pr, bbr) - mm(pi, bbi)
    g0d = g0 + d_skip.reshape(depth, groups, cg)[..., None] * jnp.eye(cg, dtype=F32)
    tg = jnp.stack([g0d, g1], axis=1).transpose(0, 1, 3, 2, 4).reshape(depth, 2, cg, groups * cg)
    lam2 = jnp.stack([lbr * lbr - lbi * lbi, 2.0 * lbr * lbi], axis=1).reshape(depth, 2, groups * nst)
    return tb, tcc, tg, lam2


def _s5_embed_weights(tb_ref, tcc_ref, tg_ref, wb_ref, wct_ref):
    nbk = wb_ref.shape[0]
    cg, nst = tcc_ref.shape[1], tb_ref.shape[2]
    bl = wb_ref.shape[2] // 2
    assert cg & (cg - 1) == 0 and nst & (nst - 1) == 0
    c_shift, n_shift = cg.bit_length() - 1, nst.bit_length() - 1
    iota = lambda shape, dim: lax.broadcasted_iota(jnp.int32, shape, dim)
    e_in = jnp.where((iota((nst, bl), 1) & (nst - 1)) == iota((nst, bl), 0), 1.0, 0.0).astype(BF16)
    e_out = jnp.where((iota((LANES, cg), 0) & (cg - 1)) == iota((LANES, cg), 1), 1.0, 0.0).astype(BF16)
    m_state = (iota((LANES, bl), 0) >> c_shift) == (iota((LANES, bl), 1) >> n_shift)
    m_direct = (iota((LANES, LANES), 0) >> c_shift) == (iota((LANES, LANES), 1) >> c_shift)
    zeros = jnp.zeros((LANES, LANES), BF16)
    for j in range(nbk):
        for k in range(4):
            half, part = divmod(k, 2)
            a = tb_ref[k, j * LANES:(j + 1) * LANES, :].astype(BF16)
            wb_ref[j, half * LANES:(half + 1) * LANES, part * bl:(part + 1) * bl] = (
                jnp.where(m_state, _dot(a, e_in), 0.0).astype(BF16))
        for k in range(4):
            eo, part = divmod(k, 2)
            x = tcc_ref[k, :, j * bl:(j + 1) * bl].astype(BF16)
            wct_ref[j, eo * LANES:(eo + 1) * LANES, part * bl:(part + 1) * bl] = (
                jnp.where(m_state, _dot(e_out, x), 0.0).astype(BF16))
        direct = [jnp.where(m_direct, _dot(e_out, tg_ref[k, :, j * LANES:(j + 1) * LANES].astype(BF16)), 0.0
                            ).astype(BF16) for k in range(2)]
        base = 2 * bl
        wct_ref[j, 0:LANES, base:base + LANES] = direct[0]
        wct_ref[j, 0:LANES, base + LANES:base + 2 * LANES] = zeros
        wct_ref[j, LANES:2 * LANES, base:base + LANES] = direct[1]
        wct_ref[j, LANES:2 * LANES, base + LANES:base + 2 * LANES] = direct[0]


def _s5_kernel(u_ref, tb_ref, tcc_ref, tg_ref, lam2_ref, wg_ref, bg_ref, x0r_ref, x0i_ref, *refs,
               batch, with_cast):
    refs, cast_src, cast_dst = _split_cast_refs(refs, 3, with_cast)
    o_ref, xr_ref, xi_ref, xs_ref, y_ref, st_ref, ue_ref, uo_ref, tmp_ref, wb_ref, wct_ref = refs
    if with_cast:
        cast_dst[...] = cast_src[...].astype(BF16)
    c_idx = pl.program_id(0)
    tt = u_ref.shape[0]
    tp = tt // 2
    prows = tp * batch
    width = u_ref.shape[1] // batch
    n_slab = width // LANES
    nbk = wb_ref.shape[0]
    bl = wb_ref.shape[2] // 2
    ns = nbk * bl
    grp = st_ref.shape[1]
    per = grp // batch

    @pl.when(c_idx == 0)
    def _():
        st_ref[0] = jnp.concatenate([x0r_ref[...]] * per, axis=0)
        st_ref[1] = jnp.concatenate([x0i_ref[...]] * per, axis=0)
        _s5_embed_weights(tb_ref, tcc_ref, tg_ref, wb_ref, wct_ref)

    for b in range(batch):
        for m in range(n_slab):
            tmp_ref[...] = u_ref[:, b * width + m * LANES:b * width + (m + 1) * LANES]
            ue_ref[m, pl.ds(b, tp, stride=batch), :] = tmp_ref[pl.ds(0, tp, stride=2), :]
            uo_ref[m, pl.ds(b, tp, stride=batch), :] = tmp_ref[pl.ds(1, tp, stride=2), :]

    for j in range(nbk):
        lhs = jnp.concatenate([ue_ref[j], uo_ref[j]], axis=1).astype(BF16)
        w = _dot(lhs, wb_ref[j])
        xs_ref[:, j * bl:(j + 1) * bl] = w[:, :bl]
        xs_ref[:, ns + j * bl:ns + (j + 1) * bl] = w[:, bl:]

    second = lax.broadcasted_iota(jnp.int32, (grp, S5_SCAN_LANES), 0) >= batch
    for cb in range(ns // S5_SCAN_LANES):
        lo = cb * S5_SCAN_LANES
        re_l = slice(lo, lo + S5_SCAN_LANES)
        im_l = slice(ns + lo, ns + lo + S5_SCAN_LANES)
        ar = lam2_ref[0:1, re_l]
        ai = lam2_ref[1:2, re_l]

        def step(g, carry, ar=ar, ai=ai, re_l=re_l, im_l=im_l):
            cr, ci = carry
            r = pl.ds(pl.multiple_of(g * grp, grp), grp)
            wr = xs_ref[r, re_l]
            wi = xs_ref[r, im_l]
            if per == 2:
                tr = ar * cr - ai * ci + pltpu.roll(wr, batch, axis=0)
                ti = ar * ci + ai * cr + pltpu.roll(wi, batch, axis=0)
                xs_ref[r, re_l] = jnp.where(second, tr, cr)
                xs_ref[r, im_l] = jnp.where(second, ti, ci)
                vr = ar * tr - ai * ti + wr
                vi = ar * ti + ai * tr + wi
                nr = jnp.where(second, vr, pltpu.roll(vr, batch, axis=0))
                ni = jnp.where(second, vi, pltpu.roll(vi, batch, axis=0))
            else:
                xs_ref[r, re_l] = cr
                xs_ref[r, im_l] = ci
                nr = ar * cr - ai * ci + wr
                ni = ar * ci + ai * cr + wi
            return nr, ni

        cr, ci = lax.fori_loop(0, prows // grp, step, (st_ref[0, :, re_l], st_ref[1, :, re_l]), unroll=2)
        st_ref[0, :, re_l] = cr
        st_ref[1, :, re_l] = ci

    for j in range(nbk):
        lhs = jnp.concatenate([xs_ref[:, j * bl:(j + 1) * bl], xs_ref[:, ns + j * bl:ns + (j + 1) * bl],
                               ue_ref[j], uo_ref[j]], axis=1).astype(BF16)
        yj = _dot_nt(lhs, wct_ref[j])
        y_ref[0:prows, j * LANES:(j + 1) * LANES] = yj[:, :LANES]
        y_ref[prows:2 * prows, j * LANES:(j + 1) * LANES] = yj[:, LANES:]

    hh = jax.nn.gelu(y_ref[...])
    gate = _sigmoid(_dot(hh.astype(BF16), wg_ref[...]) + bg_ref[...])
    out = hh * gate
    for m in range(n_slab):
        ue_ref[m] = out[0:prows, m * LANES:(m + 1) * LANES]
        uo_ref[m] = out[prows:2 * prows, m * LANES:(m + 1) * LANES]
    for b in range(batch):
        for m in range(n_slab):
            tmp_ref[pl.ds(0, tp, stride=2), :] = ue_ref[m, pl.ds(b, tp, stride=batch), :]
            tmp_ref[pl.ds(1, tp, stride=2), :] = uo_ref[m, pl.ds(b, tp, stride=batch), :]
            o_ref[:, b * width + m * LANES:b * width + (m + 1) * LANES] = tmp_ref[...].astype(BF16)

    @pl.when(c_idx == pl.num_programs(0) - 1)
    def _():
        xr_ref[...] = st_ref[0, grp - batch:grp, :]
        xi_ref[...] = st_ref[1, grp - batch:grp, :]


def _s5(u, tb, tcc, tg, lam2, w_glu, b_glu, x0r, x0i, *, batch, layer, cast_src=None):
    seq = u.shape[0]
    width = u.shape[1] // batch
    n = seq * batch
    ns = lam2.shape[2]
    nbk = width // LANES
    bl = ns // nbk
    assert batch % SUBLANES == 0 or 2 * batch == SUBLANES
    assert seq % 2 == 0 and tb.shape[2] == width
    grp = max(batch, SUBLANES)
    rows = min(S5_ROWS, n)
    tt = rows // batch
    prows = rows // 2
    lay4 = lambda c: (layer, 0, 0, 0)
    lay3 = lambda c: (layer, 0, 0)
    st_in = pl.BlockSpec((None, batch, ns), lay3)
    resident = pl.Buffered(1)
    wb_shape = (nbk, 2 * LANES, 2 * bl)
    wct_shape = (nbk, 2 * LANES, 2 * bl + 2 * LANES)
    need = (2 * rows * width * 4 + (nbk * 2 * LANES * (4 * bl + 2 * LANES) + width * width) * 2
            + 4 * width * LANES * 4 + 6 * 16 * ns * 4
            + 2 * rows * width * 2 + prows * 2 * ns * 4 + rows * width * 4 + 4 * rows * width * 4
            + 4 * prows * (4 * bl + 2 * LANES) + (8 * batch + 2 * grp) * ns * 4)
    st_shape = jax.ShapeDtypeStruct((batch, ns), F32)
    in_specs = [
        pl.BlockSpec((tt, batch * width), lambda c: (c, 0)),
        pl.BlockSpec((None,) + tb.shape[1:], lay4, pipeline_mode=resident),
        pl.BlockSpec((None,) + tcc.shape[1:], lay4, pipeline_mode=resident),
        pl.BlockSpec((None,) + tg.shape[1:], lay4, pipeline_mode=resident),
        pl.BlockSpec((None, 2, ns), lay3),
        pl.BlockSpec((None, width, width), lay3, pipeline_mode=resident),
        pl.BlockSpec((None, 1, width), lay3),
        st_in,
        st_in,
    ]
    out_specs = [
        pl.BlockSpec((tt, batch * width), lambda c: (c, 0)),
        pl.BlockSpec((batch, ns), lambda c: (0, 0)),
        pl.BlockSpec((batch, ns), lambda c: (0, 0)),
    ]
    out_shape = [jax.ShapeDtypeStruct((seq, batch * width), BF16), st_shape, st_shape]
    args = [u, tb, tcc, tg, lam2, w_glu, b_glu, x0r, x0i]
    if cast_src is not None:
        c_in, c_out, c_shape, c_bytes = _cast_job(cast_src, layer, n // rows, lambda c: c, axis=1)
        in_specs.append(c_in)
        out_specs.append(c_out)
        out_shape.append(c_shape)
        args.append(cast_src)
        need += c_bytes
    return pl.pallas_call(
        functools.partial(_s5_kernel, batch=batch, with_cast=cast_src is not None),
        out_shape=tuple(out_shape),
        grid=(n // rows,),
        in_specs=in_specs,
        out_specs=tuple(out_specs),
        scratch_shapes=[
            pltpu.VMEM((prows, 2 * ns), F32),
            pltpu.VMEM((rows, width), F32),
            pltpu.VMEM((2, grp, ns), F32),
            pltpu.VMEM((width // LANES, prows, LANES), F32),
            pltpu.VMEM((width // LANES, prows, LANES), F32),
            pltpu.VMEM((tt, LANES), F32),
            pltpu.VMEM(wb_shape, BF16),
            pltpu.VMEM(wct_shape, BF16),
        ],
        compiler_params=pltpu.CompilerParams(
            dimension_semantics=("arbitrary",),
            vmem_limit_bytes=_vmem_limit(need)),
        name="s5",
    )(*args)


def _outproj_kernel(x_ref, oh_ref, os_ref, wh_ref, ws_ref, o_ref):
    o_ref[...] = x_ref[...] + _dot(oh_ref[...], wh_ref[...]) + _dot(os_ref[...], ws_ref[...])


def _outproj(x, o_h, o_s, w_out, *, layer, tm, tn, seq):
    n, d = x.shape
    kh = o_h.shape[1]
    ks = w_out.shape[1] - kh
    assert kh == ks
    if o_s.shape[0] == n:
        os_map = lambda i, j: (i, 0)
    else:
        n_t = seq // tm
        os_map = lambda i, j: (i % n_t, i // n_t)
    need = 2 * (2 * tm * tn * 4 + tm * (kh + ks) * 2 + (kh + ks) * tn * 2) + 2 * tm * tn * 4
    return pl.pallas_call(
        _outproj_kernel,
        out_shape=jax.ShapeDtypeStruct((n, d), F32),
        grid=(n // tm, d // tn),
        in_specs=[
            pl.BlockSpec((tm, tn), lambda i, j: (i, j)),
            pl.BlockSpec((tm, kh), lambda i, j: (i, 0)),
            pl.BlockSpec((tm, ks), os_map),
            pl.BlockSpec((None, kh, tn), lambda i, j: (layer, 0, j)),
            pl.BlockSpec((None, ks, tn), lambda i, j: (layer, 1, j)),
        ],
        out_specs=pl.BlockSpec((tm, tn), lambda i, j: (i, j)),
        compiler_params=pltpu.CompilerParams(
            dimension_semantics=("parallel", "parallel"),
            vmem_limit_bytes=_vmem_limit(need)),
        name="outproj",
    )(x, o_h, o_s, w_out, w_out)


def _ffn_kernel(x_ref, g_ref, w1_ref, w2_ref, gf_ref, o_ref, h_ref, *, final_norm):
    f = pl.program_id(1)
    tm = x_ref.shape[0]

    @pl.when(f == 0)
    def _():
        g = g_ref[...]

        def body(i, carry):
            r = pl.ds(pl.multiple_of(i * NORM_ROWS, NORM_ROWS), NORM_ROWS)
            h_ref[r, :] = _rmsnorm(x_ref[r, :], g).astype(BF16)
            o_ref[r, :] = jnp.zeros((NORM_ROWS, o_ref.shape[1]), F32)
            return carry

        lax.fori_loop(0, tm // NORM_ROWS, body, 0, unroll=NORM_UNROLL)

    a = _dot(h_ref[...], w1_ref[...])
    a = jnp.square(jnp.maximum(a, 0.0)).astype(BF16)
    cw = o_ref.shape[1] // FFN_ACC_CHUNKS
    for c in range(FFN_ACC_CHUNKS):
        o_ref[:, c * cw:(c + 1) * cw] += _dot(a, w2_ref[:, c * cw:(c + 1) * cw])

    @pl.when(f == pl.num_programs(1) - 1)
    def _():
        gf = gf_ref[...]

        def body(i, carry):
            r = pl.ds(pl.multiple_of(i * NORM_ROWS, NORM_ROWS), NORM_ROWS)
            y = x_ref[r, :] + o_ref[r, :]
            if final_norm:
                y = _rmsnorm(y, gf)
            o_ref[r, :] = y
            return carry

        lax.fori_loop(0, tm // NORM_ROWS, body, 0, unroll=NORM_UNROLL)


def _ffn(x, g, w1, w2, gf, *, layer, tm, tf, final_norm):
    n, d = x.shape
    ff = w1.shape[1]
    need = tm * d * 4 + tm * d * 2 + 2 * tm * d * 4 + 4 * d * tf * 2 + tm * tf * 6 + tm * d * 4
    kernel = functools.partial(_ffn_kernel, final_norm=final_norm)
    return pl.pallas_call(
        kernel,
        out_shape=jax.ShapeDtypeStruct((n, d), F32),
        grid=(n // tm, ff // tf),
        in_specs=[
            pl.BlockSpec((tm, d), lambda i, f: (i, 0), pipeline_mode=pl.Buffered(1)),
            pl.BlockSpec((None, 1, d), lambda i, f: (layer, 0, 0)),
            pl.BlockSpec((d, tf), lambda i, f: (0, f)),
            pl.BlockSpec((tf, d), lambda i, f: (f, 0)),
            pl.BlockSpec((1, d), lambda i, f: (0, 0)),
        ],
        out_specs=pl.BlockSpec((tm, d), lambda i, f: (i, 0)),
        scratch_shapes=[pltpu.VMEM((tm, d), BF16)],
        compiler_params=pltpu.CompilerParams(
            dimension_semantics=("parallel", "arbitrary"),
            vmem_limit_bytes=_vmem_limit(need)),
        name="ffn",
    )(x, g, w1, w2, gf)


def _trunk(x, st_h, st_r, st_i, wts, ffn_w=None):
    batch, seq, d = x.shape
    n = batch * seq
    depth = wts["w_in"].shape[0]
    heads = st_h.shape[2]
    hgrn_width = heads * HEAD_DIM
    groups, nst = st_r.shape[2], st_r.shape[3]
    s5_width = wts["w_glu"].shape[1]
    tm = min(1024, n)
    xf = x.reshape(n, d)
    x0r = st_r.reshape(depth, batch, groups * nst)
    x0i = st_i.reshape(depth, batch, groups * nst)
    time_major = seq % tm == 0
    new_h, new_r, new_i, used_w = [], [], [], []
    for l in range(depth):
        cast = ffn_w is None
        p, u = _inproj(xf, wts["norm1_g"], wts["w_in"], wts["lb_logits"],
                       layer=l, tm=tm, tn=s5_width, batch=batch, seq=seq)
        o_h, s_new, *w1 = _hgrn(p, wts["onorm_g"], st_h, batch=batch, seq=seq, layer=l,
                                cast_src=wts["w_ff1"] if cast else None)
        if not time_major:
            u = u.reshape(batch, seq, s5_width).transpose(1, 0, 2).reshape(seq, batch * s5_width)
        o_s, xr, xi, *w2 = _s5(u, wts["s5_tb"], wts["s5_tcc"], wts["s5_tg"], wts["s5_lam2"], wts["w_glu"],
                               wts["b_glu"], x0r, x0i, batch=batch, layer=l,
                               cast_src=wts["w_ff2"] if cast else None)
        if not time_major:
            o_s = o_s.reshape(seq, batch, s5_width).transpose(1, 0, 2).reshape(n, s5_width)
        used_w.append((w1[0], w2[0]) if cast else ffn_w[l])
        x1 = _outproj(xf, o_h, o_s, wts["w_out"], layer=l, tm=min(512, n), tn=d, seq=seq)
        xf = _ffn(x1, wts["norm2_g"], *used_w[l], wts["final_g"],
                  layer=l, tm=tm, tf=1024, final_norm=(l == depth - 1))
        new_h.append(s_new)
        new_r.append(xr.reshape(batch, groups, nst))
        new_i.append(xi.reshape(batch, groups, nst))
    return (xf.reshape(batch, seq, d), jnp.stack(new_h), jnp.stack(new_r), jnp.stack(new_i)), used_w


def kernel(x_prompt, x_sample, state_hgrn, state_s5_re, state_s5_im, norm1_g, w_in, hgrn_lb_logits,
           hgrn_onorm_g, s5_lambda_re, s5_lambda_im, s5_log_step, s5_B_re, s5_B_im, s5_C_re, s5_C_im,
           s5_D, s5_w_glu, s5_b_glu, w_out, norm2_g, w_ff1, w_ff2, final_norm_g):
    depth, d = norm1_g.shape
    row3 = lambda a: a.reshape(depth, 1, a.shape[-1])
    lbr, lbi, zoh_r, zoh_i = _s5_discretise(s5_lambda_re, s5_lambda_im, s5_log_step)
    s5_tb, s5_tcc, s5_tg, s5_lam2 = _s5_pair_weights(lbr, lbi, zoh_r, zoh_i, s5_B_re, s5_B_im,
                                                     s5_C_re, s5_C_im, s5_D)
    wts = {
        "norm1_g": row3(norm1_g), "norm2_g": row3(norm2_g), "final_g": final_norm_g.reshape(1, d),
        "lb_logits": hgrn_lb_logits, "onorm_g": row3(hgrn_onorm_g),
        "w_in": w_in.astype(BF16), "w_out": w_out.astype(BF16),
        "w_ff1": w_ff1, "w_ff2": w_ff2,
        "w_glu": s5_w_glu.astype(BF16), "b_glu": row3(s5_b_glu),
        "s5_tb": s5_tb, "s5_tcc": s5_tcc, "s5_tg": s5_tg, "s5_lam2": s5_lam2,
    }
    bp = x_prompt.shape[0]
    zh = jnp.zeros((depth, bp) + state_hgrn.shape[2:], F32)
    zs = jnp.zeros((depth, bp) + state_s5_re.shape[2:], F32)
    (y_p, hp, rp, ip), ffn_w = _trunk(x_prompt, zh, zs, zs, wts)
    (y_s, hs, rs, is_), _ = _trunk(x_sample, state_hgrn, state_s5_re, state_s5_im, wts, ffn_w)
    return (y_p, y_s, hp, rp, ip, hs, rs, is_)
```

```python
import functools

import jax
import jax.numpy as jnp
from jax import lax
from jax.experimental import pallas as pl
from jax.experimental.pallas import tpu as pltpu

F32 = jnp.float32
BF16 = jnp.bfloat16
EPS = 1e-6

HEAD_DIM = 128
HGRN_CHUNK = 64
HGRN_SUB = 16
HGRN_HEADS_PER_STEP = 4
SUBLANES = 8
LANES = 128
S5_SCAN_LANES = 512
S5_ROWS = 512

V7X_VMEM_CAP = 56 * 1024 * 1024
NORM_ROWS = 64
NORM_UNROLL = 4
FFN_ACC_CHUNKS = 4
INPROJ_CHUNKS = 4


def _vmem_limit(nbytes):
    return int(min(V7X_VMEM_CAP, nbytes * 5 // 4 + (4 << 20)))


def _rmsnorm(x, g):
    return x * lax.rsqrt(jnp.mean(jnp.square(x), axis=-1, keepdims=True) + EPS) * g


def _sigmoid(x):
    return 1.0 / (1.0 + jnp.exp(-x))


def _log1p_exp_neg_abs(x):
    return jnp.log(1.0 + jnp.exp(-jnp.abs(x)))


def _softplus(x):
    return jnp.maximum(x, 0.0) + _log1p_exp_neg_abs(x)


def _dot(a, b):
    return jnp.dot(a, b, preferred_element_type=F32)


def _dot_nt(a, b):
    return lax.dot_general(a, b, (((1,), (1,)), ((), ())), preferred_element_type=F32)


def _inproj_kernel(x_ref, g_ref, w_ref, lbl_ref, p_ref, u_ref, h_ref, *, layer):
    tm = x_ref.shape[0]
    cw = w_ref.shape[1] // INPROJ_CHUNKS
    j = pl.program_id(1)

    @pl.when(j == 0)
    def _():
        g = g_ref[...]

        def body(i, carry):
            r = pl.ds(pl.multiple_of(i * NORM_ROWS, NORM_ROWS), NORM_ROWS)
            h_ref[r, :] = _rmsnorm(x_ref[r, :], g).astype(BF16)
            return carry

        lax.fori_loop(0, tm // NORM_ROWS, body, 0, unroll=NORM_UNROLL)

    def emit(out_ref, fn):
        for c in range(INPROJ_CHUNKS):
            sl = slice(c * cw, (c + 1) * cw)
            out_ref[:, sl] = fn(_dot(h_ref[...], w_ref[:, sl]), sl)

    def silu(a, sl):
        return a * _sigmoid(a)

    def log_forget(z, sl):
        logits = lbl_ref[:, sl]
        e = jnp.exp(logits - jnp.max(logits, axis=0, keepdims=True))
        sm = e / jnp.sum(e, axis=0, keepdims=True)
        cum0 = sm[0:1]
        cuml = cum0
        for k in range(1, layer + 1):
            cuml = cuml + sm[k:k + 1]
        lb = cuml - cum0
        log_lb = jnp.log(lb)
        b = jnp.log1p(-lb) - _softplus(-z)
        delta = log_lb - b
        return jnp.where(jnp.isnan(delta), log_lb + b,
                         jnp.maximum(log_lb, b) + _log1p_exp_neg_abs(delta))

    for tile, (out_ref, fn) in enumerate([(p_ref, silu), (p_ref, log_forget), (p_ref, lambda a, sl: a),
                                          (p_ref, silu), (u_ref, lambda a, sl: a)]):
        pl.when(j == tile)(functools.partial(emit, out_ref, fn))


def _inproj(x, g, w, lb_logits, *, layer, tm, tn, batch, seq):
    n, d = x.shape
    cols = w.shape[2]
    n_j = cols // tn
    depth = lb_logits.shape[0]
    assert n_j == 5 and lb_logits.shape[1] == tn
    time_major = seq % tm == 0
    if time_major:
        n_t = seq // tm
        u_shape, u_map = (seq, batch * tn), (lambda i, j: (i % n_t, i // n_t))
    else:
        u_shape, u_map = (n, tn), (lambda i, j: (i, 0))
    need = 2 * tm * d * 4 + tm * d * 2 + 2 * d * tn * 2 + 5 * tm * tn * 4
    return pl.pallas_call(
        functools.partial(_inproj_kernel, layer=layer),
        out_shape=(jax.ShapeDtypeStruct((n, cols - tn), F32), jax.ShapeDtypeStruct(u_shape, F32)),
        grid=(n // tm, n_j),
        in_specs=[
            pl.BlockSpec((tm, d), lambda i, j: (i, 0)),
            pl.BlockSpec((None, 1, d), lambda i, j: (layer, 0, 0)),
            pl.BlockSpec((None, d, tn), lambda i, j: (layer, 0, j)),
            pl.BlockSpec((depth, tn), lambda i, j: (0, 0)),
        ],
        out_specs=(pl.BlockSpec((tm, tn), lambda i, j: (i, jnp.minimum(j, n_j - 2))),
                   pl.BlockSpec((tm, tn), u_map)),
        scratch_shapes=[pltpu.VMEM((tm, d), BF16)],
        compiler_params=pltpu.CompilerParams(
            dimension_semantics=("parallel", "arbitrary"),
            vmem_limit_bytes=_vmem_limit(need)),
        name="inproj",
    )(x, g, w, lb_logits)


def _hgrn_decayed_queries(q_lo, q_hi, f_row, row_is):
    slabs = []
    qd_lo = qd_hi = None
    for s in range(2 * SUBLANES - 1, -1, -1):
        if s == 2 * SUBLANES - 1:
            qd_hi = jnp.where(row_is[SUBLANES - 1], q_hi, 0.0)
        elif s >= SUBLANES:
            qd_hi = jnp.where(row_is[s - SUBLANES], q_hi, qd_hi * f_row(s + 1))
        else:
            fn = f_row(s + 1)
            qd_hi = qd_hi * fn
            qd_lo = jnp.where(row_is[s], q_lo, 0.0 if s == SUBLANES - 1 else qd_lo * fn)
        slabs.append((s, 1, qd_hi))
        if s < SUBLANES:
            slabs.append((s, 0, qd_lo))
    return slabs


def _split_cast_refs(refs, n_out, with_cast):
    if not with_cast:
        return refs, None, None
    return refs[1:1 + n_out] + refs[2 + n_out:], refs[0], refs[1 + n_out]


def _hgrn_kernel(q_ref, f_ref, i_ref, g_ref, on_ref, s0_ref, acc_ref, *refs, chunk, heads_blk, with_cast):
    del acc_ref
    refs, cast_src, cast_dst = _split_cast_refs(refs, 2, with_cast)
    o_ref, sout_ref, st_ref, ks_ref, cs_ref, fs_ref, fc_ref, op_ref = refs
    if with_cast:
        cast_dst[...] = cast_src[...].astype(BF16)
    c_idx = pl.program_id(2)
    tc = q_ref.shape[0]
    n_sub = chunk // HGRN_SUB

    @pl.when(c_idx == 0)
    def _():
        for h in range(heads_blk):
            st_ref[h] = s0_ref[0, h].T

    logf = f_ref[...]
    fgate = jnp.exp(logf)
    fs_ref[...] = fgate
    ks_ref[...] = 1.0 - fgate

    t_i = lax.broadcasted_iota(jnp.int32, (chunk, chunk), 0)
    s_i = lax.broadcasted_iota(jnp.int32, (chunk, chunk), 1)
    tri = jnp.where(t_i >= s_i, 1.0, 0.0).astype(BF16)
    for j in range(tc // chunk):
        lf = logf[j * chunk:(j + 1) * chunk]
        hi = lf.astype(BF16)
        r1 = lf - hi.astype(F32)
        mid = r1.astype(BF16)
        lo = (r1 - mid.astype(F32)).astype(BF16)
        cs_ref[j * chunk:(j + 1) * chunk, :] = _dot(tri, hi) + _dot(tri, mid) + _dot(tri, lo)

    row8 = lax.broadcasted_iota(jnp.int32, (SUBLANES, HEAD_DIM), 0)
    row_is = [row8 == j for j in range(SUBLANES)]
    lane8 = lax.broadcasted_iota(jnp.int32, (SUBLANES, chunk), 1)
    lane_is = [lane8 == j for j in range(chunk)]
    onorm = on_ref[...]
    blk = lambda a, j: a[j * HGRN_SUB:(j + 1) * HGRN_SUB]

    head_cols = [slice(h * HEAD_DIM, (h + 1) * HEAD_DIM) for h in range(heads_blk)]

    def finish(rows):
        for cols in head_cols:
            o = op_ref[:, cols]
            o = o * lax.rsqrt(jnp.mean(jnp.square(o), axis=-1, keepdims=True) + EPS)
            o = o * onorm[:, cols]
            o_ref[rows, cols] = (o * g_ref[rows, cols]).astype(BF16)

    op_ref[...] = jnp.zeros(op_ref.shape, F32)

    def chunk_body(c, carry):
        base = pl.multiple_of(c * chunk, chunk)
        r = pl.ds(base, chunk)
        finish(pl.ds(pl.multiple_of(jnp.maximum(c - 1, 0) * chunk, chunk), chunk))
        fc_ref[...] = fs_ref[r, :]
        cum_all = cs_ref[r, :]

        zero_blk = jnp.zeros((HGRN_SUB, HEAD_DIM), F32)
        qs, vbs, kbs, o_inter, a_off = [], [], [], [], []
        for cols in head_cols:
            h = len(qs)
            cum = cum_all[:, cols]
            q = q_ref[r, cols]
            k = ks_ref[r, cols]
            v = i_ref[r, cols]
            ends = [cum[(j + 1) * HGRN_SUB - 1:(j + 1) * HGRN_SUB, :] for j in range(n_sub)]
            cl = ends[-1]
            ke = [blk(k, j) * jnp.exp(ends[j] - blk(cum, j)) for j in range(n_sub)]
            qe = [blk(q, j) * jnp.exp(blk(cum, j) - ends[j - 1] if j else blk(cum, j)) for j in range(n_sub)]

            st = st_ref[h]
            q_in = jnp.concatenate([qe[j] * jnp.exp(ends[j - 1]) if j else qe[j] for j in range(n_sub)], axis=0)
            o_inter.append(_dot_nt(q_in.astype(BF16), st.astype(BF16)))
            kdec = jnp.concatenate(
                [ke[j] * jnp.exp(cl - ends[j]) if j < n_sub - 1 else ke[j] for j in range(n_sub)], axis=0)
            st_ref[h] = st * jnp.exp(cl) + _dot(v.T.astype(BF16), kdec.astype(BF16))

            offs = [None]
            for i in range(1, n_sub):
                kt = jnp.concatenate(
                    [ke[j] * jnp.exp(ends[i - 1] - ends[j]) if j < i - 1 else ke[j] for j in range(i)]
                    + [zero_blk] * (n_sub - i), axis=0)
                offs.append(_dot_nt(qe[i].astype(BF16), kt.astype(BF16)))
            a_off.append(offs)
            qs.append(q)
            vbs.append(v.astype(BF16))
            kbs.append(k.astype(BF16))

        res = []
        for h, cols in enumerate(head_cols):
            per_head = []
            for i in range(n_sub):
                lo_r = i * HGRN_SUB
                f_row = lambda s, lo_r=lo_r, cols=cols: fc_ref[lo_r + s:lo_r + s + 1, cols]
                q_blk = blk(qs[h], i)
                slabs = _hgrn_decayed_queries(q_blk[:SUBLANES], q_blk[SUBLANES:], f_row, row_is)
                stack = jnp.concatenate([slab for _, _, slab in slabs], axis=0)
                per_head.append(([(s, half) for s, half, _ in slabs],
                                 _dot_nt(stack.astype(BF16), kbs[h])))
            res.append(per_head)

        o_intra = []
        for h in range(heads_blk):
            rows = []
            for i in range(n_sub):
                order, prod = res[h][i]
                halves = [jnp.zeros((SUBLANES, chunk), F32), jnp.zeros((SUBLANES, chunk), F32)]
                for idx, (s, half) in enumerate(order):
                    piece = prod[idx * SUBLANES:(idx + 1) * SUBLANES]
                    halves[half] = jnp.where(lane_is[i * HGRN_SUB + s], piece, halves[half])
                a = jnp.concatenate(halves, axis=0)
                rows.append(a + a_off[h][i] if i else a)
            attn = jnp.concatenate(rows, axis=0)
            o_intra.append(_dot(attn.astype(BF16), vbs[h]))

        for h, cols in enumerate(head_cols):
            op_ref[:, cols] = o_inter[h] + o_intra[h]
        return carry

    lax.fori_loop(0, tc // chunk, chunk_body, 0, unroll=min(2, tc // chunk))
    finish(pl.ds(tc - chunk, chunk))

    @pl.when(c_idx == pl.num_programs(2) - 1)
    def _():
        for h in range(heads_blk):
            sout_ref[0, h] = st_ref[h].T


def _cast_job(cast_src, layer, steps, step_of, axis):
    _, r, c = cast_src.shape
    if axis == 2:
        assert c % (steps * LANES) == 0
        blk = (r, c // steps)
        at = lambda *g: (0, step_of(*g))
    else:
        assert r % (steps * 2 * SUBLANES) == 0
        blk = (r // steps, c)
        at = lambda *g: (step_of(*g), 0)
    in_spec = pl.BlockSpec((None,) + blk, lambda *g: (layer,) + at(*g))
    return in_spec, pl.BlockSpec(blk, at), jax.ShapeDtypeStruct((r, c), BF16), 2 * blk[0] * blk[1] * 6


def _hgrn(p, onorm_g, s0, s_acc, *, batch, seq, layer, cast_src=None):
    n = p.shape[0]
    heads = s0.shape[2]
    hb = HGRN_HEADS_PER_STEP
    wblk = hb * HEAD_DIM
    n_hb = heads // hb
    width = heads * HEAD_DIM
    tc = min(512, seq)
    chunk = min(HGRN_CHUNK, seq)
    n_t = seq // tc
    row = lambda b, h, c: b * n_t + c
    need = 2 * 4 * tc * wblk * 4 + 2 * tc * wblk * 2 + 3 * tc * wblk * 4 + 5 * hb * HEAD_DIM * HEAD_DIM * 4
    kernel = functools.partial(_hgrn_kernel, chunk=chunk, heads_blk=hb, with_cast=cast_src is not None)
    st_spec = pl.BlockSpec((None, 1, hb, HEAD_DIM, HEAD_DIM), lambda b, h, c: (layer, b, h, 0, 0))
    in_specs = [
        pl.BlockSpec((tc, wblk), lambda b, h, c: (row(b, h, c), h)),
        pl.BlockSpec((tc, wblk), lambda b, h, c: (row(b, h, c), n_hb + h)),
        pl.BlockSpec((tc, wblk), lambda b, h, c: (row(b, h, c), 2 * n_hb + h)),
        pl.BlockSpec((tc, wblk), lambda b, h, c: (row(b, h, c), 3 * n_hb + h)),
        pl.BlockSpec((None, 1, wblk), lambda b, h, c: (layer, 0, h)),
        st_spec,
        pl.BlockSpec(memory_space=pl.ANY),
    ]
    out_specs = [pl.BlockSpec((tc, wblk), lambda b, h, c: (row(b, h, c), h)), st_spec]
    out_shape = [jax.ShapeDtypeStruct((n, width), BF16), jax.ShapeDtypeStruct(s_acc.shape, F32)]
    args = [p, p, p, p, onorm_g, s0, s_acc]
    if cast_src is not None:
        c_in, c_out, c_shape, c_bytes = _cast_job(cast_src, layer, batch * n_hb * n_t,
                                                  lambda b, h, c: (b * n_hb + h) * n_t + c, axis=2)
        in_specs.append(c_in)
        out_specs.append(c_out)
        out_shape.append(c_shape)
        args.append(cast_src)
        need += c_bytes
    return pl.pallas_call(
        kernel,
        out_shape=tuple(out_shape),
        grid=(batch, n_hb, n_t),
        in_specs=in_specs,
        out_specs=tuple(out_specs),
        input_output_aliases={6: 1},
        scratch_shapes=[
            pltpu.VMEM((hb, HEAD_DIM, HEAD_DIM), F32),
            pltpu.VMEM((tc, wblk), F32),
            pltpu.VMEM((tc, wblk), F32),
            pltpu.VMEM((tc, wblk), F32),
            pltpu.VMEM((chunk, wblk), F32),
            pltpu.VMEM((chunk, wblk), F32),
        ],
        compiler_params=pltpu.CompilerParams(
            dimension_semantics=("parallel", "parallel", "arbitrary"),
            vmem_limit_bytes=_vmem_limit(need)),
        name="hgrn",
    )(*args)


def _s5_disc_lambda_kernel(lr_ref, li_ref, ls_ref, lbr_ref, lbi_ref, cr_ref, ci_ref):
    lr, li = lr_ref[...], li_ref[...]
    dt = jnp.exp(ls_ref[...])
    mag = jnp.exp(dt * lr)
    ang = dt * li
    lbr = mag * jnp.cos(ang)
    lbi = mag * jnp.sin(ang)
    nr, ni = lbr - 1.0, lbi
    den = lr * lr + li * li
    lbr_ref[...] = lbr
    lbi_ref[...] = lbi
    cr_ref[...] = (nr * lr + ni * li) / den
    ci_ref[...] = (ni * lr - nr * li) / den


def _s5_discretise(lam_re, lam_im, log_step):
    depth, groups, nst = lam_re.shape
    rows = depth * groups
    shp = jax.ShapeDtypeStruct((rows, nst), F32)
    outs = pl.pallas_call(
        _s5_disc_lambda_kernel, out_shape=(shp, shp, shp, shp), name="s5_disc_lambda",
    )(lam_re.reshape(rows, nst), lam_im.reshape(rows, nst), log_step.reshape(rows, 1))
    return tuple(o.reshape(depth, groups, nst) for o in outs)


def _s5_pair_weights(lbr, lbi, zoh_r, zoh_i, b_re, b_im, c_re, c_im, d_skip):
    depth, groups, nst, cg = b_re.shape
    hi = lax.Precision.HIGHEST
    zr, zi = zoh_r[..., None], zoh_i[..., None]
    bbr, bbi = zr * b_re - zi * b_im, zr * b_im + zi * b_re
    lr, li = lbr[..., None], lbi[..., None]
    blr, bli = bbr * lr - bbi * li, bbr * li + bbi * lr
    tb = jnp.swapaxes(jnp.stack([blr, bli, bbr, bbi], axis=1), 3, 4).reshape(depth, 4, groups * cg, nst)

    lrc, lic = lbr[:, :, None, :], lbi[:, :, None, :]
    pr, pi = c_re * lrc - c_im * lic, c_re * lic + c_im * lrc
    qr, qi = pr * lrc - pi * lic, pr * lic + pi * lrc
    tcc = jnp.stack([pr, -pi, qr, -qi], axis=1).transpose(0, 1, 3, 2, 4).reshape(depth, 4, cg, groups * nst)

    mm = lambda a, b: jnp.einsum("lgcn,lgnd->lgcd", a, b, precision=hi)
    g0 = mm(c_re, bbr) - mm(c_im, bbi)
    g1 = mm(pr, bbr) - mm(pi, bbi)
    g0d = g0 + d_skip.reshape(depth, groups, cg)[..., None] * jnp.eye(cg, dtype=F32)
    tg = jnp.stack([g0d, g1], axis=1).transpose(0, 1, 3, 2, 4).reshape(depth, 2, cg, groups * cg)
    lam2 = jnp.stack([lbr * lbr - lbi * lbi, 2.0 * lbr * lbi], axis=1).reshape(depth, 2, groups * nst)
    return tb, tcc, tg, lam2


def _s5_embed_weights(tb_ref, tcc_ref, tg_ref, wb_ref, wct_ref):
    nbk = wb_ref.shape[0]
    cg, nst = tcc_ref.shape[1], tb_ref.shape[2]
    bl = wb_ref.shape[2] // 2
    assert cg & (cg - 1) == 0 and nst & (nst - 1) == 0
    c_shift, n_shift = cg.bit_length() - 1, nst.bit_length() - 1
    iota = lambda shape, dim: lax.broadcasted_iota(jnp.int32, shape, dim)
    e_in = jnp.where((iota((nst, bl), 1) & (nst - 1)) == iota((nst, bl), 0), 1.0, 0.0).astype(BF16)
    e_out = jnp.where((iota((LANES, cg), 0) & (cg - 1)) == iota((LANES, cg), 1), 1.0, 0.0).astype(BF16)
    m_state = (iota((LANES, bl), 0) >> c_shift) == (iota((LANES, bl), 1) >> n_shift)
    m_direct = (iota((LANES, LANES), 0) >> c_shift) == (iota((LANES, LANES), 1) >> c_shift)
    zeros = jnp.zeros((LANES, LANES), BF16)
    for j in range(nbk):
        for k in range(4):
            half, part = divmod(k, 2)
            a = tb_ref[k, j * LANES:(j + 1) * LANES, :].astype(BF16)
            wb_ref[j, half * LANES:(half + 1) * LANES, part * bl:(part + 1) * bl] = (
                jnp.where(m_state, _dot(a, e_in), 0.0).astype(BF16))
        for k in range(4):
            eo, part = divmod(k, 2)
            x = tcc_ref[k, :, j * bl:(j + 1) * bl].astype(BF16)
            wct_ref[j, eo * LANES:(eo + 1) * LANES, part * bl:(part + 1) * bl] = (
                jnp.where(m_state, _dot(e_out, x), 0.0).astype(BF16))
        direct = [jnp.where(m_direct, _dot(e_out, tg_ref[k, :, j * LANES:(j + 1) * LANES].astype(BF16)), 0.0
                            ).astype(BF16) for k in range(2)]
        base = 2 * bl
        wct_ref[j, 0:LANES, base:base + LANES] = direct[0]
        wct_ref[j, 0:LANES, base + LANES:base + 2 * LANES] = zeros
        wct_ref[j, LANES:2 * LANES, base:base + LANES] = direct[1]
        wct_ref[j, LANES:2 * LANES, base + LANES:base + 2 * LANES] = direct[0]


def _s5_kernel(u_ref, tb_ref, tcc_ref, tg_ref, lam2_ref, wg_ref, bg_ref, x0r_ref, x0i_ref, accr_ref, acci_ref,
               *refs, batch, with_cast):
    del accr_ref, acci_ref
    refs, cast_src, cast_dst = _split_cast_refs(refs, 3, with_cast)
    o_ref, xr_ref, xi_ref, xs_ref, y_ref, st_ref, ue_ref, uo_ref, tmp_ref, wb_ref, wct_ref = refs
    if with_cast:
        cast_dst[...] = cast_src[...].astype(BF16)
    c_idx = pl.program_id(0)
    tt = u_ref.shape[0]
    tp = tt // 2
    prows = tp * batch
    width = u_ref.shape[1] // batch
    n_slab = width // LANES
    nbk = wb_ref.shape[0]
    bl = wb_ref.shape[2] // 2
    ns = nbk * bl
    grp = st_ref.shape[1]
    per = grp // batch

    @pl.when(c_idx == 0)
    def _():
        st_ref[0] = jnp.concatenate([x0r_ref[...]] * per, axis=0)
        st_ref[1] = jnp.concatenate([x0i_ref[...]] * per, axis=0)
        _s5_embed_weights(tb_ref, tcc_ref, tg_ref, wb_ref, wct_ref)

    for b in range(batch):
        for m in range(n_slab):
            tmp_ref[...] = u_ref[:, b * width + m * LANES:b * width + (m + 1) * LANES]
            ue_ref[m, pl.ds(b, tp, stride=batch), :] = tmp_ref[pl.ds(0, tp, stride=2), :]
            uo_ref[m, pl.ds(b, tp, stride=batch), :] = tmp_ref[pl.ds(1, tp, stride=2), :]

    for j in range(nbk):
        lhs = jnp.concatenate([ue_ref[j], uo_ref[j]], axis=1).astype(BF16)
        w = _dot(lhs, wb_ref[j])
        xs_ref[:, j * bl:(j + 1) * bl] = w[:, :bl]
        xs_ref[:, ns + j * bl:ns + (j + 1) * bl] = w[:, bl:]

    second = lax.broadcasted_iota(jnp.int32, (grp, S5_SCAN_LANES), 0) >= batch
    for cb in range(ns // S5_SCAN_LANES):
        lo = cb * S5_SCAN_LANES
        re_l = slice(lo, lo + S5_SCAN_LANES)
        im_l = slice(ns + lo, ns + lo + S5_SCAN_LANES)
        ar = lam2_ref[0:1, re_l]
        ai = lam2_ref[1:2, re_l]

        def step(g, carry, ar=ar, ai=ai, re_l=re_l, im_l=im_l):
            cr, ci = carry
            r = pl.ds(pl.multiple_of(g * grp, grp), grp)
            wr = xs_ref[r, re_l]
            wi = xs_ref[r, im_l]
            if per == 2:
                tr = ar * cr - ai * ci + pltpu.roll(wr, batch, axis=0)
                ti = ar * ci + ai * cr + pltpu.roll(wi, batch, axis=0)
                xs_ref[r, re_l] = jnp.where(second, tr, cr)
                xs_ref[r, im_l] = jnp.where(second, ti, ci)
                vr = ar * tr - ai * ti + wr
                vi = ar * ti + ai * tr + wi
                nr = jnp.where(second, vr, pltpu.roll(vr, batch, axis=0))
                ni = jnp.where(second, vi, pltpu.roll(vi, batch, axis=0))
            else:
                xs_ref[r, re_l] = cr
                xs_ref[r, im_l] = ci
                nr = ar * cr - ai * ci + wr
                ni = ar * ci + ai * cr + wi
            return nr, ni

        cr, ci = lax.fori_loop(0, prows // grp, step, (st_ref[0, :, re_l], st_ref[1, :, re_l]), unroll=2)
        st_ref[0, :, re_l] = cr
        st_ref[1, :, re_l] = ci

    for j in range(nbk):
        lhs = jnp.concatenate([xs_ref[:, j * bl:(j + 1) * bl], xs_ref[:, ns + j * bl:ns + (j + 1) * bl],
                               ue_ref[j], uo_ref[j]], axis=1).astype(BF16)
        yj = _dot_nt(lhs, wct_ref[j])
        y_ref[0:prows, j * LANES:(j + 1) * LANES] = yj[:, :LANES]
        y_ref[prows:2 * prows, j * LANES:(j + 1) * LANES] = yj[:, LANES:]

    hh = jax.nn.gelu(y_ref[...])
    gate = _sigmoid(_dot(hh.astype(BF16), wg_ref[...]) + bg_ref[...])
    out = hh * gate
    for m in range(n_slab):
        ue_ref[m] = out[0:prows, m * LANES:(m + 1) * LANES]
        uo_ref[m] = out[prows:2 * prows, m * LANES:(m + 1) * LANES]
    for b in range(batch):
        for m in range(n_slab):
            tmp_ref[pl.ds(0, tp, stride=2), :] = ue_ref[m, pl.ds(b, tp, stride=batch), :]
            tmp_ref[pl.ds(1, tp, stride=2), :] = uo_ref[m, pl.ds(b, tp, stride=batch), :]
            o_ref[:, b * width + m * LANES:b * width + (m + 1) * LANES] = tmp_ref[...].astype(BF16)

    @pl.when(c_idx == pl.num_programs(0) - 1)
    def _():
        xr_ref[...] = st_ref[0, grp - batch:grp, :]
        xi_ref[...] = st_ref[1, grp - batch:grp, :]


def _s5(u, tb, tcc, tg, lam2, w_glu, b_glu, x0r, x0i, acc_r, acc_i, *, batch, layer, cast_src=None):
    seq = u.shape[0]
    width = u.shape[1] // batch
    n = seq * batch
    ns = lam2.shape[2]
    nbk = width // LANES
    bl = ns // nbk
    assert batch % SUBLANES == 0 or 2 * batch == SUBLANES
    assert seq % 2 == 0 and tb.shape[2] == width
    grp = max(batch, SUBLANES)
    rows = min(S5_ROWS, n)
    tt = rows // batch
    prows = rows // 2
    lay4 = lambda c: (layer, 0, 0, 0)
    lay3 = lambda c: (layer, 0, 0)
    st_in = pl.BlockSpec((None, batch, ns), lay3)
    resident = pl.Buffered(1)
    wb_shape = (nbk, 2 * LANES, 2 * bl)
    wct_shape = (nbk, 2 * LANES, 2 * bl + 2 * LANES)
    need = (2 * rows * width * 4 + (nbk * 2 * LANES * (4 * bl + 2 * LANES) + width * width) * 2
            + 4 * width * LANES * 4 + 6 * 16 * ns * 4
            + 2 * rows * width * 2 + prows * 2 * ns * 4 + rows * width * 4 + 4 * rows * width * 4
            + 4 * prows * (4 * bl + 2 * LANES) + (8 * batch + 2 * grp) * ns * 4)
    st_shape = jax.ShapeDtypeStruct(acc_r.shape, F32)
    in_specs = [
        pl.BlockSpec((tt, batch * width), lambda c: (c, 0)),
        pl.BlockSpec((None,) + tb.shape[1:], lay4, pipeline_mode=resident),
        pl.BlockSpec((None,) + tcc.shape[1:], lay4, pipeline_mode=resident),
        pl.BlockSpec((None,) + tg.shape[1:], lay4, pipeline_mode=resident),
        pl.BlockSpec((None, 2, ns), lay3),
        pl.BlockSpec((None, width, width), lay3, pipeline_mode=resident),
        pl.BlockSpec((None, 1, width), lay3),
        st_in,
        st_in,
        pl.BlockSpec(memory_space=pl.ANY),
        pl.BlockSpec(memory_space=pl.ANY),
    ]
    out_specs = [pl.BlockSpec((tt, batch * width), lambda c: (c, 0)), st_in, st_in]
    out_shape = [jax.ShapeDtypeStruct((seq, batch * width), BF16), st_shape, st_shape]
    args = [u, tb, tcc, tg, lam2, w_glu, b_glu, x0r, x0i, acc_r, acc_i]
    if cast_src is not None:
        c_in, c_out, c_shape, c_bytes = _cast_job(cast_src, layer, n // rows, lambda c: c, axis=1)
        in_specs.append(c_in)
        out_specs.append(c_out)
        out_shape.append(c_shape)
        args.append(cast_src)
        need += c_bytes
    return pl.pallas_call(
        functools.partial(_s5_kernel, batch=batch, with_cast=cast_src is not None),
        out_shape=tuple(out_shape),
        grid=(n // rows,),
        in_specs=in_specs,
        out_specs=tuple(out_specs),
        input_output_aliases={9: 1, 10: 2},
        scratch_shapes=[
            pltpu.VMEM((prows, 2 * ns), F32),
            pltpu.VMEM((rows, width), F32),
            pltpu.VMEM((2, grp, ns), F32),
            pltpu.VMEM((width // LANES, prows, LANES), F32),
            pltpu.VMEM((width // LANES, prows, LANES), F32),
            pltpu.VMEM((tt, LANES), F32),
            pltpu.VMEM(wb_shape, BF16),
            pltpu.VMEM(wct_shape, BF16),
        ],
        compiler_params=pltpu.CompilerParams(
            dimension_semantics=("arbitrary",),
            vmem_limit_bytes=_vmem_limit(need)),
        name="s5",
    )(*args)


def _outproj_kernel(x_ref, oh_ref, os_ref, wh_ref, ws_ref, o_ref):
    o_ref[...] = x_ref[...] + _dot(oh_ref[...], wh_ref[...]) + _dot(os_ref[...], ws_ref[...])


def _outproj(x, o_h, o_s, w_out, *, layer, tm, tn, seq):
    n, d = x.shape
    kh = o_h.shape[1]
    ks = w_out.shape[1] - kh
    assert kh == ks
    if o_s.shape[0] == n:
        os_map = lambda i, j: (i, 0)
    else:
        n_t = seq // tm
        os_map = lambda i, j: (i % n_t, i // n_t)
    need = 2 * (2 * tm * tn * 4 + tm * (kh + ks) * 2 + (kh + ks) * tn * 2) + 2 * tm * tn * 4
    return pl.pallas_call(
        _outproj_kernel,
        out_shape=jax.ShapeDtypeStruct((n, d), F32),
        grid=(n // tm, d // tn),
        in_specs=[
            pl.BlockSpec((tm, tn), lambda i, j: (i, j)),
            pl.BlockSpec((tm, kh), lambda i, j: (i, 0)),
            pl.BlockSpec((tm, ks), os_map),
            pl.BlockSpec((None, kh, tn), lambda i, j: (layer, 0, j)),
            pl.BlockSpec((None, ks, tn), lambda i, j: (layer, 1, j)),
        ],
        out_specs=pl.BlockSpec((tm, tn), lambda i, j: (i, j)),
        compiler_params=pltpu.CompilerParams(
            dimension_semantics=("parallel", "parallel"),
            vmem_limit_bytes=_vmem_limit(need)),
        name="outproj",
    )(x, o_h, o_s, w_out, w_out)


def _ffn_kernel(x_ref, g_ref, w1_ref, w2_ref, gf_ref, o_ref, h_ref, *, final_norm):
    f = pl.program_id(1)
    tm = x_ref.shape[0]

    @pl.when(f == 0)
    def _():
        g = g_ref[...]

        def body(i, carry):
            r = pl.ds(pl.multiple_of(i * NORM_ROWS, NORM_ROWS), NORM_ROWS)
            h_ref[r, :] = _rmsnorm(x_ref[r, :], g).astype(BF16)
            o_ref[r, :] = jnp.zeros((NORM_ROWS, o_ref.shape[1]), F32)
            return carry

        lax.fori_loop(0, tm // NORM_ROWS, body, 0, unroll=NORM_UNROLL)

    a = _dot(h_ref[...], w1_ref[...])
    a = jnp.square(jnp.maximum(a, 0.0)).astype(BF16)
    cw = o_ref.shape[1] // FFN_ACC_CHUNKS
    for c in range(FFN_ACC_CHUNKS):
        o_ref[:, c * cw:(c + 1) * cw] += _dot(a, w2_ref[:, c * cw:(c + 1) * cw])

    @pl.when(f == pl.num_programs(1) - 1)
    def _():
        gf = gf_ref[...]

        def body(i, carry):
            r = pl.ds(pl.multiple_of(i * NORM_ROWS, NORM_ROWS), NORM_ROWS)
            y = x_ref[r, :] + o_ref[r, :]
            if final_norm:
                y = _rmsnorm(y, gf)
            o_ref[r, :] = y
            return carry

        lax.fori_loop(0, tm // NORM_ROWS, body, 0, unroll=NORM_UNROLL)


def _ffn(x, g, w1, w2, gf, *, layer, tm, tf, final_norm):
    n, d = x.shape
    ff = w1.shape[1]
    need = tm * d * 4 + tm * d * 2 + 2 * tm * d * 4 + 4 * d * tf * 2 + tm * tf * 6 + tm * d * 4
    kernel = functools.partial(_ffn_kernel, final_norm=final_norm)
    return pl.pallas_call(
        kernel,
        out_shape=jax.ShapeDtypeStruct((n, d), F32),
        grid=(n // tm, ff // tf),
        in_specs=[
            pl.BlockSpec((tm, d), lambda i, f: (i, 0), pipeline_mode=pl.Buffered(1)),
            pl.BlockSpec((None, 1, d), lambda i, f: (layer, 0, 0)),
            pl.BlockSpec((d, tf), lambda i, f: (0, f)),
            pl.BlockSpec((tf, d), lambda i, f: (f, 0)),
            pl.BlockSpec((1, d), lambda i, f: (0, 0)),
        ],
        out_specs=pl.BlockSpec((tm, d), lambda i, f: (i, 0)),
        scratch_shapes=[pltpu.VMEM((tm, d), BF16)],
        compiler_params=pltpu.CompilerParams(
            dimension_semantics=("parallel", "arbitrary"),
            vmem_limit_bytes=_vmem_limit(need)),
        name="ffn",
    )(x, g, w1, w2, gf)


def _trunk(x, st_h, st_r, st_i, wts, ffn_w=None):
    batch, seq, d = x.shape
    n = batch * seq
    depth = wts["w_in"].shape[0]
    heads = st_h.shape[2]
    hgrn_width = heads * HEAD_DIM
    groups, nst = st_r.shape[2], st_r.shape[3]
    s5_width = wts["w_glu"].shape[1]
    tm = min(1024, n)
    xf = x.reshape(n, d)
    x0r = st_r.reshape(depth, batch, groups * nst)
    x0i = st_i.reshape(depth, batch, groups * nst)
    time_major = seq % tm == 0
    new_h = jnp.zeros(st_h.shape, F32)
    new_r = jnp.zeros(x0r.shape, F32)
    new_i = jnp.zeros(x0i.shape, F32)
    used_w = []
    for l in range(depth):
        cast = ffn_w is None
        p, u = _inproj(xf, wts["norm1_g"], wts["w_in"], wts["lb_logits"],
                       layer=l, tm=tm, tn=s5_width, batch=batch, seq=seq)
        o_h, new_h, *w1 = _hgrn(p, wts["onorm_g"], st_h, new_h, batch=batch, seq=seq, layer=l,
                                cast_src=wts["w_ff1"] if cast else None)
        if not time_major:
            u = u.reshape(batch, seq, s5_width).transpose(1, 0, 2).reshape(seq, batch * s5_width)
        o_s, new_r, new_i, *w2 = _s5(u, wts["s5_tb"], wts["s5_tcc"], wts["s5_tg"], wts["s5_lam2"],
                                     wts["w_glu"], wts["b_glu"], x0r, x0i, new_r, new_i,
                                     batch=batch, layer=l, cast_src=wts["w_ff2"] if cast else None)
        if not time_major:
            o_s = o_s.reshape(seq, batch, s5_width).transpose(1, 0, 2).reshape(n, s5_width)
        used_w.append((w1[0], w2[0]) if cast else ffn_w[l])
        x1 = _outproj(xf, o_h, o_s, wts["w_out"], layer=l, tm=min(512, n), tn=d, seq=seq)
        xf = _ffn(x1, wts["norm2_g"], *used_w[l], wts["final_g"],
                  layer=l, tm=tm, tf=1024, final_norm=(l == depth - 1))
    return (xf.reshape(batch, seq, d), new_h, new_r.reshape(st_r.shape), new_i.reshape(st_i.shape)), used_w


def kernel(x_prompt, x_sample, state_hgrn, state_s5_re, state_s5_im, norm1_g, w_in, hgrn_lb_logits,
           hgrn_onorm_g, s5_lambda_re, s5_lambda_im, s5_log_step, s5_B_re, s5_B_im, s5_C_re, s5_C_im,
           s5_D, s5_w_glu, s5_b_glu, w_out, norm2_g, w_ff1, w_ff2, final_norm_g):
    depth, d = norm1_g.shape
    row3 = lambda a: a.reshape(depth, 1, a.shape[-1])
    lbr, lbi, zoh_r, zoh_i = _s5_discretise(s5_lambda_re, s5_lambda_im, s5_log_step)
    s5_tb, s5_tcc, s5_tg, s5_lam2 = _s5_pair_weights(lbr, lbi, zoh_r, zoh_i, s5_B_re, s5_B_im,
                                                     s5_C_re, s5_C_im, s5_D)
    wts = {
        "norm1_g": row3(norm1_g), "norm2_g": row3(norm2_g), "final_g": final_norm_g.reshape(1, d),
        "lb_logits": hgrn_lb_logits, "onorm_g": row3(hgrn_onorm_g),
        "w_in": w_in.astype(BF16), "w_out": w_out.astype(BF16),
        "w_ff1": w_ff1, "w_ff2": w_ff2,
        "w_glu": s5_w_glu.astype(BF16), "b_glu": row3(s5_b_glu),
        "s5_tb": s5_tb, "s5_tcc": s5_tcc, "s5_tg": s5_tg, "s5_lam2": s5_lam2,
    }
    bp = x_prompt.shape[0]
    zh = jnp.zeros((depth, bp) + state_hgrn.shape[2:], F32)
    zs = jnp.zeros((depth, bp) + state_s5_re.shape[2:], F32)
    (y_p, hp, rp, ip), ffn_w = _trunk(x_prompt, zh, zs, zs, wts)
    (y_s, hs, rs, is_), _ = _trunk(x_sample, state_hgrn, state_s5_re, state_s5_im, wts, ffn_w)
    return (y_p, y_s, hp, rp, ip, hs, rs, is_)
```

```python
import functools

import jax
import jax.numpy as jnp
from jax import lax
from jax.experimental import pallas as pl
from jax.experimental.pallas import tpu as pltpu

F32 = jnp.float32
BF16 = jnp.bfloat16
EPS = 1e-6

HEAD_DIM = 128
HGRN_CHUNK = 64
HGRN_SUB = 16
HGRN_HEADS_PER_STEP = 4
SUBLANES = 8
LANES = 128
S5_SCAN_LANES = 512
S5_ROWS = 512

V7X_VMEM_CAP = 56 * 1024 * 1024
NORM_ROWS = 64
NORM_UNROLL = 4
FFN_ACC_CHUNKS = 4
INPROJ_CHUNKS = 4


def _vmem_limit(nbytes):
    return int(min(V7X_VMEM_CAP, nbytes * 5 // 4 + (4 << 20)))


def _rmsnorm(x, g):
    return x * lax.rsqrt(jnp.mean(jnp.square(x), axis=-1, keepdims=True) + EPS) * g


def _sigmoid(x):
    return 1.0 / (1.0 + jnp.exp(-x))


def _log1p_exp_neg_abs(x):
    return jnp.log(1.0 + jnp.exp(-jnp.abs(x)))


def _softplus(x):
    return jnp.maximum(x, 0.0) + _log1p_exp_neg_abs(x)


def _dot(a, b):
    return jnp.dot(a, b, preferred_element_type=F32)


def _dot_nt(a, b):
    return lax.dot_general(a, b, (((1,), (1,)), ((), ())), preferred_element_type=F32)


def _inproj_kernel(x_ref, g_ref, w_ref, lbl_ref, p_ref, u_ref, h_ref, *, layer):
    tm = x_ref.shape[0]
    cw = w_ref.shape[1] // INPROJ_CHUNKS
    j = pl.program_id(1)

    @pl.when(j == 0)
    def _():
        g = g_ref[...]

        def body(i, carry):
            r = pl.ds(pl.multiple_of(i * NORM_ROWS, NORM_ROWS), NORM_ROWS)
            h_ref[r, :] = _rmsnorm(x_ref[r, :], g).astype(BF16)
            return carry

        lax.fori_loop(0, tm // NORM_ROWS, body, 0, unroll=NORM_UNROLL)

    def emit(out_ref, fn):
        for c in range(INPROJ_CHUNKS):
            sl = slice(c * cw, (c + 1) * cw)
            out_ref[:, sl] = fn(_dot(h_ref[...], w_ref[:, sl]), sl)

    def silu(a, sl):
        return a * _sigmoid(a)

    def log_forget(z, sl):
        logits = lbl_ref[:, sl]
        e = jnp.exp(logits - jnp.max(logits, axis=0, keepdims=True))
        sm = e / jnp.sum(e, axis=0, keepdims=True)
        cum0 = sm[0:1]
        cuml = cum0
        for k in range(1, layer + 1):
            cuml = cuml + sm[k:k + 1]
        lb = cuml - cum0
        log_lb = jnp.log(lb)
        b = jnp.log1p(-lb) - _softplus(-z)
        delta = log_lb - b
        return jnp.where(jnp.isnan(delta), log_lb + b,
                         jnp.maximum(log_lb, b) + _log1p_exp_neg_abs(delta))

    for tile, (out_ref, fn) in enumerate([(p_ref, silu), (p_ref, log_forget), (p_ref, lambda a, sl: a),
                                          (p_ref, silu), (u_ref, lambda a, sl: a)]):
        pl.when(j == tile)(functools.partial(emit, out_ref, fn))


def _inproj(x, g, w, lb_logits, *, layer, tm, tn, batch, seq):
    n, d = x.shape
    cols = w.shape[2]
    n_j = cols // tn
    depth = lb_logits.shape[0]
    assert n_j == 5 and lb_logits.shape[1] == tn
    time_major = seq % tm == 0
    if time_major:
        n_t = seq // tm
        u_shape, u_map = (seq, batch * tn), (lambda i, j: (i % n_t, i // n_t))
    else:
        u_shape, u_map = (n, tn), (lambda i, j: (i, 0))
    need = 2 * tm * d * 4 + tm * d * 2 + 2 * d * tn * 2 + 5 * tm * tn * 4
    return pl.pallas_call(
        functools.partial(_inproj_kernel, layer=layer),
        out_shape=(jax.ShapeDtypeStruct((n, cols - tn), F32), jax.ShapeDtypeStruct(u_shape, F32)),
        grid=(n // tm, n_j),
        in_specs=[
            pl.BlockSpec((tm, d), lambda i, j: (i, 0)),
            pl.BlockSpec((None, 1, d), lambda i, j: (layer, 0, 0)),
            pl.BlockSpec((None, d, tn), lambda i, j: (layer, 0, j)),
            pl.BlockSpec((depth, tn), lambda i, j: (0, 0)),
        ],
        out_specs=(pl.BlockSpec((tm, tn), lambda i, j: (i, jnp.minimum(j, n_j - 2))),
                   pl.BlockSpec((tm, tn), u_map)),
        scratch_shapes=[pltpu.VMEM((tm, d), BF16)],
        compiler_params=pltpu.CompilerParams(
            dimension_semantics=("parallel", "arbitrary"),
            vmem_limit_bytes=_vmem_limit(need)),
        name="inproj",
    )(x, g, w, lb_logits)


def _hgrn_decayed_queries(q_lo, q_hi, f_row, row_is):
    slabs = []
    qd_lo = qd_hi = None
    for s in range(2 * SUBLANES - 1, -1, -1):
        if s == 2 * SUBLANES - 1:
            qd_hi = jnp.where(row_is[SUBLANES - 1], q_hi, 0.0)
        elif s >= SUBLANES:
            qd_hi = jnp.where(row_is[s - SUBLANES], q_hi, qd_hi * f_row(s + 1))
        else:
            fn = f_row(s + 1)
            qd_hi = qd_hi * fn
            qd_lo = jnp.where(row_is[s], q_lo, 0.0 if s == SUBLANES - 1 else qd_lo * fn)
        slabs.append((s, 1, qd_hi))
        if s < SUBLANES:
            slabs.append((s, 0, qd_lo))
    return slabs


def _split_cast_refs(refs, n_out, with_cast):
    if not with_cast:
        return refs, None, None
    return refs[1:1 + n_out] + refs[2 + n_out:], refs[0], refs[1 + n_out]


def _hgrn_kernel(q_ref, f_ref, i_ref, g_ref, on_ref, s0_ref, acc_ref, *refs, chunk, heads_blk, with_cast):
    del acc_ref
    refs, cast_src, cast_dst = _split_cast_refs(refs, 2, with_cast)
    o_ref, sout_ref, st_ref, ks_ref, cs_ref, fs_ref, fc_ref, op_ref = refs
    if with_cast:
        cast_dst[...] = cast_src[...].astype(BF16)
    c_idx = pl.program_id(2)
    tc = q_ref.shape[0]
    n_sub = chunk // HGRN_SUB

    @pl.when(c_idx == 0)
    def _():
        for h in range(heads_blk):
            st_ref[h] = s0_ref[0, h].T

    logf = f_ref[...]
    fgate = jnp.exp(logf)
    fs_ref[...] = fgate
    ks_ref[...] = 1.0 - fgate

    t_i = lax.broadcasted_iota(jnp.int32, (chunk, chunk), 0)
    s_i = lax.broadcasted_iota(jnp.int32, (chunk, chunk), 1)
    tri = jnp.where(t_i >= s_i, 1.0, 0.0).astype(BF16)
    for j in range(tc // chunk):
        lf = logf[j * chunk:(j + 1) * chunk]
        hi = lf.astype(BF16)
        r1 = lf - hi.astype(F32)
        mid = r1.astype(BF16)
        lo = (r1 - mid.astype(F32)).astype(BF16)
        cs_ref[j * chunk:(j + 1) * chunk, :] = _dot(tri, hi) + _dot(tri, mid) + _dot(tri, lo)

    row8 = lax.broadcasted_iota(jnp.int32, (SUBLANES, HEAD_DIM), 0)
    row_is = [row8 == j for j in range(SUBLANES)]
    lane8 = lax.broadcasted_iota(jnp.int32, (SUBLANES, chunk), 1)
    lane_is = [lane8 == j for j in range(chunk)]
    onorm = on_ref[...]
    blk = lambda a, j: a[j * HGRN_SUB:(j + 1) * HGRN_SUB]

    head_cols = [slice(h * HEAD_DIM, (h + 1) * HEAD_DIM) for h in range(heads_blk)]

    def finish(rows):
        for cols in head_cols:
            o = op_ref[:, cols]
            o = o * lax.rsqrt(jnp.mean(jnp.square(o), axis=-1, keepdims=True) + EPS)
            o = o * onorm[:, cols]
            o_ref[rows, cols] = (o * g_ref[rows, cols]).astype(BF16)

    op_ref[...] = jnp.zeros(op_ref.shape, F32)

    def chunk_body(c, carry):
        base = pl.multiple_of(c * chunk, chunk)
        r = pl.ds(base, chunk)
        finish(pl.ds(pl.multiple_of(jnp.maximum(c - 1, 0) * chunk, chunk), chunk))
        fc_ref[...] = fs_ref[r, :]
        cum_all = cs_ref[r, :]

        zero_blk = jnp.zeros((HGRN_SUB, HEAD_DIM), F32)
        qs, vbs, kbs, o_inter, a_off = [], [], [], [], []
        for cols in head_cols:
            h = len(qs)
            cum = cum_all[:, cols]
            q = q_ref[r, cols]
            k = ks_ref[r, cols]
            v = i_ref[r, cols]
            ends = [cum[(j + 1) * HGRN_SUB - 1:(j + 1) * HGRN_SUB, :] for j in range(n_sub)]
            cl = ends[-1]
            ke = [blk(k, j) * jnp.exp(ends[j] - blk(cum, j)) for j in range(n_sub)]
            qe = [blk(q, j) * jnp.exp(blk(cum, j) - ends[j - 1] if j else blk(cum, j)) for j in range(n_sub)]

            st = st_ref[h]
            q_in = jnp.concatenate([qe[j] * jnp.exp(ends[j - 1]) if j else qe[j] for j in range(n_sub)], axis=0)
            o_inter.append(_dot_nt(q_in.astype(BF16), st.astype(BF16)))
            kdec = jnp.concatenate(
                [ke[j] * jnp.exp(cl - ends[j]) if j < n_sub - 1 else ke[j] for j in range(n_sub)], axis=0)
            st_ref[h] = st * jnp.exp(cl) + _dot(v.T.astype(BF16), kdec.astype(BF16))

            offs = [None]
            for i in range(1, n_sub):
                kt = jnp.concatenate(
                    [ke[j] * jnp.exp(ends[i - 1] - ends[j]) if j < i - 1 else ke[j] for j in range(i)]
                    + [zero_blk] * (n_sub - i), axis=0)
                offs.append(_dot_nt(qe[i].astype(BF16), kt.astype(BF16)))
            a_off.append(offs)
            qs.append(q)
            vbs.append(v.astype(BF16))
            kbs.append(k.astype(BF16))

        res = []
        for h, cols in enumerate(head_cols):
            per_head = []
            for i in range(n_sub):
                lo_r = i * HGRN_SUB
                f_row = lambda s, lo_r=lo_r, cols=cols: fc_ref[lo_r + s:lo_r + s + 1, cols]
                q_blk = blk(qs[h], i)
                slabs = _hgrn_decayed_queries(q_blk[:SUBLANES], q_blk[SUBLANES:], f_row, row_is)
                stack = jnp.concatenate([slab for _, _, slab in slabs], axis=0)
                per_head.append(([(s, half) for s, half, _ in slabs],
                                 _dot_nt(stack.astype(BF16), kbs[h])))
            res.append(per_head)

        o_intra = []
        for h in range(heads_blk):
            rows = []
            for i in range(n_sub):
                order, prod = res[h][i]
                halves = [jnp.zeros((SUBLANES, chunk), F32), jnp.zeros((SUBLANES, chunk), F32)]
                for idx, (s, half) in enumerate(order):
                    piece = prod[idx * SUBLANES:(idx + 1) * SUBLANES]
                    halves[half] = jnp.where(lane_is[i * HGRN_SUB + s], piece, halves[half])
                a = jnp.concatenate(halves, axis=0)
                rows.append(a + a_off[h][i] if i else a)
            attn = jnp.concatenate(rows, axis=0)
            o_intra.append(_dot(attn.astype(BF16), vbs[h]))

        for h, cols in enumerate(head_cols):
            op_ref[:, cols] = o_inter[h] + o_intra[h]
        return carry

    lax.fori_loop(0, tc // chunk, chunk_body, 0, unroll=min(2, tc // chunk))
    finish(pl.ds(tc - chunk, chunk))

    @pl.when(c_idx == pl.num_programs(2) - 1)
    def _():
        for h in range(heads_blk):
            sout_ref[0, h] = st_ref[h].T


def _cast_job(cast_src, layer, steps, step_of, axis):
    _, r, c = cast_src.shape
    if axis == 2:
        assert c % (steps * LANES) == 0
        blk = (r, c // steps)
        at = lambda *g: (0, step_of(*g))
    else:
        assert r % (steps * 2 * SUBLANES) == 0
        blk = (r // steps, c)
        at = lambda *g: (step_of(*g), 0)
    in_spec = pl.BlockSpec((None,) + blk, lambda *g: (layer,) + at(*g))
    return in_spec, pl.BlockSpec(blk, at), jax.ShapeDtypeStruct((r, c), BF16), 2 * blk[0] * blk[1] * 6


def _hgrn(p, onorm_g, s0, s_acc, *, batch, seq, layer, cast_src=None):
    n = p.shape[0]
    heads = s0.shape[2]
    hb = HGRN_HEADS_PER_STEP
    wblk = hb * HEAD_DIM
    n_hb = heads // hb
    width = heads * HEAD_DIM
    tc = min(512, seq)
    chunk = min(HGRN_CHUNK, seq)
    n_t = seq // tc
    row = lambda b, h, c: b * n_t + c
    need = 2 * 4 * tc * wblk * 4 + 2 * tc * wblk * 2 + 3 * tc * wblk * 4 + 5 * hb * HEAD_DIM * HEAD_DIM * 4
    kernel = functools.partial(_hgrn_kernel, chunk=chunk, heads_blk=hb, with_cast=cast_src is not None)
    st_spec = pl.BlockSpec((None, 1, hb, HEAD_DIM, HEAD_DIM), lambda b, h, c: (layer, b, h, 0, 0))
    in_specs = [
        pl.BlockSpec((tc, wblk), lambda b, h, c: (row(b, h, c), h)),
        pl.BlockSpec((tc, wblk), lambda b, h, c: (row(b, h, c), n_hb + h)),
        pl.BlockSpec((tc, wblk), lambda b, h, c: (row(b, h, c), 2 * n_hb + h)),
        pl.BlockSpec((tc, wblk), lambda b, h, c: (row(b, h, c), 3 * n_hb + h)),
        pl.BlockSpec((None, 1, wblk), lambda b, h, c: (layer, 0, h)),
        st_spec,
        pl.BlockSpec(memory_space=pl.ANY),
    ]
    out_specs = [pl.BlockSpec((tc, wblk), lambda b, h, c: (row(b, h, c), h)), st_spec]
    out_shape = [jax.ShapeDtypeStruct((n, width), BF16), jax.ShapeDtypeStruct(s_acc.shape, F32)]
    args = [p, p, p, p, onorm_g, s0, s_acc]
    if cast_src is not None:
        c_in, c_out, c_shape, c_bytes = _cast_job(cast_src, layer, batch * n_hb * n_t,
                                                  lambda b, h, c: (b * n_hb + h) * n_t + c, axis=2)
        in_specs.append(c_in)
        out_specs.append(c_out)
        out_shape.append(c_shape)
        args.append(cast_src)
        need += c_bytes
    return pl.pallas_call(
        kernel,
        out_shape=tuple(out_shape),
        grid=(batch, n_hb, n_t),
        in_specs=in_specs,
        out_specs=tuple(out_specs),
        input_output_aliases={6: 1},
        scratch_shapes=[
            pltpu.VMEM((hb, HEAD_DIM, HEAD_DIM), F32),
            pltpu.VMEM((tc, wblk), F32),
            pltpu.VMEM((tc, wblk), F32),
            pltpu.VMEM((tc, wblk), F32),
            pltpu.VMEM((chunk, wblk), F32),
            pltpu.VMEM((chunk, wblk), F32),
        ],
        compiler_params=pltpu.CompilerParams(
            dimension_semantics=("parallel", "parallel", "arbitrary"),
            vmem_limit_bytes=max(V7X_VMEM_CAP, _vmem_limit(need))),
        name="hgrn",
    )(*args)


def _s5_disc_lambda_kernel(lr_ref, li_ref, ls_ref, lbr_ref, lbi_ref, cr_ref, ci_ref):
    lr, li = lr_ref[...], li_ref[...]
    dt = jnp.exp(ls_ref[...])
    mag = jnp.exp(dt * lr)
    ang = dt * li
    lbr = mag * jnp.cos(ang)
    lbi = mag * jnp.sin(ang)
    nr, ni = lbr - 1.0, lbi
    den = lr * lr + li * li
    lbr_ref[...] = lbr
    lbi_ref[...] = lbi
    cr_ref[...] = (nr * lr + ni * li) / den
    ci_ref[...] = (ni * lr - nr * li) / den


def _s5_discretise(lam_re, lam_im, log_step):
    depth, groups, nst = lam_re.shape
    rows = depth * groups
    shp = jax.ShapeDtypeStruct((rows, nst), F32)
    outs = pl.pallas_call(
        _s5_disc_lambda_kernel, out_shape=(shp, shp, shp, shp), name="s5_disc_lambda",
    )(lam_re.reshape(rows, nst), lam_im.reshape(rows, nst), log_step.reshape(rows, 1))
    return tuple(o.reshape(depth, groups, nst) for o in outs)


def _s5_pair_weights(lbr, lbi, zoh_r, zoh_i, b_re, b_im, c_re, c_im, d_skip):
    depth, groups, nst, cg = b_re.shape
    hi = lax.Precision.HIGHEST
    zr, zi = zoh_r[..., None], zoh_i[..., None]
    bbr, bbi = zr * b_re - zi * b_im, zr * b_im + zi * b_re
    lr, li = lbr[..., None], lbi[..., None]
    blr, bli = bbr * lr - bbi * li, bbr * li + bbi * lr
    tb = jnp.swapaxes(jnp.stack([blr, bli, bbr, bbi], axis=1), 3, 4).reshape(depth, 4, groups * cg, nst)

    lrc, lic = lbr[:, :, None, :], lbi[:, :, None, :]
    pr, pi = c_re * lrc - c_im * lic, c_re * lic + c_im * lrc
    qr, qi = pr * lrc - pi * lic, pr * lic + pi * lrc
    tcc = jnp.stack([pr, -pi, qr, -qi], axis=1).transpose(0, 1, 3, 2, 4).reshape(depth, 4, cg, groups * nst)

    mm = lambda a, b: jnp.einsum("lgcn,lgnd->lgcd", a, b, precision=hi)
    g0 = mm(c_re, bbr) - mm(c_im, bbi)
    g1 = mm(pr, bbr) - mm(pi, bbi)
    g0d = g0 + d_skip.reshape(depth, groups, cg)[..., None] * jnp.eye(cg, dtype=F32)
    tg = jnp.stack([g0d, g1], axis=1).transpose(0, 1, 3, 2, 4).reshape(depth, 2, cg, groups * cg)
    lam2 = jnp.stack([lbr * lbr - lbi * lbi, 2.0 * lbr * lbi], axis=1).reshape(depth, 2, groups * nst)
    return tb, tcc, tg, lam2


def _s5_embed_weights(tb_ref, tcc_ref, tg_ref, wb_ref, wct_ref):
    nbk = wb_ref.shape[0]
    cg, nst = tcc_ref.shape[1], tb_ref.shape[2]
    bl = wb_ref.shape[2] // 2
    assert cg & (cg - 1) == 0 and nst & (nst - 1) == 0
    c_shift, n_shift = cg.bit_length() - 1, nst.bit_length() - 1
    iota = lambda shape, dim: lax.broadcasted_iota(jnp.int32, shape, dim)
    e_in = jnp.where((iota((nst, bl), 1) & (nst - 1)) == iota((nst, bl), 0), 1.0, 0.0).astype(BF16)
    e_out = jnp.where((iota((LANES, cg), 0) & (cg - 1)) == iota((LANES, cg), 1), 1.0, 0.0).astype(BF16)
    m_state = (iota((LANES, bl), 0) >> c_shift) == (iota((LANES, bl), 1) >> n_shift)
    m_direct = (iota((LANES, LANES), 0) >> c_shift) == (iota((LANES, LANES), 1) >> c_shift)
    zeros = jnp.zeros((LANES, LANES), BF16)
    for j in range(nbk):
        for k in range(4):
            half, part = divmod(k, 2)
            a = tb_ref[k, j * LANES:(j + 1) * LANES, :].astype(BF16)
            wb_ref[j, half * LANES:(half + 1) * LANES, part * bl:(part + 1) * bl] = (
                jnp.where(m_state, _dot(a, e_in), 0.0).astype(BF16))
        for k in range(4):
            eo, part = divmod(k, 2)
            x = tcc_ref[k, :, j * bl:(j + 1) * bl].astype(BF16)
            wct_ref[j, eo * LANES:(eo + 1) * LANES, part * bl:(part + 1) * bl] = (
                jnp.where(m_state, _dot(e_out, x), 0.0).astype(BF16))
        direct = [jnp.where(m_direct, _dot(e_out, tg_ref[k, :, j * LANES:(j + 1) * LANES].astype(BF16)), 0.0
                            ).astype(BF16) for k in range(2)]
        base = 2 * bl
        wct_ref[j, 0:LANES, base:base + LANES] = direct[0]
        wct_ref[j, 0:LANES, base + LANES:base + 2 * LANES] = zeros
        wct_ref[j, LANES:2 * LANES, base:base + LANES] = direct[1]
        wct_ref[j, LANES:2 * LANES, base + LANES:base + 2 * LANES] = direct[0]


def _s5_kernel(u_ref, tb_ref, tcc_ref, tg_ref, lam2_ref, wg_ref, bg_ref, x0r_ref, x0i_ref, accr_ref, acci_ref,
               *refs, batch, with_cast):
    del accr_ref, acci_ref
    refs, cast_src, cast_dst = _split_cast_refs(refs, 3, with_cast)
    o_ref, xr_ref, xi_ref, xs_ref, y_ref, st_ref, ue_ref, uo_ref, tmp_ref, wb_ref, wct_ref = refs
    if with_cast:
        cast_dst[...] = cast_src[...].astype(BF16)
    c_idx = pl.program_id(0)
    tt = u_ref.shape[0]
    tp = tt // 2
    prows = tp * batch
    width = u_ref.shape[1] // batch
    n_slab = width // LANES
    nbk = wb_ref.shape[0]
    bl = wb_ref.shape[2] // 2
    ns = nbk * bl
    grp = st_ref.shape[1]
    per = grp // batch

    @pl.when(c_idx == 0)
    def _():
        st_ref[0] = jnp.concatenate([x0r_ref[...]] * per, axis=0)
        st_ref[1] = jnp.concatenate([x0i_ref[...]] * per, axis=0)
        _s5_embed_weights(tb_ref, tcc_ref, tg_ref, wb_ref, wct_ref)

    for b in range(batch):
        for m in range(n_slab):
            tmp_ref[...] = u_ref[:, b * width + m * LANES:b * width + (m + 1) * LANES]
            ue_ref[m, pl.ds(b, tp, stride=batch), :] = tmp_ref[pl.ds(0, tp, stride=2), :]
            uo_ref[m, pl.ds(b, tp, stride=batch), :] = tmp_ref[pl.ds(1, tp, stride=2), :]

    for j in range(nbk):
        lhs = jnp.concatenate([ue_ref[j], uo_ref[j]], axis=1).astype(BF16)
        w = _dot(lhs, wb_ref[j])
        xs_ref[:, j * bl:(j + 1) * bl] = w[:, :bl]
        xs_ref[:, ns + j * bl:ns + (j + 1) * bl] = w[:, bl:]

    second = lax.broadcasted_iota(jnp.int32, (grp, S5_SCAN_LANES), 0) >= batch
    for cb in range(ns // S5_SCAN_LANES):
        lo = cb * S5_SCAN_LANES
        re_l = slice(lo, lo + S5_SCAN_LANES)
        im_l = slice(ns + lo, ns + lo + S5_SCAN_LANES)
        ar = lam2_ref[0:1, re_l]
        ai = lam2_ref[1:2, re_l]

        def step(g, carry, ar=ar, ai=ai, re_l=re_l, im_l=im_l):
            cr, ci = carry
            r = pl.ds(pl.multiple_of(g * grp, grp), grp)
            wr = xs_ref[r, re_l]
            wi = xs_ref[r, im_l]
            if per == 2:
                tr = ar * cr - ai * ci + pltpu.roll(wr, batch, axis=0)
                ti = ar * ci + ai * cr + pltpu.roll(wi, batch, axis=0)
                xs_ref[r, re_l] = jnp.where(second, tr, cr)
                xs_ref[r, im_l] = jnp.where(second, ti, ci)
                vr = ar * tr - ai * ti + wr
                vi = ar * ti + ai * tr + wi
                nr = jnp.where(second, vr, pltpu.roll(vr, batch, axis=0))
                ni = jnp.where(second, vi, pltpu.roll(vi, batch, axis=0))
            else:
                xs_ref[r, re_l] = cr
                xs_ref[r, im_l] = ci
                nr = ar * cr - ai * ci + wr
                ni = ar * ci + ai * cr + wi
            return nr, ni

        cr, ci = lax.fori_loop(0, prows // grp, step, (st_ref[0, :, re_l], st_ref[1, :, re_l]), unroll=2)
        st_ref[0, :, re_l] = cr
        st_ref[1, :, re_l] = ci

    for j in range(nbk):
        lhs = jnp.concatenate([xs_ref[:, j * bl:(j + 1) * bl], xs_ref[:, ns + j * bl:ns + (j + 1) * bl],
                               ue_ref[j], uo_ref[j]], axis=1).astype(BF16)
        yj = _dot_nt(lhs, wct_ref[j])
        y_ref[0:prows, j * LANES:(j + 1) * LANES] = yj[:, :LANES]
        y_ref[prows:2 * prows, j * LANES:(j + 1) * LANES] = yj[:, LANES:]

    hh = jax.nn.gelu(y_ref[...])
    gate = _sigmoid(_dot(hh.astype(BF16), wg_ref[...]) + bg_ref[...])
    out = hh * gate
    for m in range(n_slab):
        ue_ref[m] = out[0:prows, m * LANES:(m + 1) * LANES]
        uo_ref[m] = out[prows:2 * prows, m * LANES:(m + 1) * LANES]
    for b in range(batch):
        for m in range(n_slab):
            tmp_ref[pl.ds(0, tp, stride=2), :] = ue_ref[m, pl.ds(b, tp, stride=batch), :]
            tmp_ref[pl.ds(1, tp, stride=2), :] = uo_ref[m, pl.ds(b, tp, stride=batch), :]
            o_ref[:, b * width + m * LANES:b * width + (m + 1) * LANES] = tmp_ref[...].astype(BF16)

    @pl.when(c_idx == pl.num_programs(0) - 1)
    def _():
        xr_ref[...] = st_ref[0, grp - batch:grp, :]
        xi_ref[...] = st_ref[1, grp - batch:grp, :]


def _s5(u, tb, tcc, tg, lam2, w_glu, b_glu, x0r, x0i, acc_r, acc_i, *, batch, layer, cast_src=None):
    seq = u.shape[0]
    width = u.shape[1] // batch
    n = seq * batch
    ns = lam2.shape[2]
    nbk = width // LANES
    bl = ns // nbk
    assert batch % SUBLANES == 0 or 2 * batch == SUBLANES
    assert seq % 2 == 0 and tb.shape[2] == width
    grp = max(batch, SUBLANES)
    rows = min(S5_ROWS, n)
    tt = rows // batch
    prows = rows // 2
    lay4 = lambda c: (layer, 0, 0, 0)
    lay3 = lambda c: (layer, 0, 0)
    st_in = pl.BlockSpec((None, batch, ns), lay3)
    resident = pl.Buffered(1)
    wb_shape = (nbk, 2 * LANES, 2 * bl)
    wct_shape = (nbk, 2 * LANES, 2 * bl + 2 * LANES)
    need = (2 * rows * width * 4 + (nbk * 2 * LANES * (4 * bl + 2 * LANES) + width * width) * 2
            + 4 * width * LANES * 4 + 6 * 16 * ns * 4
            + 2 * rows * width * 2 + prows * 2 * ns * 4 + rows * width * 4 + 4 * rows * width * 4
            + 4 * prows * (4 * bl + 2 * LANES) + (8 * batch + 2 * grp) * ns * 4)
    st_shape = jax.ShapeDtypeStruct(acc_r.shape, F32)
    in_specs = [
        pl.BlockSpec((tt, batch * width), lambda c: (c, 0)),
        pl.BlockSpec((None,) + tb.shape[1:], lay4, pipeline_mode=resident),
        pl.BlockSpec((None,) + tcc.shape[1:], lay4, pipeline_mode=resident),
        pl.BlockSpec((None,) + tg.shape[1:], lay4, pipeline_mode=resident),
        pl.BlockSpec((None, 2, ns), lay3),
        pl.BlockSpec((None, width, width), lay3, pipeline_mode=resident),
        pl.BlockSpec((None, 1, width), lay3),
        st_in,
        st_in,
        pl.BlockSpec(memory_space=pl.ANY),
        pl.BlockSpec(memory_space=pl.ANY),
    ]
    out_specs = [pl.BlockSpec((tt, batch * width), lambda c: (c, 0)), st_in, st_in]
    out_shape = [jax.ShapeDtypeStruct((seq, batch * width), BF16), st_shape, st_shape]
    args = [u, tb, tcc, tg, lam2, w_glu, b_glu, x0r, x0i, acc_r, acc_i]
    if cast_src is not None:
        c_in, c_out, c_shape, c_bytes = _cast_job(cast_src, layer, n // rows, lambda c: c, axis=1)
        in_specs.append(c_in)
        out_specs.append(c_out)
        out_shape.append(c_shape)
        args.append(cast_src)
        need += c_bytes
    return pl.pallas_call(
        functools.partial(_s5_kernel, batch=batch, with_cast=cast_src is not None),
        out_shape=tuple(out_shape),
        grid=(n // rows,),
        in_specs=in_specs,
        out_specs=tuple(out_specs),
        input_output_aliases={9: 1, 10: 2},
        scratch_shapes=[
            pltpu.VMEM((prows, 2 * ns), F32),
            pltpu.VMEM((rows, width), F32),
            pltpu.VMEM((2, grp, ns), F32),
            pltpu.VMEM((width // LANES, prows, LANES), F32),
            pltpu.VMEM((width // LANES, prows, LANES), F32),
            pltpu.VMEM((tt, LANES), F32),
            pltpu.VMEM(wb_shape, BF16),
            pltpu.VMEM(wct_shape, BF16),
        ],
        compiler_params=pltpu.CompilerParams(
            dimension_semantics=("arbitrary",),
            vmem_limit_bytes=_vmem_limit(need)),
        name="s5",
    )(*args)


def _outproj_kernel(x_ref, oh_ref, os_ref, wh_ref, ws_ref, o_ref):
    o_ref[...] = x_ref[...] + _dot(oh_ref[...], wh_ref[...]) + _dot(os_ref[...], ws_ref[...])


def _outproj(x, o_h, o_s, w_out, *, layer, tm, tn, seq):
    n, d = x.shape
    kh = o_h.shape[1]
    ks = w_out.shape[1] - kh
    assert kh == ks
    if o_s.shape[0] == n:
        os_map = lambda i, j: (i, 0)
    else:
        n_t = seq // tm
        os_map = lambda i, j: (i % n_t, i // n_t)
    need = 2 * (2 * tm * tn * 4 + tm * (kh + ks) * 2 + (kh + ks) * tn * 2) + 2 * tm * tn * 4
    return pl.pallas_call(
        _outproj_kernel,
        out_shape=jax.ShapeDtypeStruct((n, d), F32),
        grid=(n // tm, d // tn),
        in_specs=[
            pl.BlockSpec((tm, tn), lambda i, j: (i, j)),
            pl.BlockSpec((tm, kh), lambda i, j: (i, 0)),
            pl.BlockSpec((tm, ks), os_map),
            pl.BlockSpec((None, kh, tn), lambda i, j: (layer, 0, j)),
            pl.BlockSpec((None, ks, tn), lambda i, j: (layer, 1, j)),
        ],
        out_specs=pl.BlockSpec((tm, tn), lambda i, j: (i, j)),
        compiler_params=pltpu.CompilerParams(
            dimension_semantics=("parallel", "parallel"),
            vmem_limit_bytes=_vmem_limit(need)),
        name="outproj",
    )(x, o_h, o_s, w_out, w_out)


def _ffn_kernel(x_ref, g_ref, w1_ref, w2_ref, gf_ref, o_ref, h_ref, *, final_norm):
    f = pl.program_id(1)
    tm = x_ref.shape[0]

    @pl.when(f == 0)
    def _():
        g = g_ref[...]

        def body(i, carry):
            r = pl.ds(pl.multiple_of(i * NORM_ROWS, NORM_ROWS), NORM_ROWS)
            h_ref[r, :] = _rmsnorm(x_ref[r, :], g).astype(BF16)
            o_ref[r, :] = jnp.zeros((NORM_ROWS, o_ref.shape[1]), F32)
            return carry

        lax.fori_loop(0, tm // NORM_ROWS, body, 0, unroll=NORM_UNROLL)

    a = _dot(h_ref[...], w1_ref[...])
    a = jnp.square(jnp.maximum(a, 0.0)).astype(BF16)
    cw = o_ref.shape[1] // FFN_ACC_CHUNKS
    for c in range(FFN_ACC_CHUNKS):
        o_ref[:, c * cw:(c + 1) * cw] += _dot(a, w2_ref[:, c * cw:(c + 1) * cw])

    @pl.when(f == pl.num_programs(1) - 1)
    def _():
        gf = gf_ref[...]

        def body(i, carry):
            r = pl.ds(pl.multiple_of(i * NORM_ROWS, NORM_ROWS), NORM_ROWS)
            y = x_ref[r, :] + o_ref[r, :]
            if final_norm:
                y = _rmsnorm(y, gf)
            o_ref[r, :] = y
            return carry

        lax.fori_loop(0, tm // NORM_ROWS, body, 0, unroll=NORM_UNROLL)


def _ffn(x, g, w1, w2, gf, *, layer, tm, tf, final_norm):
    n, d = x.shape
    ff = w1.shape[1]
    need = tm * d * 4 + tm * d * 2 + 2 * tm * d * 4 + 4 * d * tf * 2 + tm * tf * 6 + tm * d * 4
    kernel = functools.partial(_ffn_kernel, final_norm=final_norm)
    return pl.pallas_call(
        kernel,
        out_shape=jax.ShapeDtypeStruct((n, d), F32),
        grid=(n // tm, ff // tf),
        in_specs=[
            pl.BlockSpec((tm, d), lambda i, f: (i, 0), pipeline_mode=pl.Buffered(1)),
            pl.BlockSpec((None, 1, d), lambda i, f: (layer, 0, 0)),
            pl.BlockSpec((d, tf), lambda i, f: (0, f)),
            pl.BlockSpec((tf, d), lambda i, f: (f, 0)),
            pl.BlockSpec((1, d), lambda i, f: (0, 0)),
        ],
        out_specs=pl.BlockSpec((tm, d), lambda i, f: (i, 0)),
        scratch_shapes=[pltpu.VMEM((tm, d), BF16)],
        compiler_params=pltpu.CompilerParams(
            dimension_semantics=("parallel", "arbitrary"),
            vmem_limit_bytes=_vmem_limit(need)),
        name="ffn",
    )(x, g, w1, w2, gf)


def _trunk(x, st_h, st_r, st_i, wts, ffn_w=None):
    batch, seq, d = x.shape
    n = batch * seq
    depth = wts["w_in"].shape[0]
    heads = st_h.shape[2]
    hgrn_width = heads * HEAD_DIM
    groups, nst = st_r.shape[2], st_r.shape[3]
    s5_width = wts["w_glu"].shape[1]
    tm = min(1024, n)
    xf = x.reshape(n, d)
    x0r = st_r.reshape(depth, batch, groups * nst)
    x0i = st_i.reshape(depth, batch, groups * nst)
    time_major = seq % tm == 0
    new_h = jnp.zeros(st_h.shape, F32)
    new_r = jnp.zeros(x0r.shape, F32)
    new_i = jnp.zeros(x0i.shape, F32)
    used_w = []
    for l in range(depth):
        cast = ffn_w is None
        p, u = _inproj(xf, wts["norm1_g"], wts["w_in"], wts["lb_logits"],
                       layer=l, tm=tm, tn=s5_width, batch=batch, seq=seq)
        o_h, new_h, *w1 = _hgrn(p, wts["onorm_g"], st_h, new_h, batch=batch, seq=seq, layer=l,
                                cast_src=wts["w_ff1"] if cast else None)
        if not time_major:
            u = u.reshape(batch, seq, s5_width).transpose(1, 0, 2).reshape(seq, batch * s5_width)
        o_s, new_r, new_i, *w2 = _s5(u, wts["s5_tb"], wts["s5_tcc"], wts["s5_tg"], wts["s5_lam2"],
                                     wts["w_glu"], wts["b_glu"], x0r, x0i, new_r, new_i,
                                     batch=batch, layer=l, cast_src=wts["w_ff2"] if cast else None)
        if not time_major:
            o_s = o_s.reshape(seq, batch, s5_width).transpose(1, 0, 2).reshape(n, s5_width)
        used_w.append((w1[0], w2[0]) if cast else ffn_w[l])
        x1 = _outproj(xf, o_h, o_s, wts["w_out"], layer=l, tm=min(512, n), tn=d, seq=seq)
        xf = _ffn(x1, wts["norm2_g"], *used_w[l], wts["final_g"],
                  layer=l, tm=tm, tf=1024, final_norm=(l == depth - 1))
    return (xf.reshape(batch, seq, d), new_h, new_r.reshape(st_r.shape), new_i.reshape(st_i.shape)), used_w


def kernel(x_prompt, x_sample, state_hgrn, state_s5_re, state_s5_im, norm1_g, w_in, hgrn_lb_logits,
           hgrn_onorm_g, s5_lambda_re, s5_lambda_im, s5_log_step, s5_B_re, s5_B_im, s5_C_re, s5_C_im,
           s5_D, s5_w_glu, s5_b_glu, w_out, norm2_g, w_ff1, w_ff2, final_norm_g):
    depth, d = norm1_g.shape
    row3 = lambda a: a.reshape(depth, 1, a.shape[-1])
    lbr, lbi, zoh_r, zoh_i = _s5_discretise(s5_lambda_re, s5_lambda_im, s5_log_step)
    s5_tb, s5_tcc, s5_tg, s5_lam2 = _s5_pair_weights(lbr, lbi, zoh_r, zoh_i, s5_B_re, s5_B_im,
                                                     s5_C_re, s5_C_im, s5_D)
    wts = {
        "norm1_g": row3(norm1_g), "norm2_g": row3(norm2_g), "final_g": final_norm_g.reshape(1, d),
        "lb_logits": hgrn_lb_logits, "onorm_g": row3(hgrn_onorm_g),
        "w_in": w_in.astype(BF16), "w_out": w_out.astype(BF16),
        "w_ff1": w_ff1, "w_ff2": w_ff2,
        "w_glu": s5_w_glu.astype(BF16), "b_glu": row3(s5_b_glu),
        "s5_tb": s5_tb, "s5_tcc": s5_tcc, "s5_tg": s5_tg, "s5_lam2": s5_lam2,
    }
    bp = x_prompt.shape[0]
    zh = jnp.zeros((depth, bp) + state_hgrn.shape[2:], F32)
    zs = jnp.zeros((depth, bp) + state_s5_re.shape[2:], F32)
    (y_p, hp, rp, ip), ffn_w = _trunk(x_prompt, zh, zs, zs, wts)
    (y_s, hs, rs, is_), _ = _trunk(x_sample, state_hgrn, state_s5_re, state_s5_im, wts, ffn_w)
    return (y_p, y_s, hp, rp, ip, hs, rs, is_)
```

```python
import functools

import jax
import jax.numpy as jnp
from jax import lax
from jax.experimental import pallas as pl
from jax.experimental.pallas import tpu as pltpu

F32 = jnp.float32
BF16 = jnp.bfloat16
EPS = 1e-6

HEAD_DIM = 128
HGRN_CHUNK = 64
HGRN_SUB = 16
HGRN_HEADS_PER_STEP = 4
SUBLANES = 8
LANES = 128
S5_SCAN_LANES = 512
S5_ROWS = 512

V7X_VMEM_CAP = 56 * 1024 * 1024
NORM_ROWS = 64
NORM_UNROLL = 4
FFN_ACC_CHUNKS = 4
INPROJ_CHUNKS = 4


def _vmem_limit(nbytes):
    return int(min(V7X_VMEM_CAP, nbytes * 5 // 4 + (4 << 20)))


def _rmsnorm(x, g):
    return x * lax.rsqrt(jnp.mean(jnp.square(x), axis=-1, keepdims=True) + EPS) * g


def _sigmoid(x):
    return 1.0 / (1.0 + jnp.exp(-x))


def _log1p_exp_neg_abs(x):
    return jnp.log(1.0 + jnp.exp(-jnp.abs(x)))


def _softplus(x):
    return jnp.maximum(x, 0.0) + _log1p_exp_neg_abs(x)


def _dot(a, b):
    return jnp.dot(a, b, preferred_element_type=F32)


def _dot_nt(a, b):
    return lax.dot_general(a, b, (((1,), (1,)), ((), ())), preferred_element_type=F32)


def _inproj_kernel(x_ref, g_ref, w_ref, lbl_ref, p_ref, u_ref, h_ref, *, layer):
    tm = x_ref.shape[0]
    cw = w_ref.shape[1] // INPROJ_CHUNKS
    j = pl.program_id(1)

    @pl.when(j == 0)
    def _():
        g = g_ref[...]

        def body(i, carry):
            r = pl.ds(pl.multiple_of(i * NORM_ROWS, NORM_ROWS), NORM_ROWS)
            h_ref[r, :] = _rmsnorm(x_ref[r, :], g).astype(BF16)
            return carry

        lax.fori_loop(0, tm // NORM_ROWS, body, 0, unroll=NORM_UNROLL)

    def emit(out_ref, fn):
        for c in range(INPROJ_CHUNKS):
            sl = slice(c * cw, (c + 1) * cw)
            out_ref[:, sl] = fn(_dot(h_ref[...], w_ref[:, sl]), sl)

    def silu(a, sl):
        return a * _sigmoid(a)

    def log_forget(z, sl):
        logits = lbl_ref[:, sl]
        e = jnp.exp(logits - jnp.max(logits, axis=0, keepdims=True))
        sm = e / jnp.sum(e, axis=0, keepdims=True)
        cum0 = sm[0:1]
        cuml = cum0
        for k in range(1, layer + 1):
            cuml = cuml + sm[k:k + 1]
        lb = cuml - cum0
        log_lb = jnp.log(lb)
        b = jnp.log1p(-lb) - _softplus(-z)
        delta = log_lb - b
        return jnp.where(jnp.isnan(delta), log_lb + b,
                         jnp.maximum(log_lb, b) + _log1p_exp_neg_abs(delta))

    for tile, (out_ref, fn) in enumerate([(p_ref, silu), (p_ref, log_forget), (p_ref, lambda a, sl: a),
                                          (p_ref, silu), (u_ref, lambda a, sl: a)]):
        pl.when(j == tile)(functools.partial(emit, out_ref, fn))


def _inproj(x, g, w, lb_logits, *, layer, w_layer, tm, tn, batch, seq):
    n, d = x.shape
    cols = w.shape[2]
    n_j = cols // tn
    depth = lb_logits.shape[0]
    assert n_j == 5 and lb_logits.shape[1] == tn
    time_major = seq % tm == 0
    if time_major:
        n_t = seq // tm
        u_shape, u_map = (seq, batch * tn), (lambda i, j: (i % n_t, i // n_t))
    else:
        u_shape, u_map = (n, tn), (lambda i, j: (i, 0))
    need = 2 * tm * d * 4 + tm * d * 2 + 2 * d * tn * 2 + 5 * tm * tn * 4
    return pl.pallas_call(
        functools.partial(_inproj_kernel, layer=layer),
        out_shape=(jax.ShapeDtypeStruct((n, cols - tn), F32), jax.ShapeDtypeStruct(u_shape, F32)),
        grid=(n // tm, n_j),
        in_specs=[
            pl.BlockSpec((tm, d), lambda i, j: (i, 0)),
            pl.BlockSpec((None, 1, d), lambda i, j: (layer, 0, 0)),
            pl.BlockSpec((None, d, tn), lambda i, j: (w_layer, 0, j)),
            pl.BlockSpec((depth, tn), lambda i, j: (0, 0)),
        ],
        out_specs=(pl.BlockSpec((tm, tn), lambda i, j: (i, jnp.minimum(j, n_j - 2))),
                   pl.BlockSpec((tm, tn), u_map)),
        scratch_shapes=[pltpu.VMEM((tm, d), BF16)],
        compiler_params=pltpu.CompilerParams(
            dimension_semantics=("parallel", "arbitrary"),
            vmem_limit_bytes=_vmem_limit(need)),
        name="inproj",
    )(x, g, w, lb_logits)


def _hgrn_decayed_queries(q_lo, q_hi, f_row, row_is):
    slabs = []
    qd_lo = qd_hi = None
    for s in range(2 * SUBLANES - 1, -1, -1):
        if s == 2 * SUBLANES - 1:
            qd_hi = jnp.where(row_is[SUBLANES - 1], q_hi, 0.0)
        elif s >= SUBLANES:
            qd_hi = jnp.where(row_is[s - SUBLANES], q_hi, qd_hi * f_row(s + 1))
        else:
            fn = f_row(s + 1)
            qd_hi = qd_hi * fn
            qd_lo = jnp.where(row_is[s], q_lo, 0.0 if s == SUBLANES - 1 else qd_lo * fn)
        slabs.append((s, 1, qd_hi))
        if s < SUBLANES:
            slabs.append((s, 0, qd_lo))
    return slabs


def _split_cast_refs(refs, n_out, n_cast):
    srcs, rest = refs[:n_cast], refs[n_cast:]
    return rest[:n_out] + rest[n_out + n_cast:], srcs, rest[n_out:n_out + n_cast]


def _hgrn_kernel(q_ref, f_ref, i_ref, g_ref, on_ref, s0_ref, acc_ref, *refs, chunk, heads_blk, n_cast):
    del acc_ref
    refs, cast_srcs, cast_dsts = _split_cast_refs(refs, 2, n_cast)
    o_ref, sout_ref, st_ref, ks_ref, cs_ref, fs_ref, fc_ref, op_ref = refs
    for src, dst in zip(cast_srcs, cast_dsts):
        dst[...] = src[...].astype(BF16)
    c_idx = pl.program_id(2)
    tc = q_ref.shape[0]
    n_sub = chunk // HGRN_SUB

    @pl.when(c_idx == 0)
    def _():
        for h in range(heads_blk):
            st_ref[h] = s0_ref[0, h].T

    logf = f_ref[...]
    fgate = jnp.exp(logf)
    fs_ref[...] = fgate
    ks_ref[...] = 1.0 - fgate

    t_i = lax.broadcasted_iota(jnp.int32, (chunk, chunk), 0)
    s_i = lax.broadcasted_iota(jnp.int32, (chunk, chunk), 1)
    tri = jnp.where(t_i >= s_i, 1.0, 0.0).astype(BF16)
    for j in range(tc // chunk):
        lf = logf[j * chunk:(j + 1) * chunk]
        hi = lf.astype(BF16)
        r1 = lf - hi.astype(F32)
        mid = r1.astype(BF16)
        lo = (r1 - mid.astype(F32)).astype(BF16)
        cs_ref[j * chunk:(j + 1) * chunk, :] = _dot(tri, hi) + _dot(tri, mid) + _dot(tri, lo)

    row8 = lax.broadcasted_iota(jnp.int32, (SUBLANES, HEAD_DIM), 0)
    row_is = [row8 == j for j in range(SUBLANES)]
    lane8 = lax.broadcasted_iota(jnp.int32, (SUBLANES, chunk), 1)
    lane_is = [lane8 == j for j in range(chunk)]
    onorm = on_ref[...]
    blk = lambda a, j: a[j * HGRN_SUB:(j + 1) * HGRN_SUB]

    head_cols = [slice(h * HEAD_DIM, (h + 1) * HEAD_DIM) for h in range(heads_blk)]

    def finish(rows):
        for cols in head_cols:
            o = op_ref[:, cols]
            o = o * lax.rsqrt(jnp.mean(jnp.square(o), axis=-1, keepdims=True) + EPS)
            o = o * onorm[:, cols]
            o_ref[rows, cols] = (o * g_ref[rows, cols]).astype(BF16)

    op_ref[...] = jnp.zeros(op_ref.shape, F32)

    def chunk_body(c, carry):
        base = pl.multiple_of(c * chunk, chunk)
        r = pl.ds(base, chunk)
        finish(pl.ds(pl.multiple_of(jnp.maximum(c - 1, 0) * chunk, chunk), chunk))
        fc_ref[...] = fs_ref[r, :]
        cum_all = cs_ref[r, :]

        zero_blk = jnp.zeros((HGRN_SUB, HEAD_DIM), F32)
        qs, vbs, kbs, o_inter, a_off = [], [], [], [], []
        for cols in head_cols:
            h = len(qs)
            cum = cum_all[:, cols]
            q = q_ref[r, cols]
            k = ks_ref[r, cols]
            v = i_ref[r, cols]
            ends = [cum[(j + 1) * HGRN_SUB - 1:(j + 1) * HGRN_SUB, :] for j in range(n_sub)]
            cl = ends[-1]
            ke = [blk(k, j) * jnp.exp(ends[j] - blk(cum, j)) for j in range(n_sub)]
            qe = [blk(q, j) * jnp.exp(blk(cum, j) - ends[j - 1] if j else blk(cum, j)) for j in range(n_sub)]

            st = st_ref[h]
            q_in = jnp.concatenate([qe[j] * jnp.exp(ends[j - 1]) if j else qe[j] for j in range(n_sub)], axis=0)
            o_inter.append(_dot_nt(q_in.astype(BF16), st.astype(BF16)))
            kdec = jnp.concatenate(
                [ke[j] * jnp.exp(cl - ends[j]) if j < n_sub - 1 else ke[j] for j in range(n_sub)], axis=0)
            st_ref[h] = st * jnp.exp(cl) + _dot(v.T.astype(BF16), kdec.astype(BF16))

            offs = [None]
            for i in range(1, n_sub):
                kt = jnp.concatenate(
                    [ke[j] * jnp.exp(ends[i - 1] - ends[j]) if j < i - 1 else ke[j] for j in range(i)]
                    + [zero_blk] * (n_sub - i), axis=0)
                offs.append(_dot_nt(qe[i].astype(BF16), kt.astype(BF16)))
            a_off.append(offs)
            qs.append(q)
            vbs.append(v.astype(BF16))
            kbs.append(k.astype(BF16))

        res = []
        for h, cols in enumerate(head_cols):
            per_head = []
            for i in range(n_sub):
                lo_r = i * HGRN_SUB
                f_row = lambda s, lo_r=lo_r, cols=cols: fc_ref[lo_r + s:lo_r + s + 1, cols]
                q_blk = blk(qs[h], i)
                slabs = _hgrn_decayed_queries(q_blk[:SUBLANES], q_blk[SUBLANES:], f_row, row_is)
                stack = jnp.concatenate([slab for _, _, slab in slabs], axis=0)
                per_head.append(([(s, half) for s, half, _ in slabs],
                                 _dot_nt(stack.astype(BF16), kbs[h])))
            res.append(per_head)

        o_intra = []
        for h in range(heads_blk):
            rows = []
            for i in range(n_sub):
                order, prod = res[h][i]
                halves = [jnp.zeros((SUBLANES, chunk), F32), jnp.zeros((SUBLANES, chunk), F32)]
                for idx, (s, half) in enumerate(order):
                    piece = prod[idx * SUBLANES:(idx + 1) * SUBLANES]
                    halves[half] = jnp.where(lane_is[i * HGRN_SUB + s], piece, halves[half])
                a = jnp.concatenate(halves, axis=0)
                rows.append(a + a_off[h][i] if i else a)
            attn = jnp.concatenate(rows, axis=0)
            o_intra.append(_dot(attn.astype(BF16), vbs[h]))

        for h, cols in enumerate(head_cols):
            op_ref[:, cols] = o_inter[h] + o_intra[h]
        return carry

    lax.fori_loop(0, tc // chunk, chunk_body, 0, unroll=min(2, tc // chunk))
    finish(pl.ds(tc - chunk, chunk))

    @pl.when(c_idx == pl.num_programs(2) - 1)
    def _():
        for h in range(heads_blk):
            sout_ref[0, h] = st_ref[h].T


def _cast_job(cast_src, layer, steps, step_of, axis):
    _, r, c = cast_src.shape
    if axis == 2:
        assert c % (steps * LANES) == 0
        blk = (r, c // steps)
        at = lambda *g: (0, step_of(*g))
    else:
        assert r % (steps * 2 * SUBLANES) == 0
        blk = (r // steps, c)
        at = lambda *g: (step_of(*g), 0)
    in_spec = pl.BlockSpec((None,) + blk, lambda *g: (layer,) + at(*g))
    return in_spec, pl.BlockSpec(blk, at), jax.ShapeDtypeStruct((r, c), BF16), 2 * blk[0] * blk[1] * 6


def _cast_job_layers(cast_src, first, steps, step_of):
    depth, r, c = cast_src.shape
    count = depth - first
    per_layer = steps // count
    assert steps % count == 0 and r % (per_layer * 2 * SUBLANES) == 0
    blk = (None, r // per_layer, c)
    in_spec = pl.BlockSpec(blk, lambda *g: (first + step_of(*g) // per_layer, step_of(*g) % per_layer, 0))
    out_spec = pl.BlockSpec(blk, lambda *g: (step_of(*g) // per_layer, step_of(*g) % per_layer, 0))
    return in_spec, out_spec, jax.ShapeDtypeStruct((count, r, c), BF16), 2 * blk[1] * c * 6


def _hgrn(p, onorm_g, s0, s_acc, *, batch, seq, layer, cast_src=None, cast_layers=()):
    n = p.shape[0]
    heads = s0.shape[2]
    hb = HGRN_HEADS_PER_STEP
    wblk = hb * HEAD_DIM
    n_hb = heads // hb
    width = heads * HEAD_DIM
    tc = min(512, seq)
    chunk = min(HGRN_CHUNK, seq)
    n_t = seq // tc
    row = lambda b, h, c: b * n_t + c
    need = 2 * 4 * tc * wblk * 4 + 2 * tc * wblk * 2 + 3 * tc * wblk * 4 + 5 * hb * HEAD_DIM * HEAD_DIM * 4
    n_cast = (cast_src is not None) + len(cast_layers)
    kernel = functools.partial(_hgrn_kernel, chunk=chunk, heads_blk=hb, n_cast=n_cast)
    st_spec = pl.BlockSpec((None, 1, hb, HEAD_DIM, HEAD_DIM), lambda b, h, c: (layer, b, h, 0, 0))
    in_specs = [
        pl.BlockSpec((tc, wblk), lambda b, h, c: (row(b, h, c), h)),
        pl.BlockSpec((tc, wblk), lambda b, h, c: (row(b, h, c), n_hb + h)),
        pl.BlockSpec((tc, wblk), lambda b, h, c: (row(b, h, c), 2 * n_hb + h)),
        pl.BlockSpec((tc, wblk), lambda b, h, c: (row(b, h, c), 3 * n_hb + h)),
        pl.BlockSpec((None, 1, wblk), lambda b, h, c: (layer, 0, h)),
        st_spec,
        pl.BlockSpec(memory_space=pl.ANY),
    ]
    out_specs = [pl.BlockSpec((tc, wblk), lambda b, h, c: (row(b, h, c), h)), st_spec]
    out_shape = [jax.ShapeDtypeStruct((n, width), BF16), jax.ShapeDtypeStruct(s_acc.shape, F32)]
    args = [p, p, p, p, onorm_g, s0, s_acc]
    steps = batch * n_hb * n_t
    step_of = lambda b, h, c: (b * n_hb + h) * n_t + c
    jobs = [] if cast_src is None else [(cast_src, _cast_job(cast_src, layer, steps, step_of, axis=2))]
    jobs += [(src, _cast_job_layers(src, first, steps, step_of)) for src, first in cast_layers]
    for src, (c_in, c_out, c_shape, c_bytes) in jobs:
        in_specs.append(c_in)
        out_specs.append(c_out)
        out_shape.append(c_shape)
        args.append(src)
        need += c_bytes
    return pl.pallas_call(
        kernel,
        out_shape=tuple(out_shape),
        grid=(batch, n_hb, n_t),
        in_specs=in_specs,
        out_specs=tuple(out_specs),
        input_output_aliases={6: 1},
        scratch_shapes=[
            pltpu.VMEM((hb, HEAD_DIM, HEAD_DIM), F32),
            pltpu.VMEM((tc, wblk), F32),
            pltpu.VMEM((tc, wblk), F32),
            pltpu.VMEM((tc, wblk), F32),
            pltpu.VMEM((chunk, wblk), F32),
            pltpu.VMEM((chunk, wblk), F32),
        ],
        compiler_params=pltpu.CompilerParams(
            dimension_semantics=("parallel", "parallel", "arbitrary"),
            vmem_limit_bytes=max(V7X_VMEM_CAP, _vmem_limit(need))),
        name="hgrn",
    )(*args)


def _s5_disc_lambda_kernel(lr_ref, li_ref, ls_ref, lbr_ref, lbi_ref, cr_ref, ci_ref):
    lr, li = lr_ref[...], li_ref[...]
    dt = jnp.exp(ls_ref[...])
    mag = jnp.exp(dt * lr)
    ang = dt * li
    lbr = mag * jnp.cos(ang)
    lbi = mag * jnp.sin(ang)
    nr, ni = lbr - 1.0, lbi
    den = lr * lr + li * li
    lbr_ref[...] = lbr
    lbi_ref[...] = lbi
    cr_ref[...] = (nr * lr + ni * li) / den
    ci_ref[...] = (ni * lr - nr * li) / den


def _s5_discretise(lam_re, lam_im, log_step):
    depth, groups, nst = lam_re.shape
    rows = depth * groups
    shp = jax.ShapeDtypeStruct((rows, nst), F32)
    outs = pl.pallas_call(
        _s5_disc_lambda_kernel, out_shape=(shp, shp, shp, shp), name="s5_disc_lambda",
    )(lam_re.reshape(rows, nst), lam_im.reshape(rows, nst), log_step.reshape(rows, 1))
    return tuple(o.reshape(depth, groups, nst) for o in outs)


def _s5_pair_weights(lbr, lbi, zoh_r, zoh_i, b_re, b_im, c_re, c_im, d_skip):
    depth, groups, nst, cg = b_re.shape
    hi = lax.Precision.HIGHEST
    zr, zi = zoh_r[..., None], zoh_i[..., None]
    bbr, bbi = zr * b_re - zi * b_im, zr * b_im + zi * b_re
    lr, li = lbr[..., None], lbi[..., None]
    blr, bli = bbr * lr - bbi * li, bbr * li + bbi * lr
    tb = jnp.swapaxes(jnp.stack([blr, bli, bbr, bbi], axis=1), 3, 4).reshape(depth, 4, groups * cg, nst)

    lrc, lic = lbr[:, :, None, :], lbi[:, :, None, :]
    pr, pi = c_re * lrc - c_im * lic, c_re * lic + c_im * lrc
    qr, qi = pr * lrc - pi * lic, pr * lic + pi * lrc
    tcc = jnp.stack([pr, -pi, qr, -qi], axis=1).transpose(0, 1, 3, 2, 4).reshape(depth, 4, cg, groups * nst)

    mm = lambda a, b: jnp.einsum("lgcn,lgnd->lgcd", a, b, precision=hi)
    g0 = mm(c_re, bbr) - mm(c_im, bbi)
    g1 = mm(pr, bbr) - mm(pi, bbi)
    g0d = g0 + d_skip.reshape(depth, groups, cg)[..., None] * jnp.eye(cg, dtype=F32)
    tg = jnp.stack([g0d, g1], axis=1).transpose(0, 1, 3, 2, 4).reshape(depth, 2, cg, groups * cg)
    lam2 = jnp.stack([lbr * lbr - lbi * lbi, 2.0 * lbr * lbi], axis=1).reshape(depth, 2, groups * nst)
    return tb, tcc, tg, lam2


def _s5_embed_weights(tb_ref, tcc_ref, tg_ref, wb_ref, wct_ref):
    nbk = wb_ref.shape[0]
    cg, nst = tcc_ref.shape[1], tb_ref.shape[2]
    bl = wb_ref.shape[2] // 2
    assert cg & (cg - 1) == 0 and nst & (nst - 1) == 0
    c_shift, n_shift = cg.bit_length() - 1, nst.bit_length() - 1
    iota = lambda shape, dim: lax.broadcasted_iota(jnp.int32, shape, dim)
    e_in = jnp.where((iota((nst, bl), 1) & (nst - 1)) == iota((nst, bl), 0), 1.0, 0.0).astype(BF16)
    e_out = jnp.where((iota((LANES, cg), 0) & (cg - 1)) == iota((LANES, cg), 1), 1.0, 0.0).astype(BF16)
    m_state = (iota((LANES, bl), 0) >> c_shift) == (iota((LANES, bl), 1) >> n_shift)
    m_direct = (iota((LANES, LANES), 0) >> c_shift) == (iota((LANES, LANES), 1) >> c_shift)
    zeros = jnp.zeros((LANES, LANES), BF16)
    for j in range(nbk):
        for k in range(4):
            half, part = divmod(k, 2)
            a = tb_ref[k, j * LANES:(j + 1) * LANES, :].astype(BF16)
            wb_ref[j, half * LANES:(half + 1) * LANES, part * bl:(part + 1) * bl] = (
                jnp.where(m_state, _dot(a, e_in), 0.0).astype(BF16))
        for k in range(4):
            eo, part = divmod(k, 2)
            x = tcc_ref[k, :, j * bl:(j + 1) * bl].astype(BF16)
            wct_ref[j, eo * LANES:(eo + 1) * LANES, part * bl:(part + 1) * bl] = (
                jnp.where(m_state, _dot(e_out, x), 0.0).astype(BF16))
        direct = [jnp.where(m_direct, _dot(e_out, tg_ref[k, :, j * LANES:(j + 1) * LANES].astype(BF16)), 0.0
                            ).astype(BF16) for k in range(2)]
        base = 2 * bl
        wct_ref[j, 0:LANES, base:base + LANES] = direct[0]
        wct_ref[j, 0:LANES, base + LANES:base + 2 * LANES] = zeros
        wct_ref[j, LANES:2 * LANES, base:base + LANES] = direct[1]
        wct_ref[j, LANES:2 * LANES, base + LANES:base + 2 * LANES] = direct[0]


def _s5_kernel(u_ref, tb_ref, tcc_ref, tg_ref, lam2_ref, wg_ref, bg_ref, x0r_ref, x0i_ref, accr_ref, acci_ref,
               *refs, batch, with_cast):
    del accr_ref, acci_ref
    refs, cast_srcs, cast_dsts = _split_cast_refs(refs, 3, int(with_cast))
    o_ref, xr_ref, xi_ref, xs_ref, y_ref, st_ref, ue_ref, uo_ref, tmp_ref, wb_ref, wct_ref = refs
    for src, dst in zip(cast_srcs, cast_dsts):
        dst[...] = src[...].astype(BF16)
    c_idx = pl.program_id(0)
    tt = u_ref.shape[0]
    tp = tt // 2
    prows = tp * batch
    width = u_ref.shape[1] // batch
    n_slab = width // LANES
    nbk = wb_ref.shape[0]
    bl = wb_ref.shape[2] // 2
    ns = nbk * bl
    grp = st_ref.shape[1]
    per = grp // batch

    @pl.when(c_idx == 0)
    def _():
        st_ref[0] = jnp.concatenate([x0r_ref[...]] * per, axis=0)
        st_ref[1] = jnp.concatenate([x0i_ref[...]] * per, axis=0)
        _s5_embed_weights(tb_ref, tcc_ref, tg_ref, wb_ref, wct_ref)

    for b in range(batch):
        for m in range(n_slab):
            tmp_ref[...] = u_ref[:, b * width + m * LANES:b * width + (m + 1) * LANES]
            ue_ref[m, pl.ds(b, tp, stride=batch), :] = tmp_ref[pl.ds(0, tp, stride=2), :]
            uo_ref[m, pl.ds(b, tp, stride=batch), :] = tmp_ref[pl.ds(1, tp, stride=2), :]

    for j in range(nbk):
        lhs = jnp.concatenate([ue_ref[j], uo_ref[j]], axis=1).astype(BF16)
        w = _dot(lhs, wb_ref[j])
        xs_ref[:, j * bl:(j + 1) * bl] = w[:, :bl]
        xs_ref[:, ns + j * bl:ns + (j + 1) * bl] = w[:, bl:]

    second = lax.broadcasted_iota(jnp.int32, (grp, S5_SCAN_LANES), 0) >= batch
    for cb in range(ns // S5_SCAN_LANES):
        lo = cb * S5_SCAN_LANES
        re_l = slice(lo, lo + S5_SCAN_LANES)
        im_l = slice(ns + lo, ns + lo + S5_SCAN_LANES)
        ar = lam2_ref[0:1, re_l]
        ai = lam2_ref[1:2, re_l]

        def step(g, carry, ar=ar, ai=ai, re_l=re_l, im_l=im_l):
            cr, ci = carry
            r = pl.ds(pl.multiple_of(g * grp, grp), grp)
            wr = xs_ref[r, re_l]
            wi = xs_ref[r, im_l]
            if per == 2:
                tr = ar * cr - ai * ci + pltpu.roll(wr, batch, axis=0)
                ti = ar * ci + ai * cr + pltpu.roll(wi, batch, axis=0)
                xs_ref[r, re_l] = jnp.where(second, tr, cr)
                xs_ref[r, im_l] = jnp.where(second, ti, ci)
                vr = ar * tr - ai * ti + wr
                vi = ar * ti + ai * tr + wi
                nr = jnp.where(second, vr, pltpu.roll(vr, batch, axis=0))
                ni = jnp.where(second, vi, pltpu.roll(vi, batch, axis=0))
            else:
                xs_ref[r, re_l] = cr
                xs_ref[r, im_l] = ci
                nr = ar * cr - ai * ci + wr
                ni = ar * ci + ai * cr + wi
            return nr, ni

        cr, ci = lax.fori_loop(0, prows // grp, step, (st_ref[0, :, re_l], st_ref[1, :, re_l]), unroll=2)
        st_ref[0, :, re_l] = cr
        st_ref[1, :, re_l] = ci

    for j in range(nbk):
        lhs = jnp.concatenate([xs_ref[:, j * bl:(j + 1) * bl], xs_ref[:, ns + j * bl:ns + (j + 1) * bl],
                               ue_ref[j], uo_ref[j]], axis=1).astype(BF16)
        yj = _dot_nt(lhs, wct_ref[j])
        y_ref[0:prows, j * LANES:(j + 1) * LANES] = yj[:, :LANES]
        y_ref[prows:2 * prows, j * LANES:(j + 1) * LANES] = yj[:, LANES:]

    hh = jax.nn.gelu(y_ref[...])
    gate = _sigmoid(_dot(hh.astype(BF16), wg_ref[...]) + bg_ref[...])
    out = hh * gate
    for m in range(n_slab):
        ue_ref[m] = out[0:prows, m * LANES:(m + 1) * LANES]
        uo_ref[m] = out[prows:2 * prows, m * LANES:(m + 1) * LANES]
    for b in range(batch):
        for m in range(n_slab):
            tmp_ref[pl.ds(0, tp, stride=2), :] = ue_ref[m, pl.ds(b, tp, stride=batch), :]
            tmp_ref[pl.ds(1, tp, stride=2), :] = uo_ref[m, pl.ds(b, tp, stride=batch), :]
            o_ref[:, b * width + m * LANES:b * width + (m + 1) * LANES] = tmp_ref[...].astype(BF16)

    @pl.when(c_idx == pl.num_programs(0) - 1)
    def _():
        xr_ref[...] = st_ref[0, grp - batch:grp, :]
        xi_ref[...] = st_ref[1, grp - batch:grp, :]


def _s5(u, tb, tcc, tg, lam2, w_glu, b_glu, x0r, x0i, acc_r, acc_i, *, batch, layer, cast_src=None):
    seq = u.shape[0]
    width = u.shape[1] // batch
    n = seq * batch
    ns = lam2.shape[2]
    nbk = width // LANES
    bl = ns // nbk
    assert batch % SUBLANES == 0 or 2 * batch == SUBLANES
    assert seq % 2 == 0 and tb.shape[2] == width
    grp = max(batch, SUBLANES)
    rows = min(S5_ROWS, n)
    tt = rows // batch
    prows = rows // 2
    lay4 = lambda c: (layer, 0, 0, 0)
    lay3 = lambda c: (layer, 0, 0)
    st_in = pl.BlockSpec((None, batch, ns), lay3)
    resident = pl.Buffered(1)
    wb_shape = (nbk, 2 * LANES, 2 * bl)
    wct_shape = (nbk, 2 * LANES, 2 * bl + 2 * LANES)
    need = (2 * rows * width * 4 + (nbk * 2 * LANES * (4 * bl + 2 * LANES) + width * width) * 2
            + 4 * width * LANES * 4 + 6 * 16 * ns * 4
            + 2 * rows * width * 2 + prows * 2 * ns * 4 + rows * width * 4 + 4 * rows * width * 4
            + 4 * prows * (4 * bl + 2 * LANES) + (8 * batch + 2 * grp) * ns * 4)
    st_shape = jax.ShapeDtypeStruct(acc_r.shape, F32)
    in_specs = [
        pl.BlockSpec((tt, batch * width), lambda c: (c, 0)),
        pl.BlockSpec((None,) + tb.shape[1:], lay4, pipeline_mode=resident),
        pl.BlockSpec((None,) + tcc.shape[1:], lay4, pipeline_mode=resident),
        pl.BlockSpec((None,) + tg.shape[1:], lay4, pipeline_mode=resident),
        pl.BlockSpec((None, 2, ns), lay3),
        pl.BlockSpec((None, width, width), lay3, pipeline_mode=resident),
        pl.BlockSpec((None, 1, width), lay3),
        st_in,
        st_in,
        pl.BlockSpec(memory_space=pl.ANY),
        pl.BlockSpec(memory_space=pl.ANY),
    ]
    out_specs = [pl.BlockSpec((tt, batch * width), lambda c: (c, 0)), st_in, st_in]
    out_shape = [jax.ShapeDtypeStruct((seq, batch * width), BF16), st_shape, st_shape]
    args = [u, tb, tcc, tg, lam2, w_glu, b_glu, x0r, x0i, acc_r, acc_i]
    if cast_src is not None:
        c_in, c_out, c_shape, c_bytes = _cast_job(cast_src, layer, n // rows, lambda c: c, axis=1)
        in_specs.append(c_in)
        out_specs.append(c_out)
        out_shape.append(c_shape)
        args.append(cast_src)
        need += c_bytes
    return pl.pallas_call(
        functools.partial(_s5_kernel, batch=batch, with_cast=cast_src is not None),
        out_shape=tuple(out_shape),
        grid=(n // rows,),
        in_specs=in_specs,
        out_specs=tuple(out_specs),
        input_output_aliases={9: 1, 10: 2},
        scratch_shapes=[
            pltpu.VMEM((prows, 2 * ns), F32),
            pltpu.VMEM((rows, width), F32),
            pltpu.VMEM((2, grp, ns), F32),
            pltpu.VMEM((width // LANES, prows, LANES), F32),
            pltpu.VMEM((width // LANES, prows, LANES), F32),
            pltpu.VMEM((tt, LANES), F32),
            pltpu.VMEM(wb_shape, BF16),
            pltpu.VMEM(wct_shape, BF16),
        ],
        compiler_params=pltpu.CompilerParams(
            dimension_semantics=("arbitrary",),
            vmem_limit_bytes=_vmem_limit(need)),
        name="s5",
    )(*args)


def _outproj_kernel(x_ref, oh_ref, os_ref, wh_ref, ws_ref, o_ref):
    o_ref[...] = x_ref[...] + _dot(oh_ref[...], wh_ref[...]) + _dot(os_ref[...], ws_ref[...])


def _outproj(x, o_h, o_s, w_out, *, layer, tm, tn, seq):
    n, d = x.shape
    kh = o_h.shape[1]
    ks = w_out.shape[1] - kh
    assert kh == ks
    if o_s.shape[0] == n:
        os_map = lambda i, j: (i, 0)
    else:
        n_t = seq // tm
        os_map = lambda i, j: (i % n_t, i // n_t)
    need = 2 * (2 * tm * tn * 4 + tm * (kh + ks) * 2 + (kh + ks) * tn * 2) + 2 * tm * tn * 4
    return pl.pallas_call(
        _outproj_kernel,
        out_shape=jax.ShapeDtypeStruct((n, d), F32),
        grid=(n // tm, d // tn),
        in_specs=[
            pl.BlockSpec((tm, tn), lambda i, j: (i, j)),
            pl.BlockSpec((tm, kh), lambda i, j: (i, 0)),
            pl.BlockSpec((tm, ks), os_map),
            pl.BlockSpec((None, kh, tn), lambda i, j: (layer, 0, j)),
            pl.BlockSpec((None, ks, tn), lambda i, j: (layer, 1, j)),
        ],
        out_specs=pl.BlockSpec((tm, tn), lambda i, j: (i, j)),
        compiler_params=pltpu.CompilerParams(
            dimension_semantics=("parallel", "parallel"),
            vmem_limit_bytes=_vmem_limit(need)),
        name="outproj",
    )(x, o_h, o_s, w_out, w_out)


def _ffn_kernel(x_ref, g_ref, w1_ref, w2_ref, gf_ref, o_ref, h_ref, *, final_norm):
    f = pl.program_id(1)
    tm = x_ref.shape[0]

    @pl.when(f == 0)
    def _():
        g = g_ref[...]

        def body(i, carry):
            r = pl.ds(pl.multiple_of(i * NORM_ROWS, NORM_ROWS), NORM_ROWS)
            h_ref[r, :] = _rmsnorm(x_ref[r, :], g).astype(BF16)
            o_ref[r, :] = jnp.zeros((NORM_ROWS, o_ref.shape[1]), F32)
            return carry

        lax.fori_loop(0, tm // NORM_ROWS, body, 0, unroll=NORM_UNROLL)

    a = _dot(h_ref[...], w1_ref[...])
    a = jnp.square(jnp.maximum(a, 0.0)).astype(BF16)
    cw = o_ref.shape[1] // FFN_ACC_CHUNKS
    for c in range(FFN_ACC_CHUNKS):
        o_ref[:, c * cw:(c + 1) * cw] += _dot(a, w2_ref[:, c * cw:(c + 1) * cw])

    @pl.when(f == pl.num_programs(1) - 1)
    def _():
        gf = gf_ref[...]

        def body(i, carry):
            r = pl.ds(pl.multiple_of(i * NORM_ROWS, NORM_ROWS), NORM_ROWS)
            y = x_ref[r, :] + o_ref[r, :]
            if final_norm:
                y = _rmsnorm(y, gf)
            o_ref[r, :] = y
            return carry

        lax.fori_loop(0, tm // NORM_ROWS, body, 0, unroll=NORM_UNROLL)


def _ffn(x, g, w1, w2, gf, *, layer, tm, tf, final_norm):
    n, d = x.shape
    ff = w1.shape[1]
    need = tm * d * 4 + tm * d * 2 + 2 * tm * d * 4 + 4 * d * tf * 2 + tm * tf * 6 + tm * d * 4
    kernel = functools.partial(_ffn_kernel, final_norm=final_norm)
    return pl.pallas_call(
        kernel,
        out_shape=jax.ShapeDtypeStruct((n, d), F32),
        grid=(n // tm, ff // tf),
        in_specs=[
            pl.BlockSpec((tm, d), lambda i, f: (i, 0), pipeline_mode=pl.Buffered(1)),
            pl.BlockSpec((None, 1, d), lambda i, f: (layer, 0, 0)),
            pl.BlockSpec((d, tf), lambda i, f: (0, f)),
            pl.BlockSpec((tf, d), lambda i, f: (f, 0)),
            pl.BlockSpec((1, d), lambda i, f: (0, 0)),
        ],
        out_specs=pl.BlockSpec((tm, d), lambda i, f: (i, 0)),
        scratch_shapes=[pltpu.VMEM((tm, d), BF16)],
        compiler_params=pltpu.CompilerParams(
            dimension_semantics=("parallel", "arbitrary"),
            vmem_limit_bytes=_vmem_limit(need)),
        name="ffn",
    )(x, g, w1, w2, gf)


def _trunk(x, st_h, st_r, st_i, wts, cast_w=None):
    batch, seq, d = x.shape
    n = batch * seq
    depth = wts["w_in"].shape[0]
    cast = cast_w is None
    made = {"ffn": []} if cast else cast_w
    heads = st_h.shape[2]
    hgrn_width = heads * HEAD_DIM
    groups, nst = st_r.shape[2], st_r.shape[3]
    s5_width = wts["w_glu"].shape[1]
    tm = min(1024, n)
    xf = x.reshape(n, d)
    x0r = st_r.reshape(depth, batch, groups * nst)
    x0i = st_i.reshape(depth, batch, groups * nst)
    time_major = seq % tm == 0
    new_h = jnp.zeros(st_h.shape, F32)
    new_r = jnp.zeros(x0r.shape, F32)
    new_i = jnp.zeros(x0i.shape, F32)
    for l in range(depth):
        w_in, w_layer = (wts["w_in0"], 0) if l == 0 else (made["w_in_rest"], l - 1)
        p, u = _inproj(xf, wts["norm1_g"], w_in, wts["lb_logits"],
                       layer=l, w_layer=w_layer, tm=tm, tn=s5_width, batch=batch, seq=seq)
        hosted = [(wts["w_out"], 0)] + ([(wts["w_in"], 1)] if depth > 1 else []) if cast and l == 0 else []
        o_h, new_h, *w1 = _hgrn(p, wts["onorm_g"], st_h, new_h, batch=batch, seq=seq, layer=l,
                                cast_src=wts["w_ff1"] if cast else None, cast_layers=hosted)
        if hosted:
            made["w_out"] = w1[1]
            made["w_in_rest"] = w1[2] if depth > 1 else None
        if not time_major:
            u = u.reshape(batch, seq, s5_width).transpose(1, 0, 2).reshape(seq, batch * s5_width)
        o_s, new_r, new_i, *w2 = _s5(u, wts["s5_tb"], wts["s5_tcc"], wts["s5_tg"], wts["s5_lam2"],
                                     wts["w_glu"], wts["b_glu"], x0r, x0i, new_r, new_i,
                                     batch=batch, layer=l, cast_src=wts["w_ff2"] if cast else None)
        if not time_major:
            o_s = o_s.reshape(seq, batch, s5_width).transpose(1, 0, 2).reshape(n, s5_width)
        if cast:
            made["ffn"].append((w1[0], w2[0]))
        x1 = _outproj(xf, o_h, o_s, made["w_out"], layer=l, tm=min(512, n), tn=d, seq=seq)
        xf = _ffn(x1, wts["norm2_g"], *made["ffn"][l], wts["final_g"],
                  layer=l, tm=tm, tf=1024, final_norm=(l == depth - 1))
    return (xf.reshape(batch, seq, d), new_h, new_r.reshape(st_r.shape), new_i.reshape(st_i.shape)), made


def kernel(x_prompt, x_sample, state_hgrn, state_s5_re, state_s5_im, norm1_g, w_in, hgrn_lb_logits,
           hgrn_onorm_g, s5_lambda_re, s5_lambda_im, s5_log_step, s5_B_re, s5_B_im, s5_C_re, s5_C_im,
           s5_D, s5_w_glu, s5_b_glu, w_out, norm2_g, w_ff1, w_ff2, final_norm_g):
    depth, d = norm1_g.shape
    row3 = lambda a: a.reshape(depth, 1, a.shape[-1])
    lbr, lbi, zoh_r, zoh_i = _s5_discretise(s5_lambda_re, s5_lambda_im, s5_log_step)
    s5_tb, s5_tcc, s5_tg, s5_lam2 = _s5_pair_weights(lbr, lbi, zoh_r, zoh_i, s5_B_re, s5_B_im,
                                                     s5_C_re, s5_C_im, s5_D)
    wts = {
        "norm1_g": row3(norm1_g), "norm2_g": row3(norm2_g), "final_g": final_norm_g.reshape(1, d),
        "lb_logits": hgrn_lb_logits, "onorm_g": row3(hgrn_onorm_g),
        "w_in0": w_in[:1].astype(BF16),
        "w_in": w_in, "w_out": w_out, "w_ff1": w_ff1, "w_ff2": w_ff2,
        "w_glu": s5_w_glu.astype(BF16), "b_glu": row3(s5_b_glu),
        "s5_tb": s5_tb, "s5_tcc": s5_tcc, "s5_tg": s5_tg, "s5_lam2": s5_lam2,
    }
    bp = x_prompt.shape[0]
    zh = jnp.zeros((depth, bp) + state_hgrn.shape[2:], F32)
    zs = jnp.zeros((depth, bp) + state_s5_re.shape[2:], F32)
    (y_p, hp, rp, ip), cast_w = _trunk(x_prompt, zh, zs, zs, wts)
    (y_s, hs, rs, is_), _ = _trunk(x_sample, state_hgrn, state_s5_re, state_s5_im, wts, cast_w)
    return (y_p, y_s, hp, rp, ip, hs, rs, is_)
```

```python
import functools

import jax
import jax.numpy as jnp
from jax import lax
from jax.experimental import pallas as pl
from jax.experimental.pallas import tpu as pltpu

F32 = jnp.float32
BF16 = jnp.bfloat16
EPS = 1e-6

HEAD_DIM = 128
HGRN_CHUNK = 64
HGRN_SUB = 16
HGRN_HEADS_PER_STEP = 4
SUBLANES = 8
LANES = 128
S5_SCAN_LANES = 512
S5_ROWS = 512

V7X_VMEM_CAP = 56 * 1024 * 1024
NORM_ROWS = 64
NORM_UNROLL = 4
FFN_ACC_CHUNKS = 4
INPROJ_CHUNKS = 4


def _vmem_limit(nbytes):
    return int(min(V7X_VMEM_CAP, nbytes * 5 // 4 + (4 << 20)))


def _rmsnorm(x, g):
    return x * lax.rsqrt(jnp.mean(jnp.square(x), axis=-1, keepdims=True) + EPS) * g


def _sigmoid(x):
    return 1.0 / (1.0 + jnp.exp(-x))


def _log1p_exp_neg_abs(x):
    return jnp.log(1.0 + jnp.exp(-jnp.abs(x)))


def _softplus(x):
    return jnp.maximum(x, 0.0) + _log1p_exp_neg_abs(x)


def _dot(a, b):
    return jnp.dot(a, b, preferred_element_type=F32)


def _dot_nt(a, b):
    return lax.dot_general(a, b, (((1,), (1,)), ((), ())), preferred_element_type=F32)


def _inproj_kernel(x_ref, g_ref, w_ref, lbl_ref, p_ref, u_ref, h_ref, *, layer):
    tm = x_ref.shape[0]
    cw = w_ref.shape[1] // INPROJ_CHUNKS
    j = pl.program_id(1)

    @pl.when(j == 0)
    def _():
        g = g_ref[...]

        def body(i, carry):
            r = pl.ds(pl.multiple_of(i * NORM_ROWS, NORM_ROWS), NORM_ROWS)
            h_ref[r, :] = _rmsnorm(x_ref[r, :], g).astype(BF16)
            return carry

        lax.fori_loop(0, tm // NORM_ROWS, body, 0, unroll=NORM_UNROLL)

    def emit(out_ref, fn):
        for c in range(INPROJ_CHUNKS):
            sl = slice(c * cw, (c + 1) * cw)
            out_ref[:, sl] = fn(_dot(h_ref[...], w_ref[:, sl]), sl)

    def silu(a, sl):
        return a * _sigmoid(a)

    def log_forget(z, sl):
        logits = lbl_ref[:, sl]
        e = jnp.exp(logits - jnp.max(logits, axis=0, keepdims=True))
        sm = e / jnp.sum(e, axis=0, keepdims=True)
        cum0 = sm[0:1]
        cuml = cum0
        for k in range(1, layer + 1):
            cuml = cuml + sm[k:k + 1]
        lb = cuml - cum0
        log_lb = jnp.log(lb)
        b = jnp.log1p(-lb) - _softplus(-z)
        delta = log_lb - b
        return jnp.where(jnp.isnan(delta), log_lb + b,
                         jnp.maximum(log_lb, b) + _log1p_exp_neg_abs(delta))

    for tile, (out_ref, fn) in enumerate([(p_ref, silu), (p_ref, log_forget), (p_ref, lambda a, sl: a),
                                          (p_ref, silu), (u_ref, lambda a, sl: a)]):
        pl.when(j == tile)(functools.partial(emit, out_ref, fn))


def _inproj(x, g, w, lb_logits, *, layer, w_layer, tm, tn, batch, seq):
    n, d = x.shape
    cols = w.shape[2]
    n_j = cols // tn
    depth = lb_logits.shape[0]
    assert n_j == 5 and lb_logits.shape[1] == tn
    time_major = seq % tm == 0
    if time_major:
        n_t = seq // tm
        u_shape, u_map = (seq, batch * tn), (lambda i, j: (i % n_t, i // n_t))
    else:
        u_shape, u_map = (n, tn), (lambda i, j: (i, 0))
    need = 2 * tm * d * 4 + tm * d * 2 + 2 * d * tn * 2 + 5 * tm * tn * 4
    return pl.pallas_call(
        functools.partial(_inproj_kernel, layer=layer),
        out_shape=(jax.ShapeDtypeStruct((n, cols - tn), F32), jax.ShapeDtypeStruct(u_shape, F32)),
        grid=(n // tm, n_j),
        in_specs=[
            pl.BlockSpec((tm, d), lambda i, j: (i, 0)),
            pl.BlockSpec((None, 1, d), lambda i, j: (layer, 0, 0)),
            pl.BlockSpec((None, d, tn), lambda i, j: (w_layer, 0, j)),
            pl.BlockSpec((depth, tn), lambda i, j: (0, 0)),
        ],
        out_specs=(pl.BlockSpec((tm, tn), lambda i, j: (i, jnp.minimum(j, n_j - 2))),
                   pl.BlockSpec((tm, tn), u_map)),
        scratch_shapes=[pltpu.VMEM((tm, d), BF16)],
        compiler_params=pltpu.CompilerParams(
            dimension_semantics=("parallel", "arbitrary"),
            vmem_limit_bytes=_vmem_limit(need)),
        name="inproj",
    )(x, g, w, lb_logits)


def _cast_kernel(src_ref, dst_ref):
    dst_ref[...] = src_ref[...].astype(BF16)


def _cast_first_layer(w, *, rows=256):
    _, r, c = w.shape
    spec = pl.BlockSpec((None, rows, c), lambda i: (0, i, 0))
    return pl.pallas_call(
        _cast_kernel, out_shape=jax.ShapeDtypeStruct((1, r, c), BF16), grid=(r // rows,),
        in_specs=[spec], out_specs=spec,
        compiler_params=pltpu.CompilerParams(vmem_limit_bytes=_vmem_limit(2 * rows * c * 6)),
        name="cast_first_layer",
    )(w)


def _hgrn_decayed_queries(q_lo, q_hi, f_row, row_is):
    slabs = []
    qd_lo = qd_hi = None
    for s in range(2 * SUBLANES - 1, -1, -1):
        if s == 2 * SUBLANES - 1:
            qd_hi = jnp.where(row_is[SUBLANES - 1], q_hi, 0.0)
        elif s >= SUBLANES:
            qd_hi = jnp.where(row_is[s - SUBLANES], q_hi, qd_hi * f_row(s + 1))
        else:
            fn = f_row(s + 1)
            qd_hi = qd_hi * fn
            qd_lo = jnp.where(row_is[s], q_lo, 0.0 if s == SUBLANES - 1 else qd_lo * fn)
        slabs.append((s, 1, qd_hi))
        if s < SUBLANES:
            slabs.append((s, 0, qd_lo))
    return slabs


def _split_cast_refs(refs, n_out, n_cast):
    srcs, rest = refs[:n_cast], refs[n_cast:]
    return rest[:n_out] + rest[n_out + n_cast:], srcs, rest[n_out:n_out + n_cast]


def _hgrn_kernel(q_ref, f_ref, i_ref, g_ref, on_ref, s0_ref, acc_ref, *refs, chunk, heads_blk, n_cast):
    del acc_ref
    refs, cast_srcs, cast_dsts = _split_cast_refs(refs, 2, n_cast)
    o_ref, sout_ref, st_ref, ks_ref, cs_ref, fs_ref, fc_ref, op_ref = refs
    for src, dst in zip(cast_srcs, cast_dsts):
        dst[...] = src[...].astype(BF16)
    c_idx = pl.program_id(2)
    tc = q_ref.shape[0]
    n_sub = chunk // HGRN_SUB

    @pl.when(c_idx == 0)
    def _():
        for h in range(heads_blk):
            st_ref[h] = s0_ref[0, h].T

    logf = f_ref[...]
    fgate = jnp.exp(logf)
    fs_ref[...] = fgate
    ks_ref[...] = 1.0 - fgate

    t_i = lax.broadcasted_iota(jnp.int32, (chunk, chunk), 0)
    s_i = lax.broadcasted_iota(jnp.int32, (chunk, chunk), 1)
    tri = jnp.where(t_i >= s_i, 1.0, 0.0).astype(BF16)
    for j in range(tc // chunk):
        lf = logf[j * chunk:(j + 1) * chunk]
        hi = lf.astype(BF16)
        r1 = lf - hi.astype(F32)
        mid = r1.astype(BF16)
        lo = (r1 - mid.astype(F32)).astype(BF16)
        cs_ref[j * chunk:(j + 1) * chunk, :] = _dot(tri, hi) + _dot(tri, mid) + _dot(tri, lo)

    row8 = lax.broadcasted_iota(jnp.int32, (SUBLANES, HEAD_DIM), 0)
    row_is = [row8 == j for j in range(SUBLANES)]
    lane8 = lax.broadcasted_iota(jnp.int32, (SUBLANES, chunk), 1)
    lane_is = [lane8 == j for j in range(chunk)]
    onorm = on_ref[...]
    blk = lambda a, j: a[j * HGRN_SUB:(j + 1) * HGRN_SUB]

    head_cols = [slice(h * HEAD_DIM, (h + 1) * HEAD_DIM) for h in range(heads_blk)]

    def finish(rows):
        for cols in head_cols:
            o = op_ref[:, cols]
            o = o * lax.rsqrt(jnp.mean(jnp.square(o), axis=-1, keepdims=True) + EPS)
            o = o * onorm[:, cols]
            o_ref[rows, cols] = (o * g_ref[rows, cols]).astype(BF16)

    op_ref[...] = jnp.zeros(op_ref.shape, F32)

    def chunk_body(c, carry):
        base = pl.multiple_of(c * chunk, chunk)
        r = pl.ds(base, chunk)
        finish(pl.ds(pl.multiple_of(jnp.maximum(c - 1, 0) * chunk, chunk), chunk))
        fc_ref[...] = fs_ref[r, :]
        cum_all = cs_ref[r, :]

        zero_blk = jnp.zeros((HGRN_SUB, HEAD_DIM), F32)
        qs, vbs, kbs, o_inter, a_off = [], [], [], [], []
        for cols in head_cols:
            h = len(qs)
            cum = cum_all[:, cols]
            q = q_ref[r, cols]
            k = ks_ref[r, cols]
            v = i_ref[r, cols]
            ends = [cum[(j + 1) * HGRN_SUB - 1:(j + 1) * HGRN_SUB, :] for j in range(n_sub)]
            cl = ends[-1]
            ke = [blk(k, j) * jnp.exp(ends[j] - blk(cum, j)) for j in range(n_sub)]
            qe = [blk(q, j) * jnp.exp(blk(cum, j) - ends[j - 1] if j else blk(cum, j)) for j in range(n_sub)]

            st = st_ref[h]
            q_in = jnp.concatenate([qe[j] * jnp.exp(ends[j - 1]) if j else qe[j] for j in range(n_sub)], axis=0)
            o_inter.append(_dot_nt(q_in.astype(BF16), st.astype(BF16)))
            kdec = jnp.concatenate(
                [ke[j] * jnp.exp(cl - ends[j]) if j < n_sub - 1 else ke[j] for j in range(n_sub)], axis=0)
            st_ref[h] = st * jnp.exp(cl) + _dot(v.T.astype(BF16), kdec.astype(BF16))

            offs = [None]
            for i in range(1, n_sub):
                kt = jnp.concatenate(
                    [ke[j] * jnp.exp(ends[i - 1] - ends[j]) if j < i - 1 else ke[j] for j in range(i)]
                    + [zero_blk] * (n_sub - i), axis=0)
                offs.append(_dot_nt(qe[i].astype(BF16), kt.astype(BF16)))
            a_off.append(offs)
            qs.append(q)
            vbs.append(v.astype(BF16))
            kbs.append(k.astype(BF16))

        res = []
        for h, cols in enumerate(head_cols):
            per_head = []
            for i in range(n_sub):
                lo_r = i * HGRN_SUB
                f_row = lambda s, lo_r=lo_r, cols=cols: fc_ref[lo_r + s:lo_r + s + 1, cols]
                q_blk = blk(qs[h], i)
                slabs = _hgrn_decayed_queries(q_blk[:SUBLANES], q_blk[SUBLANES:], f_row, row_is)
                stack = jnp.concatenate([slab for _, _, slab in slabs], axis=0)
                per_head.append(([(s, half) for s, half, _ in slabs],
                                 _dot_nt(stack.astype(BF16), kbs[h])))
            res.append(per_head)

        o_intra = []
        for h in range(heads_blk):
            rows = []
            for i in range(n_sub):
                order, prod = res[h][i]
                halves = [jnp.zeros((SUBLANES, chunk), F32), jnp.zeros((SUBLANES, chunk), F32)]
                for idx, (s, half) in enumerate(order):
                    piece = prod[idx * SUBLANES:(idx + 1) * SUBLANES]
                    halves[half] = jnp.where(lane_is[i * HGRN_SUB + s], piece, halves[half])
                a = jnp.concatenate(halves, axis=0)
                rows.append(a + a_off[h][i] if i else a)
            attn = jnp.concatenate(rows, axis=0)
            o_intra.append(_dot(attn.astype(BF16), vbs[h]))

        for h, cols in enumerate(head_cols):
            op_ref[:, cols] = o_inter[h] + o_intra[h]
        return carry

    lax.fori_loop(0, tc // chunk, chunk_body, 0, unroll=min(2, tc // chunk))
    finish(pl.ds(tc - chunk, chunk))

    @pl.when(c_idx == pl.num_programs(2) - 1)
    def _():
        for h in range(heads_blk):
            sout_ref[0, h] = st_ref[h].T


def _cast_job(cast_src, layer, steps, step_of, axis):
    _, r, c = cast_src.shape
    if axis == 2:
        assert c % (steps * LANES) == 0
        blk = (r, c // steps)
        at = lambda *g: (0, step_of(*g))
    else:
        assert r % (steps * 2 * SUBLANES) == 0
        blk = (r // steps, c)
        at = lambda *g: (step_of(*g), 0)
    in_spec = pl.BlockSpec((None,) + blk, lambda *g: (layer,) + at(*g))
    return in_spec, pl.BlockSpec(blk, at), jax.ShapeDtypeStruct((r, c), BF16), 2 * blk[0] * blk[1] * 6


def _cast_job_layers(cast_src, first, steps, step_of):
    depth, r, c = cast_src.shape
    count = depth - first
    per_layer = steps // count
    assert steps % count == 0 and r % (per_layer * 2 * SUBLANES) == 0
    blk = (None, r // per_layer, c)
    in_spec = pl.BlockSpec(blk, lambda *g: (first + step_of(*g) // per_layer, step_of(*g) % per_layer, 0))
    out_spec = pl.BlockSpec(blk, lambda *g: (step_of(*g) // per_layer, step_of(*g) % per_layer, 0))
    return in_spec, out_spec, jax.ShapeDtypeStruct((count, r, c), BF16), 2 * blk[1] * c * 6


def _hgrn(p, onorm_g, s0, s_acc, *, batch, seq, layer, cast_src=None, cast_layers=()):
    n = p.shape[0]
    heads = s0.shape[2]
    hb = HGRN_HEADS_PER_STEP
    wblk = hb * HEAD_DIM
    n_hb = heads // hb
    width = heads * HEAD_DIM
    tc = min(512, seq)
    chunk = min(HGRN_CHUNK, seq)
    n_t = seq // tc
    row = lambda b, h, c: b * n_t + c
    need = 2 * 4 * tc * wblk * 4 + 2 * tc * wblk * 2 + 3 * tc * wblk * 4 + 5 * hb * HEAD_DIM * HEAD_DIM * 4
    n_cast = (cast_src is not None) + len(cast_layers)
    kernel = functools.partial(_hgrn_kernel, chunk=chunk, heads_blk=hb, n_cast=n_cast)
    st_spec = pl.BlockSpec((None, 1, hb, HEAD_DIM, HEAD_DIM), lambda b, h, c: (layer, b, h, 0, 0))
    in_specs = [
        pl.BlockSpec((tc, wblk), lambda b, h, c: (row(b, h, c), h)),
        pl.BlockSpec((tc, wblk), lambda b, h, c: (row(b, h, c), n_hb + h)),
        pl.BlockSpec((tc, wblk), lambda b, h, c: (row(b, h, c), 2 * n_hb + h)),
        pl.BlockSpec((tc, wblk), lambda b, h, c: (row(b, h, c), 3 * n_hb + h)),
        pl.BlockSpec((None, 1, wblk), lambda b, h, c: (layer, 0, h)),
        st_spec,
        pl.BlockSpec(memory_space=pl.ANY),
    ]
    out_specs = [pl.BlockSpec((tc, wblk), lambda b, h, c: (row(b, h, c), h)), st_spec]
    out_shape = [jax.ShapeDtypeStruct((n, width), BF16), jax.ShapeDtypeStruct(s_acc.shape, F32)]
    args = [p, p, p, p, onorm_g, s0, s_acc]
    steps = batch * n_hb * n_t
    step_of = lambda b, h, c: (b * n_hb + h) * n_t + c
    jobs = [] if cast_src is None else [(cast_src, _cast_job(cast_src, layer, steps, step_of, axis=2))]
    jobs += [(src, _cast_job_layers(src, first, steps, step_of)) for src, first in cast_layers]
    for src, (c_in, c_out, c_shape, c_bytes) in jobs:
        in_specs.append(c_in)
        out_specs.append(c_out)
        out_shape.append(c_shape)
        args.append(src)
        need += c_bytes
    return pl.pallas_call(
        kernel,
        out_shape=tuple(out_shape),
        grid=(batch, n_hb, n_t),
        in_specs=in_specs,
        out_specs=tuple(out_specs),
        input_output_aliases={6: 1},
        scratch_shapes=[
            pltpu.VMEM((hb, HEAD_DIM, HEAD_DIM), F32),
            pltpu.VMEM((tc, wblk), F32),
            pltpu.VMEM((tc, wblk), F32),
            pltpu.VMEM((tc, wblk), F32),
            pltpu.VMEM((chunk, wblk), F32),
            pltpu.VMEM((chunk, wblk), F32),
        ],
        compiler_params=pltpu.CompilerParams(
            dimension_semantics=("parallel", "parallel", "arbitrary"),
            vmem_limit_bytes=max(V7X_VMEM_CAP, _vmem_limit(need))),
        name="hgrn",
    )(*args)


def _s5_disc_lambda_kernel(lr_ref, li_ref, ls_ref, lbr_ref, lbi_ref, cr_ref, ci_ref):
    lr, li = lr_ref[...], li_ref[...]
    dt = jnp.exp(ls_ref[...])
    mag = jnp.exp(dt * lr)
    ang = dt * li
    lbr = mag * jnp.cos(ang)
    lbi = mag * jnp.sin(ang)
    nr, ni = lbr - 1.0, lbi
    den = lr * lr + li * li
    lbr_ref[...] = lbr
    lbi_ref[...] = lbi
    cr_ref[...] = (nr * lr + ni * li) / den
    ci_ref[...] = (ni * lr - nr * li) / den


def _s5_discretise(lam_re, lam_im, log_step):
    depth, groups, nst = lam_re.shape
    rows = depth * groups
    shp = jax.ShapeDtypeStruct((rows, nst), F32)
    outs = pl.pallas_call(
        _s5_disc_lambda_kernel, out_shape=(shp, shp, shp, shp), name="s5_disc_lambda",
    )(lam_re.reshape(rows, nst), lam_im.reshape(rows, nst), log_step.reshape(rows, 1))
    return tuple(o.reshape(depth, groups, nst) for o in outs)


def _s5_pair_weights(lbr, lbi, zoh_r, zoh_i, b_re, b_im, c_re, c_im, d_skip):
    depth, groups, nst, cg = b_re.shape
    hi = lax.Precision.HIGHEST
    zr, zi = zoh_r[..., None], zoh_i[..., None]
    bbr, bbi = zr * b_re - zi * b_im, zr * b_im + zi * b_re
    lr, li = lbr[..., None], lbi[..., None]
    blr, bli = bbr * lr - bbi * li, bbr * li + bbi * lr
    tb = jnp.swapaxes(jnp.stack([blr, bli, bbr, bbi], axis=1), 3, 4).reshape(depth, 4, groups * cg, nst)

    lrc, lic = lbr[:, :, None, :], lbi[:, :, None, :]
    pr, pi = c_re * lrc - c_im * lic, c_re * lic + c_im * lrc
    qr, qi = pr * lrc - pi * lic, pr * lic + pi * lrc
    tcc = jnp.stack([pr, -pi, qr, -qi], axis=1).transpose(0, 1, 3, 2, 4).reshape(depth, 4, cg, groups * nst)

    mm = lambda a, b: jnp.einsum("lgcn,lgnd->lgcd", a, b, precision=hi)
    g0 = mm(c_re, bbr) - mm(c_im, bbi)
    g1 = mm(pr, bbr) - mm(pi, bbi)
    g0d = g0 + d_skip.reshape(depth, groups, cg)[..., None] * jnp.eye(cg, dtype=F32)
    tg = jnp.stack([g0d, g1], axis=1).transpose(0, 1, 3, 2, 4).reshape(depth, 2, cg, groups * cg)
    lam2 = jnp.stack([lbr * lbr - lbi * lbi, 2.0 * lbr * lbi], axis=1).reshape(depth, 2, groups * nst)
    return tb, tcc, tg, lam2


def _s5_embed_weights(tb_ref, tcc_ref, tg_ref, wb_ref, wct_ref):
    nbk = wb_ref.shape[0]
    cg, nst = tcc_ref.shape[1], tb_ref.shape[2]
    bl = wb_ref.shape[2] // 2
    assert cg & (cg - 1) == 0 and nst & (nst - 1) == 0
    c_shift, n_shift = cg.bit_length() - 1, nst.bit_length() - 1
    iota = lambda shape, dim: lax.broadcasted_iota(jnp.int32, shape, dim)
    e_in = jnp.where((iota((nst, bl), 1) & (nst - 1)) == iota((nst, bl), 0), 1.0, 0.0).astype(BF16)
    e_out = jnp.where((iota((LANES, cg), 0) & (cg - 1)) == iota((LANES, cg), 1), 1.0, 0.0).astype(BF16)
    m_state = (iota((LANES, bl), 0) >> c_shift) == (iota((LANES, bl), 1) >> n_shift)
    m_direct = (iota((LANES, LANES), 0) >> c_shift) == (iota((LANES, LANES), 1) >> c_shift)
    zeros = jnp.zeros((LANES, LANES), BF16)
    for j in range(nbk):
        for k in range(4):
            half, part = divmod(k, 2)
            a = tb_ref[k, j * LANES:(j + 1) * LANES, :].astype(BF16)
            wb_ref[j, half * LANES:(half + 1) * LANES, part * bl:(part + 1) * bl] = (
                jnp.where(m_state, _dot(a, e_in), 0.0).astype(BF16))
        for k in range(4):
            eo, part = divmod(k, 2)
            x = tcc_ref[k, :, j * bl:(j + 1) * bl].astype(BF16)
            wct_ref[j, eo * LANES:(eo + 1) * LANES, part * bl:(part + 1) * bl] = (
                jnp.where(m_state, _dot(e_out, x), 0.0).astype(BF16))
        direct = [jnp.where(m_direct, _dot(e_out, tg_ref[k, :, j * LANES:(j + 1) * LANES].astype(BF16)), 0.0
                            ).astype(BF16) for k in range(2)]
        base = 2 * bl
        wct_ref[j, 0:LANES, base:base + LANES] = direct[0]
        wct_ref[j, 0:LANES, base + LANES:base + 2 * LANES] = zeros
        wct_ref[j, LANES:2 * LANES, base:base + LANES] = direct[1]
        wct_ref[j, LANES:2 * LANES, base + LANES:base + 2 * LANES] = direct[0]


def _s5_kernel(u_ref, tb_ref, tcc_ref, tg_ref, lam2_ref, wg_ref, bg_ref, x0r_ref, x0i_ref, accr_ref, acci_ref,
               *refs, batch, with_cast):
    del accr_ref, acci_ref
    refs, cast_srcs, cast_dsts = _split_cast_refs(refs, 3, int(with_cast))
    o_ref, xr_ref, xi_ref, xs_ref, y_ref, st_ref, ue_ref, uo_ref, tmp_ref, wb_ref, wct_ref = refs
    for src, dst in zip(cast_srcs, cast_dsts):
        dst[...] = src[...].astype(BF16)
    c_idx = pl.program_id(0)
    tt = u_ref.shape[0]
    tp = tt // 2
    prows = tp * batch
    width = u_ref.shape[1] // batch
    n_slab = width // LANES
    nbk = wb_ref.shape[0]
    bl = wb_ref.shape[2] // 2
    ns = nbk * bl
    grp = st_ref.shape[1]
    per = grp // batch

    @pl.when(c_idx == 0)
    def _():
        st_ref[0] = jnp.concatenate([x0r_ref[...]] * per, axis=0)
        st_ref[1] = jnp.concatenate([x0i_ref[...]] * per, axis=0)
        _s5_embed_weights(tb_ref, tcc_ref, tg_ref, wb_ref, wct_ref)

    for b in range(batch):
        for m in range(n_slab):
            tmp_ref[...] = u_ref[:, b * width + m * LANES:b * width + (m + 1) * LANES]
            ue_ref[m, pl.ds(b, tp, stride=batch), :] = tmp_ref[pl.ds(0, tp, stride=2), :]
            uo_ref[m, pl.ds(b, tp, stride=batch), :] = tmp_ref[pl.ds(1, tp, stride=2), :]

    for j in range(nbk):
        lhs = jnp.concatenate([ue_ref[j], uo_ref[j]], axis=1).astype(BF16)
        w = _dot(lhs, wb_ref[j])
        xs_ref[:, j * bl:(j + 1) * bl] = w[:, :bl]
        xs_ref[:, ns + j * bl:ns + (j + 1) * bl] = w[:, bl:]

    second = lax.broadcasted_iota(jnp.int32, (grp, S5_SCAN_LANES), 0) >= batch
    for cb in range(ns // S5_SCAN_LANES):
        lo = cb * S5_SCAN_LANES
        re_l = slice(lo, lo + S5_SCAN_LANES)
        im_l = slice(ns + lo, ns + lo + S5_SCAN_LANES)
        ar = lam2_ref[0:1, re_l]
        ai = lam2_ref[1:2, re_l]

        def step(g, carry, ar=ar, ai=ai, re_l=re_l, im_l=im_l):
            cr, ci = carry
            r = pl.ds(pl.multiple_of(g * grp, grp), grp)
            wr = xs_ref[r, re_l]
            wi = xs_ref[r, im_l]
            if per == 2:
                tr = ar * cr - ai * ci + pltpu.roll(wr, batch, axis=0)
                ti = ar * ci + ai * cr + pltpu.roll(wi, batch, axis=0)
                xs_ref[r, re_l] = jnp.where(second, tr, cr)
                xs_ref[r, im_l] = jnp.where(second, ti, ci)
                vr = ar * tr - ai * ti + wr
                vi = ar * ti + ai * tr + wi
                nr = jnp.where(second, vr, pltpu.roll(vr, batch, axis=0))
                ni = jnp.where(second, vi, pltpu.roll(vi, batch, axis=0))
            else:
                xs_ref[r, re_l] = cr
                xs_ref[r, im_l] = ci
                nr = ar * cr - ai * ci + wr
                ni = ar * ci + ai * cr + wi
            return nr, ni

        cr, ci = lax.fori_loop(0, prows // grp, step, (st_ref[0, :, re_l], st_ref[1, :, re_l]), unroll=2)
        st_ref[0, :, re_l] = cr
        st_ref[1, :, re_l] = ci

    for j in range(nbk):
        lhs = jnp.concatenate([xs_ref[:, j * bl:(j + 1) * bl], xs_ref[:, ns + j * bl:ns + (j + 1) * bl],
                               ue_ref[j], uo_ref[j]], axis=1).astype(BF16)
        yj = _dot_nt(lhs, wct_ref[j])
        y_ref[0:prows, j * LANES:(j + 1) * LANES] = yj[:, :LANES]
        y_ref[prows:2 * prows, j * LANES:(j + 1) * LANES] = yj[:, LANES:]

    hh = jax.nn.gelu(y_ref[...])
    gate = _sigmoid(_dot(hh.astype(BF16), wg_ref[...]) + bg_ref[...])
    out = hh * gate
    for m in range(n_slab):
        ue_ref[m] = out[0:prows, m * LANES:(m + 1) * LANES]
        uo_ref[m] = out[prows:2 * prows, m * LANES:(m + 1) * LANES]
    for b in range(batch):
        for m in range(n_slab):
            tmp_ref[pl.ds(0, tp, stride=2), :] = ue_ref[m, pl.ds(b, tp, stride=batch), :]
            tmp_ref[pl.ds(1, tp, stride=2), :] = uo_ref[m, pl.ds(b, tp, stride=batch), :]
            o_ref[:, b * width + m * LANES:b * width + (m + 1) * LANES] = tmp_ref[...].astype(BF16)

    @pl.when(c_idx == pl.num_programs(0) - 1)
    def _():
        xr_ref[...] = st_ref[0, grp - batch:grp, :]
        xi_ref[...] = st_ref[1, grp - batch:grp, :]


def _s5(u, tb, tcc, tg, lam2, w_glu, b_glu, x0r, x0i, acc_r, acc_i, *, batch, layer, cast_src=None):
    seq = u.shape[0]
    width = u.shape[1] // batch
    n = seq * batch
    ns = lam2.shape[2]
    nbk = width // LANES
    bl = ns // nbk
    assert batch % SUBLANES == 0 or 2 * batch == SUBLANES
    assert seq % 2 == 0 and tb.shape[2] == width
    grp = max(batch, SUBLANES)
    rows = min(S5_ROWS, n)
    tt = rows // batch
    prows = rows // 2
    lay4 = lambda c: (layer, 0, 0, 0)
    lay3 = lambda c: (layer, 0, 0)
    st_in = pl.BlockSpec((None, batch, ns), lay3)
    resident = pl.Buffered(1)
    wb_shape = (nbk, 2 * LANES, 2 * bl)
    wct_shape = (nbk, 2 * LANES, 2 * bl + 2 * LANES)
    need = (2 * rows * width * 4 + (nbk * 2 * LANES * (4 * bl + 2 * LANES) + width * width) * 2
            + 4 * width * LANES * 4 + 6 * 16 * ns * 4
            + 2 * rows * width * 2 + prows * 2 * ns * 4 + rows * width * 4 + 4 * rows * width * 4
            + 4 * prows * (4 * bl + 2 * LANES) + (8 * batch + 2 * grp) * ns * 4)
    st_shape = jax.ShapeDtypeStruct(acc_r.shape, F32)
    in_specs = [
        pl.BlockSpec((tt, batch * width), lambda c: (c, 0)),
        pl.BlockSpec((None,) + tb.shape[1:], lay4, pipeline_mode=resident),
        pl.BlockSpec((None,) + tcc.shape[1:], lay4, pipeline_mode=resident),
        pl.BlockSpec((None,) + tg.shape[1:], lay4, pipeline_mode=resident),
        pl.BlockSpec((None, 2, ns), lay3),
        pl.BlockSpec((None, width, width), lay3, pipeline_mode=resident),
        pl.BlockSpec((None, 1, width), lay3),
        st_in,
        st_in,
        pl.BlockSpec(memory_space=pl.ANY),
        pl.BlockSpec(memory_space=pl.ANY),
    ]
    out_specs = [pl.BlockSpec((tt, batch * width), lambda c: (c, 0)), st_in, st_in]
    out_shape = [jax.ShapeDtypeStruct((seq, batch * width), BF16), st_shape, st_shape]
    args = [u, tb, tcc, tg, lam2, w_glu, b_glu, x0r, x0i, acc_r, acc_i]
    if cast_src is not None:
        c_in, c_out, c_shape, c_bytes = _cast_job(cast_src, layer, n // rows, lambda c: c, axis=1)
        in_specs.append(c_in)
        out_specs.append(c_out)
        out_shape.append(c_shape)
        args.append(cast_src)
        need += c_bytes
    return pl.pallas_call(
        functools.partial(_s5_kernel, batch=batch, with_cast=cast_src is not None),
        out_shape=tuple(out_shape),
        grid=(n // rows,),
        in_specs=in_specs,
        out_specs=tuple(out_specs),
        input_output_aliases={9: 1, 10: 2},
        scratch_shapes=[
            pltpu.VMEM((prows, 2 * ns), F32),
            pltpu.VMEM((rows, width), F32),
            pltpu.VMEM((2, grp, ns), F32),
            pltpu.VMEM((width // LANES, prows, LANES), F32),
            pltpu.VMEM((width // LANES, prows, LANES), F32),
            pltpu.VMEM((tt, LANES), F32),
            pltpu.VMEM(wb_shape, BF16),
            pltpu.VMEM(wct_shape, BF16),
        ],
        compiler_params=pltpu.CompilerParams(
            dimension_semantics=("arbitrary",),
            vmem_limit_bytes=_vmem_limit(need)),
        name="s5",
    )(*args)


def _outproj_kernel(x_ref, oh_ref, os_ref, wh_ref, ws_ref, o_ref):
    o_ref[...] = x_ref[...] + _dot(oh_ref[...], wh_ref[...]) + _dot(os_ref[...], ws_ref[...])


def _outproj(x, o_h, o_s, w_out, *, layer, tm, tn, seq):
    n, d = x.shape
    kh = o_h.shape[1]
    ks = w_out.shape[1] - kh
    assert kh == ks
    if o_s.shape[0] == n:
        os_map = lambda i, j: (i, 0)
    else:
        n_t = seq // tm
        os_map = lambda i, j: (i % n_t, i // n_t)
    need = 2 * (2 * tm * tn * 4 + tm * (kh + ks) * 2 + (kh + ks) * tn * 2) + 2 * tm * tn * 4
    return pl.pallas_call(
        _outproj_kernel,
        out_shape=jax.ShapeDtypeStruct((n, d), F32),
        grid=(n // tm, d // tn),
        in_specs=[
            pl.BlockSpec((tm, tn), lambda i, j: (i, j)),
            pl.BlockSpec((tm, kh), lambda i, j: (i, 0)),
            pl.BlockSpec((tm, ks), os_map),
            pl.BlockSpec((None, kh, tn), lambda i, j: (layer, 0, j)),
            pl.BlockSpec((None, ks, tn), lambda i, j: (layer, 1, j)),
        ],
        out_specs=pl.BlockSpec((tm, tn), lambda i, j: (i, j)),
        compiler_params=pltpu.CompilerParams(
            dimension_semantics=("parallel", "parallel"),
            vmem_limit_bytes=_vmem_limit(need)),
        name="outproj",
    )(x, o_h, o_s, w_out, w_out)


def _ffn_kernel(x_ref, g_ref, w1_ref, w2_ref, gf_ref, o_ref, h_ref, *, final_norm):
    f = pl.program_id(1)
    tm = x_ref.shape[0]

    @pl.when(f == 0)
    def _():
        g = g_ref[...]

        def body(i, carry):
            r = pl.ds(pl.multiple_of(i * NORM_ROWS, NORM_ROWS), NORM_ROWS)
            h_ref[r, :] = _rmsnorm(x_ref[r, :], g).astype(BF16)
            o_ref[r, :] = jnp.zeros((NORM_ROWS, o_ref.shape[1]), F32)
            return carry

        lax.fori_loop(0, tm // NORM_ROWS, body, 0, unroll=NORM_UNROLL)

    a = _dot(h_ref[...], w1_ref[...])
    a = jnp.square(jnp.maximum(a, 0.0)).astype(BF16)
    cw = o_ref.shape[1] // FFN_ACC_CHUNKS
    for c in range(FFN_ACC_CHUNKS):
        o_ref[:, c * cw:(c + 1) * cw] += _dot(a, w2_ref[:, c * cw:(c + 1) * cw])

    @pl.when(f == pl.num_programs(1) - 1)
    def _():
        gf = gf_ref[...]

        def body(i, carry):
            r = pl.ds(pl.multiple_of(i * NORM_ROWS, NORM_ROWS), NORM_ROWS)
            y = x_ref[r, :] + o_ref[r, :]
            if final_norm:
                y = _rmsnorm(y, gf)
            o_ref[r, :] = y
            return carry

        lax.fori_loop(0, tm // NORM_ROWS, body, 0, unroll=NORM_UNROLL)


def _ffn(x, g, w1, w2, gf, *, layer, tm, tf, final_norm):
    n, d = x.shape
    ff = w1.shape[1]
    need = tm * d * 4 + tm * d * 2 + 2 * tm * d * 4 + 4 * d * tf * 2 + tm * tf * 6 + tm * d * 4
    kernel = functools.partial(_ffn_kernel, final_norm=final_norm)
    return pl.pallas_call(
        kernel,
        out_shape=jax.ShapeDtypeStruct((n, d), F32),
        grid=(n // tm, ff // tf),
        in_specs=[
            pl.BlockSpec((tm, d), lambda i, f: (i, 0), pipeline_mode=pl.Buffered(1)),
            pl.BlockSpec((None, 1, d), lambda i, f: (layer, 0, 0)),
            pl.BlockSpec((d, tf), lambda i, f: (0, f)),
            pl.BlockSpec((tf, d), lambda i, f: (f, 0)),
            pl.BlockSpec((1, d), lambda i, f: (0, 0)),
        ],
        out_specs=pl.BlockSpec((tm, d), lambda i, f: (i, 0)),
        scratch_shapes=[pltpu.VMEM((tm, d), BF16)],
        compiler_params=pltpu.CompilerParams(
            dimension_semantics=("parallel", "arbitrary"),
            vmem_limit_bytes=_vmem_limit(need)),
        name="ffn",
    )(x, g, w1, w2, gf)


def _trunk(x, st_h, st_r, st_i, wts, cast_w=None):
    batch, seq, d = x.shape
    n = batch * seq
    depth = wts["w_in"].shape[0]
    cast = cast_w is None
    made = {"ffn": []} if cast else cast_w
    heads = st_h.shape[2]
    hgrn_width = heads * HEAD_DIM
    groups, nst = st_r.shape[2], st_r.shape[3]
    s5_width = wts["w_glu"].shape[1]
    tm = min(1024, n)
    xf = x.reshape(n, d)
    x0r = st_r.reshape(depth, batch, groups * nst)
    x0i = st_i.reshape(depth, batch, groups * nst)
    time_major = seq % tm == 0
    new_h = jnp.zeros(st_h.shape, F32)
    new_r = jnp.zeros(x0r.shape, F32)
    new_i = jnp.zeros(x0i.shape, F32)
    for l in range(depth):
        w_in, w_layer = (wts["w_in0"], 0) if l == 0 else (made["w_in_rest"], l - 1)
        p, u = _inproj(xf, wts["norm1_g"], w_in, wts["lb_logits"],
                       layer=l, w_layer=w_layer, tm=tm, tn=s5_width, batch=batch, seq=seq)
        hosted = [(wts["w_out"], 0)] + ([(wts["w_in"], 1)] if depth > 1 else []) if cast and l == 0 else []
        o_h, new_h, *w1 = _hgrn(p, wts["onorm_g"], st_h, new_h, batch=batch, seq=seq, layer=l,
                                cast_src=wts["w_ff1"] if cast else None, cast_layers=hosted)
        if hosted:
            made["w_out"] = w1[1]
            made["w_in_rest"] = w1[2] if depth > 1 else None
        if not time_major:
            u = u.reshape(batch, seq, s5_width).transpose(1, 0, 2).reshape(seq, batch * s5_width)
        o_s, new_r, new_i, *w2 = _s5(u, wts["s5_tb"], wts["s5_tcc"], wts["s5_tg"], wts["s5_lam2"],
                                     wts["w_glu"], wts["b_glu"], x0r, x0i, new_r, new_i,
                                     batch=batch, layer=l, cast_src=wts["w_ff2"] if cast else None)
        if not time_major:
            o_s = o_s.reshape(seq, batch, s5_width).transpose(1, 0, 2).reshape(n, s5_width)
        if cast:
            made["ffn"].append((w1[0], w2[0]))
        x1 = _outproj(xf, o_h, o_s, made["w_out"], layer=l, tm=min(512, n), tn=d, seq=seq)
        xf = _ffn(x1, wts["norm2_g"], *made["ffn"][l], wts["final_g"],
                  layer=l, tm=tm, tf=1024, final_norm=(l == depth - 1))
    return (xf.reshape(batch, seq, d), new_h, new_r.reshape(st_r.shape), new_i.reshape(st_i.shape)), made


def kernel(x_prompt, x_sample, state_hgrn, state_s5_re, state_s5_im, norm1_g, w_in, hgrn_lb_logits,
           hgrn_onorm_g, s5_lambda_re, s5_lambda_im, s5_log_step, s5_B_re, s5_B_im, s5_C_re, s5_C_im,
           s5_D, s5_w_glu, s5_b_glu, w_out, norm2_g, w_ff1, w_ff2, final_norm_g):
    depth, d = norm1_g.shape
    row3 = lambda a: a.reshape(depth, 1, a.shape[-1])
    lbr, lbi, zoh_r, zoh_i = _s5_discretise(s5_lambda_re, s5_lambda_im, s5_log_step)
    s5_tb, s5_tcc, s5_tg, s5_lam2 = _s5_pair_weights(lbr, lbi, zoh_r, zoh_i, s5_B_re, s5_B_im,
                                                     s5_C_re, s5_C_im, s5_D)
    wts = {
        "norm1_g": row3(norm1_g), "norm2_g": row3(norm2_g), "final_g": final_norm_g.reshape(1, d),
        "lb_logits": hgrn_lb_logits, "onorm_g": row3(hgrn_onorm_g),
        "w_in0": _cast_first_layer(w_in),
        "w_in": w_in, "w_out": w_out, "w_ff1": w_ff1, "w_ff2": w_ff2,
        "w_glu": s5_w_glu.astype(BF16), "b_glu": row3(s5_b_glu),
        "s5_tb": s5_tb, "s5_tcc": s5_tcc, "s5_tg": s5_tg, "s5_lam2": s5_lam2,
    }
    bp = x_prompt.shape[0]
    zh = jnp.zeros((depth, bp) + state_hgrn.shape[2:], F32)
    zs = jnp.zeros((depth, bp) + state_s5_re.shape[2:], F32)
    (y_p, hp, rp, ip), cast_w = _trunk(x_prompt, zh, zs, zs, wts)
    (y_s, hs, rs, is_), _ = _trunk(x_sample, state_hgrn, state_s5_re, state_s5_im, wts, cast_w)
    return (y_p, y_s, hp, rp, ip, hs, rs, is_)
```

```python
import functools

import jax
import jax.numpy as jnp
from jax import lax
from jax.experimental import pallas as pl
from jax.experimental.pallas import tpu as pltpu

F32 = jnp.float32
BF16 = jnp.bfloat16
EPS = 1e-6

HEAD_DIM = 128
HGRN_CHUNK = 64
HGRN_SUB = 16
HGRN_HEADS_PER_STEP = 4
SUBLANES = 8
LANES = 128
S5_SCAN_LANES = 512
S5_ROWS = 512

V7X_VMEM_CAP = 56 * 1024 * 1024
NORM_ROWS = 64
NORM_UNROLL = 4
FFN_ACC_CHUNKS = 4
INPROJ_CHUNKS = 4


def _vmem_limit(nbytes):
    return int(min(V7X_VMEM_CAP, nbytes * 5 // 4 + (4 << 20)))


def _rmsnorm(x, g):
    return x * lax.rsqrt(jnp.mean(jnp.square(x), axis=-1, keepdims=True) + EPS) * g


def _sigmoid(x):
    return 1.0 / (1.0 + jnp.exp(-x))


def _log1p_exp_neg_abs(x):
    return jnp.log(1.0 + jnp.exp(-jnp.abs(x)))


def _softplus(x):
    return jnp.maximum(x, 0.0) + _log1p_exp_neg_abs(x)


def _dot(a, b):
    return jnp.dot(a, b, preferred_element_type=F32)


def _dot_nt(a, b):
    return lax.dot_general(a, b, (((1,), (1,)), ((), ())), preferred_element_type=F32)


def _inproj_kernel(x_ref, g_ref, w_ref, lbl_ref, p_ref, u_ref, h_ref, *, layer):
    tm = x_ref.shape[0]
    cw = w_ref.shape[1] // INPROJ_CHUNKS
    j = pl.program_id(1)

    @pl.when(j == 0)
    def _():
        g = g_ref[...]

        def body(i, carry):
            r = pl.ds(pl.multiple_of(i * NORM_ROWS, NORM_ROWS), NORM_ROWS)
            h_ref[r, :] = _rmsnorm(x_ref[r, :], g).astype(BF16)
            return carry

        lax.fori_loop(0, tm // NORM_ROWS, body, 0, unroll=NORM_UNROLL)

    def emit(out_ref, fn):
        for c in range(INPROJ_CHUNKS):
            sl = slice(c * cw, (c + 1) * cw)
            out_ref[:, sl] = fn(_dot(h_ref[...], w_ref[:, sl]), sl)

    def silu(a, sl):
        return a * _sigmoid(a)

    def log_forget(z, sl):
        logits = lbl_ref[:, sl]
        e = jnp.exp(logits - jnp.max(logits, axis=0, keepdims=True))
        sm = e / jnp.sum(e, axis=0, keepdims=True)
        cum0 = sm[0:1]
        cuml = cum0
        for k in range(1, layer + 1):
            cuml = cuml + sm[k:k + 1]
        lb = cuml - cum0
        log_lb = jnp.log(lb)
        b = jnp.log1p(-lb) - _softplus(-z)
        delta = log_lb - b
        return jnp.where(jnp.isnan(delta), log_lb + b,
                         jnp.maximum(log_lb, b) + _log1p_exp_neg_abs(delta))

    for tile, (out_ref, fn) in enumerate([(p_ref, silu), (p_ref, log_forget), (p_ref, lambda a, sl: a),
                                          (p_ref, silu), (u_ref, lambda a, sl: a)]):
        pl.when(j == tile)(functools.partial(emit, out_ref, fn))


def _inproj(x, g, w, lb_logits, *, layer, w_layer, tm, tn, batch, seq):
    n, d = x.shape
    cols = w.shape[2]
    n_j = cols // tn
    depth = lb_logits.shape[0]
    assert n_j == 5 and lb_logits.shape[1] == tn
    time_major = seq % tm == 0
    if time_major:
        n_t = seq // tm
        u_shape, u_map = (seq, batch * tn), (lambda i, j: (i % n_t, i // n_t))
    else:
        u_shape, u_map = (n, tn), (lambda i, j: (i, 0))
    need = 2 * tm * d * 4 + tm * d * 2 + 2 * d * tn * 2 + 5 * tm * tn * 4
    return pl.pallas_call(
        functools.partial(_inproj_kernel, layer=layer),
        out_shape=(jax.ShapeDtypeStruct((n, cols - tn), F32), jax.ShapeDtypeStruct(u_shape, F32)),
        grid=(n // tm, n_j),
        in_specs=[
            pl.BlockSpec((tm, d), lambda i, j: (i, 0)),
            pl.BlockSpec((None, 1, d), lambda i, j: (layer, 0, 0)),
            pl.BlockSpec((None, d, tn), lambda i, j: (w_layer, 0, j)),
            pl.BlockSpec((depth, tn), lambda i, j: (0, 0)),
        ],
        out_specs=(pl.BlockSpec((tm, tn), lambda i, j: (i, jnp.minimum(j, n_j - 2))),
                   pl.BlockSpec((tm, tn), u_map)),
        scratch_shapes=[pltpu.VMEM((tm, d), BF16)],
        compiler_params=pltpu.CompilerParams(
            dimension_semantics=("parallel", "arbitrary"),
            vmem_limit_bytes=_vmem_limit(need)),
        name="inproj",
    )(x, g, w, lb_logits)


def _cast_kernel(src_ref, dst_ref):
    dst_ref[...] = src_ref[...].astype(BF16)


def _cast_first_layer(w, *, rows=256):
    _, r, c = w.shape
    spec = pl.BlockSpec((None, rows, c), lambda i: (0, i, 0))
    return pl.pallas_call(
        _cast_kernel, out_shape=jax.ShapeDtypeStruct((1, r, c), BF16), grid=(r // rows,),
        in_specs=[spec], out_specs=spec,
        compiler_params=pltpu.CompilerParams(vmem_limit_bytes=_vmem_limit(2 * rows * c * 6)),
        name="cast_first_layer",
    )(w)


def _hgrn_decayed_queries(q_lo, q_hi, f_row, row_is):
    slabs = []
    qd_lo = qd_hi = None
    for s in range(2 * SUBLANES - 1, -1, -1):
        if s == 2 * SUBLANES - 1:
            qd_hi = jnp.where(row_is[SUBLANES - 1], q_hi, 0.0)
        elif s >= SUBLANES:
            qd_hi = jnp.where(row_is[s - SUBLANES], q_hi, qd_hi * f_row(s + 1))
        else:
            fn = f_row(s + 1)
            qd_hi = qd_hi * fn
            qd_lo = jnp.where(row_is[s], q_lo, 0.0 if s == SUBLANES - 1 else qd_lo * fn)
        slabs.append((s, 1, qd_hi))
        if s < SUBLANES:
            slabs.append((s, 0, qd_lo))
    return slabs


def _split_cast_refs(refs, n_out, n_cast):
    srcs, rest = refs[:n_cast], refs[n_cast:]
    return rest[:n_out] + rest[n_out + n_cast:], srcs, rest[n_out:n_out + n_cast]


def _hgrn_kernel(q_ref, f_ref, i_ref, g_ref, on_ref, s0_ref, acc_ref, *refs, chunk, heads_blk, n_cast):
    del acc_ref
    refs, cast_srcs, cast_dsts = _split_cast_refs(refs, 2, n_cast)
    o_ref, sout_ref, st_ref, ks_ref, cs_ref, fs_ref, fc_ref, op_ref = refs
    for src, dst in zip(cast_srcs, cast_dsts):
        dst[...] = src[...].astype(BF16)
    c_idx = pl.program_id(2)
    tc = q_ref.shape[0]
    n_sub = chunk // HGRN_SUB

    @pl.when(c_idx == 0)
    def _():
        for h in range(heads_blk):
            st_ref[h] = s0_ref[0, h].T

    logf = f_ref[...]
    fgate = jnp.exp(logf)
    fs_ref[...] = fgate
    ks_ref[...] = 1.0 - fgate

    t_i = lax.broadcasted_iota(jnp.int32, (chunk, chunk), 0)
    s_i = lax.broadcasted_iota(jnp.int32, (chunk, chunk), 1)
    tri = jnp.where(t_i >= s_i, 1.0, 0.0).astype(BF16)
    for j in range(tc // chunk):
        lf = logf[j * chunk:(j + 1) * chunk]
        hi = lf.astype(BF16)
        r1 = lf - hi.astype(F32)
        mid = r1.astype(BF16)
        lo = (r1 - mid.astype(F32)).astype(BF16)
        cs_ref[j * chunk:(j + 1) * chunk, :] = _dot(tri, hi) + _dot(tri, mid) + _dot(tri, lo)

    row8 = lax.broadcasted_iota(jnp.int32, (SUBLANES, HEAD_DIM), 0)
    row_is = [row8 == j for j in range(SUBLANES)]
    lane8 = lax.broadcasted_iota(jnp.int32, (SUBLANES, chunk), 1)
    lane_is = [lane8 == j for j in range(chunk)]
    onorm = on_ref[...]
    blk = lambda a, j: a[j * HGRN_SUB:(j + 1) * HGRN_SUB]

    head_cols = [slice(h * HEAD_DIM, (h + 1) * HEAD_DIM) for h in range(heads_blk)]

    def finish(rows):
        for cols in head_cols:
            o = op_ref[:, cols]
            o = o * lax.rsqrt(jnp.mean(jnp.square(o), axis=-1, keepdims=True) + EPS)
            o = o * onorm[:, cols]
            o_ref[rows, cols] = (o * g_ref[rows, cols]).astype(BF16)

    op_ref[...] = jnp.zeros(op_ref.shape, F32)

    def chunk_body(c, carry):
        base = pl.multiple_of(c * chunk, chunk)
        r = pl.ds(base, chunk)
        finish(pl.ds(pl.multiple_of(jnp.maximum(c - 1, 0) * chunk, chunk), chunk))
        fc_ref[...] = fs_ref[r, :]
        cum_all = cs_ref[r, :]

        zero_blk = jnp.zeros((HGRN_SUB, HEAD_DIM), F32)
        qs, vbs, kbs, o_inter, a_off = [], [], [], [], []
        for cols in head_cols:
            h = len(qs)
            cum = cum_all[:, cols]
            q = q_ref[r, cols]
            k = ks_ref[r, cols]
            v = i_ref[r, cols]
            ends = [cum[(j + 1) * HGRN_SUB - 1:(j + 1) * HGRN_SUB, :] for j in range(n_sub)]
            cl = ends[-1]
            ke = [blk(k, j) * jnp.exp(ends[j] - blk(cum, j)) for j in range(n_sub)]
            qe = [blk(q, j) * jnp.exp(blk(cum, j) - ends[j - 1] if j else blk(cum, j)) for j in range(n_sub)]

            st = st_ref[h]
            q_in = jnp.concatenate([qe[j] * jnp.exp(ends[j - 1]) if j else qe[j] for j in range(n_sub)], axis=0)
            o_inter.append(_dot_nt(q_in.astype(BF16), st.astype(BF16)))
            kdec = jnp.concatenate(
                [ke[j] * jnp.exp(cl - ends[j]) if j < n_sub - 1 else ke[j] for j in range(n_sub)], axis=0)
            st_ref[h] = st * jnp.exp(cl) + _dot(v.T.astype(BF16), kdec.astype(BF16))

            offs = [None]
            for i in range(1, n_sub):
                kt = jnp.concatenate(
                    [ke[j] * jnp.exp(ends[i - 1] - ends[j]) if j < i - 1 else ke[j] for j in range(i)]
                    + [zero_blk] * (n_sub - i), axis=0)
                offs.append(_dot_nt(qe[i].astype(BF16), kt.astype(BF16)))
            a_off.append(offs)
            qs.append(q)
            vbs.append(v.astype(BF16))
            kbs.append(k.astype(BF16))

        res = []
        for h, cols in enumerate(head_cols):
            per_head = []
            for i in range(n_sub):
                lo_r = i * HGRN_SUB
                f_row = lambda s, lo_r=lo_r, cols=cols: fc_ref[lo_r + s:lo_r + s + 1, cols]
                q_blk = blk(qs[h], i)
                slabs = _hgrn_decayed_queries(q_blk[:SUBLANES], q_blk[SUBLANES:], f_row, row_is)
                stack = jnp.concatenate([slab for _, _, slab in slabs], axis=0)
                per_head.append(([(s, half) for s, half, _ in slabs],
                                 _dot_nt(stack.astype(BF16), kbs[h])))
            res.append(per_head)

        o_intra = []
        for h in range(heads_blk):
            rows = []
            for i in range(n_sub):
                order, prod = res[h][i]
                halves = [jnp.zeros((SUBLANES, chunk), F32), jnp.zeros((SUBLANES, chunk), F32)]
                for idx, (s, half) in enumerate(order):
                    piece = prod[idx * SUBLANES:(idx + 1) * SUBLANES]
                    halves[half] = jnp.where(lane_is[i * HGRN_SUB + s], piece, halves[half])
                a = jnp.concatenate(halves, axis=0)
                rows.append(a + a_off[h][i] if i else a)
            attn = jnp.concatenate(rows, axis=0)
            o_intra.append(_dot(attn.astype(BF16), vbs[h]))

        for h, cols in enumerate(head_cols):
            op_ref[:, cols] = o_inter[h] + o_intra[h]
        return carry

    lax.fori_loop(0, tc // chunk, chunk_body, 0, unroll=min(2, tc // chunk))
    finish(pl.ds(tc - chunk, chunk))

    @pl.when(c_idx == pl.num_programs(2) - 1)
    def _():
        for h in range(heads_blk):
            sout_ref[0, h] = st_ref[h].T


def _cast_job(cast_src, layer, steps, step_of, axis):
    _, r, c = cast_src.shape
    if axis == 2:
        assert c % (steps * LANES) == 0
        blk = (r, c // steps)
        at = lambda *g: (0, step_of(*g))
    else:
        assert r % (steps * 2 * SUBLANES) == 0
        blk = (r // steps, c)
        at = lambda *g: (step_of(*g), 0)
    in_spec = pl.BlockSpec((None,) + blk, lambda *g: (layer,) + at(*g))
    return in_spec, pl.BlockSpec(blk, at), jax.ShapeDtypeStruct((r, c), BF16), 2 * blk[0] * blk[1] * 6


def _cast_job_layers(cast_src, first, steps, step_of):
    depth, r, c = cast_src.shape
    count = depth - first
    per_layer = steps // count
    assert steps % count == 0 and r % (per_layer * 2 * SUBLANES) == 0
    blk = (None, r // per_layer, c)
    in_spec = pl.BlockSpec(blk, lambda *g: (first + step_of(*g) // per_layer, step_of(*g) % per_layer, 0))
    out_spec = pl.BlockSpec(blk, lambda *g: (step_of(*g) // per_layer, step_of(*g) % per_layer, 0))
    return in_spec, out_spec, jax.ShapeDtypeStruct((count, r, c), BF16), 2 * blk[1] * c * 6


def _hgrn(p, onorm_g, s0, s_acc, *, batch, seq, layer, cast_src=None, cast_layers=()):
    n = p.shape[0]
    heads = s0.shape[2]
    hb = HGRN_HEADS_PER_STEP if seq > HGRN_CHUNK else heads
    wblk = hb * HEAD_DIM
    n_hb = heads // hb
    width = heads * HEAD_DIM
    tc = min(512, seq)
    chunk = min(HGRN_CHUNK, seq)
    n_t = seq // tc
    row = lambda b, h, c: b * n_t + c
    need = 2 * 4 * tc * wblk * 4 + 2 * tc * wblk * 2 + 3 * tc * wblk * 4 + 5 * hb * HEAD_DIM * HEAD_DIM * 4
    n_cast = (cast_src is not None) + len(cast_layers)
    kernel = functools.partial(_hgrn_kernel, chunk=chunk, heads_blk=hb, n_cast=n_cast)
    st_spec = pl.BlockSpec((None, 1, hb, HEAD_DIM, HEAD_DIM), lambda b, h, c: (layer, b, h, 0, 0))
    in_specs = [
        pl.BlockSpec((tc, wblk), lambda b, h, c: (row(b, h, c), h)),
        pl.BlockSpec((tc, wblk), lambda b, h, c: (row(b, h, c), n_hb + h)),
        pl.BlockSpec((tc, wblk), lambda b, h, c: (row(b, h, c), 2 * n_hb + h)),
        pl.BlockSpec((tc, wblk), lambda b, h, c: (row(b, h, c), 3 * n_hb + h)),
        pl.BlockSpec((None, 1, wblk), lambda b, h, c: (layer, 0, h)),
        st_spec,
        pl.BlockSpec(memory_space=pl.ANY),
    ]
    out_specs = [pl.BlockSpec((tc, wblk), lambda b, h, c: (row(b, h, c), h)), st_spec]
    out_shape = [jax.ShapeDtypeStruct((n, width), BF16), jax.ShapeDtypeStruct(s_acc.shape, F32)]
    args = [p, p, p, p, onorm_g, s0, s_acc]
    steps = batch * n_hb * n_t
    step_of = lambda b, h, c: (b * n_hb + h) * n_t + c
    jobs = [] if cast_src is None else [(cast_src, _cast_job(cast_src, layer, steps, step_of, axis=2))]
    jobs += [(src, _cast_job_layers(src, first, steps, step_of)) for src, first in cast_layers]
    for src, (c_in, c_out, c_shape, c_bytes) in jobs:
        in_specs.append(c_in)
        out_specs.append(c_out)
        out_shape.append(c_shape)
        args.append(src)
        need += c_bytes
    return pl.pallas_call(
        kernel,
        out_shape=tuple(out_shape),
        grid=(batch, n_hb, n_t),
        in_specs=in_specs,
        out_specs=tuple(out_specs),
        input_output_aliases={6: 1},
        scratch_shapes=[
            pltpu.VMEM((hb, HEAD_DIM, HEAD_DIM), F32),
            pltpu.VMEM((tc, wblk), F32),
            pltpu.VMEM((tc, wblk), F32),
            pltpu.VMEM((tc, wblk), F32),
            pltpu.VMEM((chunk, wblk), F32),
            pltpu.VMEM((chunk, wblk), F32),
        ],
        compiler_params=pltpu.CompilerParams(
            dimension_semantics=("parallel", "parallel", "arbitrary"),
            vmem_limit_bytes=max(V7X_VMEM_CAP, _vmem_limit(need))),
        name="hgrn",
    )(*args)


def _s5_disc_lambda_kernel(lr_ref, li_ref, ls_ref, lbr_ref, lbi_ref, cr_ref, ci_ref):
    lr, li = lr_ref[...], li_ref[...]
    dt = jnp.exp(ls_ref[...])
    mag = jnp.exp(dt * lr)
    ang = dt * li
    lbr = mag * jnp.cos(ang)
    lbi = mag * jnp.sin(ang)
    nr, ni = lbr - 1.0, lbi
    den = lr * lr + li * li
    lbr_ref[...] = lbr
    lbi_ref[...] = lbi
    cr_ref[...] = (nr * lr + ni * li) / den
    ci_ref[...] = (ni * lr - nr * li) / den


def _s5_discretise(lam_re, lam_im, log_step):
    depth, groups, nst = lam_re.shape
    rows = depth * groups
    shp = jax.ShapeDtypeStruct((rows, nst), F32)
    outs = pl.pallas_call(
        _s5_disc_lambda_kernel, out_shape=(shp, shp, shp, shp), name="s5_disc_lambda",
    )(lam_re.reshape(rows, nst), lam_im.reshape(rows, nst), log_step.reshape(rows, 1))
    return tuple(o.reshape(depth, groups, nst) for o in outs)


def _s5_pair_weights(lbr, lbi, zoh_r, zoh_i, b_re, b_im, c_re, c_im, d_skip):
    depth, groups, nst, cg = b_re.shape
    hi = lax.Precision.HIGHEST
    zr, zi = zoh_r[..., None], zoh_i[..., None]
    bbr, bbi = zr * b_re - zi * b_im, zr * b_im + zi * b_re
    lr, li = lbr[..., None], lbi[..., None]
    blr, bli = bbr * lr - bbi * li, bbr * li + bbi * lr
    tb = jnp.swapaxes(jnp.stack([blr, bli, bbr, bbi], axis=1), 3, 4).reshape(depth, 4, groups * cg, nst)

    lrc, lic = lbr[:, :, None, :], lbi[:, :, None, :]
    pr, pi = c_re * lrc - c_im * lic, c_re * lic + c_im * lrc
    qr, qi = pr * lrc - pi * lic, pr * lic + pi * lrc
    tcc = jnp.stack([pr, -pi, qr, -qi], axis=1).transpose(0, 1, 3, 2, 4).reshape(depth, 4, cg, groups * nst)

    mm = lambda a, b: jnp.einsum("lgcn,lgnd->lgcd", a, b, precision=hi)
    g0 = mm(c_re, bbr) - mm(c_im, bbi)
    g1 = mm(pr, bbr) - mm(pi, bbi)
    g0d = g0 + d_skip.reshape(depth, groups, cg)[..., None] * jnp.eye(cg, dtype=F32)
    tg = jnp.stack([g0d, g1], axis=1).transpose(0, 1, 3, 2, 4).reshape(depth, 2, cg, groups * cg)
    lam2 = jnp.stack([lbr * lbr - lbi * lbi, 2.0 * lbr * lbi], axis=1).reshape(depth, 2, groups * nst)
    return tb, tcc, tg, lam2


def _s5_embed_weights(tb_ref, tcc_ref, tg_ref, wb_ref, wct_ref):
    nbk = wb_ref.shape[0]
    cg, nst = tcc_ref.shape[1], tb_ref.shape[2]
    bl = wb_ref.shape[2] // 2
    assert cg & (cg - 1) == 0 and nst & (nst - 1) == 0
    c_shift, n_shift = cg.bit_length() - 1, nst.bit_length() - 1
    iota = lambda shape, dim: lax.broadcasted_iota(jnp.int32, shape, dim)
    e_in = jnp.where((iota((nst, bl), 1) & (nst - 1)) == iota((nst, bl), 0), 1.0, 0.0).astype(BF16)
    e_out = jnp.where((iota((LANES, cg), 0) & (cg - 1)) == iota((LANES, cg), 1), 1.0, 0.0).astype(BF16)
    m_state = (iota((LANES, bl), 0) >> c_shift) == (iota((LANES, bl), 1) >> n_shift)
    m_direct = (iota((LANES, LANES), 0) >> c_shift) == (iota((LANES, LANES), 1) >> c_shift)
    zeros = jnp.zeros((LANES, LANES), BF16)
    for j in range(nbk):
        for k in range(4):
            half, part = divmod(k, 2)
            a = tb_ref[k, j * LANES:(j + 1) * LANES, :].astype(BF16)
            wb_ref[j, half * LANES:(half + 1) * LANES, part * bl:(part + 1) * bl] = (
                jnp.where(m_state, _dot(a, e_in), 0.0).astype(BF16))
        for k in range(4):
            eo, part = divmod(k, 2)
            x = tcc_ref[k, :, j * bl:(j + 1) * bl].astype(BF16)
            wct_ref[j, eo * LANES:(eo + 1) * LANES, part * bl:(part + 1) * bl] = (
                jnp.where(m_state, _dot(e_out, x), 0.0).astype(BF16))
        direct = [jnp.where(m_direct, _dot(e_out, tg_ref[k, :, j * LANES:(j + 1) * LANES].astype(BF16)), 0.0
                            ).astype(BF16) for k in range(2)]
        base = 2 * bl
        wct_ref[j, 0:LANES, base:base + LANES] = direct[0]
        wct_ref[j, 0:LANES, base + LANES:base + 2 * LANES] = zeros
        wct_ref[j, LANES:2 * LANES, base:base + LANES] = direct[1]
        wct_ref[j, LANES:2 * LANES, base + LANES:base + 2 * LANES] = direct[0]


def _s5_kernel(u_ref, tb_ref, tcc_ref, tg_ref, lam2_ref, wg_ref, bg_ref, x0r_ref, x0i_ref, accr_ref, acci_ref,
               *refs, batch, with_cast):
    del accr_ref, acci_ref
    refs, cast_srcs, cast_dsts = _split_cast_refs(refs, 3, int(with_cast))
    o_ref, xr_ref, xi_ref, xs_ref, y_ref, st_ref, ue_ref, uo_ref, tmp_ref, wb_ref, wct_ref = refs
    for src, dst in zip(cast_srcs, cast_dsts):
        dst[...] = src[...].astype(BF16)
    c_idx = pl.program_id(0)
    tt = u_ref.shape[0]
    tp = tt // 2
    prows = tp * batch
    width = u_ref.shape[1] // batch
    n_slab = width // LANES
    nbk = wb_ref.shape[0]
    bl = wb_ref.shape[2] // 2
    ns = nbk * bl
    grp = st_ref.shape[1]
    per = grp // batch

    @pl.when(c_idx == 0)
    def _():
        st_ref[0] = jnp.concatenate([x0r_ref[...]] * per, axis=0)
        st_ref[1] = jnp.concatenate([x0i_ref[...]] * per, axis=0)
        _s5_embed_weights(tb_ref, tcc_ref, tg_ref, wb_ref, wct_ref)

    for b in range(batch):
        for m in range(n_slab):
            tmp_ref[...] = u_ref[:, b * width + m * LANES:b * width + (m + 1) * LANES]
            ue_ref[m, pl.ds(b, tp, stride=batch), :] = tmp_ref[pl.ds(0, tp, stride=2), :]
            uo_ref[m, pl.ds(b, tp, stride=batch), :] = tmp_ref[pl.ds(1, tp, stride=2), :]

    for j in range(nbk):
        lhs = jnp.concatenate([ue_ref[j], uo_ref[j]], axis=1).astype(BF16)
        w = _dot(lhs, wb_ref[j])
        xs_ref[:, j * bl:(j + 1) * bl] = w[:, :bl]
        xs_ref[:, ns + j * bl:ns + (j + 1) * bl] = w[:, bl:]

    second = lax.broadcasted_iota(jnp.int32, (grp, S5_SCAN_LANES), 0) >= batch
    for cb in range(ns // S5_SCAN_LANES):
        lo = cb * S5_SCAN_LANES
        re_l = slice(lo, lo + S5_SCAN_LANES)
        im_l = slice(ns + lo, ns + lo + S5_SCAN_LANES)
        ar = lam2_ref[0:1, re_l]
        ai = lam2_ref[1:2, re_l]

        def step(g, carry, ar=ar, ai=ai, re_l=re_l, im_l=im_l):
            cr, ci = carry
            r = pl.ds(pl.multiple_of(g * grp, grp), grp)
            wr = xs_ref[r, re_l]
            wi = xs_ref[r, im_l]
            if per == 2:
                tr = ar * cr - ai * ci + pltpu.roll(wr, batch, axis=0)
                ti = ar * ci + ai * cr + pltpu.roll(wi, batch, axis=0)
                xs_ref[r, re_l] = jnp.where(second, tr, cr)
                xs_ref[r, im_l] = jnp.where(second, ti, ci)
                vr = ar * tr - ai * ti + wr
                vi = ar * ti + ai * tr + wi
                nr = jnp.where(second, vr, pltpu.roll(vr, batch, axis=0))
                ni = jnp.where(second, vi, pltpu.roll(vi, batch, axis=0))
            else:
                xs_ref[r, re_l] = cr
                xs_ref[r, im_l] = ci
                nr = ar * cr - ai * ci + wr
                ni = ar * ci + ai * cr + wi
            return nr, ni

        cr, ci = lax.fori_loop(0, prows // grp, step, (st_ref[0, :, re_l], st_ref[1, :, re_l]), unroll=2)
        st_ref[0, :, re_l] = cr
        st_ref[1, :, re_l] = ci

    for j in range(nbk):
        lhs = jnp.concatenate([xs_ref[:, j * bl:(j + 1) * bl], xs_ref[:, ns + j * bl:ns + (j + 1) * bl],
                               ue_ref[j], uo_ref[j]], axis=1).astype(BF16)
        yj = _dot_nt(lhs, wct_ref[j])
        y_ref[0:prows, j * LANES:(j + 1) * LANES] = yj[:, :LANES]
        y_ref[prows:2 * prows, j * LANES:(j + 1) * LANES] = yj[:, LANES:]

    hh = jax.nn.gelu(y_ref[...])
    gate = _sigmoid(_dot(hh.astype(BF16), wg_ref[...]) + bg_ref[...])
    out = hh * gate
    for m in range(n_slab):
        ue_ref[m] = out[0:prows, m * LANES:(m + 1) * LANES]
        uo_ref[m] = out[prows:2 * prows, m * LANES:(m + 1) * LANES]
    for b in range(batch):
        for m in range(n_slab):
            tmp_ref[pl.ds(0, tp, stride=2), :] = ue_ref[m, pl.ds(b, tp, stride=batch), :]
            tmp_ref[pl.ds(1, tp, stride=2), :] = uo_ref[m, pl.ds(b, tp, stride=batch), :]
            o_ref[:, b * width + m * LANES:b * width + (m + 1) * LANES] = tmp_ref[...].astype(BF16)

    @pl.when(c_idx == pl.num_programs(0) - 1)
    def _():
        xr_ref[...] = st_ref[0, grp - batch:grp, :]
        xi_ref[...] = st_ref[1, grp - batch:grp, :]


def _s5(u, tb, tcc, tg, lam2, w_glu, b_glu, x0r, x0i, acc_r, acc_i, *, batch, layer, cast_src=None):
    seq = u.shape[0]
    width = u.shape[1] // batch
    n = seq * batch
    ns = lam2.shape[2]
    nbk = width // LANES
    bl = ns // nbk
    assert batch % SUBLANES == 0 or 2 * batch == SUBLANES
    assert seq % 2 == 0 and tb.shape[2] == width
    grp = max(batch, SUBLANES)
    rows = min(S5_ROWS, n)
    tt = rows // batch
    prows = rows // 2
    lay4 = lambda c: (layer, 0, 0, 0)
    lay3 = lambda c: (layer, 0, 0)
    st_in = pl.BlockSpec((None, batch, ns), lay3)
    resident = pl.Buffered(1)
    wb_shape = (nbk, 2 * LANES, 2 * bl)
    wct_shape = (nbk, 2 * LANES, 2 * bl + 2 * LANES)
    need = (2 * rows * width * 4 + (nbk * 2 * LANES * (4 * bl + 2 * LANES) + width * width) * 2
            + 4 * width * LANES * 4 + 6 * 16 * ns * 4
            + 2 * rows * width * 2 + prows * 2 * ns * 4 + rows * width * 4 + 4 * rows * width * 4
            + 4 * prows * (4 * bl + 2 * LANES) + (8 * batch + 2 * grp) * ns * 4)
    st_shape = jax.ShapeDtypeStruct(acc_r.shape, F32)
    in_specs = [
        pl.BlockSpec((tt, batch * width), lambda c: (c, 0)),
        pl.BlockSpec((None,) + tb.shape[1:], lay4, pipeline_mode=resident),
        pl.BlockSpec((None,) + tcc.shape[1:], lay4, pipeline_mode=resident),
        pl.BlockSpec((None,) + tg.shape[1:], lay4, pipeline_mode=resident),
        pl.BlockSpec((None, 2, ns), lay3),
        pl.BlockSpec((None, width, width), lay3, pipeline_mode=resident),
        pl.BlockSpec((None, 1, width), lay3),
        st_in,
        st_in,
        pl.BlockSpec(memory_space=pl.ANY),
        pl.BlockSpec(memory_space=pl.ANY),
    ]
    out_specs = [pl.BlockSpec((tt, batch * width), lambda c: (c, 0)), st_in, st_in]
    out_shape = [jax.ShapeDtypeStruct((seq, batch * width), BF16), st_shape, st_shape]
    args = [u, tb, tcc, tg, lam2, w_glu, b_glu, x0r, x0i, acc_r, acc_i]
    if cast_src is not None:
        c_in, c_out, c_shape, c_bytes = _cast_job(cast_src, layer, n // rows, lambda c: c, axis=1)
        in_specs.append(c_in)
        out_specs.append(c_out)
        out_shape.append(c_shape)
        args.append(cast_src)
        need += c_bytes
    return pl.pallas_call(
        functools.partial(_s5_kernel, batch=batch, with_cast=cast_src is not None),
        out_shape=tuple(out_shape),
        grid=(n // rows,),
        in_specs=in_specs,
        out_specs=tuple(out_specs),
        input_output_aliases={9: 1, 10: 2},
        scratch_shapes=[
            pltpu.VMEM((prows, 2 * ns), F32),
            pltpu.VMEM((rows, width), F32),
            pltpu.VMEM((2, grp, ns), F32),
            pltpu.VMEM((width // LANES, prows, LANES), F32),
            pltpu.VMEM((width // LANES, prows, LANES), F32),
            pltpu.VMEM((tt, LANES), F32),
            pltpu.VMEM(wb_shape, BF16),
            pltpu.VMEM(wct_shape, BF16),
        ],
        compiler_params=pltpu.CompilerParams(
            dimension_semantics=("arbitrary",),
            vmem_limit_bytes=_vmem_limit(need)),
        name="s5",
    )(*args)


def _outproj_kernel(x_ref, oh_ref, os_ref, wh_ref, ws_ref, o_ref):
    o_ref[...] = x_ref[...] + _dot(oh_ref[...], wh_ref[...]) + _dot(os_ref[...], ws_ref[...])


def _outproj(x, o_h, o_s, w_out, *, layer, tm, tn, seq):
    n, d = x.shape
    kh = o_h.shape[1]
    ks = w_out.shape[1] - kh
    assert kh == ks
    if o_s.shape[0] == n:
        os_map = lambda i, j: (i, 0)
    else:
        n_t = seq // tm
        os_map = lambda i, j: (i % n_t, i // n_t)
    need = 2 * (2 * tm * tn * 4 + tm * (kh + ks) * 2 + (kh + ks) * tn * 2) + 2 * tm * tn * 4
    return pl.pallas_call(
        _outproj_kernel,
        out_shape=jax.ShapeDtypeStruct((n, d), F32),
        grid=(n // tm, d // tn),
        in_specs=[
            pl.BlockSpec((tm, tn), lambda i, j: (i, j)),
            pl.BlockSpec((tm, kh), lambda i, j: (i, 0)),
            pl.BlockSpec((tm, ks), os_map),
            pl.BlockSpec((None, kh, tn), lambda i, j: (layer, 0, j)),
            pl.BlockSpec((None, ks, tn), lambda i, j: (layer, 1, j)),
        ],
        out_specs=pl.BlockSpec((tm, tn), lambda i, j: (i, j)),
        compiler_params=pltpu.CompilerParams(
            dimension_semantics=("parallel", "parallel"),
            vmem_limit_bytes=_vmem_limit(need)),
        name="outproj",
    )(x, o_h, o_s, w_out, w_out)


def _ffn_kernel(x_ref, g_ref, w1_ref, w2_ref, gf_ref, o_ref, h_ref, *, final_norm):
    f = pl.program_id(1)
    tm = x_ref.shape[0]

    @pl.when(f == 0)
    def _():
        g = g_ref[...]

        def body(i, carry):
            r = pl.ds(pl.multiple_of(i * NORM_ROWS, NORM_ROWS), NORM_ROWS)
            h_ref[r, :] = _rmsnorm(x_ref[r, :], g).astype(BF16)
            o_ref[r, :] = jnp.zeros((NORM_ROWS, o_ref.shape[1]), F32)
            return carry

        lax.fori_loop(0, tm // NORM_ROWS, body, 0, unroll=NORM_UNROLL)

    a = _dot(h_ref[...], w1_ref[...])
    a = jnp.square(jnp.maximum(a, 0.0)).astype(BF16)
    cw = o_ref.shape[1] // FFN_ACC_CHUNKS
    for c in range(FFN_ACC_CHUNKS):
        o_ref[:, c * cw:(c + 1) * cw] += _dot(a, w2_ref[:, c * cw:(c + 1) * cw])

    @pl.when(f == pl.num_programs(1) - 1)
    def _():
        gf = gf_ref[...]

        def body(i, carry):
            r = pl.ds(pl.multiple_of(i * NORM_ROWS, NORM_ROWS), NORM_ROWS)
            y = x_ref[r, :] + o_ref[r, :]
            if final_norm:
                y = _rmsnorm(y, gf)
            o_ref[r, :] = y
            return carry

        lax.fori_loop(0, tm // NORM_ROWS, body, 0, unroll=NORM_UNROLL)


def _ffn(x, g, w1, w2, gf, *, layer, tm, tf, final_norm):
    n, d = x.shape
    ff = w1.shape[1]
    need = tm * d * 4 + tm * d * 2 + 2 * tm * d * 4 + 4 * d * tf * 2 + tm * tf * 6 + tm * d * 4
    kernel = functools.partial(_ffn_kernel, final_norm=final_norm)
    return pl.pallas_call(
        kernel,
        out_shape=jax.ShapeDtypeStruct((n, d), F32),
        grid=(n // tm, ff // tf),
        in_specs=[
            pl.BlockSpec((tm, d), lambda i, f: (i, 0), pipeline_mode=pl.Buffered(1)),
            pl.BlockSpec((None, 1, d), lambda i, f: (layer, 0, 0)),
            pl.BlockSpec((d, tf), lambda i, f: (0, f)),
            pl.BlockSpec((tf, d), lambda i, f: (f, 0)),
            pl.BlockSpec((1, d), lambda i, f: (0, 0)),
        ],
        out_specs=pl.BlockSpec((tm, d), lambda i, f: (i, 0)),
        scratch_shapes=[pltpu.VMEM((tm, d), BF16)],
        compiler_params=pltpu.CompilerParams(
            dimension_semantics=("parallel", "arbitrary"),
            vmem_limit_bytes=_vmem_limit(need)),
        name="ffn",
    )(x, g, w1, w2, gf)


def _trunk(x, st_h, st_r, st_i, wts, cast_w=None):
    batch, seq, d = x.shape
    n = batch * seq
    depth = wts["w_in"].shape[0]
    cast = cast_w is None
    made = {"ffn": []} if cast else cast_w
    heads = st_h.shape[2]
    hgrn_width = heads * HEAD_DIM
    groups, nst = st_r.shape[2], st_r.shape[3]
    s5_width = wts["w_glu"].shape[1]
    tm = min(1024, n)
    xf = x.reshape(n, d)
    x0r = st_r.reshape(depth, batch, groups * nst)
    x0i = st_i.reshape(depth, batch, groups * nst)
    time_major = seq % tm == 0
    new_h = jnp.zeros(st_h.shape, F32)
    new_r = jnp.zeros(x0r.shape, F32)
    new_i = jnp.zeros(x0i.shape, F32)
    for l in range(depth):
        w_in, w_layer = (wts["w_in0"], 0) if l == 0 else (made["w_in_rest"], l - 1)
        p, u = _inproj(xf, wts["norm1_g"], w_in, wts["lb_logits"],
                       layer=l, w_layer=w_layer, tm=tm, tn=s5_width, batch=batch, seq=seq)
        hosted = [(wts["w_out"], 0)] + ([(wts["w_in"], 1)] if depth > 1 else []) if cast and l == 0 else []
        o_h, new_h, *w1 = _hgrn(p, wts["onorm_g"], st_h, new_h, batch=batch, seq=seq, layer=l,
                                cast_src=wts["w_ff1"] if cast else None, cast_layers=hosted)
        if hosted:
            made["w_out"] = w1[1]
            made["w_in_rest"] = w1[2] if depth > 1 else None
        if not time_major:
            u = u.reshape(batch, seq, s5_width).transpose(1, 0, 2).reshape(seq, batch * s5_width)
        o_s, new_r, new_i, *w2 = _s5(u, wts["s5_tb"], wts["s5_tcc"], wts["s5_tg"], wts["s5_lam2"],
                                     wts["w_glu"], wts["b_glu"], x0r, x0i, new_r, new_i,
                                     batch=batch, layer=l, cast_src=wts["w_ff2"] if cast else None)
        if not time_major:
            o_s = o_s.reshape(seq, batch, s5_width).transpose(1, 0, 2).reshape(n, s5_width)
        if cast:
            made["ffn"].append((w1[0], w2[0]))
        x1 = _outproj(xf, o_h, o_s, made["w_out"], layer=l, tm=min(512, n), tn=d, seq=seq)
        xf = _ffn(x1, wts["norm2_g"], *made["ffn"][l], wts["final_g"],
                  layer=l, tm=tm, tf=1024, final_norm=(l == depth - 1))
    return (xf.reshape(batch, seq, d), new_h, new_r.reshape(st_r.shape), new_i.reshape(st_i.shape)), made


def kernel(x_prompt, x_sample, state_hgrn, state_s5_re, state_s5_im, norm1_g, w_in, hgrn_lb_logits,
           hgrn_onorm_g, s5_lambda_re, s5_lambda_im, s5_log_step, s5_B_re, s5_B_im, s5_C_re, s5_C_im,
           s5_D, s5_w_glu, s5_b_glu, w_out, norm2_g, w_ff1, w_ff2, final_norm_g):
    depth, d = norm1_g.shape
    row3 = lambda a: a.reshape(depth, 1, a.shape[-1])
    lbr, lbi, zoh_r, zoh_i = _s5_discretise(s5_lambda_re, s5_lambda_im, s5_log_step)
    s5_tb, s5_tcc, s5_tg, s5_lam2 = _s5_pair_weights(lbr, lbi, zoh_r, zoh_i, s5_B_re, s5_B_im,
                                                     s5_C_re, s5_C_im, s5_D)
    wts = {
        "norm1_g": row3(norm1_g), "norm2_g": row3(norm2_g), "final_g": final_norm_g.reshape(1, d),
        "lb_logits": hgrn_lb_logits, "onorm_g": row3(hgrn_onorm_g),
        "w_in0": _cast_first_layer(w_in),
        "w_in": w_in, "w_out": w_out, "w_ff1": w_ff1, "w_ff2": w_ff2,
        "w_glu": s5_w_glu.astype(BF16), "b_glu": row3(s5_b_glu),
        "s5_tb": s5_tb, "s5_tcc": s5_tcc, "s5_tg": s5_tg, "s5_lam2": s5_lam2,
    }
    bp = x_prompt.shape[0]
    zh = jnp.zeros((depth, bp) + state_hgrn.shape[2:], F32)
    zs = jnp.zeros((depth, bp) + state_s5_re.shape[2:], F32)
    (y_p, hp, rp, ip), cast_w = _trunk(x_prompt, zh, zs, zs, wts)
    (y_s, hs, rs, is_), _ = _trunk(x_sample, state_hgrn, state_s5_re, state_s5_im, wts, cast_w)
    return (y_p, y_s, hp, rp, ip, hs, rs, is_)
```

```python
import functools

import jax
import jax.numpy as jnp
from jax import lax
from jax.experimental import pallas as pl
from jax.experimental.pallas import tpu as pltpu

F32 = jnp.float32
BF16 = jnp.bfloat16
EPS = 1e-6

HEAD_DIM = 128
HGRN_CHUNK = 64
HGRN_SUB = 16
HGRN_HEADS_PER_STEP = 8
SUBLANES = 8
LANES = 128
S5_SCAN_LANES = 512
S5_ROWS = 512

V7X_VMEM_CAP = 56 * 1024 * 1024
NORM_ROWS = 64
NORM_UNROLL = 4
FFN_ACC_CHUNKS = 4
INPROJ_CHUNKS = 4


def _vmem_limit(nbytes):
    return int(min(V7X_VMEM_CAP, nbytes * 5 // 4 + (4 << 20)))


def _rmsnorm(x, g):
    return x * lax.rsqrt(jnp.mean(jnp.square(x), axis=-1, keepdims=True) + EPS) * g


def _sigmoid(x):
    return 1.0 / (1.0 + jnp.exp(-x))


def _log1p_exp_neg_abs(x):
    return jnp.log(1.0 + jnp.exp(-jnp.abs(x)))


def _softplus(x):
    return jnp.maximum(x, 0.0) + _log1p_exp_neg_abs(x)


def _dot(a, b):
    return jnp.dot(a, b, preferred_element_type=F32)


def _dot_nt(a, b):
    return lax.dot_general(a, b, (((1,), (1,)), ((), ())), preferred_element_type=F32)


def _inproj_kernel(x_ref, g_ref, w_ref, lbl_ref, p_ref, u_ref, h_ref, *, layer):
    tm = x_ref.shape[0]
    cw = w_ref.shape[1] // INPROJ_CHUNKS
    j = pl.program_id(1)

    @pl.when(j == 0)
    def _():
        g = g_ref[...]

        def body(i, carry):
            r = pl.ds(pl.multiple_of(i * NORM_ROWS, NORM_ROWS), NORM_ROWS)
            h_ref[r, :] = _rmsnorm(x_ref[r, :], g).astype(BF16)
            return carry

        lax.fori_loop(0, tm // NORM_ROWS, body, 0, unroll=NORM_UNROLL)

    def emit(out_ref, fn):
        for c in range(INPROJ_CHUNKS):
            sl = slice(c * cw, (c + 1) * cw)
            out_ref[:, sl] = fn(_dot(h_ref[...], w_ref[:, sl]), sl)

    def silu(a, sl):
        return a * _sigmoid(a)

    def log_forget(z, sl):
        logits = lbl_ref[:, sl]
        e = jnp.exp(logits - jnp.max(logits, axis=0, keepdims=True))
        sm = e / jnp.sum(e, axis=0, keepdims=True)
        cum0 = sm[0:1]
        cuml = cum0
        for k in range(1, layer + 1):
            cuml = cuml + sm[k:k + 1]
        lb = cuml - cum0
        log_lb = jnp.log(lb)
        b = jnp.log1p(-lb) - _softplus(-z)
        delta = log_lb - b
        return jnp.where(jnp.isnan(delta), log_lb + b,
                         jnp.maximum(log_lb, b) + _log1p_exp_neg_abs(delta))

    for tile, (out_ref, fn) in enumerate([(p_ref, silu), (p_ref, log_forget), (p_ref, lambda a, sl: a),
                                          (p_ref, silu), (u_ref, lambda a, sl: a)]):
        pl.when(j == tile)(functools.partial(emit, out_ref, fn))


def _inproj(x, g, w, lb_logits, *, layer, w_layer, tm, tn, batch, seq):
    n, d = x.shape
    cols = w.shape[2]
    n_j = cols // tn
    depth = lb_logits.shape[0]
    assert n_j == 5 and lb_logits.shape[1] == tn
    time_major = seq % tm == 0
    if time_major:
        n_t = seq // tm
        u_shape, u_map = (seq, batch * tn), (lambda i, j: (i % n_t, i // n_t))
    else:
        u_shape, u_map = (n, tn), (lambda i, j: (i, 0))
    need = 2 * tm * d * 4 + tm * d * 2 + 2 * d * tn * 2 + 5 * tm * tn * 4
    return pl.pallas_call(
        functools.partial(_inproj_kernel, layer=layer),
        out_shape=(jax.ShapeDtypeStruct((n, cols - tn), F32), jax.ShapeDtypeStruct(u_shape, F32)),
        grid=(n // tm, n_j),
        in_specs=[
            pl.BlockSpec((tm, d), lambda i, j: (i, 0)),
            pl.BlockSpec((None, 1, d), lambda i, j: (layer, 0, 0)),
            pl.BlockSpec((None, d, tn), lambda i, j: (w_layer, 0, j)),
            pl.BlockSpec((depth, tn), lambda i, j: (0, 0)),
        ],
        out_specs=(pl.BlockSpec((tm, tn), lambda i, j: (i, jnp.minimum(j, n_j - 2))),
                   pl.BlockSpec((tm, tn), u_map)),
        scratch_shapes=[pltpu.VMEM((tm, d), BF16)],
        compiler_params=pltpu.CompilerParams(
            dimension_semantics=("parallel", "arbitrary"),
            vmem_limit_bytes=_vmem_limit(need)),
        name="inproj",
    )(x, g, w, lb_logits)


def _cast_kernel(src_ref, dst_ref):
    dst_ref[...] = src_ref[...].astype(BF16)


def _cast_first_layer(w, *, rows=256):
    _, r, c = w.shape
    spec = pl.BlockSpec((None, rows, c), lambda i: (0, i, 0))
    return pl.pallas_call(
        _cast_kernel, out_shape=jax.ShapeDtypeStruct((1, r, c), BF16), grid=(r // rows,),
        in_specs=[spec], out_specs=spec,
        compiler_params=pltpu.CompilerParams(vmem_limit_bytes=_vmem_limit(2 * rows * c * 6)),
        name="cast_first_layer",
    )(w)


def _hgrn_decayed_queries(q_lo, q_hi, f_row, row_is):
    slabs = []
    qd_lo = qd_hi = None
    for s in range(2 * SUBLANES - 1, -1, -1):
        if s == 2 * SUBLANES - 1:
            qd_hi = jnp.where(row_is[SUBLANES - 1], q_hi, 0.0)
        elif s >= SUBLANES:
            qd_hi = jnp.where(row_is[s - SUBLANES], q_hi, qd_hi * f_row(s + 1))
        else:
            fn = f_row(s + 1)
            qd_hi = qd_hi * fn
            qd_lo = jnp.where(row_is[s], q_lo, 0.0 if s == SUBLANES - 1 else qd_lo * fn)
        slabs.append((s, 1, qd_hi))
        if s < SUBLANES:
            slabs.append((s, 0, qd_lo))
    return slabs


def _split_cast_refs(refs, n_out, n_cast):
    srcs, rest = refs[:n_cast], refs[n_cast:]
    return rest[:n_out] + rest[n_out + n_cast:], srcs, rest[n_out:n_out + n_cast]


def _hgrn_kernel(q_ref, f_ref, i_ref, g_ref, on_ref, s0_ref, acc_ref, *refs, chunk, heads_blk, n_cast):
    del acc_ref
    refs, cast_srcs, cast_dsts = _split_cast_refs(refs, 2, n_cast)
    o_ref, sout_ref, st_ref, ks_ref, cs_ref, fs_ref, fc_ref, op_ref = refs
    for src, dst in zip(cast_srcs, cast_dsts):
        dst[...] = src[...].astype(BF16)
    c_idx = pl.program_id(2)
    tc = q_ref.shape[0]
    n_sub = chunk // HGRN_SUB

    @pl.when(c_idx == 0)
    def _():
        for h in range(heads_blk):
            st_ref[h] = s0_ref[0, h].T

    logf = f_ref[...]
    fgate = jnp.exp(logf)
    fs_ref[...] = fgate
    ks_ref[...] = 1.0 - fgate

    t_i = lax.broadcasted_iota(jnp.int32, (chunk, chunk), 0)
    s_i = lax.broadcasted_iota(jnp.int32, (chunk, chunk), 1)
    tri = jnp.where(t_i >= s_i, 1.0, 0.0).astype(BF16)
    for j in range(tc // chunk):
        lf = logf[j * chunk:(j + 1) * chunk]
        hi = lf.astype(BF16)
        r1 = lf - hi.astype(F32)
        mid = r1.astype(BF16)
        lo = (r1 - mid.astype(F32)).astype(BF16)
        cs_ref[j * chunk:(j + 1) * chunk, :] = _dot(tri, hi) + _dot(tri, mid) + _dot(tri, lo)

    row8 = lax.broadcasted_iota(jnp.int32, (SUBLANES, HEAD_DIM), 0)
    row_is = [row8 == j for j in range(SUBLANES)]
    lane8 = lax.broadcasted_iota(jnp.int32, (SUBLANES, chunk), 1)
    lane_is = [lane8 == j for j in range(chunk)]
    onorm = on_ref[...]
    blk = lambda a, j: a[j * HGRN_SUB:(j + 1) * HGRN_SUB]

    head_cols = [slice(h * HEAD_DIM, (h + 1) * HEAD_DIM) for h in range(heads_blk)]

    def finish(rows):
        for cols in head_cols:
            o = op_ref[:, cols]
            o = o * lax.rsqrt(jnp.mean(jnp.square(o), axis=-1, keepdims=True) + EPS)
            o = o * onorm[:, cols]
            o_ref[rows, cols] = (o * g_ref[rows, cols]).astype(BF16)

    op_ref[...] = jnp.zeros(op_ref.shape, F32)

    def chunk_body(c, carry):
        base = pl.multiple_of(c * chunk, chunk)
        r = pl.ds(base, chunk)
        finish(pl.ds(pl.multiple_of(jnp.maximum(c - 1, 0) * chunk, chunk), chunk))
        fc_ref[...] = fs_ref[r, :]
        cum_all = cs_ref[r, :]

        zero_blk = jnp.zeros((HGRN_SUB, HEAD_DIM), F32)
        qs, vbs, kbs, o_inter, a_off = [], [], [], [], []
        for cols in head_cols:
            h = len(qs)
            cum = cum_all[:, cols]
            q = q_ref[r, cols]
            k = ks_ref[r, cols]
            v = i_ref[r, cols]
            ends = [cum[(j + 1) * HGRN_SUB - 1:(j + 1) * HGRN_SUB, :] for j in range(n_sub)]
            cl = ends[-1]
            ke = [blk(k, j) * jnp.exp(ends[j] - blk(cum, j)) for j in range(n_sub)]
            qe = [blk(q, j) * jnp.exp(blk(cum, j) - ends[j - 1] if j else blk(cum, j)) for j in range(n_sub)]

            st = st_ref[h]
            q_in = jnp.concatenate([qe[j] * jnp.exp(ends[j - 1]) if j else qe[j] for j in range(n_sub)], axis=0)
            o_inter.append(_dot_nt(q_in.astype(BF16), st.astype(BF16)))
            kdec = jnp.concatenate(
                [ke[j] * jnp.exp(cl - ends[j]) if j < n_sub - 1 else ke[j] for j in range(n_sub)], axis=0)
            st_ref[h] = st * jnp.exp(cl) + _dot(v.T.astype(BF16), kdec.astype(BF16))

            offs = [None]
            for i in range(1, n_sub):
                kt = jnp.concatenate(
                    [ke[j] * jnp.exp(ends[i - 1] - ends[j]) if j < i - 1 else ke[j] for j in range(i)]
                    + [zero_blk] * (n_sub - i), axis=0)
                offs.append(_dot_nt(qe[i].astype(BF16), kt.astype(BF16)))
            a_off.append(offs)
            qs.append(q)
            vbs.append(v.astype(BF16))
            kbs.append(k.astype(BF16))

        res = []
        for h, cols in enumerate(head_cols):
            per_head = []
            for i in range(n_sub):
                lo_r = i * HGRN_SUB
                f_row = lambda s, lo_r=lo_r, cols=cols: fc_ref[lo_r + s:lo_r + s + 1, cols]
                q_blk = blk(qs[h], i)
                slabs = _hgrn_decayed_queries(q_blk[:SUBLANES], q_blk[SUBLANES:], f_row, row_is)
                stack = jnp.concatenate([slab for _, _, slab in slabs], axis=0)
                per_head.append(([(s, half) for s, half, _ in slabs],
                                 _dot_nt(stack.astype(BF16), kbs[h])))
            res.append(per_head)

        o_intra = []
        for h in range(heads_blk):
            rows = []
            for i in range(n_sub):
                order, prod = res[h][i]
                halves = [jnp.zeros((SUBLANES, chunk), F32), jnp.zeros((SUBLANES, chunk), F32)]
                for idx, (s, half) in enumerate(order):
                    piece = prod[idx * SUBLANES:(idx + 1) * SUBLANES]
                    halves[half] = jnp.where(lane_is[i * HGRN_SUB + s], piece, halves[half])
                a = jnp.concatenate(halves, axis=0)
                rows.append(a + a_off[h][i] if i else a)
            attn = jnp.concatenate(rows, axis=0)
            o_intra.append(_dot(attn.astype(BF16), vbs[h]))

        for h, cols in enumerate(head_cols):
            op_ref[:, cols] = o_inter[h] + o_intra[h]
        return carry

    lax.fori_loop(0, tc // chunk, chunk_body, 0, unroll=min(2, tc // chunk))
    finish(pl.ds(tc - chunk, chunk))

    @pl.when(c_idx == pl.num_programs(2) - 1)
    def _():
        for h in range(heads_blk):
            sout_ref[0, h] = st_ref[h].T


def _cast_job(cast_src, layer, steps, step_of, axis):
    _, r, c = cast_src.shape
    if axis == 2:
        assert c % (steps * LANES) == 0
        blk = (r, c // steps)
        at = lambda *g: (0, step_of(*g))
    else:
        assert r % (steps * 2 * SUBLANES) == 0
        blk = (r // steps, c)
        at = lambda *g: (step_of(*g), 0)
    in_spec = pl.BlockSpec((None,) + blk, lambda *g: (layer,) + at(*g))
    return in_spec, pl.BlockSpec(blk, at), jax.ShapeDtypeStruct((r, c), BF16), 2 * blk[0] * blk[1] * 6


def _cast_job_layers(cast_src, first, steps, step_of):
    depth, r, c = cast_src.shape
    count = depth - first
    per_layer = steps // count
    assert steps % count == 0 and r % (per_layer * 2 * SUBLANES) == 0
    blk = (None, r // per_layer, c)
    in_spec = pl.BlockSpec(blk, lambda *g: (first + step_of(*g) // per_layer, step_of(*g) % per_layer, 0))
    out_spec = pl.BlockSpec(blk, lambda *g: (step_of(*g) // per_layer, step_of(*g) % per_layer, 0))
    return in_spec, out_spec, jax.ShapeDtypeStruct((count, r, c), BF16), 2 * blk[1] * c * 6


def _hgrn(p, onorm_g, s0, s_acc, *, batch, seq, layer, cast_src=None, cast_layers=()):
    n = p.shape[0]
    heads = s0.shape[2]
    hb = HGRN_HEADS_PER_STEP if seq > HGRN_CHUNK else heads
    wblk = hb * HEAD_DIM
    n_hb = heads // hb
    width = heads * HEAD_DIM
    tc = min(512, seq)
    chunk = min(HGRN_CHUNK, seq)
    n_t = seq // tc
    row = lambda b, h, c: b * n_t + c
    need = 2 * 4 * tc * wblk * 4 + 2 * tc * wblk * 2 + 3 * tc * wblk * 4 + 5 * hb * HEAD_DIM * HEAD_DIM * 4
    n_cast = (cast_src is not None) + len(cast_layers)
    kernel = functools.partial(_hgrn_kernel, chunk=chunk, heads_blk=hb, n_cast=n_cast)
    st_spec = pl.BlockSpec((None, 1, hb, HEAD_DIM, HEAD_DIM), lambda b, h, c: (layer, b, h, 0, 0))
    in_specs = [
        pl.BlockSpec((tc, wblk), lambda b, h, c: (row(b, h, c), h)),
        pl.BlockSpec((tc, wblk), lambda b, h, c: (row(b, h, c), n_hb + h)),
        pl.BlockSpec((tc, wblk), lambda b, h, c: (row(b, h, c), 2 * n_hb + h)),
        pl.BlockSpec((tc, wblk), lambda b, h, c: (row(b, h, c), 3 * n_hb + h)),
        pl.BlockSpec((None, 1, wblk), lambda b, h, c: (layer, 0, h)),
        st_spec,
        pl.BlockSpec(memory_space=pl.ANY),
    ]
    out_specs = [pl.BlockSpec((tc, wblk), lambda b, h, c: (row(b, h, c), h)), st_spec]
    out_shape = [jax.ShapeDtypeStruct((n, width), BF16), jax.ShapeDtypeStruct(s_acc.shape, F32)]
    args = [p, p, p, p, onorm_g, s0, s_acc]
    steps = batch * n_hb * n_t
    step_of = lambda b, h, c: (b * n_hb + h) * n_t + c
    jobs = [] if cast_src is None else [(cast_src, _cast_job(cast_src, layer, steps, step_of, axis=2))]
    jobs += [(src, _cast_job_layers(src, first, steps, step_of)) for src, first in cast_layers]
    for src, (c_in, c_out, c_shape, c_bytes) in jobs:
        in_specs.append(c_in)
        out_specs.append(c_out)
        out_shape.append(c_shape)
        args.append(src)
        need += c_bytes
    return pl.pallas_call(
        kernel,
        out_shape=tuple(out_shape),
        grid=(batch, n_hb, n_t),
        in_specs=in_specs,
        out_specs=tuple(out_specs),
        input_output_aliases={6: 1},
        scratch_shapes=[
            pltpu.VMEM((hb, HEAD_DIM, HEAD_DIM), F32),
            pltpu.VMEM((tc, wblk), F32),
            pltpu.VMEM((tc, wblk), F32),
            pltpu.VMEM((tc, wblk), F32),
            pltpu.VMEM((chunk, wblk), F32),
            pltpu.VMEM((chunk, wblk), F32),
        ],
        compiler_params=pltpu.CompilerParams(
            dimension_semantics=("parallel", "parallel", "arbitrary"),
            vmem_limit_bytes=max(V7X_VMEM_CAP, _vmem_limit(need))),
        name="hgrn",
    )(*args)


def _s5_disc_lambda_kernel(lr_ref, li_ref, ls_ref, lbr_ref, lbi_ref, cr_ref, ci_ref):
    lr, li = lr_ref[...], li_ref[...]
    dt = jnp.exp(ls_ref[...])
    mag = jnp.exp(dt * lr)
    ang = dt * li
    lbr = mag * jnp.cos(ang)
    lbi = mag * jnp.sin(ang)
    nr, ni = lbr - 1.0, lbi
    den = lr * lr + li * li
    lbr_ref[...] = lbr
    lbi_ref[...] = lbi
    cr_ref[...] = (nr * lr + ni * li) / den
    ci_ref[...] = (ni * lr - nr * li) / den


def _s5_discretise(lam_re, lam_im, log_step):
    depth, groups, nst = lam_re.shape
    rows = depth * groups
    shp = jax.ShapeDtypeStruct((rows, nst), F32)
    outs = pl.pallas_call(
        _s5_disc_lambda_kernel, out_shape=(shp, shp, shp, shp), name="s5_disc_lambda",
    )(lam_re.reshape(rows, nst), lam_im.reshape(rows, nst), log_step.reshape(rows, 1))
    return tuple(o.reshape(depth, groups, nst) for o in outs)


def _s5_pair_weights(lbr, lbi, zoh_r, zoh_i, b_re, b_im, c_re, c_im, d_skip):
    depth, groups, nst, cg = b_re.shape
    hi = lax.Precision.HIGHEST
    zr, zi = zoh_r[..., None], zoh_i[..., None]
    bbr, bbi = zr * b_re - zi * b_im, zr * b_im + zi * b_re
    lr, li = lbr[..., None], lbi[..., None]
    blr, bli = bbr * lr - bbi * li, bbr * li + bbi * lr
    tb = jnp.swapaxes(jnp.stack([blr, bli, bbr, bbi], axis=1), 3, 4).reshape(depth, 4, groups * cg, nst)

    lrc, lic = lbr[:, :, None, :], lbi[:, :, None, :]
    pr, pi = c_re * lrc - c_im * lic, c_re * lic + c_im * lrc
    qr, qi = pr * lrc - pi * lic, pr * lic + pi * lrc
    tcc = jnp.stack([pr, -pi, qr, -qi], axis=1).transpose(0, 1, 3, 2, 4).reshape(depth, 4, cg, groups * nst)

    mm = lambda a, b: jnp.einsum("lgcn,lgnd->lgcd", a, b, precision=hi)
    g0 = mm(c_re, bbr) - mm(c_im, bbi)
    g1 = mm(pr, bbr) - mm(pi, bbi)
    g0d = g0 + d_skip.reshape(depth, groups, cg)[..., None] * jnp.eye(cg, dtype=F32)
    tg = jnp.stack([g0d, g1], axis=1).transpose(0, 1, 3, 2, 4).reshape(depth, 2, cg, groups * cg)
    lam2 = jnp.stack([lbr * lbr - lbi * lbi, 2.0 * lbr * lbi], axis=1).reshape(depth, 2, groups * nst)
    return tb, tcc, tg, lam2


def _s5_embed_weights(tb_ref, tcc_ref, tg_ref, wb_ref, wct_ref):
    nbk = wb_ref.shape[0]
    cg, nst = tcc_ref.shape[1], tb_ref.shape[2]
    bl = wb_ref.shape[2] // 2
    assert cg & (cg - 1) == 0 and nst & (nst - 1) == 0
    c_shift, n_shift = cg.bit_length() - 1, nst.bit_length() - 1
    iota = lambda shape, dim: lax.broadcasted_iota(jnp.int32, shape, dim)
    e_in = jnp.where((iota((nst, bl), 1) & (nst - 1)) == iota((nst, bl), 0), 1.0, 0.0).astype(BF16)
    e_out = jnp.where((iota((LANES, cg), 0) & (cg - 1)) == iota((LANES, cg), 1), 1.0, 0.0).astype(BF16)
    m_state = (iota((LANES, bl), 0) >> c_shift) == (iota((LANES, bl), 1) >> n_shift)
    m_direct = (iota((LANES, LANES), 0) >> c_shift) == (iota((LANES, LANES), 1) >> c_shift)
    zeros = jnp.zeros((LANES, LANES), BF16)
    for j in range(nbk):
        for k in range(4):
            half, part = divmod(k, 2)
            a = tb_ref[k, j * LANES:(j + 1) * LANES, :].astype(BF16)
            wb_ref[j, half * LANES:(half + 1) * LANES, part * bl:(part + 1) * bl] = (
                jnp.where(m_state, _dot(a, e_in), 0.0).astype(BF16))
        for k in range(4):
            eo, part = divmod(k, 2)
            x = tcc_ref[k, :, j * bl:(j + 1) * bl].astype(BF16)
            wct_ref[j, eo * LANES:(eo + 1) * LANES, part * bl:(part + 1) * bl] = (
                jnp.where(m_state, _dot(e_out, x), 0.0).astype(BF16))
        direct = [jnp.where(m_direct, _dot(e_out, tg_ref[k, :, j * LANES:(j + 1) * LANES].astype(BF16)), 0.0
                            ).astype(BF16) for k in range(2)]
        base = 2 * bl
        wct_ref[j, 0:LANES, base:base + LANES] = direct[0]
        wct_ref[j, 0:LANES, base + LANES:base + 2 * LANES] = zeros
        wct_ref[j, LANES:2 * LANES, base:base + LANES] = direct[1]
        wct_ref[j, LANES:2 * LANES, base + LANES:base + 2 * LANES] = direct[0]


def _s5_kernel(u_ref, tb_ref, tcc_ref, tg_ref, lam2_ref, wg_ref, bg_ref, x0r_ref, x0i_ref, accr_ref, acci_ref,
               *refs, batch, with_cast):
    del accr_ref, acci_ref
    refs, cast_srcs, cast_dsts = _split_cast_refs(refs, 3, int(with_cast))
    o_ref, xr_ref, xi_ref, xs_ref, y_ref, st_ref, ue_ref, uo_ref, tmp_ref, wb_ref, wct_ref = refs
    for src, dst in zip(cast_srcs, cast_dsts):
        dst[...] = src[...].astype(BF16)
    c_idx = pl.program_id(0)
    tt = u_ref.shape[0]
    tp = tt // 2
    prows = tp * batch
    width = u_ref.shape[1] // batch
    n_slab = width // LANES
    nbk = wb_ref.shape[0]
    bl = wb_ref.shape[2] // 2
    ns = nbk * bl
    grp = st_ref.shape[1]
    per = grp // batch

    @pl.when(c_idx == 0)
    def _():
        st_ref[0] = jnp.concatenate([x0r_ref[...]] * per, axis=0)
        st_ref[1] = jnp.concatenate([x0i_ref[...]] * per, axis=0)
        _s5_embed_weights(tb_ref, tcc_ref, tg_ref, wb_ref, wct_ref)

    for b in range(batch):
        for m in range(n_slab):
            tmp_ref[...] = u_ref[:, b * width + m * LANES:b * width + (m + 1) * LANES]
            ue_ref[m, pl.ds(b, tp, stride=batch), :] = tmp_ref[pl.ds(0, tp, stride=2), :]
            uo_ref[m, pl.ds(b, tp, stride=batch), :] = tmp_ref[pl.ds(1, tp, stride=2), :]

    for j in range(nbk):
        lhs = jnp.concatenate([ue_ref[j], uo_ref[j]], axis=1).astype(BF16)
        w = _dot(lhs, wb_ref[j])
        xs_ref[:, j * bl:(j + 1) * bl] = w[:, :bl]
        xs_ref[:, ns + j * bl:ns + (j + 1) * bl] = w[:, bl:]

    second = lax.broadcasted_iota(jnp.int32, (grp, S5_SCAN_LANES), 0) >= batch
    for cb in range(ns // S5_SCAN_LANES):
        lo = cb * S5_SCAN_LANES
        re_l = slice(lo, lo + S5_SCAN_LANES)
        im_l = slice(ns + lo, ns + lo + S5_SCAN_LANES)
        ar = lam2_ref[0:1, re_l]
        ai = lam2_ref[1:2, re_l]

        def step(g, carry, ar=ar, ai=ai, re_l=re_l, im_l=im_l):
            cr, ci = carry
            r = pl.ds(pl.multiple_of(g * grp, grp), grp)
            wr = xs_ref[r, re_l]
            wi = xs_ref[r, im_l]
            if per == 2:
                tr = ar * cr - ai * ci + pltpu.roll(wr, batch, axis=0)
                ti = ar * ci + ai * cr + pltpu.roll(wi, batch, axis=0)
                xs_ref[r, re_l] = jnp.where(second, tr, cr)
                xs_ref[r, im_l] = jnp.where(second, ti, ci)
                vr = ar * tr - ai * ti + wr
                vi = ar * ti + ai * tr + wi
                nr = jnp.where(second, vr, pltpu.roll(vr, batch, axis=0))
                ni = jnp.where(second, vi, pltpu.roll(vi, batch, axis=0))
            else:
                xs_ref[r, re_l] = cr
                xs_ref[r, im_l] = ci
                nr = ar * cr - ai * ci + wr
                ni = ar * ci + ai * cr + wi
            return nr, ni

        cr, ci = lax.fori_loop(0, prows // grp, step, (st_ref[0, :, re_l], st_ref[1, :, re_l]), unroll=2)
        st_ref[0, :, re_l] = cr
        st_ref[1, :, re_l] = ci

    for j in range(nbk):
        lhs = jnp.concatenate([xs_ref[:, j * bl:(j + 1) * bl], xs_ref[:, ns + j * bl:ns + (j + 1) * bl],
                               ue_ref[j], uo_ref[j]], axis=1).astype(BF16)
        yj = _dot_nt(lhs, wct_ref[j])
        y_ref[0:prows, j * LANES:(j + 1) * LANES] = yj[:, :LANES]
        y_ref[prows:2 * prows, j * LANES:(j + 1) * LANES] = yj[:, LANES:]

    hh = jax.nn.gelu(y_ref[...])
    gate = _sigmoid(_dot(hh.astype(BF16), wg_ref[...]) + bg_ref[...])
    out = hh * gate
    for m in range(n_slab):
        ue_ref[m] = out[0:prows, m * LANES:(m + 1) * LANES]
        uo_ref[m] = out[prows:2 * prows, m * LANES:(m + 1) * LANES]
    for b in range(batch):
        for m in range(n_slab):
            tmp_ref[pl.ds(0, tp, stride=2), :] = ue_ref[m, pl.ds(b, tp, stride=batch), :]
            tmp_ref[pl.ds(1, tp, stride=2), :] = uo_ref[m, pl.ds(b, tp, stride=batch), :]
            o_ref[:, b * width + m * LANES:b * width + (m + 1) * LANES] = tmp_ref[...].astype(BF16)

    @pl.when(c_idx == pl.num_programs(0) - 1)
    def _():
        xr_ref[...] = st_ref[0, grp - batch:grp, :]
        xi_ref[...] = st_ref[1, grp - batch:grp, :]


def _s5(u, tb, tcc, tg, lam2, w_glu, b_glu, x0r, x0i, acc_r, acc_i, *, batch, layer, cast_src=None):
    seq = u.shape[0]
    width = u.shape[1] // batch
    n = seq * batch
    ns = lam2.shape[2]
    nbk = width // LANES
    bl = ns // nbk
    assert batch % SUBLANES == 0 or 2 * batch == SUBLANES
    assert seq % 2 == 0 and tb.shape[2] == width
    grp = max(batch, SUBLANES)
    rows = min(S5_ROWS, n)
    tt = rows // batch
    prows = rows // 2
    lay4 = lambda c: (layer, 0, 0, 0)
    lay3 = lambda c: (layer, 0, 0)
    st_in = pl.BlockSpec((None, batch, ns), lay3)
    resident = pl.Buffered(1)
    wb_shape = (nbk, 2 * LANES, 2 * bl)
    wct_shape = (nbk, 2 * LANES, 2 * bl + 2 * LANES)
    need = (2 * rows * width * 4 + (nbk * 2 * LANES * (4 * bl + 2 * LANES) + width * width) * 2
            + 4 * width * LANES * 4 + 6 * 16 * ns * 4
            + 2 * rows * width * 2 + prows * 2 * ns * 4 + rows * width * 4 + 4 * rows * width * 4
            + 4 * prows * (4 * bl + 2 * LANES) + (8 * batch + 2 * grp) * ns * 4)
    st_shape = jax.ShapeDtypeStruct(acc_r.shape, F32)
    in_specs = [
        pl.BlockSpec((tt, batch * width), lambda c: (c, 0)),
        pl.BlockSpec((None,) + tb.shape[1:], lay4, pipeline_mode=resident),
        pl.BlockSpec((None,) + tcc.shape[1:], lay4, pipeline_mode=resident),
        pl.BlockSpec((None,) + tg.shape[1:], lay4, pipeline_mode=resident),
        pl.BlockSpec((None, 2, ns), lay3),
        pl.BlockSpec((None, width, width), lay3, pipeline_mode=resident),
        pl.BlockSpec((None, 1, width), lay3),
        st_in,
        st_in,
        pl.BlockSpec(memory_space=pl.ANY),
        pl.BlockSpec(memory_space=pl.ANY),
    ]
    out_specs = [pl.BlockSpec((tt, batch * width), lambda c: (c, 0)), st_in, st_in]
    out_shape = [jax.ShapeDtypeStruct((seq, batch * width), BF16), st_shape, st_shape]
    args = [u, tb, tcc, tg, lam2, w_glu, b_glu, x0r, x0i, acc_r, acc_i]
    if cast_src is not None:
        c_in, c_out, c_shape, c_bytes = _cast_job(cast_src, layer, n // rows, lambda c: c, axis=1)
        in_specs.append(c_in)
        out_specs.append(c_out)
        out_shape.append(c_shape)
        args.append(cast_src)
        need += c_bytes
    return pl.pallas_call(
        functools.partial(_s5_kernel, batch=batch, with_cast=cast_src is not None),
        out_shape=tuple(out_shape),
        grid=(n // rows,),
        in_specs=in_specs,
        out_specs=tuple(out_specs),
        input_output_aliases={9: 1, 10: 2},
        scratch_shapes=[
            pltpu.VMEM((prows, 2 * ns), F32),
            pltpu.VMEM((rows, width), F32),
            pltpu.VMEM((2, grp, ns), F32),
            pltpu.VMEM((width // LANES, prows, LANES), F32),
            pltpu.VMEM((width // LANES, prows, LANES), F32),
            pltpu.VMEM((tt, LANES), F32),
            pltpu.VMEM(wb_shape, BF16),
            pltpu.VMEM(wct_shape, BF16),
        ],
        compiler_params=pltpu.CompilerParams(
            dimension_semantics=("arbitrary",),
            vmem_limit_bytes=_vmem_limit(need)),
        name="s5",
    )(*args)


def _outproj_kernel(x_ref, oh_ref, os_ref, wh_ref, ws_ref, o_ref):
    o_ref[...] = x_ref[...] + _dot(oh_ref[...], wh_ref[...]) + _dot(os_ref[...], ws_ref[...])


def _outproj(x, o_h, o_s, w_out, *, layer, tm, tn, seq):
    n, d = x.shape
    kh = o_h.shape[1]
    ks = w_out.shape[1] - kh
    assert kh == ks
    if o_s.shape[0] == n:
        os_map = lambda i, j: (i, 0)
    else:
        n_t = seq // tm
        os_map = lambda i, j: (i % n_t, i // n_t)
    need = 2 * (2 * tm * tn * 4 + tm * (kh + ks) * 2 + (kh + ks) * tn * 2) + 2 * tm * tn * 4
    return pl.pallas_call(
        _outproj_kernel,
        out_shape=jax.ShapeDtypeStruct((n, d), F32),
        grid=(n // tm, d // tn),
        in_specs=[
            pl.BlockSpec((tm, tn), lambda i, j: (i, j)),
            pl.BlockSpec((tm, kh), lambda i, j: (i, 0)),
            pl.BlockSpec((tm, ks), os_map),
            pl.BlockSpec((None, kh, tn), lambda i, j: (layer, 0, j)),
            pl.BlockSpec((None, ks, tn), lambda i, j: (layer, 1, j)),
        ],
        out_specs=pl.BlockSpec((tm, tn), lambda i, j: (i, j)),
        compiler_params=pltpu.CompilerParams(
            dimension_semantics=("parallel", "parallel"),
            vmem_limit_bytes=_vmem_limit(need)),
        name="outproj",
    )(x, o_h, o_s, w_out, w_out)


def _ffn_kernel(x_ref, g_ref, w1_ref, w2_ref, gf_ref, o_ref, h_ref, *, final_norm):
    f = pl.program_id(1)
    tm = x_ref.shape[0]

    @pl.when(f == 0)
    def _():
        g = g_ref[...]

        def body(i, carry):
            r = pl.ds(pl.multiple_of(i * NORM_ROWS, NORM_ROWS), NORM_ROWS)
            h_ref[r, :] = _rmsnorm(x_ref[r, :], g).astype(BF16)
            o_ref[r, :] = jnp.zeros((NORM_ROWS, o_ref.shape[1]), F32)
            return carry

        lax.fori_loop(0, tm // NORM_ROWS, body, 0, unroll=NORM_UNROLL)

    a = _dot(h_ref[...], w1_ref[...])
    a = jnp.square(jnp.maximum(a, 0.0)).astype(BF16)
    cw = o_ref.shape[1] // FFN_ACC_CHUNKS
    for c in range(FFN_ACC_CHUNKS):
        o_ref[:, c * cw:(c + 1) * cw] += _dot(a, w2_ref[:, c * cw:(c + 1) * cw])

    @pl.when(f == pl.num_programs(1) - 1)
    def _():
        gf = gf_ref[...]

        def body(i, carry):
            r = pl.ds(pl.multiple_of(i * NORM_ROWS, NORM_ROWS), NORM_ROWS)
            y = x_ref[r, :] + o_ref[r, :]
            if final_norm:
                y = _rmsnorm(y, gf)
            o_ref[r, :] = y
            return carry

        lax.fori_loop(0, tm // NORM_ROWS, body, 0, unroll=NORM_UNROLL)


def _ffn(x, g, w1, w2, gf, *, layer, tm, tf, final_norm):
    n, d = x.shape
    ff = w1.shape[1]
    need = tm * d * 4 + tm * d * 2 + 2 * tm * d * 4 + 4 * d * tf * 2 + tm * tf * 6 + tm * d * 4
    kernel = functools.partial(_ffn_kernel, final_norm=final_norm)
    return pl.pallas_call(
        kernel,
        out_shape=jax.ShapeDtypeStruct((n, d), F32),
        grid=(n // tm, ff // tf),
        in_specs=[
            pl.BlockSpec((tm, d), lambda i, f: (i, 0), pipeline_mode=pl.Buffered(1)),
            pl.BlockSpec((None, 1, d), lambda i, f: (layer, 0, 0)),
            pl.BlockSpec((d, tf), lambda i, f: (0, f)),
            pl.BlockSpec((tf, d), lambda i, f: (f, 0)),
            pl.BlockSpec((1, d), lambda i, f: (0, 0)),
        ],
        out_specs=pl.BlockSpec((tm, d), lambda i, f: (i, 0)),
        scratch_shapes=[pltpu.VMEM((tm, d), BF16)],
        compiler_params=pltpu.CompilerParams(
            dimension_semantics=("parallel", "arbitrary"),
            vmem_limit_bytes=_vmem_limit(need)),
        name="ffn",
    )(x, g, w1, w2, gf)


def _trunk(x, st_h, st_r, st_i, wts, cast_w=None):
    batch, seq, d = x.shape
    n = batch * seq
    depth = wts["w_in"].shape[0]
    cast = cast_w is None
    made = {"ffn": []} if cast else cast_w
    heads = st_h.shape[2]
    hgrn_width = heads * HEAD_DIM
    groups, nst = st_r.shape[2], st_r.shape[3]
    s5_width = wts["w_glu"].shape[1]
    tm = min(1024, n)
    xf = x.reshape(n, d)
    x0r = st_r.reshape(depth, batch, groups * nst)
    x0i = st_i.reshape(depth, batch, groups * nst)
    time_major = seq % tm == 0
    new_h = jnp.zeros(st_h.shape, F32)
    new_r = jnp.zeros(x0r.shape, F32)
    new_i = jnp.zeros(x0i.shape, F32)
    for l in range(depth):
        w_in, w_layer = (wts["w_in0"], 0) if l == 0 else (made["w_in_rest"], l - 1)
        p, u = _inproj(xf, wts["norm1_g"], w_in, wts["lb_logits"],
                       layer=l, w_layer=w_layer, tm=tm, tn=s5_width, batch=batch, seq=seq)
        hosted = [(wts["w_out"], 0)] + ([(wts["w_in"], 1)] if depth > 1 else []) if cast and l == 0 else []
        o_h, new_h, *w1 = _hgrn(p, wts["onorm_g"], st_h, new_h, batch=batch, seq=seq, layer=l,
                                cast_src=wts["w_ff1"] if cast else None, cast_layers=hosted)
        if hosted:
            made["w_out"] = w1[1]
            made["w_in_rest"] = w1[2] if depth > 1 else None
        if not time_major:
            u = u.reshape(batch, seq, s5_width).transpose(1, 0, 2).reshape(seq, batch * s5_width)
        o_s, new_r, new_i, *w2 = _s5(u, wts["s5_tb"], wts["s5_tcc"], wts["s5_tg"], wts["s5_lam2"],
                                     wts["w_glu"], wts["b_glu"], x0r, x0i, new_r, new_i,
                                     batch=batch, layer=l, cast_src=wts["w_ff2"] if cast else None)
        if not time_major:
            o_s = o_s.reshape(seq, batch, s5_width).transpose(1, 0, 2).reshape(n, s5_width)
        if cast:
            made["ffn"].append((w1[0], w2[0]))
        x1 = _outproj(xf, o_h, o_s, made["w_out"], layer=l, tm=min(512, n), tn=d, seq=seq)
        xf = _ffn(x1, wts["norm2_g"], *made["ffn"][l], wts["final_g"],
                  layer=l, tm=tm, tf=1024, final_norm=(l == depth - 1))
    return (xf.reshape(batch, seq, d), new_h, new_r.reshape(st_r.shape), new_i.reshape(st_i.shape)), made


def kernel(x_prompt, x_sample, state_hgrn, state_s5_re, state_s5_im, norm1_g, w_in, hgrn_lb_logits,
           hgrn_onorm_g, s5_lambda_re, s5_lambda_im, s5_log_step, s5_B_re, s5_B_im, s5_C_re, s5_C_im,
           s5_D, s5_w_glu, s5_b_glu, w_out, norm2_g, w_ff1, w_ff2, final_norm_g):
    depth, d = norm1_g.shape
    row3 = lambda a: a.reshape(depth, 1, a.shape[-1])
    lbr, lbi, zoh_r, zoh_i = _s5_discretise(s5_lambda_re, s5_lambda_im, s5_log_step)
    s5_tb, s5_tcc, s5_tg, s5_lam2 = _s5_pair_weights(lbr, lbi, zoh_r, zoh_i, s5_B_re, s5_B_im,
                                                     s5_C_re, s5_C_im, s5_D)
    wts = {
        "norm1_g": row3(norm1_g), "norm2_g": row3(norm2_g), "final_g": final_norm_g.reshape(1, d),
        "lb_logits": hgrn_lb_logits, "onorm_g": row3(hgrn_onorm_g),
        "w_in0": _cast_first_layer(w_in),
        "w_in": w_in, "w_out": w_out, "w_ff1": w_ff1, "w_ff2": w_ff2,
        "w_glu": s5_w_glu.astype(BF16), "b_glu": row3(s5_b_glu),
        "s5_tb": s5_tb, "s5_tcc": s5_tcc, "s5_tg": s5_tg, "s5_lam2": s5_lam2,
    }
    bp = x_prompt.shape[0]
    zh = jnp.zeros((depth, bp) + state_hgrn.shape[2:], F32)
    zs = jnp.zeros((depth, bp) + state_s5_re.shape[2:], F32)
    (y_p, hp, rp, ip), cast_w = _trunk(x_prompt, zh, zs, zs, wts)
    (y_s, hs, rs, is_), _ = _trunk(x_sample, state_hgrn, state_s5_re, state_s5_im, wts, cast_w)
    return (y_p, y_s, hp, rp, ip, hs, rs, is_)
```

```python
import functools

import jax
import jax.numpy as jnp
from jax import lax
from jax.experimental import pallas as pl
from jax.experimental.pallas import tpu as pltpu

F32 = jnp.float32
BF16 = jnp.bfloat16
EPS = 1e-6

HEAD_DIM = 128
HGRN_CHUNK = 64
HGRN_SUB = 16
HGRN_HEADS_PER_STEP = 8
SUBLANES = 8
LANES = 128
S5_SCAN_LANES = 512
S5_ROWS = 512

V7X_VMEM_CAP = 56 * 1024 * 1024
NORM_ROWS = 64
NORM_UNROLL = 8
FFN_ACC_CHUNKS = 4
INPROJ_CHUNKS = 4


def _vmem_limit(nbytes):
    return int(min(V7X_VMEM_CAP, nbytes * 5 // 4 + (4 << 20)))


def _rmsnorm(x, g):
    return x * lax.rsqrt(jnp.mean(jnp.square(x), axis=-1, keepdims=True) + EPS) * g


def _sigmoid(x):
    return 1.0 / (1.0 + jnp.exp(-x))


def _log1p_exp_neg_abs(x):
    return jnp.log(1.0 + jnp.exp(-jnp.abs(x)))


def _softplus(x):
    return jnp.maximum(x, 0.0) + _log1p_exp_neg_abs(x)


def _dot(a, b):
    return jnp.dot(a, b, preferred_element_type=F32)


def _dot_nt(a, b):
    return lax.dot_general(a, b, (((1,), (1,)), ((), ())), preferred_element_type=F32)


def _inproj_kernel(x_ref, g_ref, w_ref, lbl_ref, p_ref, u_ref, h_ref, *, layer):
    tm = x_ref.shape[0]
    cw = w_ref.shape[1] // INPROJ_CHUNKS
    j = pl.program_id(1)

    @pl.when(j == 0)
    def _():
        g = g_ref[...]

        def body(i, carry):
            r = pl.ds(pl.multiple_of(i * NORM_ROWS, NORM_ROWS), NORM_ROWS)
            h_ref[r, :] = _rmsnorm(x_ref[r, :], g).astype(BF16)
            return carry

        lax.fori_loop(0, tm // NORM_ROWS, body, 0, unroll=NORM_UNROLL)

    def emit(out_ref, fn):
        for c in range(INPROJ_CHUNKS):
            sl = slice(c * cw, (c + 1) * cw)
            out_ref[:, sl] = fn(_dot(h_ref[...], w_ref[:, sl]), sl)

    def silu(a, sl):
        return a * _sigmoid(a)

    def log_forget(z, sl):
        logits = lbl_ref[:, sl]
        e = jnp.exp(logits - jnp.max(logits, axis=0, keepdims=True))
        sm = e / jnp.sum(e, axis=0, keepdims=True)
        cum0 = sm[0:1]
        cuml = cum0
        for k in range(1, layer + 1):
            cuml = cuml + sm[k:k + 1]
        lb = cuml - cum0
        log_lb = jnp.log(lb)
        b = jnp.log1p(-lb) - _softplus(-z)
        delta = log_lb - b
        return jnp.where(jnp.isnan(delta), log_lb + b,
                         jnp.maximum(log_lb, b) + _log1p_exp_neg_abs(delta))

    for tile, (out_ref, fn) in enumerate([(p_ref, silu), (p_ref, log_forget), (p_ref, lambda a, sl: a),
                                          (p_ref, silu), (u_ref, lambda a, sl: a)]):
        pl.when(j == tile)(functools.partial(emit, out_ref, fn))


def _inproj(x, g, w, lb_logits, *, layer, w_layer, tm, tn, batch, seq):
    n, d = x.shape
    cols = w.shape[2]
    n_j = cols // tn
    depth = lb_logits.shape[0]
    assert n_j == 5 and lb_logits.shape[1] == tn
    time_major = seq % tm == 0
    if time_major:
        n_t = seq // tm
        u_shape, u_map = (seq, batch * tn), (lambda i, j: (i % n_t, i // n_t))
    else:
        u_shape, u_map = (n, tn), (lambda i, j: (i, 0))
    need = 2 * tm * d * 4 + tm * d * 2 + 2 * d * tn * 2 + 5 * tm * tn * 4
    return pl.pallas_call(
        functools.partial(_inproj_kernel, layer=layer),
        out_shape=(jax.ShapeDtypeStruct((n, cols - tn), F32), jax.ShapeDtypeStruct(u_shape, F32)),
        grid=(n // tm, n_j),
        in_specs=[
            pl.BlockSpec((tm, d), lambda i, j: (i, 0)),
            pl.BlockSpec((None, 1, d), lambda i, j: (layer, 0, 0)),
            pl.BlockSpec((None, d, tn), lambda i, j: (w_layer, 0, j)),
            pl.BlockSpec((depth, tn), lambda i, j: (0, 0)),
        ],
        out_specs=(pl.BlockSpec((tm, tn), lambda i, j: (i, jnp.minimum(j, n_j - 2))),
                   pl.BlockSpec((tm, tn), u_map)),
        scratch_shapes=[pltpu.VMEM((tm, d), BF16)],
        compiler_params=pltpu.CompilerParams(
            dimension_semantics=("parallel", "arbitrary"),
            vmem_limit_bytes=_vmem_limit(need)),
        name="inproj",
    )(x, g, w, lb_logits)


def _cast_kernel(src_ref, dst_ref):
    dst_ref[...] = src_ref[...].astype(BF16)


def _cast_first_layer(w, *, rows=256):
    _, r, c = w.shape
    spec = pl.BlockSpec((None, rows, c), lambda i: (0, i, 0))
    return pl.pallas_call(
        _cast_kernel, out_shape=jax.ShapeDtypeStruct((1, r, c), BF16), grid=(r // rows,),
        in_specs=[spec], out_specs=spec,
        compiler_params=pltpu.CompilerParams(vmem_limit_bytes=_vmem_limit(2 * rows * c * 6)),
        name="cast_first_layer",
    )(w)


def _hgrn_decayed_queries(q_lo, q_hi, f_row, row_is):
    slabs = []
    qd_lo = qd_hi = None
    for s in range(2 * SUBLANES - 1, -1, -1):
        if s == 2 * SUBLANES - 1:
            qd_hi = jnp.where(row_is[SUBLANES - 1], q_hi, 0.0)
        elif s >= SUBLANES:
            qd_hi = jnp.where(row_is[s - SUBLANES], q_hi, qd_hi * f_row(s + 1))
        else:
            fn = f_row(s + 1)
            qd_hi = qd_hi * fn
            qd_lo = jnp.where(row_is[s], q_lo, 0.0 if s == SUBLANES - 1 else qd_lo * fn)
        slabs.append((s, 1, qd_hi))
        if s < SUBLANES:
            slabs.append((s, 0, qd_lo))
    return slabs


def _split_cast_refs(refs, n_out, n_cast):
    srcs, rest = refs[:n_cast], refs[n_cast:]
    return rest[:n_out] + rest[n_out + n_cast:], srcs, rest[n_out:n_out + n_cast]


def _hgrn_kernel(q_ref, f_ref, i_ref, g_ref, on_ref, s0_ref, acc_ref, *refs, chunk, heads_blk, n_cast):
    del acc_ref
    refs, cast_srcs, cast_dsts = _split_cast_refs(refs, 2, n_cast)
    o_ref, sout_ref, st_ref, ks_ref, cs_ref, fs_ref, fc_ref, op_ref = refs
    for src, dst in zip(cast_srcs, cast_dsts):
        dst[...] = src[...].astype(BF16)
    c_idx = pl.program_id(2)
    tc = q_ref.shape[0]
    n_sub = chunk // HGRN_SUB

    @pl.when(c_idx == 0)
    def _():
        for h in range(heads_blk):
            st_ref[h] = s0_ref[0, h].T

    logf = f_ref[...]
    fgate = jnp.exp(logf)
    fs_ref[...] = fgate
    ks_ref[...] = 1.0 - fgate

    t_i = lax.broadcasted_iota(jnp.int32, (chunk, chunk), 0)
    s_i = lax.broadcasted_iota(jnp.int32, (chunk, chunk), 1)
    tri = jnp.where(t_i >= s_i, 1.0, 0.0).astype(BF16)
    for j in range(tc // chunk):
        lf = logf[j * chunk:(j + 1) * chunk]
        hi = lf.astype(BF16)
        r1 = lf - hi.astype(F32)
        mid = r1.astype(BF16)
        lo = (r1 - mid.astype(F32)).astype(BF16)
        cs_ref[j * chunk:(j + 1) * chunk, :] = _dot(tri, hi) + _dot(tri, mid) + _dot(tri, lo)

    row8 = lax.broadcasted_iota(jnp.int32, (SUBLANES, HEAD_DIM), 0)
    row_is = [row8 == j for j in range(SUBLANES)]
    lane8 = lax.broadcasted_iota(jnp.int32, (SUBLANES, chunk), 1)
    lane_is = [lane8 == j for j in range(chunk)]
    onorm = on_ref[...]
    blk = lambda a, j: a[j * HGRN_SUB:(j + 1) * HGRN_SUB]

    head_cols = [slice(h * HEAD_DIM, (h + 1) * HEAD_DIM) for h in range(heads_blk)]

    def finish(rows):
        for cols in head_cols:
            o = op_ref[:, cols]
            o = o * lax.rsqrt(jnp.mean(jnp.square(o), axis=-1, keepdims=True) + EPS)
            o = o * onorm[:, cols]
            o_ref[rows, cols] = (o * g_ref[rows, cols]).astype(BF16)

    op_ref[...] = jnp.zeros(op_ref.shape, F32)

    def chunk_body(c, carry):
        base = pl.multiple_of(c * chunk, chunk)
        r = pl.ds(base, chunk)
        finish(pl.ds(pl.multiple_of(jnp.maximum(c - 1, 0) * chunk, chunk), chunk))
        fc_ref[...] = fs_ref[r, :]
        cum_all = cs_ref[r, :]

        zero_blk = jnp.zeros((HGRN_SUB, HEAD_DIM), F32)
        qs, vbs, kbs, o_inter, a_off = [], [], [], [], []
        for cols in head_cols:
            h = len(qs)
            cum = cum_all[:, cols]
            q = q_ref[r, cols]
            k = ks_ref[r, cols]
            v = i_ref[r, cols]
            ends = [cum[(j + 1) * HGRN_SUB - 1:(j + 1) * HGRN_SUB, :] for j in range(n_sub)]
            cl = ends[-1]
            ke = [blk(k, j) * jnp.exp(ends[j] - blk(cum, j)) for j in range(n_sub)]
            qe = [blk(q, j) * jnp.exp(blk(cum, j) - ends[j - 1] if j else blk(cum, j)) for j in range(n_sub)]

            st = st_ref[h]
            q_in = jnp.concatenate([qe[j] * jnp.exp(ends[j - 1]) if j else qe[j] for j in range(n_sub)], axis=0)
            o_inter.append(_dot_nt(q_in.astype(BF16), st.astype(BF16)))
            kdec = jnp.concatenate(
                [ke[j] * jnp.exp(cl - ends[j]) if j < n_sub - 1 else ke[j] for j in range(n_sub)], axis=0)
            st_ref[h] = st * jnp.exp(cl) + _dot(v.T.astype(BF16), kdec.astype(BF16))

            offs = [None]
            for i in range(1, n_sub):
                kt = jnp.concatenate(
                    [ke[j] * jnp.exp(ends[i - 1] - ends[j]) if j < i - 1 else ke[j] for j in range(i)]
                    + [zero_blk] * (n_sub - i), axis=0)
                offs.append(_dot_nt(qe[i].astype(BF16), kt.astype(BF16)))
            a_off.append(offs)
            qs.append(q)
            vbs.append(v.astype(BF16))
            kbs.append(k.astype(BF16))

        res = []
        for h, cols in enumerate(head_cols):
            per_head = []
            for i in range(n_sub):
                lo_r = i * HGRN_SUB
                f_row = lambda s, lo_r=lo_r, cols=cols: fc_ref[lo_r + s:lo_r + s + 1, cols]
                q_blk = blk(qs[h], i)
                slabs = _hgrn_decayed_queries(q_blk[:SUBLANES], q_blk[SUBLANES:], f_row, row_is)
                stack = jnp.concatenate([slab for _, _, slab in slabs], axis=0)
                per_head.append(([(s, half) for s, half, _ in slabs],
                                 _dot_nt(stack.astype(BF16), kbs[h])))
            res.append(per_head)

        o_intra = []
        for h in range(heads_blk):
            rows = []
            for i in range(n_sub):
                order, prod = res[h][i]
                halves = [jnp.zeros((SUBLANES, chunk), F32), jnp.zeros((SUBLANES, chunk), F32)]
                for idx, (s, half) in enumerate(order):
                    piece = prod[idx * SUBLANES:(idx + 1) * SUBLANES]
                    halves[half] = jnp.where(lane_is[i * HGRN_SUB + s], piece, halves[half])
                a = jnp.concatenate(halves, axis=0)
                rows.append(a + a_off[h][i] if i else a)
            attn = jnp.concatenate(rows, axis=0)
            o_intra.append(_dot(attn.astype(BF16), vbs[h]))

        for h, cols in enumerate(head_cols):
            op_ref[:, cols] = o_inter[h] + o_intra[h]
        return carry

    lax.fori_loop(0, tc // chunk, chunk_body, 0, unroll=min(2, tc // chunk))
    finish(pl.ds(tc - chunk, chunk))

    @pl.when(c_idx == pl.num_programs(2) - 1)
    def _():
        for h in range(heads_blk):
            sout_ref[0, h] = st_ref[h].T


def _cast_job(cast_src, layer, steps, step_of, axis):
    _, r, c = cast_src.shape
    if axis == 2:
        assert c % (steps * LANES) == 0
        blk = (r, c // steps)
        at = lambda *g: (0, step_of(*g))
    else:
        assert r % (steps * 2 * SUBLANES) == 0
        blk = (r // steps, c)
        at = lambda *g: (step_of(*g), 0)
    in_spec = pl.BlockSpec((None,) + blk, lambda *g: (layer,) + at(*g))
    return in_spec, pl.BlockSpec(blk, at), jax.ShapeDtypeStruct((r, c), BF16), 2 * blk[0] * blk[1] * 6


def _cast_job_layers(cast_src, first, steps, step_of):
    depth, r, c = cast_src.shape
    count = depth - first
    per_layer = steps // count
    assert steps % count == 0 and r % (per_layer * 2 * SUBLANES) == 0
    blk = (None, r // per_layer, c)
    in_spec = pl.BlockSpec(blk, lambda *g: (first + step_of(*g) // per_layer, step_of(*g) % per_layer, 0))
    out_spec = pl.BlockSpec(blk, lambda *g: (step_of(*g) // per_layer, step_of(*g) % per_layer, 0))
    return in_spec, out_spec, jax.ShapeDtypeStruct((count, r, c), BF16), 2 * blk[1] * c * 6


def _hgrn(p, onorm_g, s0, s_acc, *, batch, seq, layer, cast_src=None, cast_layers=()):
    n = p.shape[0]
    heads = s0.shape[2]
    hb = HGRN_HEADS_PER_STEP if seq > HGRN_CHUNK else heads
    wblk = hb * HEAD_DIM
    n_hb = heads // hb
    width = heads * HEAD_DIM
    tc = min(512, seq)
    chunk = min(HGRN_CHUNK, seq)
    n_t = seq // tc
    row = lambda b, h, c: b * n_t + c
    need = 2 * 4 * tc * wblk * 4 + 2 * tc * wblk * 2 + 3 * tc * wblk * 4 + 5 * hb * HEAD_DIM * HEAD_DIM * 4
    n_cast = (cast_src is not None) + len(cast_layers)
    kernel = functools.partial(_hgrn_kernel, chunk=chunk, heads_blk=hb, n_cast=n_cast)
    st_spec = pl.BlockSpec((None, 1, hb, HEAD_DIM, HEAD_DIM), lambda b, h, c: (layer, b, h, 0, 0))
    in_specs = [
        pl.BlockSpec((tc, wblk), lambda b, h, c: (row(b, h, c), h)),
        pl.BlockSpec((tc, wblk), lambda b, h, c: (row(b, h, c), n_hb + h)),
        pl.BlockSpec((tc, wblk), lambda b, h, c: (row(b, h, c), 2 * n_hb + h)),
        pl.BlockSpec((tc, wblk), lambda b, h, c: (row(b, h, c), 3 * n_hb + h)),
        pl.BlockSpec((None, 1, wblk), lambda b, h, c: (layer, 0, h)),
        st_spec,
        pl.BlockSpec(memory_space=pl.ANY),
    ]
    out_specs = [pl.BlockSpec((tc, wblk), lambda b, h, c: (row(b, h, c), h)), st_spec]
    out_shape = [jax.ShapeDtypeStruct((n, width), BF16), jax.ShapeDtypeStruct(s_acc.shape, F32)]
    args = [p, p, p, p, onorm_g, s0, s_acc]
    steps = batch * n_hb * n_t
    step_of = lambda b, h, c: (b * n_hb + h) * n_t + c
    jobs = [] if cast_src is None else [(cast_src, _cast_job(cast_src, layer, steps, step_of, axis=2))]
    jobs += [(src, _cast_job_layers(src, first, steps, step_of)) for src, first in cast_layers]
    for src, (c_in, c_out, c_shape, c_bytes) in jobs:
        in_specs.append(c_in)
        out_specs.append(c_out)
        out_shape.append(c_shape)
        args.append(src)
        need += c_bytes
    return pl.pallas_call(
        kernel,
        out_shape=tuple(out_shape),
        grid=(batch, n_hb, n_t),
        in_specs=in_specs,
        out_specs=tuple(out_specs),
        input_output_aliases={6: 1},
        scratch_shapes=[
            pltpu.VMEM((hb, HEAD_DIM, HEAD_DIM), F32),
            pltpu.VMEM((tc, wblk), F32),
            pltpu.VMEM((tc, wblk), F32),
            pltpu.VMEM((tc, wblk), F32),
            pltpu.VMEM((chunk, wblk), F32),
            pltpu.VMEM((chunk, wblk), F32),
        ],
        compiler_params=pltpu.CompilerParams(
            dimension_semantics=("parallel", "parallel", "arbitrary"),
            vmem_limit_bytes=max(V7X_VMEM_CAP, _vmem_limit(need))),
        name="hgrn",
    )(*args)


def _s5_disc_lambda_kernel(lr_ref, li_ref, ls_ref, lbr_ref, lbi_ref, cr_ref, ci_ref):
    lr, li = lr_ref[...], li_ref[...]
    dt = jnp.exp(ls_ref[...])
    mag = jnp.exp(dt * lr)
    ang = dt * li
    lbr = mag * jnp.cos(ang)
    lbi = mag * jnp.sin(ang)
    nr, ni = lbr - 1.0, lbi
    den = lr * lr + li * li
    lbr_ref[...] = lbr
    lbi_ref[...] = lbi
    cr_ref[...] = (nr * lr + ni * li) / den
    ci_ref[...] = (ni * lr - nr * li) / den


def _s5_discretise(lam_re, lam_im, log_step):
    depth, groups, nst = lam_re.shape
    rows = depth * groups
    shp = jax.ShapeDtypeStruct((rows, nst), F32)
    outs = pl.pallas_call(
        _s5_disc_lambda_kernel, out_shape=(shp, shp, shp, shp), name="s5_disc_lambda",
    )(lam_re.reshape(rows, nst), lam_im.reshape(rows, nst), log_step.reshape(rows, 1))
    return tuple(o.reshape(depth, groups, nst) for o in outs)


def _s5_pair_weights(lbr, lbi, zoh_r, zoh_i, b_re, b_im, c_re, c_im, d_skip):
    depth, groups, nst, cg = b_re.shape
    hi = lax.Precision.HIGHEST
    zr, zi = zoh_r[..., None], zoh_i[..., None]
    bbr, bbi = zr * b_re - zi * b_im, zr * b_im + zi * b_re
    lr, li = lbr[..., None], lbi[..., None]
    blr, bli = bbr * lr - bbi * li, bbr * li + bbi * lr
    tb = jnp.swapaxes(jnp.stack([blr, bli, bbr, bbi], axis=1), 3, 4).reshape(depth, 4, groups * cg, nst)

    lrc, lic = lbr[:, :, None, :], lbi[:, :, None, :]
    pr, pi = c_re * lrc - c_im * lic, c_re * lic + c_im * lrc
    qr, qi = pr * lrc - pi * lic, pr * lic + pi * lrc
    tcc = jnp.stack([pr, -pi, qr, -qi], axis=1).transpose(0, 1, 3, 2, 4).reshape(depth, 4, cg, groups * nst)

    mm = lambda a, b: jnp.einsum("lgcn,lgnd->lgcd", a, b, precision=hi)
    g0 = mm(c_re, bbr) - mm(c_im, bbi)
    g1 = mm(pr, bbr) - mm(pi, bbi)
    g0d = g0 + d_skip.reshape(depth, groups, cg)[..., None] * jnp.eye(cg, dtype=F32)
    tg = jnp.stack([g0d, g1], axis=1).transpose(0, 1, 3, 2, 4).reshape(depth, 2, cg, groups * cg)
    lam2 = jnp.stack([lbr * lbr - lbi * lbi, 2.0 * lbr * lbi], axis=1).reshape(depth, 2, groups * nst)
    return tb, tcc, tg, lam2


def _s5_embed_weights(tb_ref, tcc_ref, tg_ref, wb_ref, wct_ref):
    nbk = wb_ref.shape[0]
    cg, nst = tcc_ref.shape[1], tb_ref.shape[2]
    bl = wb_ref.shape[2] // 2
    assert cg & (cg - 1) == 0 and nst & (nst - 1) == 0
    c_shift, n_shift = cg.bit_length() - 1, nst.bit_length() - 1
    iota = lambda shape, dim: lax.broadcasted_iota(jnp.int32, shape, dim)
    e_in = jnp.where((iota((nst, bl), 1) & (nst - 1)) == iota((nst, bl), 0), 1.0, 0.0).astype(BF16)
    e_out = jnp.where((iota((LANES, cg), 0) & (cg - 1)) == iota((LANES, cg), 1), 1.0, 0.0).astype(BF16)
    m_state = (iota((LANES, bl), 0) >> c_shift) == (iota((LANES, bl), 1) >> n_shift)
    m_direct = (iota((LANES, LANES), 0) >> c_shift) == (iota((LANES, LANES), 1) >> c_shift)
    zeros = jnp.zeros((LANES, LANES), BF16)
    for j in range(nbk):
        for k in range(4):
            half, part = divmod(k, 2)
            a = tb_ref[k, j * LANES:(j + 1) * LANES, :].astype(BF16)
            wb_ref[j, half * LANES:(half + 1) * LANES, part * bl:(part + 1) * bl] = (
                jnp.where(m_state, _dot(a, e_in), 0.0).astype(BF16))
        for k in range(4):
            eo, part = divmod(k, 2)
            x = tcc_ref[k, :, j * bl:(j + 1) * bl].astype(BF16)
            wct_ref[j, eo * LANES:(eo + 1) * LANES, part * bl:(part + 1) * bl] = (
                jnp.where(m_state, _dot(e_out, x), 0.0).astype(BF16))
        direct = [jnp.where(m_direct, _dot(e_out, tg_ref[k, :, j * LANES:(j + 1) * LANES].astype(BF16)), 0.0
                            ).astype(BF16) for k in range(2)]
        base = 2 * bl
        wct_ref[j, 0:LANES, base:base + LANES] = direct[0]
        wct_ref[j, 0:LANES, base + LANES:base + 2 * LANES] = zeros
        wct_ref[j, LANES:2 * LANES, base:base + LANES] = direct[1]
        wct_ref[j, LANES:2 * LANES, base + LANES:base + 2 * LANES] = direct[0]


def _s5_kernel(u_ref, tb_ref, tcc_ref, tg_ref, lam2_ref, wg_ref, bg_ref, x0r_ref, x0i_ref, accr_ref, acci_ref,
               *refs, batch, with_cast):
    del accr_ref, acci_ref
    refs, cast_srcs, cast_dsts = _split_cast_refs(refs, 3, int(with_cast))
    o_ref, xr_ref, xi_ref, xs_ref, y_ref, st_ref, ue_ref, uo_ref, tmp_ref, wb_ref, wct_ref = refs
    for src, dst in zip(cast_srcs, cast_dsts):
        dst[...] = src[...].astype(BF16)
    c_idx = pl.program_id(0)
    tt = u_ref.shape[0]
    tp = tt // 2
    prows = tp * batch
    width = u_ref.shape[1] // batch
    n_slab = width // LANES
    nbk = wb_ref.shape[0]
    bl = wb_ref.shape[2] // 2
    ns = nbk * bl
    grp = st_ref.shape[1]
    per = grp // batch

    @pl.when(c_idx == 0)
    def _():
        st_ref[0] = jnp.concatenate([x0r_ref[...]] * per, axis=0)
        st_ref[1] = jnp.concatenate([x0i_ref[...]] * per, axis=0)
        _s5_embed_weights(tb_ref, tcc_ref, tg_ref, wb_ref, wct_ref)

    for b in range(batch):
        for m in range(n_slab):
            tmp_ref[...] = u_ref[:, b * width + m * LANES:b * width + (m + 1) * LANES]
            ue_ref[m, pl.ds(b, tp, stride=batch), :] = tmp_ref[pl.ds(0, tp, stride=2), :]
            uo_ref[m, pl.ds(b, tp, stride=batch), :] = tmp_ref[pl.ds(1, tp, stride=2), :]

    for j in range(nbk):
        lhs = jnp.concatenate([ue_ref[j], uo_ref[j]], axis=1).astype(BF16)
        w = _dot(lhs, wb_ref[j])
        xs_ref[:, j * bl:(j + 1) * bl] = w[:, :bl]
        xs_ref[:, ns + j * bl:ns + (j + 1) * bl] = w[:, bl:]

    second = lax.broadcasted_iota(jnp.int32, (grp, S5_SCAN_LANES), 0) >= batch
    for cb in range(ns // S5_SCAN_LANES):
        lo = cb * S5_SCAN_LANES
        re_l = slice(lo, lo + S5_SCAN_LANES)
        im_l = slice(ns + lo, ns + lo + S5_SCAN_LANES)
        ar = lam2_ref[0:1, re_l]
        ai = lam2_ref[1:2, re_l]

        def step(g, carry, ar=ar, ai=ai, re_l=re_l, im_l=im_l):
            cr, ci = carry
            r = pl.ds(pl.multiple_of(g * grp, grp), grp)
            wr = xs_ref[r, re_l]
            wi = xs_ref[r, im_l]
            if per == 2:
                tr = ar * cr - ai * ci + pltpu.roll(wr, batch, axis=0)
                ti = ar * ci + ai * cr + pltpu.roll(wi, batch, axis=0)
                xs_ref[r, re_l] = jnp.where(second, tr, cr)
                xs_ref[r, im_l] = jnp.where(second, ti, ci)
                vr = ar * tr - ai * ti + wr
                vi = ar * ti + ai * tr + wi
                nr = jnp.where(second, vr, pltpu.roll(vr, batch, axis=0))
                ni = jnp.where(second, vi, pltpu.roll(vi, batch, axis=0))
            else:
                xs_ref[r, re_l] = cr
                xs_ref[r, im_l] = ci
                nr = ar * cr - ai * ci + wr
                ni = ar * ci + ai * cr + wi
            return nr, ni

        cr, ci = lax.fori_loop(0, prows // grp, step, (st_ref[0, :, re_l], st_ref[1, :, re_l]), unroll=2)
        st_ref[0, :, re_l] = cr
        st_ref[1, :, re_l] = ci

    for j in range(nbk):
        lhs = jnp.concatenate([xs_ref[:, j * bl:(j + 1) * bl], xs_ref[:, ns + j * bl:ns + (j + 1) * bl],
                               ue_ref[j], uo_ref[j]], axis=1).astype(BF16)
        yj = _dot_nt(lhs, wct_ref[j])
        y_ref[0:prows, j * LANES:(j + 1) * LANES] = yj[:, :LANES]
        y_ref[prows:2 * prows, j * LANES:(j + 1) * LANES] = yj[:, LANES:]

    hh = jax.nn.gelu(y_ref[...])
    gate = _sigmoid(_dot(hh.astype(BF16), wg_ref[...]) + bg_ref[...])
    out = hh * gate
    for m in range(n_slab):
        ue_ref[m] = out[0:prows, m * LANES:(m + 1) * LANES]
        uo_ref[m] = out[prows:2 * prows, m * LANES:(m + 1) * LANES]
    for b in range(batch):
        for m in range(n_slab):
            tmp_ref[pl.ds(0, tp, stride=2), :] = ue_ref[m, pl.ds(b, tp, stride=batch), :]
            tmp_ref[pl.ds(1, tp, stride=2), :] = uo_ref[m, pl.ds(b, tp, stride=batch), :]
            o_ref[:, b * width + m * LANES:b * width + (m + 1) * LANES] = tmp_ref[...].astype(BF16)

    @pl.when(c_idx == pl.num_programs(0) - 1)
    def _():
        xr_ref[...] = st_ref[0, grp - batch:grp, :]
        xi_ref[...] = st_ref[1, grp - batch:grp, :]


def _s5(u, tb, tcc, tg, lam2, w_glu, b_glu, x0r, x0i, acc_r, acc_i, *, batch, layer, cast_src=None):
    seq = u.shape[0]
    width = u.shape[1] // batch
    n = seq * batch
    ns = lam2.shape[2]
    nbk = width // LANES
    bl = ns // nbk
    assert batch % SUBLANES == 0 or 2 * batch == SUBLANES
    assert seq % 2 == 0 and tb.shape[2] == width
    grp = max(batch, SUBLANES)
    rows = min(S5_ROWS, n)
    tt = rows // batch
    prows = rows // 2
    lay4 = lambda c: (layer, 0, 0, 0)
    lay3 = lambda c: (layer, 0, 0)
    st_in = pl.BlockSpec((None, batch, ns), lay3)
    resident = pl.Buffered(1)
    wb_shape = (nbk, 2 * LANES, 2 * bl)
    wct_shape = (nbk, 2 * LANES, 2 * bl + 2 * LANES)
    need = (2 * rows * width * 4 + (nbk * 2 * LANES * (4 * bl + 2 * LANES) + width * width) * 2
            + 4 * width * LANES * 4 + 6 * 16 * ns * 4
            + 2 * rows * width * 2 + prows * 2 * ns * 4 + rows * width * 4 + 4 * rows * width * 4
            + 4 * prows * (4 * bl + 2 * LANES) + (8 * batch + 2 * grp) * ns * 4)
    st_shape = jax.ShapeDtypeStruct(acc_r.shape, F32)
    in_specs = [
        pl.BlockSpec((tt, batch * width), lambda c: (c, 0)),
        pl.BlockSpec((None,) + tb.shape[1:], lay4, pipeline_mode=resident),
        pl.BlockSpec((None,) + tcc.shape[1:], lay4, pipeline_mode=resident),
        pl.BlockSpec((None,) + tg.shape[1:], lay4, pipeline_mode=resident),
        pl.BlockSpec((None, 2, ns), lay3),
        pl.BlockSpec((None, width, width), lay3, pipeline_mode=resident),
        pl.BlockSpec((None, 1, width), lay3),
        st_in,
        st_in,
        pl.BlockSpec(memory_space=pl.ANY),
        pl.BlockSpec(memory_space=pl.ANY),
    ]
    out_specs = [pl.BlockSpec((tt, batch * width), lambda c: (c, 0)), st_in, st_in]
    out_shape = [jax.ShapeDtypeStruct((seq, batch * width), BF16), st_shape, st_shape]
    args = [u, tb, tcc, tg, lam2, w_glu, b_glu, x0r, x0i, acc_r, acc_i]
    if cast_src is not None:
        c_in, c_out, c_shape, c_bytes = _cast_job(cast_src, layer, n // rows, lambda c: c, axis=1)
        in_specs.append(c_in)
        out_specs.append(c_out)
        out_shape.append(c_shape)
        args.append(cast_src)
        need += c_bytes
    return pl.pallas_call(
        functools.partial(_s5_kernel, batch=batch, with_cast=cast_src is not None),
        out_shape=tuple(out_shape),
        grid=(n // rows,),
        in_specs=in_specs,
        out_specs=tuple(out_specs),
        input_output_aliases={9: 1, 10: 2},
        scratch_shapes=[
            pltpu.VMEM((prows, 2 * ns), F32),
            pltpu.VMEM((rows, width), F32),
            pltpu.VMEM((2, grp, ns), F32),
            pltpu.VMEM((width // LANES, prows, LANES), F32),
            pltpu.VMEM((width // LANES, prows, LANES), F32),
            pltpu.VMEM((tt, LANES), F32),
            pltpu.VMEM(wb_shape, BF16),
            pltpu.VMEM(wct_shape, BF16),
        ],
        compiler_params=pltpu.CompilerParams(
            dimension_semantics=("arbitrary",),
            vmem_limit_bytes=_vmem_limit(need)),
        name="s5",
    )(*args)


def _outproj_kernel(x_ref, oh_ref, os_ref, wh_ref, ws_ref, o_ref):
    o_ref[...] = x_ref[...] + _dot(oh_ref[...], wh_ref[...]) + _dot(os_ref[...], ws_ref[...])


def _outproj(x, o_h, o_s, w_out, *, layer, tm, tn, seq):
    n, d = x.shape
    kh = o_h.shape[1]
    ks = w_out.shape[1] - kh
    assert kh == ks
    if o_s.shape[0] == n:
        os_map = lambda i, j: (i, 0)
    else:
        n_t = seq // tm
        os_map = lambda i, j: (i % n_t, i // n_t)
    need = 2 * (2 * tm * tn * 4 + tm * (kh + ks) * 2 + (kh + ks) * tn * 2) + 2 * tm * tn * 4
    return pl.pallas_call(
        _outproj_kernel,
        out_shape=jax.ShapeDtypeStruct((n, d), F32),
        grid=(n // tm, d // tn),
        in_specs=[
            pl.BlockSpec((tm, tn), lambda i, j: (i, j)),
            pl.BlockSpec((tm, kh), lambda i, j: (i, 0)),
            pl.BlockSpec((tm, ks), os_map),
            pl.BlockSpec((None, kh, tn), lambda i, j: (layer, 0, j)),
            pl.BlockSpec((None, ks, tn), lambda i, j: (layer, 1, j)),
        ],
        out_specs=pl.BlockSpec((tm, tn), lambda i, j: (i, j)),
        compiler_params=pltpu.CompilerParams(
            dimension_semantics=("parallel", "parallel"),
            vmem_limit_bytes=_vmem_limit(need)),
        name="outproj",
    )(x, o_h, o_s, w_out, w_out)


def _ffn_kernel(x_ref, g_ref, w1_ref, w2_ref, gf_ref, o_ref, h_ref, *, final_norm):
    f = pl.program_id(1)
    tm = x_ref.shape[0]

    @pl.when(f == 0)
    def _():
        g = g_ref[...]

        def body(i, carry):
            r = pl.ds(pl.multiple_of(i * NORM_ROWS, NORM_ROWS), NORM_ROWS)
            h_ref[r, :] = _rmsnorm(x_ref[r, :], g).astype(BF16)
            o_ref[r, :] = jnp.zeros((NORM_ROWS, o_ref.shape[1]), F32)
            return carry

        lax.fori_loop(0, tm // NORM_ROWS, body, 0, unroll=NORM_UNROLL)

    a = _dot(h_ref[...], w1_ref[...])
    a = jnp.square(jnp.maximum(a, 0.0)).astype(BF16)
    cw = o_ref.shape[1] // FFN_ACC_CHUNKS
    for c in range(FFN_ACC_CHUNKS):
        o_ref[:, c * cw:(c + 1) * cw] += _dot(a, w2_ref[:, c * cw:(c + 1) * cw])

    @pl.when(f == pl.num_programs(1) - 1)
    def _():
        gf = gf_ref[...]

        def body(i, carry):
            r = pl.ds(pl.multiple_of(i * NORM_ROWS, NORM_ROWS), NORM_ROWS)
            y = x_ref[r, :] + o_ref[r, :]
            if final_norm:
                y = _rmsnorm(y, gf)
            o_ref[r, :] = y
            return carry

        lax.fori_loop(0, tm // NORM_ROWS, body, 0, unroll=NORM_UNROLL)


def _ffn(x, g, w1, w2, gf, *, layer, tm, tf, final_norm):
    n, d = x.shape
    ff = w1.shape[1]
    need = tm * d * 4 + tm * d * 2 + 2 * tm * d * 4 + 4 * d * tf * 2 + tm * tf * 6 + tm * d * 4
    kernel = functools.partial(_ffn_kernel, final_norm=final_norm)
    return pl.pallas_call(
        kernel,
        out_shape=jax.ShapeDtypeStruct((n, d), F32),
        grid=(n // tm, ff // tf),
        in_specs=[
            pl.BlockSpec((tm, d), lambda i, f: (i, 0), pipeline_mode=pl.Buffered(1)),
            pl.BlockSpec((None, 1, d), lambda i, f: (layer, 0, 0)),
            pl.BlockSpec((d, tf), lambda i, f: (0, f)),
            pl.BlockSpec((tf, d), lambda i, f: (f, 0)),
            pl.BlockSpec((1, d), lambda i, f: (0, 0)),
        ],
        out_specs=pl.BlockSpec((tm, d), lambda i, f: (i, 0)),
        scratch_shapes=[pltpu.VMEM((tm, d), BF16)],
        compiler_params=pltpu.CompilerParams(
            dimension_semantics=("parallel", "arbitrary"),
            vmem_limit_bytes=_vmem_limit(need)),
        name="ffn",
    )(x, g, w1, w2, gf)


def _trunk(x, st_h, st_r, st_i, wts, cast_w=None):
    batch, seq, d = x.shape
    n = batch * seq
    depth = wts["w_in"].shape[0]
    cast = cast_w is None
    made = {"ffn": []} if cast else cast_w
    heads = st_h.shape[2]
    hgrn_width = heads * HEAD_DIM
    groups, nst = st_r.shape[2], st_r.shape[3]
    s5_width = wts["w_glu"].shape[1]
    tm = min(1024, n)
    xf = x.reshape(n, d)
    x0r = st_r.reshape(depth, batch, groups * nst)
    x0i = st_i.reshape(depth, batch, groups * nst)
    time_major = seq % tm == 0
    new_h = jnp.zeros(st_h.shape, F32)
    new_r = jnp.zeros(x0r.shape, F32)
    new_i = jnp.zeros(x0i.shape, F32)
    for l in range(depth):
        w_in, w_layer = (wts["w_in0"], 0) if l == 0 else (made["w_in_rest"], l - 1)
        p, u = _inproj(xf, wts["norm1_g"], w_in, wts["lb_logits"],
                       layer=l, w_layer=w_layer, tm=tm, tn=s5_width, batch=batch, seq=seq)
        hosted = [(wts["w_out"], 0)] + ([(wts["w_in"], 1)] if depth > 1 else []) if cast and l == 0 else []
        o_h, new_h, *w1 = _hgrn(p, wts["onorm_g"], st_h, new_h, batch=batch, seq=seq, layer=l,
                                cast_src=wts["w_ff1"] if cast else None, cast_layers=hosted)
        if hosted:
            made["w_out"] = w1[1]
            made["w_in_rest"] = w1[2] if depth > 1 else None
        if not time_major:
            u = u.reshape(batch, seq, s5_width).transpose(1, 0, 2).reshape(seq, batch * s5_width)
        o_s, new_r, new_i, *w2 = _s5(u, wts["s5_tb"], wts["s5_tcc"], wts["s5_tg"], wts["s5_lam2"],
                                     wts["w_glu"], wts["b_glu"], x0r, x0i, new_r, new_i,
                                     batch=batch, layer=l, cast_src=wts["w_ff2"] if cast else None)
        if not time_major:
            o_s = o_s.reshape(seq, batch, s5_width).transpose(1, 0, 2).reshape(n, s5_width)
        if cast:
            made["ffn"].append((w1[0], w2[0]))
        x1 = _outproj(xf, o_h, o_s, made["w_out"], layer=l, tm=min(512, n), tn=d, seq=seq)
        xf = _ffn(x1, wts["norm2_g"], *made["ffn"][l], wts["final_g"],
                  layer=l, tm=tm, tf=1024, final_norm=(l == depth - 1))
    return (xf.reshape(batch, seq, d), new_h, new_r.reshape(st_r.shape), new_i.reshape(st_i.shape)), made


def kernel(x_prompt, x_sample, state_hgrn, state_s5_re, state_s5_im, norm1_g, w_in, hgrn_lb_logits,
           hgrn_onorm_g, s5_lambda_re, s5_lambda_im, s5_log_step, s5_B_re, s5_B_im, s5_C_re, s5_C_im,
           s5_D, s5_w_glu, s5_b_glu, w_out, norm2_g, w_ff1, w_ff2, final_norm_g):
    depth, d = norm1_g.shape
    row3 = lambda a: a.reshape(depth, 1, a.shape[-1])
    lbr, lbi, zoh_r, zoh_i = _s5_discretise(s5_lambda_re, s5_lambda_im, s5_log_step)
    s5_tb, s5_tcc, s5_tg, s5_lam2 = _s5_pair_weights(lbr, lbi, zoh_r, zoh_i, s5_B_re, s5_B_im,
                                                     s5_C_re, s5_C_im, s5_D)
    wts = {
        "norm1_g": row3(norm1_g), "norm2_g": row3(norm2_g), "final_g": final_norm_g.reshape(1, d),
        "lb_logits": hgrn_lb_logits, "onorm_g": row3(hgrn_onorm_g),
        "w_in0": _cast_first_layer(w_in),
        "w_in": w_in, "w_out": w_out, "w_ff1": w_ff1, "w_ff2": w_ff2,
        "w_glu": s5_w_glu.astype(BF16), "b_glu": row3(s5_b_glu),
        "s5_tb": s5_tb, "s5_tcc": s5_tcc, "s5_tg": s5_tg, "s5_lam2": s5_lam2,
    }
    bp = x_prompt.shape[0]
    zh = jnp.zeros((depth, bp) + state_hgrn.shape[2:], F32)
    zs = jnp.zeros((depth, bp) + state_s5_re.shape[2:], F32)
    (y_p, hp, rp, ip), cast_w = _trunk(x_prompt, zh, zs, zs, wts)
    (y_s, hs, rs, is_), _ = _trunk(x_sample, state_hgrn, state_s5_re, state_s5_im, wts, cast_w)
    return (y_p, y_s, hp, rp, ip, hs, rs, is_)
```
